```python
import math
import jax
import jax.numpy as jnp
from jax import lax
import numpy as np

D_MODEL = 1024
BATCH = 4
SEQ = 4096
DEPTH = 4
DEC_BATCH = 128
DEC_SEQ = 4
PAST_LEN = 8192
PAGE_SIZE = 128

HEAD_DIM = 64
N_Q_HEADS = 8
N_KV_HEADS = 2
Q_PER_KV = N_Q_HEADS // N_KV_HEADS
ATTN_WIDTH = N_Q_HEADS * HEAD_DIM
KV_WIDTH = N_KV_HEADS * HEAD_DIM
WINDOW = 128
ROPE_THETA = 10000.0
ATTN_SCALE = 1.0 / math.sqrt(HEAD_DIM)
SSM_WIDTH = D_MODEL // 4
SSM_GROUP = 16
N_SSM_GROUPS = SSM_WIDTH // SSM_GROUP
SSM_STATE = 64
CONV_WIDTH = D_MODEL // 4
CONV_K = 31
MIX_WIDTH = ATTN_WIDTH + SSM_WIDTH + CONV_WIDTH
IN_WIDTH = ATTN_WIDTH + 2 * KV_WIDTH + SSM_WIDTH + 2 * CONV_WIDTH
D_FF = -(-8 * D_MODEL // (3 * 256)) * 256
EPS = 1e-6
NEG = -1e30

kernel_name = 'hybrid_s5_conformer_swa_decoder_step'


def _rmsnorm(x, g):
    xf = x.astype(jnp.float32)
    y = xf * lax.rsqrt(jnp.mean(xf * xf, axis=-1, keepdims=True) + EPS)
    return (y * g.astype(jnp.float32)).astype(x.dtype)


def _layernorm(x, g, b):
    xf = x.astype(jnp.float32)
    xc = xf - jnp.mean(xf, axis=-1, keepdims=True)
    var = jnp.mean(xc * xc, axis=-1, keepdims=True)
    y = xc * lax.rsqrt(var + EPS) * g.astype(jnp.float32) + b.astype(jnp.float32)
    return y.astype(x.dtype)


def _rope(x, pos):
    half = HEAD_DIM // 2
    inv_freq = ROPE_THETA ** (-jnp.arange(half, dtype=jnp.float32) / half)
    ang = pos.astype(jnp.float32)[:, None] * inv_freq[None, :]
    cos = jnp.cos(ang)[None, :, None, :]
    sin = jnp.sin(ang)[None, :, None, :]
    xf = x.astype(jnp.float32)
    x1, x2 = xf[..., :half], xf[..., half:]
    return jnp.concatenate([x1 * cos - x2 * sin, x2 * cos + x1 * sin], axis=-1).astype(x.dtype)


def _sink_softmax(s, sinks):
    sk = sinks.astype(jnp.float32).reshape(N_KV_HEADS, Q_PER_KV, 1, 1)
    m = jnp.maximum(jnp.max(s, axis=-1, keepdims=True), sk)
    e = jnp.exp(s - m)
    return e / (jnp.sum(e, axis=-1, keepdims=True) + jnp.exp(sk - m))


def _attn_banded(q, k, v, sinks):
    b, l = q.shape[0], q.shape[1]
    nb = l // WINDOW
    qb = q.reshape(b, nb, WINDOW, N_KV_HEADS, Q_PER_KV, HEAD_DIM)
    kb = k.reshape(b, nb, WINDOW, N_KV_HEADS, HEAD_DIM)
    vb = v.reshape(b, nb, WINDOW, N_KV_HEADS, HEAD_DIM)
    pad = ((0, 0), (1, 0), (0, 0), (0, 0), (0, 0))
    kk = jnp.concatenate([jnp.pad(kb, pad)[:, :-1], kb], axis=2)
    vv = jnp.concatenate([jnp.pad(vb, pad)[:, :-1], vb], axis=2)
    s = jnp.einsum('bnqhgd,bnkhd->bnhgqk', qb, kk, preferred_element_type=jnp.float32) * ATTN_SCALE
    blk = jnp.arange(nb)[:, None, None]
    qi = jnp.arange(WINDOW)[None, :, None]
    kj = jnp.arange(2 * WINDOW)[None, None, :]
    rel = WINDOW + qi - kj
    mask = (rel >= 0) & (rel < WINDOW) & (blk * WINDOW - WINDOW + kj >= 0)
    s = jnp.where(mask[None, :, None, None], s, NEG)
    p = _sink_softmax(s, sinks)
    o = jnp.einsum('bnhgqk,bnkhd->bnqhgd', p.astype(v.dtype), vv)
    return o.reshape(b, l, ATTN_WIDTH)


def _attn_window_cache(q, k, v, k_buf, v_buf, sinks, pos0):
    b, t = q.shape[0], q.shape[1]
    wb = k_buf.shape[1]
    kk = jnp.concatenate([k_buf.astype(k.dtype), k], axis=1)
    vv = jnp.concatenate([v_buf.astype(v.dtype), v], axis=1)
    qg = q.reshape(b, t, N_KV_HEADS, Q_PER_KV, HEAD_DIM)
    s = jnp.einsum('bqhgd,bkhd->bhgqk', qg, kk, preferred_element_type=jnp.float32) * ATTN_SCALE
    qpos = pos0 + jnp.arange(t)
    kpos = pos0 - wb + jnp.arange(wb + t)
    rel = qpos[:, None] - kpos[None, :]
    mask = (rel >= 0) & (rel < WINDOW)
    s = jnp.where(mask, s, NEG)
    p = _sink_softmax(s, sinks)
    o = jnp.einsum('bhgqk,bkhd->bqhgd', p.astype(v.dtype), vv).reshape(b, t, ATTN_WIDTH)
    return o, kk[:, t:], vv[:, t:]


def _complex_affine_combine(e1, e2):
    a1r, a1i, b1r, b1i = e1
    a2r, a2i, b2r, b2i = e2
    ar = a2r * a1r - a2i * a1i
    ai = a2r * a1i + a2i * a1r
    br = a2r * b1r - a2i * b1i + b2r
    bi = a2r * b1i + a2i * b1r + b2i
    return ar, ai, br, bi


def _s5(zu, h0_re, h0_im, lam_re, lam_im, log_dt, b_re, b_im, c_re, c_im, d_skip, w_glu, b_glu):
    f32 = jnp.float32
    bsz, l = zu.shape[0], zu.shape[1]
    u = zu.astype(f32).reshape(bsz, l, N_SSM_GROUPS, SSM_GROUP)
    lr, li = lam_re.astype(f32), lam_im.astype(f32)
    dt = jnp.exp(log_dt.astype(f32))[:, None]
    mag = jnp.exp(lr * dt)
    ab_re = mag * jnp.cos(li * dt)
    ab_im = mag * jnp.sin(li * dt)
    den = lr * lr + li * li
    nr = ab_re - 1.0
    coef_re = (nr * lr + ab_im * li) / den
    coef_im = (ab_im * lr - nr * li) / den
    br, bi = b_re.astype(f32), b_im.astype(f32)
    bb_re = coef_re[..., None] * br - coef_im[..., None] * bi
    bb_im = coef_re[..., None] * bi + coef_im[..., None] * br
    bu_re = jnp.einsum('gpc,blgc->blgp', bb_re, u)
    bu_im = jnp.einsum('gpc,blgc->blgp', bb_im, u)
    a_re = jnp.broadcast_to(ab_re, bu_re.shape)
    a_im = jnp.broadcast_to(ab_im, bu_im.shape)
    acc_r, acc_i, sb_r, sb_i = lax.associative_scan(_complex_affine_combine, (a_re, a_im, bu_re, bu_im), axis=1)
    hr0 = h0_re.astype(f32)[:, None]
    hi0 = h0_im.astype(f32)[:, None]
    h_re = acc_r * hr0 - acc_i * hi0 + sb_r
    h_im = acc_r * hi0 + acc_i * hr0 + sb_i
    y = jnp.einsum('gcp,blgp->blgc', c_re.astype(f32), h_re) - jnp.einsum('gcp,blgp->blgc', c_im.astype(f32), h_im)
    y = (y + d_skip.astype(f32).reshape(N_SSM_GROUPS, SSM_GROUP) * u).reshape(bsz, l, SSM_WIDTH)
    z = jax.nn.gelu(y)
    out = z * jax.nn.sigmoid(z @ w_glu.astype(f32) + b_glu.astype(f32))
    return out.astype(zu.dtype), h_re[:, -1], h_im[:, -1]


def _conv_module(za, zg, buf, conv_w, conv_b, ln_g, ln_b):
    v = za * jax.nn.sigmoid(zg)
    full = jnp.concatenate([buf.astype(v.dtype), v], axis=1)
    y = lax.conv_general_dilated(full, conv_w.astype(v.dtype)[:, None, :], window_strides=(1,), padding='VALID',
                                 dimension_numbers=('NWC', 'WIO', 'NWC'), feature_group_count=CONV_WIDTH)
    y = y + conv_b.astype(v.dtype)
    y = jax.nn.silu(_layernorm(y, ln_g, ln_b))
    return y, full[:, full.shape[1] - (CONV_K - 1):]


def _layer(x, c, pos0, k_buf, v_buf, h0_re, h0_im, conv_buf, w, prompt):
    b, l = x.shape[0], x.shape[1]
    mod = jax.nn.silu(c) @ w['w_mod'] + w['b_mod']
    sh1, sc1, g1, sh2, sc2, g2 = jnp.split(mod[:, None, :], 6, axis=-1)
    h = _rmsnorm(x, w['norm1_g']) * (1 + sc1) + sh1
    z = h @ w['w_in']
    o1 = ATTN_WIDTH
    o2 = o1 + KV_WIDTH
    o3 = o2 + KV_WIDTH
    o4 = o3 + SSM_WIDTH
    o5 = o4 + CONV_WIDTH
    zq, zk, zv, zu, za, zg = jnp.split(z, [o1, o2, o3, o4, o5], axis=-1)
    pos = pos0 + jnp.arange(l)
    q = _rope(zq.reshape(b, l, N_Q_HEADS, HEAD_DIM), pos)
    k = _rope(zk.reshape(b, l, N_KV_HEADS, HEAD_DIM), pos)
    v = zv.reshape(b, l, N_KV_HEADS, HEAD_DIM)
    if prompt:
        o_attn = _attn_banded(q, k, v, w['sinks'])
        n_keep = min(WINDOW, l)
        new_k, new_v = k[:, l - n_keep:], v[:, l - n_keep:]
    else:
        o_attn, new_k, new_v = _attn_window_cache(q, k, v, k_buf, v_buf, w['sinks'], pos0)
    o_ssm, h_re, h_im = _s5(zu, h0_re, h0_im, w['lam_re'], w['lam_im'], w['log_dt'], w['b_re'], w['b_im'],
                            w['c_re'], w['c_im'], w['d_skip'], w['w_glu'], w['b_glu'])
    o_conv, new_conv = _conv_module(za, zg, conv_buf, w['conv_w'], w['conv_b'], w['conv_ln_g'], w['conv_ln_b'])
    mixed = jnp.concatenate([o_attn.astype(x.dtype), o_ssm.astype(x.dtype), o_conv.astype(x.dtype)], axis=-1)
    x = x + g1 * (mixed @ w['w_out'])
    h2 = _rmsnorm(x, w['norm2_g']) * (1 + sc2) + sh2
    ffn = (jax.nn.silu(h2 @ w['w_gate']) * (h2 @ w['w_up'])) @ w['w_down']
    x = x + g2 * ffn
    return x, new_k, new_v, h_re, h_im, new_conv


def setup_inputs(seed: int = 0) -> dict:
    key = jax.random.key(seed)
    ks = jax.random.split(key, 40)
    f32 = jnp.float32

    def nrm(k, shape, s):
        return jax.random.normal(k, shape, f32) * s

    wb = min(WINDOW, PAST_LEN)
    lam_im = jnp.pi * jnp.arange(SSM_STATE, dtype=f32)[None, None, :] + nrm(ks[15], (DEPTH, N_SSM_GROUPS, SSM_STATE), 0.01)
    return {
        'x_prompt': nrm(ks[0], (BATCH, SEQ, D_MODEL), 1.0),
        'x_sample': nrm(ks[1], (DEC_BATCH, DEC_SEQ, D_MODEL), 1.0),
        'c_prompt': nrm(ks[2], (BATCH, D_MODEL), 1.0),
        'c_sample': nrm(ks[3], (DEC_BATCH, D_MODEL), 1.0),
        'cache_k': nrm(ks[4], (DEPTH, DEC_BATCH, wb, N_KV_HEADS, HEAD_DIM), 1.0),
        'cache_v': nrm(ks[5], (DEPTH, DEC_BATCH, wb, N_KV_HEADS, HEAD_DIM), 1.0),
        'state_ssm_re': nrm(ks[6], (DEPTH, DEC_BATCH, N_SSM_GROUPS, SSM_STATE), 0.5),
        'state_ssm_im': nrm(ks[7], (DEPTH, DEC_BATCH, N_SSM_GROUPS, SSM_STATE), 0.5),
        'state_conv': nrm(ks[8], (DEPTH, DEC_BATCH, CONV_K - 1, CONV_WIDTH), 0.5),
        'norm1_g': 1.0 + nrm(ks[9], (DEPTH, D_MODEL), 0.05),
        'norm2_g': 1.0 + nrm(ks[10], (DEPTH, D_MODEL), 0.05),
        'w_mod': nrm(ks[11], (DEPTH, D_MODEL, 6 * D_MODEL), 0.5 * D_MODEL ** -0.5),
        'b_mod': nrm(ks[12], (DEPTH, 6 * D_MODEL), 0.01),
        'w_in': nrm(ks[13], (DEPTH, D_MODEL, IN_WIDTH), D_MODEL ** -0.5),
        'attn_sinks': nrm(ks[14], (DEPTH, N_Q_HEADS), 0.5),
        'ssm_lam_re': -0.5 + nrm(ks[16], (DEPTH, N_SSM_GROUPS, SSM_STATE), 0.01),
        'ssm_lam_im': lam_im,
        'ssm_log_dt': jax.random.uniform(ks[17], (DEPTH, N_SSM_GROUPS), f32, minval=math.log(1e-3), maxval=math.log(1e-1)),
        'ssm_b_re': nrm(ks[18], (DEPTH, N_SSM_GROUPS, SSM_STATE, SSM_GROUP), (2 * SSM_GROUP) ** -0.5),
        'ssm_b_im': nrm(ks[19], (DEPTH, N_SSM_GROUPS, SSM_STATE, SSM_GROUP), (2 * SSM_GROUP) ** -0.5),
        'ssm_c_re': nrm(ks[20], (DEPTH, N_SSM_GROUPS, SSM_GROUP, SSM_STATE), (2 * SSM_STATE) ** -0.5),
        'ssm_c_im': nrm(ks[21], (DEPTH, N_SSM_GROUPS, SSM_GROUP, SSM_STATE), (2 * SSM_STATE) ** -0.5),
        'ssm_d': nrm(ks[22], (DEPTH, SSM_WIDTH), 1.0),
        'ssm_w_glu': nrm(ks[23], (DEPTH, SSM_WIDTH, SSM_WIDTH), SSM_WIDTH ** -0.5),
        'ssm_b_glu': nrm(ks[24], (DEPTH, SSM_WIDTH), 0.01),
        'conv_w': nrm(ks[25], (DEPTH, CONV_K, CONV_WIDTH), CONV_K ** -0.5),
        'conv_b': nrm(ks[26], (DEPTH, CONV_WIDTH), 0.01),
        'conv_ln_g': 1.0 + nrm(ks[27], (DEPTH, CONV_WIDTH), 0.05),
        'conv_ln_b': nrm(ks[28], (DEPTH, CONV_WIDTH), 0.01),
        'w_out': nrm(ks[29], (DEPTH, MIX_WIDTH, D_MODEL), MIX_WIDTH ** -0.5),
        'w_gate': nrm(ks[30], (DEPTH, D_MODEL, D_FF), D_MODEL ** -0.5),
        'w_up': nrm(ks[31], (DEPTH, D_MODEL, D_FF), D_MODEL ** -0.5),
        'w_down': nrm(ks[32], (DEPTH, D_FF, D_MODEL), D_FF ** -0.5),
        'final_norm_g': 1.0 + nrm(ks[33], (D_MODEL,), 0.05),
    }


def reference(x_prompt, x_sample, c_prompt, c_sample, cache_k, cache_v, state_ssm_re, state_ssm_im, state_conv,
              norm1_g, norm2_g, w_mod, b_mod, w_in, attn_sinks, ssm_lam_re, ssm_lam_im, ssm_log_dt,
              ssm_b_re, ssm_b_im, ssm_c_re, ssm_c_im, ssm_d, ssm_w_glu, ssm_b_glu,
              conv_w, conv_b, conv_ln_g, conv_ln_b, w_out, w_gate, w_up, w_down, final_norm_g):
    xp, xs = x_prompt, x_sample
    bp = xp.shape[0]
    zero_h = jnp.zeros((bp, N_SSM_GROUPS, SSM_STATE), jnp.float32)
    zero_conv = jnp.zeros((bp, CONV_K - 1, CONV_WIDTH), xp.dtype)
    pk, pv, pre, pim, pcv = [], [], [], [], []
    sk, sv, sre, sim, scv = [], [], [], [], []
    for l in range(DEPTH):
        w = {
            'norm1_g': norm1_g[l], 'norm2_g': norm2_g[l], 'w_mod': w_mod[l], 'b_mod': b_mod[l],
            'w_in': w_in[l], 'sinks': attn_sinks[l],
            'lam_re': ssm_lam_re[l], 'lam_im': ssm_lam_im[l], 'log_dt': ssm_log_dt[l],
            'b_re': ssm_b_re[l], 'b_im': ssm_b_im[l], 'c_re': ssm_c_re[l], 'c_im': ssm_c_im[l],
            'd_skip': ssm_d[l], 'w_glu': ssm_w_glu[l], 'b_glu': ssm_b_glu[l],
            'conv_w': conv_w[l], 'conv_b': conv_b[l], 'conv_ln_g': conv_ln_g[l], 'conv_ln_b': conv_ln_b[l],
            'w_out': w_out[l], 'w_gate': w_gate[l], 'w_up': w_up[l], 'w_down': w_down[l],
        }
        xp, k1, v1, r1, i1, cv1 = _layer(xp, c_prompt, 0, None, None, zero_h, zero_h, zero_conv, w, True)
        xs, k2, v2, r2, i2, cv2 = _layer(xs, c_sample, PAST_LEN, cache_k[l], cache_v[l], state_ssm_re[l],
                                         state_ssm_im[l], state_conv[l], w, False)
        pk.append(k1); pv.append(v1); pre.append(r1); pim.append(i1); pcv.append(cv1)
        sk.append(k2); sv.append(v2); sre.append(r2); sim.append(i2); scv.append(cv2)
    y_prompt = _rmsnorm(xp, final_norm_g)
    y_sample = _rmsnorm(xs, final_norm_g)
    return (y_prompt, y_sample,
            jnp.stack(pk), jnp.stack(pv), jnp.stack(pre), jnp.stack(pim), jnp.stack(pcv),
            jnp.stack(sk), jnp.stack(sv), jnp.stack(sre), jnp.stack(sim), jnp.stack(scv))
```

```python
import functools
import math

import jax
import jax.numpy as jnp
from jax import lax
from jax.experimental import pallas as pl
from jax.experimental.pallas import tpu as pltpu

F32 = jnp.float32
MXU_DTYPE = jnp.bfloat16

V7X_VMEM_BYTES = 64 * 1024 * 1024
VMEM_LIMIT_BYTES = V7X_VMEM_BYTES - 8 * 1024 * 1024
LANES = 128
SUBLANES = 8

D_MODEL = 1024
HEAD_DIM = 64
N_Q_HEADS = 8
N_KV_HEADS = 2
Q_PER_KV = N_Q_HEADS // N_KV_HEADS
ATTN_WIDTH = N_Q_HEADS * HEAD_DIM
KV_WIDTH = N_KV_HEADS * HEAD_DIM
WINDOW = 128
ROPE_THETA = 10000.0
ATTN_SCALE = 1.0 / math.sqrt(HEAD_DIM)
SSM_WIDTH = 256
SSM_GROUP = 16
N_SSM_GROUPS = 16
SSM_STATE = 64
SSM_FLAT = N_SSM_GROUPS * SSM_STATE
CONV_WIDTH = 256
CONV_K = 31
CONV_HALO = 32
IN_WIDTH = ATTN_WIDTH + 2 * KV_WIDTH + SSM_WIDTH + 2 * CONV_WIDTH
D_FF = 2816
EPS = 1e-6
NEG = -1e30
N_MOD = 6
PROMPT_SEGMENTS = 2


def _params(*semantics):
    return pltpu.CompilerParams(dimension_semantics=semantics, vmem_limit_bytes=VMEM_LIMIT_BYTES)


def _sigmoid(x):
    return 1.0 / (1.0 + jnp.exp(-x))


def _silu(x):
    return x * _sigmoid(x)


def _rms(x, g):
    return x * lax.rsqrt(jnp.mean(x * x, axis=-1, keepdims=True) + EPS) * g


def _mm(a, b):
    return jnp.dot(a, b, preferred_element_type=F32)


def _mm_nt(a, b):
    return lax.dot_general(a, b, (((1,), (1,)), ((), ())), preferred_element_type=F32)


def _mod_kernel(c_ref, w_ref, b_ref, o_ref):
    a = _silu(c_ref[...]).astype(MXU_DTYPE)
    o_ref[...] = _mm(a, w_ref[...].astype(MXU_DTYPE)) + b_ref[...]


def _modulation(c, w_mod, b_mod):
    depth, d, n = w_mod.shape
    rows = c.shape[0]
    tn = 1536
    return pl.pallas_call(
        _mod_kernel,
        grid=(depth, n // tn),
        in_specs=[
            pl.BlockSpec((rows, d), lambda l, j: (0, 0)),
            pl.BlockSpec((None, d, tn), lambda l, j: (l, 0, j)),
            pl.BlockSpec((None, 1, tn), lambda l, j: (l, 0, j)),
        ],
        out_specs=pl.BlockSpec((None, rows, tn), lambda l, j: (l, 0, j)),
        out_shape=jax.ShapeDtypeStruct((depth, rows, n), F32),
        compiler_params=_params("parallel", "parallel"),
        name="modulation",
    )(c, w_mod, b_mod.reshape(depth, 1, n))


def _mod_spec(mod, chunk, tm, tiles_per_batch):
    if mod.ndim == 3:
        return pl.BlockSpec((None, 1, D_MODEL), lambda i: (i // tiles_per_batch, 0, chunk))
    return pl.BlockSpec((tm, D_MODEL), lambda i: (i, chunk))


def _inproj_kernel(x_ref, sh_ref, sc_ref, g_ref, w_ref, cos_ref, sin_ref,
                   q_ref, kv_ref, u_ref, vc_ref):
    tm = x_ref.shape[0]
    h = _rms(x_ref[...], g_ref[...]) * (1.0 + sc_ref[...]) + sh_ref[...]
    z = _mm(h.astype(MXU_DTYPE), w_ref[...])
    cos = cos_ref[...]
    sin = sin_ref[...]
    lane = lax.broadcasted_iota(jnp.int32, (tm, LANES), 1)
    first_half = (lane % HEAD_DIM) < (HEAD_DIM // 2)

    def rope(t):
        partner = jnp.where(first_half, pltpu.roll(t, LANES - HEAD_DIM // 2, 1),
                            pltpu.roll(t, HEAD_DIM // 2, 1))
        return t * cos + partner * sin

    for j in range(ATTN_WIDTH // LANES):
        sl = slice(j * LANES, (j + 1) * LANES)
        q_ref[:, sl] = (rope(z[:, sl]) * ATTN_SCALE).astype(q_ref.dtype)
    o = ATTN_WIDTH
    kv_ref[:, 0:KV_WIDTH] = rope(z[:, o:o + KV_WIDTH])
    kv_ref[:, KV_WIDTH:2 * KV_WIDTH] = z[:, o + KV_WIDTH:o + 2 * KV_WIDTH]
    o += 2 * KV_WIDTH
    u_ref[...] = z[:, o:o + SSM_WIDTH]
    o += SSM_WIDTH
    za = z[:, o:o + CONV_WIDTH]
    zg = z[:, o + CONV_WIDTH:o + 2 * CONV_WIDTH]
    vc_ref[...] = za * _sigmoid(zg)


def _in_projection(x, mod, norm_g, w_in, cos, sin, tm, tiles_per_batch):
    t = x.shape[0]
    pos_tiles = cos.shape[0] // tm
    row = lambda i: (i, 0)
    fixed = lambda i: (0, 0)
    return pl.pallas_call(
        _inproj_kernel,
        grid=(t // tm,),
        in_specs=[
            pl.BlockSpec((tm, D_MODEL), row),
            _mod_spec(mod, 0, tm, tiles_per_batch),
            _mod_spec(mod, 1, tm, tiles_per_batch),
            pl.BlockSpec((1, D_MODEL), fixed),
            pl.BlockSpec((D_MODEL, IN_WIDTH), fixed),
            pl.BlockSpec((tm, LANES), lambda i: (i % pos_tiles, 0)),
            pl.BlockSpec((tm, LANES), lambda i: (i % pos_tiles, 0)),
        ],
        out_specs=[
            pl.BlockSpec((tm, ATTN_WIDTH), row),
            pl.BlockSpec((tm, 2 * KV_WIDTH), row),
            pl.BlockSpec((tm, SSM_WIDTH), row),
            pl.BlockSpec((tm, CONV_WIDTH), row),
        ],
        out_shape=[
            jax.ShapeDtypeStruct((t, ATTN_WIDTH), MXU_DTYPE),
            jax.ShapeDtypeStruct((t, 2 * KV_WIDTH), F32),
            jax.ShapeDtypeStruct((t, SSM_WIDTH), F32),
            jax.ShapeDtypeStruct((t, CONV_WIDTH), F32),
        ],
        compiler_params=_params("parallel"),
        name="in_projection",
    )(x, mod, mod, norm_g.reshape(1, D_MODEL), w_in, cos, sin)


def _rope_tables(pos):
    half = HEAD_DIM // 2
    inv_freq = ROPE_THETA ** (-jnp.arange(half, dtype=F32) / half)
    ang = pos.astype(F32)[:, None] * inv_freq[None, :]
    cos = jnp.tile(jnp.cos(ang), (1, LANES // half))
    sin = jnp.sin(ang)
    sin = jnp.tile(jnp.concatenate([-sin, sin], axis=1), (1, LANES // HEAD_DIM))
    return cos, sin


def _sink_softmax(s, sink_col):
    m = jnp.maximum(jnp.max(s, axis=-1, keepdims=True), sink_col)
    e = jnp.exp(s - m)
    return e / (jnp.sum(e, axis=-1, keepdims=True) + jnp.exp(sink_col - m))


def _head_pair_select(x, pick_second):
    lane = lax.broadcasted_iota(jnp.int32, x.shape, 1)
    swapped = pltpu.roll(x, HEAD_DIM, 1)
    low = lane < HEAD_DIM
    return jnp.where(low, swapped, x) if pick_second else jnp.where(low, x, swapped)


def _attn_prompt_kernel(sink_ref, q_ref, kvp_ref, kvc_ref, o_ref):
    n = pl.program_id(1)
    w = WINDOW
    gw = Q_PER_KV * HEAD_DIM
    rows = Q_PER_KV * w
    qi = lax.broadcasted_iota(jnp.int32, (rows, 2 * w), 0) % w
    kj = lax.broadcasted_iota(jnp.int32, (rows, 2 * w), 1)
    rel = w + qi - kj
    first_key = jnp.where(n > 0, 0, w)
    mask = (rel >= 0) & (rel < w) & (kj >= first_key)
    row_head = lax.broadcasted_iota(jnp.int32, (rows, 1), 0) // w
    lane_head = lax.broadcasted_iota(jnp.int32, (w, gw), 1) // HEAD_DIM
    head_mask = [jnp.where(lane_head == g, 1.0, 0.0) for g in range(Q_PER_KV)]
    kk = jnp.concatenate([kvp_ref[:, 0:KV_WIDTH], kvc_ref[:, 0:KV_WIDTH]], axis=0)
    vv = jnp.concatenate([kvp_ref[:, KV_WIDTH:2 * KV_WIDTH], kvc_ref[:, KV_WIDTH:2 * KV_WIDTH]], axis=0)
    for kvh in range(N_KV_HEADS):
        k2 = _head_pair_select(kk, kvh == 1).astype(MXU_DTYPE)
        v2 = _head_pair_select(vv, kvh == 1).astype(MXU_DTYPE)
        krep = jnp.concatenate([k2] * (Q_PER_KV // 2), axis=1)
        vrep = jnp.concatenate([v2] * (Q_PER_KV // 2), axis=1)
        q4 = q_ref[:, kvh * gw:(kvh + 1) * gw]
        qs = jnp.concatenate([q4 * head_mask[g].astype(q4.dtype) for g in range(Q_PER_KV)], axis=0)
        s = _mm_nt(qs, krep)
        s = jnp.where(mask, s, NEG)
        sink_col = jnp.zeros((rows, 1), F32)
        for g in range(Q_PER_KV):
            sink_col = jnp.where(row_head == g, sink_ref[kvh * Q_PER_KV + g], sink_col)
        p = _sink_softmax(s, sink_col)
        r = _mm(p.astype(MXU_DTYPE), vrep)
        o4 = jnp.zeros((w, gw), F32)
        for g in range(Q_PER_KV):
            o4 = o4 + r[g * w:(g + 1) * w] * head_mask[g]
        o_ref[:, kvh * gw:(kvh + 1) * gw] = o4.astype(o_ref.dtype)


def _attention_prompt(q, kv, sinks, batch, seq):
    nb = seq // WINDOW
    return pl.pallas_call(
        _attn_prompt_kernel,
        grid=(batch, nb),
        in_specs=[
            pl.BlockSpec(memory_space=pltpu.SMEM),
            pl.BlockSpec((WINDOW, ATTN_WIDTH), lambda b, n: (b * nb + n, 0)),
            pl.BlockSpec((WINDOW, 2 * KV_WIDTH), lambda b, n: (b * nb + jnp.maximum(n - 1, 0), 0)),
            pl.BlockSpec((WINDOW, 2 * KV_WIDTH), lambda b, n: (b * nb + n, 0)),
        ],
        out_specs=pl.BlockSpec((WINDOW, ATTN_WIDTH), lambda b, n: (b * nb + n, 0)),
        out_shape=jax.ShapeDtypeStruct((batch * seq, ATTN_WIDTH), MXU_DTYPE),
        compiler_params=_params("parallel", "parallel"),
        name="attention_prompt",
    )(sinks, q, kv, kv)


def _attn_sample_kernel(sink_ref, q_ref, kc_ref, kn_ref, vc_ref, vn_ref, o_ref, kbuf, vbuf, *, n_new):
    bt = q_ref.shape[0]
    rows = q_ref.shape[1]
    wb = kc_ref.shape[1]
    nkeys = kbuf.shape[0]
    r = lax.broadcasted_iota(jnp.int32, (rows, nkeys), 0)
    j = lax.broadcasted_iota(jnp.int32, (rows, nkeys), 1)
    rel = wb + (r % n_new) - j
    mask = (rel >= 0) & (rel < WINDOW) & (j < wb + n_new)
    rh = lax.broadcasted_iota(jnp.int32, (rows, 1), 0) // n_new
    sink_col = jnp.zeros((rows, 1), F32)
    for h in range(N_Q_HEADS):
        sink_col = jnp.where(rh == h, sink_ref[h], sink_col)
    second_group = lax.broadcasted_iota(jnp.int32, (rows, LANES), 0) >= (rows // N_KV_HEADS)
    pad = nkeys - wb - SUBLANES
    kbuf[wb + SUBLANES:nkeys, :] = jnp.zeros((pad, LANES), F32)
    vbuf[wb + SUBLANES:nkeys, :] = jnp.zeros((pad, LANES), F32)

    def body(b, carry):
        kbuf[0:wb, :] = kc_ref[b]
        kbuf[wb:wb + SUBLANES, :] = kn_ref[b]
        vbuf[0:wb, :] = vc_ref[b]
        vbuf[wb:wb + SUBLANES, :] = vn_ref[b]
        s = _mm_nt(q_ref[b], kbuf[...].astype(MXU_DTYPE))
        s = jnp.where(mask, s, NEG)
        p = _sink_softmax(s, sink_col)
        o = _mm(p.astype(MXU_DTYPE), vbuf[...].astype(MXU_DTYPE))
        o_ref[b] = jnp.where(second_group, pltpu.roll(o, HEAD_DIM, 1), o).astype(o_ref.dtype)
        return carry

    lax.fori_loop(0, bt, body, 0)


def _attention_sample(qexp, k_cache, k_new, v_cache, v_new, sinks, n_new):
    batch, rows, _ = qexp.shape
    wb = k_cache.shape[1]
    bt = 8
    nkeys = 2 * WINDOW
    blk = lambda shape: pl.BlockSpec((bt,) + shape, lambda i: (i, 0, 0))
    return pl.pallas_call(
        functools.partial(_attn_sample_kernel, n_new=n_new),
        grid=(batch // bt,),
        in_specs=[
            pl.BlockSpec(memory_space=pltpu.SMEM),
            blk((rows, LANES)),
            blk((wb, LANES)), blk((SUBLANES, LANES)),
            blk((wb, LANES)), blk((SUBLANES, LANES)),
        ],
        out_specs=blk((rows, LANES)),
        out_shape=jax.ShapeDtypeStruct((batch, rows, LANES), MXU_DTYPE),
        scratch_shapes=[pltpu.VMEM((nkeys, LANES), F32), pltpu.VMEM((nkeys, LANES), F32)],
        compiler_params=_params("parallel"),
        name="attention_sample",
    )(sinks, qexp, k_cache, k_new, v_cache, v_new)


def _s5_discretise_kernel(lr_ref, li_ref, ldt_ref, bre_ref, bim_ref, are_ref, aim_ref, bbre_ref, bbim_ref):
    lr = lr_ref[...]
    li = li_ref[...]
    dt = jnp.exp(ldt_ref[...])
    mag = jnp.exp(lr * dt)
    ab_re = mag * jnp.cos(li * dt)
    ab_im = mag * jnp.sin(li * dt)
    den = lr * lr + li * li
    nr = ab_re - 1.0
    coef_re = (nr * lr + ab_im * li) / den
    coef_im = (ab_im * lr - nr * li) / den
    are_ref[...] = ab_re
    aim_ref[...] = ab_im
    br = bre_ref[...]
    bi = bim_ref[...]
    bbre_ref[...] = coef_re * br - coef_im * bi
    bbim_ref[...] = coef_re * bi + coef_im * br


def _s5_discretise(lam_re, lam_im, log_dt, b_re, b_im):
    depth = lam_re.shape[0]
    flat = lambda a: a.reshape(depth, 1, SSM_FLAT)
    ldt = jnp.broadcast_to(log_dt[:, :, None], lam_re.shape)
    bt = lambda a: jnp.transpose(a, (0, 3, 1, 2)).reshape(depth, SSM_GROUP, SSM_FLAT)
    vec = pl.BlockSpec((None, 1, SSM_FLAT), lambda l: (l, 0, 0))
    mat = pl.BlockSpec((None, SSM_GROUP, SSM_FLAT), lambda l: (l, 0, 0))
    return pl.pallas_call(
        _s5_discretise_kernel,
        grid=(depth,),
        in_specs=[vec, vec, vec, mat, mat],
        out_specs=[vec, vec, mat, mat],
        out_shape=[jax.ShapeDtypeStruct((depth, 1, SSM_FLAT), F32)] * 2
        + [jax.ShapeDtypeStruct((depth, SSM_GROUP, SSM_FLAT), F32)] * 2,
        compiler_params=_params("parallel"),
        name="s5_discretise",
    )(flat(lam_re), flat(lam_im), flat(ldt), bt(b_re), bt(b_im))


def _group_block_diag_in(bb):
    eye = jnp.eye(N_SSM_GROUPS, dtype=bb.dtype)
    b3 = bb.reshape(SSM_GROUP, N_SSM_GROUPS, SSM_STATE)
    full = eye[:, None, :, None] * jnp.transpose(b3, (1, 0, 2))[:, :, None, :]
    return full.reshape(SSM_WIDTH, SSM_FLAT)


def _group_block_diag_out(c):
    eye = jnp.eye(N_SSM_GROUPS, dtype=c.dtype)
    full = eye[:, None, :, None] * jnp.transpose(c, (0, 2, 1))[:, :, None, :]
    return full.reshape(SSM_FLAT, SSM_WIDTH)


def _s5_kernel(u_ref, hre0_ref, him0_ref, are_ref, aim_ref, bre_ref, bim_ref, *rest, nseq, emit):
    if emit:
        (cre_ref, cim_ref, d_ref, wglu_ref, bglu_ref,
         o_ref, hre_out, him_out, hre_s, him_s, bure_s, buim_s, hsre_s, hsim_s) = rest
    else:
        hre_out, him_out, hre_s, him_s, bure_s, buim_s = rest
    steps = u_ref.shape[0] // nseq

    @pl.when(pl.program_id(0) == 0)
    def _():
        hre_s[...] = hre0_ref[...]
        him_s[...] = him0_ref[...]

    u = u_ref[...]
    ub = u.astype(MXU_DTYPE)
    bure_s[...] = _mm(ub, bre_ref[...])
    buim_s[...] = _mm(ub, bim_ref[...])
    ar = jnp.broadcast_to(are_ref[...], (nseq, SSM_FLAT))
    ai = jnp.broadcast_to(aim_ref[...], (nseq, SSM_FLAT))

    def step(t, carry):
        hr, hi = carry
        rows = pl.ds(pl.multiple_of(t * nseq, nseq), nseq)
        nhr = ar * hr - ai * hi + bure_s[rows, :]
        nhi = ar * hi + ai * hr + buim_s[rows, :]
        if emit:
            hsre_s[rows, :] = nhr
            hsim_s[rows, :] = nhi
        return nhr, nhi

    hr, hi = lax.fori_loop(0, steps, step, (hre_s[...], him_s[...]), unroll=min(steps, 8))
    hre_s[...] = hr
    him_s[...] = hi
    hre_out[...] = hr
    him_out[...] = hi
    if emit:
        y = _mm(hsre_s[...].astype(MXU_DTYPE), cre_ref[...]) - _mm(hsim_s[...].astype(MXU_DTYPE), cim_ref[...])
        z = jax.nn.gelu(y + d_ref[...] * u)
        gate = _mm(z.astype(MXU_DTYPE), wglu_ref[...]) + bglu_ref[...]
        o_ref[...] = (z * _sigmoid(gate)).astype(o_ref.dtype)


def _s5_scan(u_tm, hre0, him0, are, aim, bre_blk, bim_blk, out_params, nseq, steps_per_tile):
    rows = u_tm.shape[0]
    tr = steps_per_tile * nseq
    emit = out_params is not None
    fixed = lambda i: (0, 0)
    state = pl.BlockSpec((nseq, SSM_FLAT), fixed)
    vec = pl.BlockSpec((1, SSM_FLAT), fixed)
    in_specs = [pl.BlockSpec((tr, SSM_WIDTH), lambda i: (i, 0)), state, state, vec, vec,
                pl.BlockSpec((SSM_WIDTH, SSM_FLAT), fixed), pl.BlockSpec((SSM_WIDTH, SSM_FLAT), fixed)]
    args = [u_tm, hre0, him0, are, aim, bre_blk, bim_blk]
    out_specs = [state, state]
    out_shape = [jax.ShapeDtypeStruct((nseq, SSM_FLAT), F32)] * 2
    scratch = [pltpu.VMEM((nseq, SSM_FLAT), F32)] * 2 + [pltpu.VMEM((tr, SSM_FLAT), F32)] * 2
    if emit:
        in_specs += [pl.BlockSpec((SSM_FLAT, SSM_WIDTH), fixed), pl.BlockSpec((SSM_FLAT, SSM_WIDTH), fixed),
                     pl.BlockSpec((1, SSM_WIDTH), fixed), pl.BlockSpec((SSM_WIDTH, SSM_WIDTH), fixed),
                     pl.BlockSpec((1, SSM_WIDTH), fixed)]
        args += list(out_params)
        out_specs = [pl.BlockSpec((tr, SSM_WIDTH), lambda i: (i, 0))] + out_specs
        out_shape = [jax.ShapeDtypeStruct((rows, SSM_WIDTH), MXU_DTYPE)] + out_shape
        scratch += [pltpu.VMEM((tr, SSM_FLAT), F32)] * 2
    return pl.pallas_call(
        functools.partial(_s5_kernel, nseq=nseq, emit=emit),
        grid=(rows // tr,),
        in_specs=in_specs,
        out_specs=out_specs,
        out_shape=out_shape,
        scratch_shapes=scratch,
        compiler_params=_params("arbitrary"),
        name="s5_scan" if emit else "s5_segment_states",
    )(*args)


def _layernorm_silu(y, g, b):
    yc = y - jnp.mean(y, axis=-1, keepdims=True)
    var = jnp.mean(yc * yc, axis=-1, keepdims=True)
    return _silu(yc * lax.rsqrt(var + EPS) * g + b)


def _conv_prompt_kernel(prev_ref, cur_ref, w_ref, b_ref, g_ref, beta_ref, o_ref, buf, *, tiles_per_batch, sub):
    tc = cur_ref.shape[0]
    first = (pl.program_id(0) % tiles_per_batch) == 0
    buf[0:CONV_HALO, :] = jnp.where(first, 0.0, prev_ref[...])
    buf[CONV_HALO:CONV_HALO + tc, :] = cur_ref[...]
    base = CONV_HALO - (CONV_K - 1)
    for r in range(tc // sub):
        acc = jnp.zeros((sub, CONV_WIDTH), F32)
        for k in range(CONV_K):
            acc = acc + w_ref[k:k + 1, :] * buf[r * sub + base + k:r * sub + base + k + sub, :]
        y = _layernorm_silu(acc + b_ref[...], g_ref[...], beta_ref[...])
        o_ref[r * sub:(r + 1) * sub, :] = y.astype(o_ref.dtype)


def _conv_prompt(vc, conv_w, conv_b, ln_g, ln_b, seq, tc):
    t = vc.shape[0]
    tiles_per_batch = seq // tc
    halo_per_tile = tc // CONV_HALO
    fixed = lambda i: (0, 0)
    vec = pl.BlockSpec((1, CONV_WIDTH), fixed)
    return pl.pallas_call(
        functools.partial(_conv_prompt_kernel, tiles_per_batch=tiles_per_batch, sub=64),
        grid=(t // tc,),
        in_specs=[
            pl.BlockSpec((CONV_HALO, CONV_WIDTH), lambda i: (jnp.maximum(i * halo_per_tile - 1, 0), 0)),
            pl.BlockSpec((tc, CONV_WIDTH), lambda i: (i, 0)),
            pl.BlockSpec((CONV_K, CONV_WIDTH), fixed), vec, vec, vec,
        ],
        out_specs=pl.BlockSpec((tc, CONV_WIDTH), lambda i: (i, 0)),
        out_shape=jax.ShapeDtypeStruct((t, CONV_WIDTH), MXU_DTYPE),
        scratch_shapes=[pltpu.VMEM((CONV_HALO + tc, CONV_WIDTH), F32)],
        compiler_params=_params("parallel"),
        name="conv_prompt",
    )(vc, vc, conv_w, conv_b.reshape(1, -1), ln_g.reshape(1, -1), ln_b.reshape(1, -1))


def _conv_sample_kernel(state_ref, v_ref, ws_ref, wv_ref, b_ref, g_ref, beta_ref, o_ref):
    n_new = v_ref.shape[1]
    st = state_ref[...]
    v = v_ref[...]
    for t in range(n_new):
        acc = jnp.sum(st * ws_ref[t], axis=1) + jnp.sum(v * wv_ref[t], axis=1)
        o_ref[t] = _layernorm_silu(acc + b_ref[...], g_ref[...], beta_ref[...]).astype(o_ref.dtype)


def _conv_sample(state, v, conv_w, conv_b, ln_g, ln_b):
    batch, ns, width = state.shape
    n_new = v.shape[1]
    ws = jnp.stack([jnp.pad(conv_w[:ns - t], ((t, 0), (0, 0))) for t in range(n_new)])
    wv = jnp.stack([jnp.pad(conv_w[ns - t:], ((0, n_new - 1 - t), (0, 0))) for t in range(n_new)])
    bt = min(32, batch)
    fixed2 = lambda i: (0, 0)
    fixed3 = lambda i: (0, 0, 0)
    vec = pl.BlockSpec((1, width), fixed2)
    return pl.pallas_call(
        _conv_sample_kernel,
        grid=(batch // bt,),
        in_specs=[
            pl.BlockSpec((bt, ns, width), lambda i: (i, 0, 0)),
            pl.BlockSpec((bt, n_new, width), lambda i: (i, 0, 0)),
            pl.BlockSpec((n_new, ns, width), fixed3),
            pl.BlockSpec((n_new, n_new, width), fixed3),
            vec, vec, vec,
        ],
        out_specs=pl.BlockSpec((n_new, bt, width), lambda i: (0, i, 0)),
        out_shape=jax.ShapeDtypeStruct((n_new, batch, width), MXU_DTYPE),
        compiler_params=_params("parallel"),
        name="conv_sample",
    )(state, v, ws, wv, conv_b.reshape(1, -1), ln_g.reshape(1, -1), ln_b.reshape(1, -1))


def _outffn_kernel(x_ref, a_ref, s_ref, c_ref, g1_ref, sh2_ref, sc2_ref, g2_ref, n2_ref,
                   wo_ref, wg_ref, wu_ref, wd_ref, fg_ref, o_ref, *, final_norm, ff_chunk):
    o1 = ATTN_WIDTH
    o2 = o1 + SSM_WIDTH
    proj = (_mm(a_ref[...], wo_ref[0:o1, :]) + _mm(s_ref[...], wo_ref[o1:o2, :])
            + _mm(c_ref[...], wo_ref[o2:o2 + CONV_WIDTH, :]))
    x2 = x_ref[...] + g1_ref[...] * proj
    h2 = (_rms(x2, n2_ref[...]) * (1.0 + sc2_ref[...]) + sh2_ref[...]).astype(MXU_DTYPE)
    ffn = jnp.zeros(x2.shape, F32)
    for c in range(D_FF // ff_chunk):
        cs = slice(c * ff_chunk, (c + 1) * ff_chunk)
        act = _silu(_mm(h2, wg_ref[:, cs])) * _mm(h2, wu_ref[:, cs])
        ffn = ffn + _mm(act.astype(MXU_DTYPE), wd_ref[cs, :])
    y = x2 + g2_ref[...] * ffn
    if final_norm:
        y = _rms(y, fg_ref[...])
    o_ref[...] = y


def _out_ffn(x, o_attn, o_ssm, o_conv, mod, norm_g, w_out, w_gate, w_up, w_down, final_g, final_norm,
             tm, tiles_per_batch):
    t = x.shape[0]
    row = lambda i: (i, 0)
    fixed = lambda i: (0, 0)
    resident = lambda shape: pl.BlockSpec(shape, fixed, pipeline_mode=pl.Buffered(1))
    return pl.pallas_call(
        functools.partial(_outffn_kernel, final_norm=final_norm, ff_chunk=D_FF // 2),
        grid=(t // tm,),
        in_specs=[
            pl.BlockSpec((tm, D_MODEL), row),
            pl.BlockSpec((tm, ATTN_WIDTH), row),
            pl.BlockSpec((tm, SSM_WIDTH), row),
            pl.BlockSpec((tm, CONV_WIDTH), row),
            _mod_spec(mod, 2, tm, tiles_per_batch),
            _mod_spec(mod, 3, tm, tiles_per_batch),
            _mod_spec(mod, 4, tm, tiles_per_batch),
            _mod_spec(mod, 5, tm, tiles_per_batch),
            pl.BlockSpec((1, D_MODEL), fixed),
            resident((D_MODEL, D_MODEL)),
            resident((D_MODEL, D_FF)),
            resident((D_MODEL, D_FF)),
            resident((D_FF, D_MODEL)),
            pl.BlockSpec((1, D_MODEL), fixed),
        ],
        out_specs=pl.BlockSpec((tm, D_MODEL), row),
        out_shape=jax.ShapeDtypeStruct((t, D_MODEL), F32),
        compiler_params=_params("parallel"),
        name="out_projection_ffn",
    )(x, o_attn, o_ssm, o_conv, mod, mod, mod, mod, norm_g.reshape(1, D_MODEL),
      w_out, w_gate, w_up, w_down, final_g.reshape(1, D_MODEL))


def _layer_prompt(x, mod, w, rope, batch, seq, final_norm):
    tm = min(512, seq)
    tiles_per_batch = seq // tm
    q, kv, u, vc = _in_projection(x, mod, w['norm1_g'], w['w_in'], rope[0], rope[1], tm, tiles_per_batch)
    o_attn = _attention_prompt(q, kv, w['sinks'], batch, seq)
    kv4 = kv.reshape(batch, seq, 2, N_KV_HEADS, HEAD_DIM)
    n_keep = min(WINDOW, seq)
    new_k = kv4[:, seq - n_keep:, 0]
    new_v = kv4[:, seq - n_keep:, 1]

    nseg = PROMPT_SEGMENTS
    lseg = seq // nseg
    nseq = batch * nseg
    u_tm = jnp.transpose(u.reshape(nseq, lseg, SSM_WIDTH), (1, 0, 2)).reshape(lseg * nseq, SSM_WIDTH)
    zero = jnp.zeros((nseq, SSM_FLAT), F32)
    steps = min(128, lseg)
    scan_in = (w['a_re'], w['a_im'], w['bre_blk'], w['bim_blk'])
    end_re, end_im = _s5_scan(u_tm, zero, zero, *scan_in, None, nseq, steps)

    def shift(e):
        e = e.reshape(batch, nseg, SSM_FLAT)
        return jnp.concatenate([jnp.zeros_like(e[:, :1]), e[:, :-1]], axis=1).reshape(nseq, SSM_FLAT)

    o_tm, h_re, h_im = _s5_scan(u_tm, shift(end_re), shift(end_im), *scan_in, w['s5_out'], nseq, steps)
    o_ssm = jnp.transpose(o_tm.reshape(lseg, nseq, SSM_WIDTH), (1, 0, 2)).reshape(batch * seq, SSM_WIDTH)
    last = lambda h: h.reshape(batch, nseg, N_SSM_GROUPS, SSM_STATE)[:, nseg - 1]

    o_conv = _conv_prompt(vc, w['conv_w'], w['conv_b'], w['conv_ln_g'], w['conv_ln_b'], seq, min(512, seq))
    new_conv = vc.reshape(batch, seq, CONV_WIDTH)[:, seq - (CONV_K - 1):]

    x = _out_ffn(x, o_attn, o_ssm, o_conv, mod, w['norm2_g'], w['w_out'], w['w_gate'], w['w_up'], w['w_down'],
                 w['final_g'], final_norm, tm, tiles_per_batch)
    return x, new_k, new_v, last(h_re), last(h_im), new_conv


def _layer_sample(x, mod, w, rope, batch, n_new, k_buf, v_buf, h0_re, h0_im, conv_buf, final_norm):
    t = batch * n_new
    q, kv, u, vc = _in_projection(x, mod, w['norm1_g'], w['w_in'], rope[0], rope[1], t, 1)

    wb = k_buf.shape[1]
    q4 = jnp.transpose(q.reshape(batch, n_new, N_Q_HEADS, HEAD_DIM), (0, 2, 1, 3))
    q4 = q4.reshape(batch, N_KV_HEADS, Q_PER_KV * n_new, HEAD_DIM)
    zq = jnp.zeros_like(q4[:, 0])
    qexp = jnp.concatenate([jnp.concatenate([q4[:, 0], zq], axis=-1),
                            jnp.concatenate([zq, q4[:, 1]], axis=-1)], axis=1)
    kv3 = kv.reshape(batch, n_new, 2 * KV_WIDTH)
    padn = ((0, 0), (0, SUBLANES - n_new), (0, 0))
    k_new, v_new = kv3[:, :, :KV_WIDTH], kv3[:, :, KV_WIDTH:]
    kc = k_buf.reshape(batch, wb, KV_WIDTH)
    vcache = v_buf.reshape(batch, wb, KV_WIDTH)
    o = _attention_sample(qexp, kc, jnp.pad(k_new, padn), vcache, jnp.pad(v_new, padn), w['sinks'], n_new)
    o = o[:, :, :HEAD_DIM].reshape(batch, N_Q_HEADS, n_new, HEAD_DIM)
    o_attn = jnp.transpose(o, (0, 2, 1, 3)).reshape(t, ATTN_WIDTH)
    new_k = jnp.concatenate([kc, k_new], axis=1)[:, n_new:].reshape(batch, wb, N_KV_HEADS, HEAD_DIM)
    new_v = jnp.concatenate([vcache, v_new], axis=1)[:, n_new:].reshape(batch, wb, N_KV_HEADS, HEAD_DIM)

    u_tm = jnp.transpose(u.reshape(batch, n_new, SSM_WIDTH), (1, 0, 2)).reshape(t, SSM_WIDTH)
    o_tm, h_re, h_im = _s5_scan(u_tm, h0_re.reshape(batch, SSM_FLAT), h0_im.reshape(batch, SSM_FLAT),
                                w['a_re'], w['a_im'], w['bre_blk'], w['bim_blk'], w['s5_out'], batch, n_new)
    o_ssm = jnp.transpose(o_tm.reshape(n_new, batch, SSM_WIDTH), (1, 0, 2)).reshape(t, SSM_WIDTH)
    st = lambda h: h.reshape(batch, N_SSM_GROUPS, SSM_STATE)

    vc3 = vc.reshape(batch, n_new, CONV_WIDTH)
    o_conv = _conv_sample(conv_buf, vc3, w['conv_w'], w['conv_b'], w['conv_ln_g'], w['conv_ln_b'])
    o_conv = jnp.transpose(o_conv, (1, 0, 2)).reshape(t, CONV_WIDTH)
    new_conv = jnp.concatenate([conv_buf, vc3], axis=1)[:, n_new:]

    x = _out_ffn(x, o_attn, o_ssm, o_conv, mod, w['norm2_g'], w['w_out'], w['w_gate'], w['w_up'], w['w_down'],
                 w['final_g'], final_norm, t, 1)
    return x, new_k, new_v, st(h_re), st(h_im), new_conv


def kernel(x_prompt, x_sample, c_prompt, c_sample, cache_k, cache_v, state_ssm_re, state_ssm_im, state_conv,
           norm1_g, norm2_g, w_mod, b_mod, w_in, attn_sinks, ssm_lam_re, ssm_lam_im, ssm_log_dt,
           ssm_b_re, ssm_b_im, ssm_c_re, ssm_c_im, ssm_d, ssm_w_glu, ssm_b_glu,
           conv_w, conv_b, conv_ln_g, conv_ln_b, w_out, w_gate, w_up, w_down, final_norm_g):
    bp, seq, d = x_prompt.shape
    bs, n_new, _ = x_sample.shape
    depth = w_in.shape[0]
    past_len = 8192

    c_all = jnp.concatenate([c_prompt, c_sample], axis=0)
    pad_rows = -c_all.shape[0] % SUBLANES
    mods = _modulation(jnp.pad(c_all, ((0, pad_rows), (0, 0))), w_mod, b_mod)
    mod_p = mods[:, :bp].reshape(depth, bp, 1, N_MOD * d)
    mod_s = jnp.repeat(mods[:, bp:bp + bs], n_new, axis=1)

    a_re, a_im, bb_re, bb_im = _s5_discretise(ssm_lam_re, ssm_lam_im, ssm_log_dt, ssm_b_re, ssm_b_im)
    rope_p = _rope_tables(jnp.arange(seq))
    rope_s = _rope_tables(jnp.tile(past_len + jnp.arange(n_new), bs))

    cast = lambda a: a.astype(MXU_DTYPE)
    xp = x_prompt.reshape(bp * seq, d)
    xs = x_sample.reshape(bs * n_new, d)
    outs_p, outs_s = [], []
    for l in range(depth):
        w = {
            'norm1_g': norm1_g[l], 'norm2_g': norm2_g[l], 'w_in': cast(w_in[l]), 'sinks': attn_sinks[l],
            'a_re': a_re[l], 'a_im': a_im[l],
            'bre_blk': cast(_group_block_diag_in(bb_re[l])), 'bim_blk': cast(_group_block_diag_in(bb_im[l])),
            's5_out': (cast(_group_block_diag_out(ssm_c_re[l])), cast(_group_block_diag_out(ssm_c_im[l])),
                       ssm_d[l].reshape(1, SSM_WIDTH), cast(ssm_w_glu[l]), ssm_b_glu[l].reshape(1, SSM_WIDTH)),
            'conv_w': conv_w[l], 'conv_b': conv_b[l], 'conv_ln_g': conv_ln_g[l], 'conv_ln_b': conv_ln_b[l],
            'w_out': cast(w_out[l]), 'w_gate': cast(w_gate[l]), 'w_up': cast(w_up[l]), 'w_down': cast(w_down[l]),
            'final_g': final_norm_g,
        }
        final = l == depth - 1
        xp, *op = _layer_prompt(xp, mod_p[l], w, rope_p, bp, seq, final)
        xs, *os_ = _layer_sample(xs, mod_s[l], w, rope_s, bs, n_new, cache_k[l], cache_v[l],
                                 state_ssm_re[l], state_ssm_im[l], state_conv[l], final)
        outs_p.append(op)
        outs_s.append(os_)
    stack = lambda outs, i: jnp.stack([o[i] for o in outs])
    return (xp.reshape(bp, seq, d), xs.reshape(bs, n_new, d),
            *[stack(outs_p, i) for i in range(5)], *[stack(outs_s, i) for i in range(5)])
```

```python
import functools
import math

import jax
import jax.numpy as jnp
from jax import lax
from jax.experimental import pallas as pl
from jax.experimental.pallas import tpu as pltpu

F32 = jnp.float32
MXU_DTYPE = jnp.bfloat16

V7X_VMEM_BYTES = 64 * 1024 * 1024
VMEM_LIMIT_BYTES = V7X_VMEM_BYTES - 8 * 1024 * 1024
LANES = 128
SUBLANES = 8

D_MODEL = 1024
HEAD_DIM = 64
N_Q_HEADS = 8
N_KV_HEADS = 2
Q_PER_KV = N_Q_HEADS // N_KV_HEADS
ATTN_WIDTH = N_Q_HEADS * HEAD_DIM
KV_WIDTH = N_KV_HEADS * HEAD_DIM
WINDOW = 128
ROPE_THETA = 10000.0
ATTN_SCALE = 1.0 / math.sqrt(HEAD_DIM)
SSM_WIDTH = 256
SSM_GROUP = 16
N_SSM_GROUPS = 16
SSM_STATE = 64
SSM_FLAT = N_SSM_GROUPS * SSM_STATE
CONV_WIDTH = 256
CONV_K = 31
CONV_HALO = 32
IN_WIDTH = ATTN_WIDTH + 2 * KV_WIDTH + SSM_WIDTH + 2 * CONV_WIDTH
D_FF = 2816
EPS = 1e-6
NEG = -1e30
N_MOD = 6
PROMPT_SEGMENTS = 2
PAST_LEN = 8192


def _params(*semantics):
    return pltpu.CompilerParams(dimension_semantics=semantics, vmem_limit_bytes=VMEM_LIMIT_BYTES)


def _layer_block(shape, layer, **kw):
    zeros = (0,) * len(shape)
    return pl.BlockSpec((None,) + tuple(shape), lambda *_: (layer,) + zeros, **kw)


def _sigmoid(x):
    return 1.0 / (1.0 + jnp.exp(-x))


def _silu(x):
    return x * _sigmoid(x)


def _rms(x, g):
    return x * lax.rsqrt(jnp.mean(x * x, axis=-1, keepdims=True) + EPS) * g


def _mm(a, b):
    return jnp.dot(a, b, preferred_element_type=F32)


def _mm_nt(a, b):
    return lax.dot_general(a, b, (((1,), (1,)), ((), ())), preferred_element_type=F32)


def _rows(m, n):
    return m if m.shape[0] == 1 else jnp.concatenate([m] * (n // m.shape[0]), axis=0)


def _mod_kernel(c_ref, w_ref, b_ref, o_ref):
    a = _silu(c_ref[...]).astype(MXU_DTYPE)
    o_ref[...] = _mm(a, w_ref[...].astype(MXU_DTYPE)) + b_ref[...]


def _modulation(c, w_mod, b_mod):
    depth, d, n = w_mod.shape
    rows = c.shape[0]
    tn = 1536
    return pl.pallas_call(
        _mod_kernel,
        grid=(depth, n // tn),
        in_specs=[
            pl.BlockSpec((rows, d), lambda l, j: (0, 0)),
            pl.BlockSpec((None, d, tn), lambda l, j: (l, 0, j)),
            pl.BlockSpec((None, 1, tn), lambda l, j: (l, 0, j)),
        ],
        out_specs=pl.BlockSpec((None, rows, tn), lambda l, j: (l, 0, j)),
        out_shape=jax.ShapeDtypeStruct((depth, rows, n), F32),
        compiler_params=_params("parallel", "parallel"),
        name="modulation",
    )(c, w_mod, b_mod.reshape(depth, 1, n))


def _mod_spec(mod, layer, chunk, tiles_per_batch):
    if mod.ndim == 4:
        return pl.BlockSpec((None, None, 1, D_MODEL), lambda i: (layer, i // tiles_per_batch, 0, chunk))
    return pl.BlockSpec((None, mod.shape[1], D_MODEL), lambda i: (layer, 0, chunk))


def _inproj_kernel(x_ref, sh_ref, sc_ref, g_ref, w_ref, cos_ref, sin_ref,
                   q_ref, kv_ref, u_ref, vc_ref):
    tm = x_ref.shape[0]
    h = _rms(x_ref[...], g_ref[...]) * (1.0 + _rows(sc_ref[...], tm)) + _rows(sh_ref[...], tm)
    z = _mm(h.astype(MXU_DTYPE), w_ref[...])
    cos = cos_ref[...]
    sin = sin_ref[...]
    lane = lax.broadcasted_iota(jnp.int32, (tm, LANES), 1)
    first_half = (lane % HEAD_DIM) < (HEAD_DIM // 2)

    def rope(t):
        partner = jnp.where(first_half, pltpu.roll(t, LANES - HEAD_DIM // 2, 1),
                            pltpu.roll(t, HEAD_DIM // 2, 1))
        return t * cos + partner * sin

    for j in range(ATTN_WIDTH // LANES):
        sl = slice(j * LANES, (j + 1) * LANES)
        q_ref[:, sl] = (rope(z[:, sl]) * ATTN_SCALE).astype(q_ref.dtype)
    o = ATTN_WIDTH
    kv_ref[:, 0:KV_WIDTH] = rope(z[:, o:o + KV_WIDTH])
    kv_ref[:, KV_WIDTH:2 * KV_WIDTH] = z[:, o + KV_WIDTH:o + 2 * KV_WIDTH]
    o += 2 * KV_WIDTH
    u_ref[...] = z[:, o:o + SSM_WIDTH]
    o += SSM_WIDTH
    za = z[:, o:o + CONV_WIDTH]
    zg = z[:, o + CONV_WIDTH:o + 2 * CONV_WIDTH]
    vc_ref[...] = za * _sigmoid(zg)


def _in_projection(x, mod, p, layer, cos, sin, tm, tiles_per_batch):
    t = x.shape[0]
    pos_tiles = cos.shape[0] // tm
    row = lambda i: (i, 0)
    return pl.pallas_call(
        _inproj_kernel,
        grid=(t // tm,),
        in_specs=[
            pl.BlockSpec((tm, D_MODEL), row),
            _mod_spec(mod, layer, 0, tiles_per_batch),
            _mod_spec(mod, layer, 1, tiles_per_batch),
            _layer_block((1, D_MODEL), layer),
            _layer_block((D_MODEL, IN_WIDTH), layer),
            pl.BlockSpec((tm, LANES), lambda i: (i % pos_tiles, 0)),
            pl.BlockSpec((tm, LANES), lambda i: (i % pos_tiles, 0)),
        ],
        out_specs=[
            pl.BlockSpec((tm, ATTN_WIDTH), row),
            pl.BlockSpec((tm, 2 * KV_WIDTH), row),
            pl.BlockSpec((tm, SSM_WIDTH), row),
            pl.BlockSpec((tm, CONV_WIDTH), row),
        ],
        out_shape=[
            jax.ShapeDtypeStruct((t, ATTN_WIDTH), MXU_DTYPE),
            jax.ShapeDtypeStruct((t, 2 * KV_WIDTH), F32),
            jax.ShapeDtypeStruct((t, SSM_WIDTH), F32),
            jax.ShapeDtypeStruct((t, CONV_WIDTH), F32),
        ],
        compiler_params=_params("parallel"),
        name="in_projection",
    )(x, mod, mod, p['norm1_g'], p['w_in'], cos, sin)


def _rope_tables(pos):
    half = HEAD_DIM // 2
    inv_freq = ROPE_THETA ** (-jnp.arange(half, dtype=F32) / half)
    ang = pos.astype(F32)[:, None] * inv_freq[None, :]
    cos = jnp.tile(jnp.cos(ang), (1, LANES // half))
    sin = jnp.sin(ang)
    sin = jnp.tile(jnp.concatenate([-sin, sin], axis=1), (1, LANES // HEAD_DIM))
    return cos, sin


def _sink_softmax(s, sink_col):
    m = jnp.maximum(jnp.max(s, axis=-1, keepdims=True), sink_col)
    e = jnp.exp(s - m)
    return e * (1.0 / (jnp.sum(e, axis=-1, keepdims=True) + jnp.exp(sink_col - m)))


def _head_pair_select(x, pick_second):
    lane = lax.broadcasted_iota(jnp.int32, x.shape, 1)
    swapped = pltpu.roll(x, HEAD_DIM, 1)
    low = lane < HEAD_DIM
    return jnp.where(low, swapped, x) if pick_second else jnp.where(low, x, swapped)


def _attn_prompt_kernel(sink_ref, q_ref, kvp_ref, kvc_ref, o_ref, *, layer):
    n = pl.program_id(1)
    w = WINDOW
    n_blocks = q_ref.shape[0] // w
    rows = Q_PER_KV * w
    r_idx = lax.broadcasted_iota(jnp.int32, (rows, w), 0) % w
    c_idx = lax.broadcasted_iota(jnp.int32, (rows, w), 1)
    from_prev = c_idx > r_idx
    first_bias = jnp.where(n > 0, 0.0, NEG)
    row_head = lax.broadcasted_iota(jnp.int32, (rows, 1), 0) // w
    low = lax.broadcasted_iota(jnp.int32, (w, LANES), 1) < HEAD_DIM
    half_mask = [jnp.where(low, 1.0, 0.0).astype(MXU_DTYPE), jnp.where(low, 0.0, 1.0).astype(MXU_DTYPE)]
    sink_cols = []
    for kvh in range(N_KV_HEADS):
        sink_col = jnp.zeros((rows, 1), F32)
        for g in range(Q_PER_KV):
            sink_col = jnp.where(row_head == g, sink_ref[layer, kvh * Q_PER_KV + g], sink_col)
        sink_cols.append(sink_col)
    for i in range(n_blocks):
        own = slice(i * w, (i + 1) * w)
        prev_ref, prev = (kvp_ref, slice(0, w)) if i == 0 else (kvc_ref, slice((i - 1) * w, i * w))
        kk = jnp.concatenate([prev_ref[prev, 0:KV_WIDTH], kvc_ref[own, 0:KV_WIDTH]], axis=0)
        vv = jnp.concatenate([prev_ref[prev, KV_WIDTH:2 * KV_WIDTH], kvc_ref[own, KV_WIDTH:2 * KV_WIDTH]], axis=0)
        for kvh in range(N_KV_HEADS):
            k2 = _head_pair_select(kk, kvh == 1).astype(MXU_DTYPE)
            v2 = _head_pair_select(vv, kvh == 1).astype(MXU_DTYPE)
            pieces = []
            for g in range(Q_PER_KV):
                h = kvh * Q_PER_KV + g
                qcol = q_ref[own, (h // 2) * LANES:(h // 2 + 1) * LANES]
                pieces.append(qcol * half_mask[h % 2])
            s = _mm_nt(jnp.concatenate(pieces, axis=0), k2)
            s_prev = s[:, 0:w] + first_bias if i == 0 else s[:, 0:w]
            p = _sink_softmax(jnp.where(from_prev, s_prev, s[:, w:2 * w]), sink_cols[kvh])
            p2 = jnp.concatenate([jnp.where(from_prev, p, 0.0), jnp.where(from_prev, 0.0, p)], axis=1)
            r = _mm(p2.astype(MXU_DTYPE), v2)
            for j in range(Q_PER_KV // 2):
                col = kvh * (Q_PER_KV // 2) + j
                o_ref[own, col * LANES:(col + 1) * LANES] = jnp.where(
                    low, r[2 * j * w:(2 * j + 1) * w], r[(2 * j + 1) * w:(2 * j + 2) * w]).astype(o_ref.dtype)


def _attention_prompt(q, kv, sinks, layer, batch, seq):
    tq = min(4 * WINDOW, seq)
    nt = seq // tq
    per = tq // WINDOW
    return pl.pallas_call(
        functools.partial(_attn_prompt_kernel, layer=layer),
        grid=(batch, nt),
        in_specs=[
            pl.BlockSpec(memory_space=pltpu.SMEM),
            pl.BlockSpec((tq, ATTN_WIDTH), lambda b, n: (b * nt + n, 0)),
            pl.BlockSpec((WINDOW, 2 * KV_WIDTH), lambda b, n: ((b * nt + n) * per - jnp.minimum(n, 1), 0)),
            pl.BlockSpec((tq, 2 * KV_WIDTH), lambda b, n: (b * nt + n, 0)),
        ],
        out_specs=pl.BlockSpec((tq, ATTN_WIDTH), lambda b, n: (b * nt + n, 0)),
        out_shape=jax.ShapeDtypeStruct((batch * seq, ATTN_WIDTH), MXU_DTYPE),
        compiler_params=_params("parallel", "parallel"),
        name="attention_prompt",
    )(sinks, q, kv, kv)


def _attn_sample_kernel(sink_ref, q_ref, kc_ref, kn_ref, vc_ref, vn_ref, o_ref, kbuf, vbuf, *, layer, n_new):
    bt = q_ref.shape[0]
    rows = q_ref.shape[1]
    wb = kc_ref.shape[1]
    nkeys = kbuf.shape[0]
    r = lax.broadcasted_iota(jnp.int32, (rows, nkeys), 0)
    j = lax.broadcasted_iota(jnp.int32, (rows, nkeys), 1)
    rel = wb + (r % n_new) - j
    mask = (rel >= 0) & (rel < WINDOW) & (j < wb + n_new)
    rh = lax.broadcasted_iota(jnp.int32, (rows, 1), 0) // n_new
    sink_col = jnp.zeros((rows, 1), F32)
    for h in range(N_Q_HEADS):
        sink_col = jnp.where(rh == h, sink_ref[layer, h], sink_col)
    second_group = lax.broadcasted_iota(jnp.int32, (rows, LANES), 0) >= (rows // N_KV_HEADS)
    pad = nkeys - wb - SUBLANES
    kbuf[wb + SUBLANES:nkeys, :] = jnp.zeros((pad, LANES), F32)
    vbuf[wb + SUBLANES:nkeys, :] = jnp.zeros((pad, LANES), F32)

    def body(b, carry):
        kbuf[0:wb, :] = kc_ref[b]
        kbuf[wb:wb + SUBLANES, :] = kn_ref[b]
        vbuf[0:wb, :] = vc_ref[b]
        vbuf[wb:wb + SUBLANES, :] = vn_ref[b]
        s = _mm_nt(q_ref[b], kbuf[...].astype(MXU_DTYPE))
        s = jnp.where(mask, s, NEG)
        p = _sink_softmax(s, sink_col)
        o = _mm(p.astype(MXU_DTYPE), vbuf[...].astype(MXU_DTYPE))
        o_ref[b] = jnp.where(second_group, pltpu.roll(o, HEAD_DIM, 1), o).astype(o_ref.dtype)
        return carry

    lax.fori_loop(0, bt, body, 0)


def _attention_sample(qexp, k_cache, k_new, v_cache, v_new, sinks, layer, n_new):
    batch, rows, _ = qexp.shape
    wb = k_cache.shape[1]
    bt = 8
    nkeys = 2 * WINDOW
    blk = lambda shape: pl.BlockSpec((bt,) + shape, lambda i: (i, 0, 0))
    return pl.pallas_call(
        functools.partial(_attn_sample_kernel, layer=layer, n_new=n_new),
        grid=(batch // bt,),
        in_specs=[
            pl.BlockSpec(memory_space=pltpu.SMEM),
            blk((rows, LANES)),
            blk((wb, LANES)), blk((SUBLANES, LANES)),
            blk((wb, LANES)), blk((SUBLANES, LANES)),
        ],
        out_specs=blk((rows, LANES)),
        out_shape=jax.ShapeDtypeStruct((batch, rows, LANES), MXU_DTYPE),
        scratch_shapes=[pltpu.VMEM((nkeys, LANES), F32), pltpu.VMEM((nkeys, LANES), F32)],
        compiler_params=_params("parallel"),
        name="attention_sample",
    )(sinks, qexp, k_cache, k_new, v_cache, v_new)


def _s5_discretise_kernel(lr_ref, li_ref, ldt_ref, bre_ref, bim_ref, are_ref, aim_ref, bbre_ref, bbim_ref):
    lr = lr_ref[...]
    li = li_ref[...]
    dt = jnp.exp(ldt_ref[...])
    mag = jnp.exp(lr * dt)
    ab_re = mag * jnp.cos(li * dt)
    ab_im = mag * jnp.sin(li * dt)
    den = lr * lr + li * li
    nr = ab_re - 1.0
    coef_re = (nr * lr + ab_im * li) / den
    coef_im = (ab_im * lr - nr * li) / den
    are_ref[...] = ab_re
    aim_ref[...] = ab_im
    br = bre_ref[...]
    bi = bim_ref[...]
    bbre_ref[...] = coef_re * br - coef_im * bi
    bbim_ref[...] = coef_re * bi + coef_im * br


def _s5_discretise(lam_re, lam_im, log_dt, b_re, b_im):
    depth = lam_re.shape[0]
    flat = lambda a: a.reshape(depth, 1, SSM_FLAT)
    ldt = jnp.broadcast_to(log_dt[:, :, None], lam_re.shape)
    bt = lambda a: jnp.transpose(a, (0, 3, 1, 2)).reshape(depth, SSM_GROUP, SSM_FLAT)
    vec = pl.BlockSpec((None, 1, SSM_FLAT), lambda l: (l, 0, 0))
    mat = pl.BlockSpec((None, SSM_GROUP, SSM_FLAT), lambda l: (l, 0, 0))
    return pl.pallas_call(
        _s5_discretise_kernel,
        grid=(depth,),
        in_specs=[vec, vec, vec, mat, mat],
        out_specs=[vec, vec, mat, mat],
        out_shape=[jax.ShapeDtypeStruct((depth, 1, SSM_FLAT), F32)] * 2
        + [jax.ShapeDtypeStruct((depth, SSM_GROUP, SSM_FLAT), F32)] * 2,
        compiler_params=_params("parallel"),
        name="s5_discretise",
    )(flat(lam_re), flat(lam_im), flat(ldt), bt(b_re), bt(b_im))


def _group_block_diag_in(bb):
    depth = bb.shape[0]
    eye = jnp.eye(N_SSM_GROUPS, dtype=bb.dtype)
    b4 = jnp.transpose(bb.reshape(depth, SSM_GROUP, N_SSM_GROUPS, SSM_STATE), (0, 2, 1, 3))
    full = eye[None, :, None, :, None] * b4[:, :, :, None, :]
    return full.reshape(depth, SSM_WIDTH, SSM_FLAT)


def _group_block_diag_out(c):
    depth = c.shape[0]
    eye = jnp.eye(N_SSM_GROUPS, dtype=c.dtype)
    full = eye[None, :, None, :, None] * jnp.transpose(c, (0, 1, 3, 2))[:, :, :, None, :]
    return full.reshape(depth, SSM_FLAT, SSM_WIDTH)


def _s5_kernel(u_ref, hre0_ref, him0_ref, are_ref, aim_ref, bre_ref, bim_ref, *rest, nseq, emit):
    if emit:
        (cre_ref, cim_ref, d_ref, wglu_ref, bglu_ref,
         o_ref, hre_out, him_out, hre_s, him_s, bure_s, buim_s, hsre_s, hsim_s) = rest
    else:
        hre_out, him_out, hre_s, him_s, bure_s, buim_s = rest
    steps = u_ref.shape[0] // nseq

    @pl.when(pl.program_id(0) == 0)
    def _():
        hre_s[...] = hre0_ref[...]
        him_s[...] = him0_ref[...]

    u = u_ref[...]
    ub = u.astype(MXU_DTYPE)
    bure_s[...] = _mm(ub, bre_ref[...])
    buim_s[...] = _mm(ub, bim_ref[...])
    ar = jnp.broadcast_to(are_ref[...], (nseq, SSM_FLAT))
    ai = jnp.broadcast_to(aim_ref[...], (nseq, SSM_FLAT))

    def step(t, carry):
        hr, hi = carry
        rows = pl.ds(pl.multiple_of(t * nseq, nseq), nseq)
        nhr = ar * hr - ai * hi + bure_s[rows, :]
        nhi = ar * hi + ai * hr + buim_s[rows, :]
        if emit:
            hsre_s[rows, :] = nhr
            hsim_s[rows, :] = nhi
        return nhr, nhi

    hr, hi = lax.fori_loop(0, steps, step, (hre_s[...], him_s[...]), unroll=min(steps, 8))
    hre_s[...] = hr
    him_s[...] = hi
    hre_out[...] = hr
    him_out[...] = hi
    if emit:
        y = _mm(hsre_s[...].astype(MXU_DTYPE), cre_ref[...]) - _mm(hsim_s[...].astype(MXU_DTYPE), cim_ref[...])
        z = jax.nn.gelu(y + d_ref[...] * u)
        gate = _mm(z.astype(MXU_DTYPE), wglu_ref[...]) + bglu_ref[...]
        o_ref[...] = (z * _sigmoid(gate)).astype(o_ref.dtype)


def _s5_scan(u_tm, hre0, him0, p, layer, emit, nseq, steps_per_tile):
    rows = u_tm.shape[0]
    tr = steps_per_tile * nseq
    fixed = lambda i: (0, 0)
    state = pl.BlockSpec((nseq, SSM_FLAT), fixed)
    in_specs = [pl.BlockSpec((tr, SSM_WIDTH), lambda i: (i, 0)), state, state,
                _layer_block((1, SSM_FLAT), layer), _layer_block((1, SSM_FLAT), layer),
                _layer_block((SSM_WIDTH, SSM_FLAT), layer), _layer_block((SSM_WIDTH, SSM_FLAT), layer)]
    args = [u_tm, hre0, him0, p['a_re'], p['a_im'], p['bre_blk'], p['bim_blk']]
    out_specs = [state, state]
    out_shape = [jax.ShapeDtypeStruct((nseq, SSM_FLAT), F32)] * 2
    scratch = [pltpu.VMEM((nseq, SSM_FLAT), F32)] * 2 + [pltpu.VMEM((tr, SSM_FLAT), F32)] * 2
    if emit:
        in_specs += [_layer_block((SSM_FLAT, SSM_WIDTH), layer), _layer_block((SSM_FLAT, SSM_WIDTH), layer),
                     _layer_block((1, SSM_WIDTH), layer), _layer_block((SSM_WIDTH, SSM_WIDTH), layer),
                     _layer_block((1, SSM_WIDTH), layer)]
        args += [p['cre_blk'], p['cim_blk'], p['ssm_d'], p['ssm_w_glu'], p['ssm_b_glu']]
        out_specs = [pl.BlockSpec((tr, SSM_WIDTH), lambda i: (i, 0))] + out_specs
        out_shape = [jax.ShapeDtypeStruct((rows, SSM_WIDTH), MXU_DTYPE)] + out_shape
        scratch += [pltpu.VMEM((tr, SSM_FLAT), F32)] * 2
    return pl.pallas_call(
        functools.partial(_s5_kernel, nseq=nseq, emit=emit),
        grid=(rows // tr,),
        in_specs=in_specs,
        out_specs=out_specs,
        out_shape=out_shape,
        scratch_shapes=scratch,
        compiler_params=_params("arbitrary"),
        name="s5_scan" if emit else "s5_segment_states",
    )(*args)


def _layernorm_silu(y, g, b):
    yc = y - jnp.mean(y, axis=-1, keepdims=True)
    var = jnp.mean(yc * yc, axis=-1, keepdims=True)
    return _silu(yc * lax.rsqrt(var + EPS) * g + b)


def _conv_prompt_kernel(prev_ref, cur_ref, w_ref, b_ref, g_ref, beta_ref, o_ref, buf, shifted,
                        *, tiles_per_batch, sub):
    tc = cur_ref.shape[0]
    first = (pl.program_id(0) % tiles_per_batch) == 0
    buf[0:CONV_HALO, :] = jnp.where(first, 0.0, prev_ref[...])
    buf[CONV_HALO:CONV_HALO + tc, :] = cur_ref[...]
    base = CONV_HALO - (CONV_K - 1)
    span = shifted.shape[1]
    for s in range(1, SUBLANES):
        shifted[s - 1] = buf[s:s + span, :]
    for r in range(tc // sub):
        acc = jnp.zeros((sub, CONV_WIDTH), F32)
        for k in range(CONV_K):
            off = base + k
            j, s = off // SUBLANES, off % SUBLANES
            start = r * sub + j * SUBLANES
            tap = buf[start:start + sub, :] if s == 0 else shifted[s - 1, start:start + sub, :]
            acc = acc + w_ref[k:k + 1, :] * tap
        y = _layernorm_silu(acc + b_ref[...], g_ref[...], beta_ref[...])
        o_ref[r * sub:(r + 1) * sub, :] = y.astype(o_ref.dtype)


def _conv_prompt(vc, p, layer, seq, tc):
    t = vc.shape[0]
    tiles_per_batch = seq // tc
    halo_per_tile = tc // CONV_HALO
    vec = _layer_block((1, CONV_WIDTH), layer)
    span = tc + CONV_HALO - SUBLANES
    return pl.pallas_call(
        functools.partial(_conv_prompt_kernel, tiles_per_batch=tiles_per_batch, sub=64),
        grid=(t // tc,),
        in_specs=[
            pl.BlockSpec((CONV_HALO, CONV_WIDTH), lambda i: (jnp.maximum(i * halo_per_tile - 1, 0), 0)),
            pl.BlockSpec((tc, CONV_WIDTH), lambda i: (i, 0)),
            _layer_block((CONV_K, CONV_WIDTH), layer), vec, vec, vec,
        ],
        out_specs=pl.BlockSpec((tc, CONV_WIDTH), lambda i: (i, 0)),
        out_shape=jax.ShapeDtypeStruct((t, CONV_WIDTH), MXU_DTYPE),
        scratch_shapes=[pltpu.VMEM((CONV_HALO + tc, CONV_WIDTH), F32),
                        pltpu.VMEM((SUBLANES - 1, span, CONV_WIDTH), F32)],
        compiler_params=_params("parallel"),
        name="conv_prompt",
    )(vc, vc, p['conv_w'], p['conv_b'], p['conv_ln_g'], p['conv_ln_b'])


def _conv_sample_kernel(state_ref, v_ref, w_ref, b_ref, g_ref, beta_ref, o_ref, ns_ref):
    ns = state_ref.shape[0]
    n_new = v_ref.shape[0]
    row = lambda j: state_ref[j] if j < ns else v_ref[j - ns]
    for t in range(n_new):
        acc = jnp.zeros(o_ref.shape[1:], F32)
        for k in range(CONV_K):
            acc = acc + w_ref[k:k + 1, :] * row(t + k)
        o_ref[t] = _layernorm_silu(acc + b_ref[...], g_ref[...], beta_ref[...]).astype(o_ref.dtype)
    for j in range(ns):
        ns_ref[j] = row(j + n_new)


def _conv_sample(state_all, v, p, layer):
    _, ns, batch, width = state_all.shape
    n_new = v.shape[0]
    bt = min(32, batch)
    vec = _layer_block((1, width), layer)
    return pl.pallas_call(
        _conv_sample_kernel,
        grid=(batch // bt,),
        in_specs=[
            pl.BlockSpec((None, ns, bt, width), lambda i: (layer, 0, i, 0)),
            pl.BlockSpec((n_new, bt, width), lambda i: (0, i, 0)),
            _layer_block((CONV_K, width), layer), vec, vec, vec,
        ],
        out_specs=[pl.BlockSpec((n_new, bt, width), lambda i: (0, i, 0)),
                   pl.BlockSpec((ns, bt, width), lambda i: (0, i, 0))],
        out_shape=[jax.ShapeDtypeStruct((n_new, batch, width), MXU_DTYPE),
                   jax.ShapeDtypeStruct((ns, batch, width), F32)],
        compiler_params=_params("parallel"),
        name="conv_sample",
    )(state_all, v, p['conv_w'], p['conv_b'], p['conv_ln_g'], p['conv_ln_b'])


def _outffn_kernel(x_ref, a_ref, s_ref, c_ref, g1_ref, sh2_ref, sc2_ref, g2_ref, n2_ref,
                   wo_ref, wg_ref, wu_ref, wd_ref, fg_ref, o_ref, *, final_norm, ff_chunk):
    tm = x_ref.shape[0]
    o1 = ATTN_WIDTH
    o2 = o1 + SSM_WIDTH
    proj = (_mm(a_ref[...], wo_ref[0:o1, :]) + _mm(s_ref[...], wo_ref[o1:o2, :])
            + _mm(c_ref[...], wo_ref[o2:o2 + CONV_WIDTH, :]))
    x2 = x_ref[...] + _rows(g1_ref[...], tm) * proj
    h2 = _rms(x2, n2_ref[...]) * (1.0 + _rows(sc2_ref[...], tm)) + _rows(sh2_ref[...], tm)
    h2 = h2.astype(MXU_DTYPE)
    ffn = jnp.zeros(x2.shape, F32)
    for c in range(D_FF // ff_chunk):
        cs = slice(c * ff_chunk, (c + 1) * ff_chunk)
        act = _silu(_mm(h2, wg_ref[:, cs])) * _mm(h2, wu_ref[:, cs])
        ffn = ffn + _mm(act.astype(MXU_DTYPE), wd_ref[cs, :])
    y = x2 + _rows(g2_ref[...], tm) * ffn
    if final_norm:
        y = _rms(y, fg_ref[...])
    o_ref[...] = y


def _out_ffn(x, o_attn, o_ssm, o_conv, mod, p, layer, final_norm, tm, tiles_per_batch):
    t = x.shape[0]
    row = lambda i: (i, 0)
    resident = lambda shape: _layer_block(shape, layer, pipeline_mode=pl.Buffered(1))
    return pl.pallas_call(
        functools.partial(_outffn_kernel, final_norm=final_norm, ff_chunk=D_FF // 2),
        grid=(t // tm,),
        in_specs=[
            pl.BlockSpec((tm, D_MODEL), row),
            pl.BlockSpec((tm, ATTN_WIDTH), row),
            pl.BlockSpec((tm, SSM_WIDTH), row),
            pl.BlockSpec((tm, CONV_WIDTH), row),
            _mod_spec(mod, layer, 2, tiles_per_batch),
            _mod_spec(mod, layer, 3, tiles_per_batch),
            _mod_spec(mod, layer, 4, tiles_per_batch),
            _mod_spec(mod, layer, 5, tiles_per_batch),
            _layer_block((1, D_MODEL), layer),
            resident((D_MODEL, D_MODEL)),
            resident((D_MODEL, D_FF)),
            resident((D_MODEL, D_FF)),
            resident((D_FF, D_MODEL)),
            pl.BlockSpec((1, D_MODEL), lambda i: (0, 0)),
        ],
        out_specs=pl.BlockSpec((tm, D_MODEL), row),
        out_shape=jax.ShapeDtypeStruct((t, D_MODEL), F32),
        compiler_params=_params("parallel"),
        name="out_projection_ffn",
    )(x, o_attn, o_ssm, o_conv, mod, mod, mod, mod, p['norm2_g'],
      p['w_out'], p['w_gate'], p['w_up'], p['w_down'], p['final_g'])


def _layer_prompt(x, mod, p, layer, rope, batch, seq, final_norm):
    tm = min(512, seq)
    tiles_per_batch = seq // tm
    q, kv, u, vc = _in_projection(x, mod, p, layer, rope[0], rope[1], tm, tiles_per_batch)
    o_attn = _attention_prompt(q, kv, p['sinks'], layer, batch, seq)
    n_keep = min(WINDOW, seq)
    kv_keep = kv.reshape(batch, seq, 2 * KV_WIDTH)[:, seq - n_keep:]
    new_k = kv_keep[:, :, :KV_WIDTH].reshape(batch, n_keep, N_KV_HEADS, HEAD_DIM)
    new_v = kv_keep[:, :, KV_WIDTH:].reshape(batch, n_keep, N_KV_HEADS, HEAD_DIM)

    nseg = PROMPT_SEGMENTS
    lseg = seq // nseg
    nseq = batch * nseg
    u_tm = jnp.transpose(u.reshape(nseq, lseg, SSM_WIDTH), (1, 0, 2)).reshape(lseg * nseq, SSM_WIDTH)
    zero = jnp.zeros((nseq, SSM_FLAT), F32)
    steps = min(128, lseg)
    end_re, end_im = _s5_scan(u_tm, zero, zero, p, layer, False, nseq, steps)

    def shift(e):
        e = e.reshape(batch, nseg, SSM_FLAT)
        return jnp.concatenate([jnp.zeros_like(e[:, :1]), e[:, :1]], axis=1).reshape(nseq, SSM_FLAT)

    o_tm, h_re, h_im = _s5_scan(u_tm, shift(end_re), shift(end_im), p, layer, True, nseq, steps)
    o_ssm = jnp.transpose(o_tm.reshape(lseg, nseq, SSM_WIDTH), (1, 0, 2)).reshape(batch * seq, SSM_WIDTH)
    last = lambda h: h.reshape(batch, nseg, N_SSM_GROUPS, SSM_STATE)[:, nseg - 1]

    o_conv = _conv_prompt(vc, p, layer, seq, min(512, seq))
    new_conv = vc.reshape(batch, seq, CONV_WIDTH)[:, seq - (CONV_K - 1):]

    x = _out_ffn(x, o_attn, o_ssm, o_conv, mod, p, layer, final_norm, tm, tiles_per_batch)
    return x, new_k, new_v, last(h_re), last(h_im), new_conv


def _layer_sample(x, mod, p, layer, rope, batch, n_new, k_buf, v_buf, h0_re, h0_im, conv_all, final_norm):
    t = batch * n_new
    q, kv, u, vc = _in_projection(x, mod, p, layer, rope[0], rope[1], t, 1)

    wb = k_buf.shape[1]
    q4 = jnp.transpose(q.reshape(n_new, batch, N_Q_HEADS, HEAD_DIM), (1, 2, 0, 3))
    q4 = q4.reshape(batch, N_KV_HEADS, Q_PER_KV * n_new, HEAD_DIM)
    zq = jnp.zeros_like(q4[:, 0])
    qexp = jnp.concatenate([jnp.concatenate([q4[:, 0], zq], axis=-1),
                            jnp.concatenate([zq, q4[:, 1]], axis=-1)], axis=1)
    kv3 = jnp.transpose(kv.reshape(n_new, batch, 2 * KV_WIDTH), (1, 0, 2))
    padn = ((0, 0), (0, SUBLANES - n_new), (0, 0))
    k_new, v_new = kv3[:, :, :KV_WIDTH], kv3[:, :, KV_WIDTH:]
    kc = k_buf.reshape(batch, wb, KV_WIDTH)
    vcache = v_buf.reshape(batch, wb, KV_WIDTH)
    o = _attention_sample(qexp, kc, jnp.pad(k_new, padn), vcache, jnp.pad(v_new, padn), p['sinks'], layer, n_new)
    o = o[:, :, :HEAD_DIM].reshape(batch, N_Q_HEADS, n_new, HEAD_DIM)
    o_attn = jnp.transpose(o, (2, 0, 1, 3)).reshape(t, ATTN_WIDTH)
    new_k = jnp.concatenate([kc, k_new], axis=1)[:, n_new:].reshape(batch, wb, N_KV_HEADS, HEAD_DIM)
    new_v = jnp.concatenate([vcache, v_new], axis=1)[:, n_new:].reshape(batch, wb, N_KV_HEADS, HEAD_DIM)

    o_ssm, h_re, h_im = _s5_scan(u, h0_re.reshape(batch, SSM_FLAT), h0_im.reshape(batch, SSM_FLAT),
                                 p, layer, True, batch, n_new)
    st = lambda h: h.reshape(batch, N_SSM_GROUPS, SSM_STATE)

    o_conv, new_conv = _conv_sample(conv_all, vc.reshape(n_new, batch, CONV_WIDTH), p, layer)
    o_conv = o_conv.reshape(t, CONV_WIDTH)

    x = _out_ffn(x, o_attn, o_ssm, o_conv, mod, p, layer, final_norm, t, 1)
    return x, new_k, new_v, st(h_re), st(h_im), new_conv


def kernel(x_prompt, x_sample, c_prompt, c_sample, cache_k, cache_v, state_ssm_re, state_ssm_im, state_conv,
           norm1_g, norm2_g, w_mod, b_mod, w_in, attn_sinks, ssm_lam_re, ssm_lam_im, ssm_log_dt,
           ssm_b_re, ssm_b_im, ssm_c_re, ssm_c_im, ssm_d, ssm_w_glu, ssm_b_glu,
           conv_w, conv_b, conv_ln_g, conv_ln_b, w_out, w_gate, w_up, w_down, final_norm_g):
    bp, seq, d = x_prompt.shape
    bs, n_new, _ = x_sample.shape
    depth = w_in.shape[0]
    assert PROMPT_SEGMENTS == 2 and seq % (PROMPT_SEGMENTS * SUBLANES) == 0

    c_all = jnp.concatenate([c_prompt, c_sample], axis=0)
    pad_rows = -c_all.shape[0] % SUBLANES
    mods = _modulation(jnp.pad(c_all, ((0, pad_rows), (0, 0))), w_mod, b_mod)
    mod_p = mods[:, :bp].reshape(depth, bp, 1, N_MOD * d)
    mod_s = mods[:, bp:bp + bs]

    a_re, a_im, bb_re, bb_im = _s5_discretise(ssm_lam_re, ssm_lam_im, ssm_log_dt, ssm_b_re, ssm_b_im)
    rope_p = _rope_tables(jnp.arange(seq))
    rope_s = _rope_tables(jnp.repeat(PAST_LEN + jnp.arange(n_new), bs))

    cast = lambda a: a.astype(MXU_DTYPE)
    vec = lambda a: a.reshape(depth, 1, a.shape[-1])
    p = {
        'norm1_g': vec(norm1_g), 'norm2_g': vec(norm2_g), 'w_in': cast(w_in), 'sinks': attn_sinks,
        'a_re': a_re, 'a_im': a_im,
        'bre_blk': cast(_group_block_diag_in(bb_re)), 'bim_blk': cast(_group_block_diag_in(bb_im)),
        'cre_blk': cast(_group_block_diag_out(ssm_c_re)), 'cim_blk': cast(_group_block_diag_out(ssm_c_im)),
        'ssm_d': vec(ssm_d), 'ssm_w_glu': cast(ssm_w_glu), 'ssm_b_glu': vec(ssm_b_glu),
        'conv_w': conv_w, 'conv_b': vec(conv_b), 'conv_ln_g': vec(conv_ln_g), 'conv_ln_b': vec(conv_ln_b),
        'w_out': cast(w_out), 'w_gate': cast(w_gate), 'w_up': cast(w_up), 'w_down': cast(w_down),
        'final_g': final_norm_g.reshape(1, d),
    }
    conv_all = jnp.transpose(state_conv, (0, 2, 1, 3))

    xp = x_prompt.reshape(bp * seq, d)
    xs = jnp.transpose(x_sample, (1, 0, 2)).reshape(n_new * bs, d)
    outs_p, outs_s = [], []
    for l in range(depth):
        final = l == depth - 1
        xp, *op = _layer_prompt(xp, mod_p, p, l, rope_p, bp, seq, final)
        xs, *os_ = _layer_sample(xs, mod_s, p, l, rope_s, bs, n_new, cache_k[l], cache_v[l],
                                 state_ssm_re[l], state_ssm_im[l], conv_all, final)
        outs_p.append(op)
        outs_s.append(os_)
    stack = lambda outs, i: jnp.stack([o[i] for o in outs])
    y_sample = jnp.transpose(xs.reshape(n_new, bs, d), (1, 0, 2))
    new_conv_s = jnp.transpose(stack(outs_s, 4), (0, 2, 1, 3))
    return (xp.reshape(bp, seq, d), y_sample,
            *[stack(outs_p, i) for i in range(5)], *[stack(outs_s, i) for i in range(4)], new_conv_s)
```

```python
import functools
import math

import jax
import jax.numpy as jnp
from jax import lax
from jax.experimental import pallas as pl
from jax.experimental.pallas import tpu as pltpu

F32 = jnp.float32
MXU_DTYPE = jnp.bfloat16

V7X_VMEM_BYTES = 64 * 1024 * 1024
VMEM_LIMIT_BYTES = V7X_VMEM_BYTES - 8 * 1024 * 1024
LANES = 128
SUBLANES = 8

D_MODEL = 1024
HEAD_DIM = 64
N_Q_HEADS = 8
N_KV_HEADS = 2
Q_PER_KV = N_Q_HEADS // N_KV_HEADS
ATTN_WIDTH = N_Q_HEADS * HEAD_DIM
KV_WIDTH = N_KV_HEADS * HEAD_DIM
WINDOW = 128
ROPE_THETA = 10000.0
ATTN_SCALE = 1.0 / math.sqrt(HEAD_DIM)
SSM_WIDTH = 256
SSM_GROUP = 16
N_SSM_GROUPS = 16
SSM_STATE = 64
SSM_FLAT = N_SSM_GROUPS * SSM_STATE
CONV_WIDTH = 256
CONV_K = 31
CONV_HALO = 32
IN_WIDTH = ATTN_WIDTH + 2 * KV_WIDTH + SSM_WIDTH + 2 * CONV_WIDTH
D_FF = 2816
EPS = 1e-6
NEG = -1e30
N_MOD = 6
PROMPT_SEGMENTS = 2
PAST_LEN = 8192


def _params(*semantics):
    return pltpu.CompilerParams(dimension_semantics=semantics, vmem_limit_bytes=VMEM_LIMIT_BYTES)


def _layer_block(shape, layer, **kw):
    zeros = (0,) * len(shape)
    return pl.BlockSpec((None,) + tuple(shape), lambda *_: (layer,) + zeros, **kw)


def _sigmoid(x):
    return 1.0 / (1.0 + jnp.exp(-x))


def _silu(x):
    return x * _sigmoid(x)


def _rms(x, g):
    return x * lax.rsqrt(jnp.mean(x * x, axis=-1, keepdims=True) + EPS) * g


def _mm(a, b):
    return jnp.dot(a, b, preferred_element_type=F32)


def _mm_nt(a, b):
    return lax.dot_general(a, b, (((1,), (1,)), ((), ())), preferred_element_type=F32)


def _rows(m, n):
    return m if m.shape[0] == 1 else jnp.concatenate([m] * (n // m.shape[0]), axis=0)


def _mod_kernel(c_ref, w_ref, b_ref, o_ref):
    a = _silu(c_ref[...]).astype(MXU_DTYPE)
    o_ref[...] = _mm(a, w_ref[...].astype(MXU_DTYPE)) + b_ref[...]


def _modulation(c, w_mod, b_mod):
    depth, d, n = w_mod.shape
    rows = c.shape[0]
    tn = 1536
    return pl.pallas_call(
        _mod_kernel,
        grid=(depth, n // tn),
        in_specs=[
            pl.BlockSpec((rows, d), lambda l, j: (0, 0)),
            pl.BlockSpec((None, d, tn), lambda l, j: (l, 0, j)),
            pl.BlockSpec((None, 1, tn), lambda l, j: (l, 0, j)),
        ],
        out_specs=pl.BlockSpec((None, rows, tn), lambda l, j: (l, 0, j)),
        out_shape=jax.ShapeDtypeStruct((depth, rows, n), F32),
        compiler_params=_params("parallel", "parallel"),
        name="modulation",
    )(c, w_mod, b_mod.reshape(depth, 1, n))


def _mod_spec(mod, layer, chunk, tiles_per_batch):
    if mod.ndim == 4:
        return pl.BlockSpec((None, None, 1, D_MODEL), lambda i: (layer, i // tiles_per_batch, 0, chunk))
    return pl.BlockSpec((None, mod.shape[1], D_MODEL), lambda i: (layer, 0, chunk))


def _inproj_kernel(x_ref, sh_ref, sc_ref, g_ref, w_ref, cos_ref, sin_ref,
                   q_ref, kv_ref, u_ref, vc_ref):
    tm = x_ref.shape[0]
    h = _rms(x_ref[...], g_ref[...]) * (1.0 + _rows(sc_ref[...], tm)) + _rows(sh_ref[...], tm)
    z = _mm(h.astype(MXU_DTYPE), w_ref[...])
    cos = cos_ref[...]
    sin = sin_ref[...]
    lane = lax.broadcasted_iota(jnp.int32, (tm, LANES), 1)
    first_half = (lane % HEAD_DIM) < (HEAD_DIM // 2)

    def rope(t):
        partner = jnp.where(first_half, pltpu.roll(t, LANES - HEAD_DIM // 2, 1),
                            pltpu.roll(t, HEAD_DIM // 2, 1))
        return t * cos + partner * sin

    for j in range(ATTN_WIDTH // LANES):
        sl = slice(j * LANES, (j + 1) * LANES)
        q_ref[:, sl] = (rope(z[:, sl]) * ATTN_SCALE).astype(q_ref.dtype)
    o = ATTN_WIDTH
    kv_ref[:, 0:KV_WIDTH] = rope(z[:, o:o + KV_WIDTH])
    kv_ref[:, KV_WIDTH:2 * KV_WIDTH] = z[:, o + KV_WIDTH:o + 2 * KV_WIDTH]
    o += 2 * KV_WIDTH
    u_ref[...] = z[:, o:o + SSM_WIDTH]
    o += SSM_WIDTH
    za = z[:, o:o + CONV_WIDTH]
    zg = z[:, o + CONV_WIDTH:o + 2 * CONV_WIDTH]
    vc_ref[...] = za * _sigmoid(zg)


def _in_projection(x, mod, p, layer, cos, sin, tm, tiles_per_batch):
    t = x.shape[0]
    pos_tiles = cos.shape[0] // tm
    row = lambda i: (i, 0)
    return pl.pallas_call(
        _inproj_kernel,
        grid=(t // tm,),
        in_specs=[
            pl.BlockSpec((tm, D_MODEL), row),
            _mod_spec(mod, layer, 0, tiles_per_batch),
            _mod_spec(mod, layer, 1, tiles_per_batch),
            _layer_block((1, D_MODEL), layer),
            _layer_block((D_MODEL, IN_WIDTH), layer),
            pl.BlockSpec((tm, LANES), lambda i: (i % pos_tiles, 0)),
            pl.BlockSpec((tm, LANES), lambda i: (i % pos_tiles, 0)),
        ],
        out_specs=[
            pl.BlockSpec((tm, ATTN_WIDTH), row),
            pl.BlockSpec((tm, 2 * KV_WIDTH), row),
            pl.BlockSpec((tm, SSM_WIDTH), row),
            pl.BlockSpec((tm, CONV_WIDTH), row),
        ],
        out_shape=[
            jax.ShapeDtypeStruct((t, ATTN_WIDTH), MXU_DTYPE),
            jax.ShapeDtypeStruct((t, 2 * KV_WIDTH), F32),
            jax.ShapeDtypeStruct((t, SSM_WIDTH), F32),
            jax.ShapeDtypeStruct((t, CONV_WIDTH), F32),
        ],
        compiler_params=_params("parallel"),
        name="in_projection",
    )(x, mod, mod, p['norm1_g'], p['w_in'], cos, sin)


def _rope_tables(pos):
    half = HEAD_DIM // 2
    inv_freq = ROPE_THETA ** (-jnp.arange(half, dtype=F32) / half)
    ang = pos.astype(F32)[:, None] * inv_freq[None, :]
    cos = jnp.tile(jnp.cos(ang), (1, LANES // half))
    sin = jnp.sin(ang)
    sin = jnp.tile(jnp.concatenate([-sin, sin], axis=1), (1, LANES // HEAD_DIM))
    return cos, sin


def _sink_softmax(s, sink_col):
    m = jnp.maximum(jnp.max(s, axis=-1, keepdims=True), sink_col)
    e = jnp.exp(s - m)
    return e * (1.0 / (jnp.sum(e, axis=-1, keepdims=True) + jnp.exp(sink_col - m)))


def _head_pair_select(x, pick_second):
    lane = lax.broadcasted_iota(jnp.int32, x.shape, 1)
    swapped = pltpu.roll(x, HEAD_DIM, 1)
    low = lane < HEAD_DIM
    return jnp.where(low, swapped, x) if pick_second else jnp.where(low, x, swapped)


def _attn_prompt_kernel(sink_ref, q_ref, kvp_ref, kvc_ref, o_ref, *, layer):
    n = pl.program_id(1)
    w = WINDOW
    n_blocks = q_ref.shape[0] // w
    rows = Q_PER_KV * w
    r_idx = lax.broadcasted_iota(jnp.int32, (rows, w), 0) % w
    c_idx = lax.broadcasted_iota(jnp.int32, (rows, w), 1)
    from_prev = c_idx > r_idx
    first_bias = jnp.where(n > 0, 0.0, NEG)
    row_head = lax.broadcasted_iota(jnp.int32, (rows, 1), 0) // w
    low = lax.broadcasted_iota(jnp.int32, (w, LANES), 1) < HEAD_DIM
    half_mask = [jnp.where(low, 1.0, 0.0).astype(MXU_DTYPE), jnp.where(low, 0.0, 1.0).astype(MXU_DTYPE)]
    sink_cols = []
    for kvh in range(N_KV_HEADS):
        sink_col = jnp.zeros((rows, 1), F32)
        for g in range(Q_PER_KV):
            sink_col = jnp.where(row_head == g, sink_ref[layer, kvh * Q_PER_KV + g], sink_col)
        sink_cols.append(sink_col)
    chains = [(i, kvh) for i in range(n_blocks) for kvh in range(N_KV_HEADS)]
    windows, values = [], []
    for i, kvh in chains:
        own = slice(i * w, (i + 1) * w)
        prev_ref, prev = (kvp_ref, slice(0, w)) if i == 0 else (kvc_ref, slice((i - 1) * w, i * w))
        kk = jnp.concatenate([prev_ref[prev, 0:KV_WIDTH], kvc_ref[own, 0:KV_WIDTH]], axis=0)
        vv = jnp.concatenate([prev_ref[prev, KV_WIDTH:2 * KV_WIDTH], kvc_ref[own, KV_WIDTH:2 * KV_WIDTH]], axis=0)
        k2 = _head_pair_select(kk, kvh == 1).astype(MXU_DTYPE)
        values.append(_head_pair_select(vv, kvh == 1).astype(MXU_DTYPE))
        pieces = []
        for g in range(Q_PER_KV):
            h = kvh * Q_PER_KV + g
            qcol = q_ref[own, (h // 2) * LANES:(h // 2 + 1) * LANES]
            pieces.append(qcol * half_mask[h % 2])
        s = _mm_nt(jnp.concatenate(pieces, axis=0), k2)
        s_prev = s[:, 0:w] + first_bias if i == 0 else s[:, 0:w]
        windows.append(jnp.where(from_prev, s_prev, s[:, w:2 * w]))
    probs = []
    for (i, kvh), sc in zip(chains, windows):
        p = _sink_softmax(sc, sink_cols[kvh])
        p2 = jnp.concatenate([jnp.where(from_prev, p, 0.0), jnp.where(from_prev, 0.0, p)], axis=1)
        probs.append(p2.astype(MXU_DTYPE))
    for (i, kvh), p2, v2 in zip(chains, probs, values):
        own = slice(i * w, (i + 1) * w)
        r = _mm(p2, v2)
        for j in range(Q_PER_KV // 2):
            col = kvh * (Q_PER_KV // 2) + j
            o_ref[own, col * LANES:(col + 1) * LANES] = jnp.where(
                low, r[2 * j * w:(2 * j + 1) * w], r[(2 * j + 1) * w:(2 * j + 2) * w]).astype(o_ref.dtype)


def _attention_prompt(q, kv, sinks, layer, batch, seq):
    tq = min(4 * WINDOW, seq)
    nt = seq // tq
    per = tq // WINDOW
    return pl.pallas_call(
        functools.partial(_attn_prompt_kernel, layer=layer),
        grid=(batch, nt),
        in_specs=[
            pl.BlockSpec(memory_space=pltpu.SMEM),
            pl.BlockSpec((tq, ATTN_WIDTH), lambda b, n: (b * nt + n, 0)),
            pl.BlockSpec((WINDOW, 2 * KV_WIDTH), lambda b, n: ((b * nt + n) * per - jnp.minimum(n, 1), 0)),
            pl.BlockSpec((tq, 2 * KV_WIDTH), lambda b, n: (b * nt + n, 0)),
        ],
        out_specs=pl.BlockSpec((tq, ATTN_WIDTH), lambda b, n: (b * nt + n, 0)),
        out_shape=jax.ShapeDtypeStruct((batch * seq, ATTN_WIDTH), MXU_DTYPE),
        compiler_params=_params("parallel", "parallel"),
        name="attention_prompt",
    )(sinks, q, kv, kv)


def _attn_sample_kernel(sink_ref, q_ref, kt_ref, vt_ref, knt_ref, vnt_ref, o_ref, nk_ref, nv_ref, *, layer, n_new):
    bt, rows, _ = q_ref.shape
    nkv, hd, wb = kt_ref.shape[1:]
    kept = wb - n_new
    r = lax.broadcasted_iota(jnp.int32, (bt * rows, 2 * wb), 0)
    j = lax.broadcasted_iota(jnp.int32, (bt * rows, 2 * wb), 1)
    t_idx = r % n_new
    t_new = j - wb - kept
    mask = ((j < wb) & (j > t_idx)) | ((t_new >= 0) & (t_new <= t_idx))
    row_head = (lax.broadcasted_iota(jnp.int32, (bt * rows, 1), 0) % rows) // n_new
    sink_col = jnp.zeros((bt * rows, 1), F32)
    for h in range(N_Q_HEADS):
        sink_col = jnp.where(row_head == h, sink_ref[layer, h], sink_col)
    keep = lax.broadcasted_iota(jnp.int32, (nkv, hd, wb), 2) < kept
    second_group = lax.broadcasted_iota(jnp.int32, (rows, nkv * hd), 0) >= (rows // nkv)
    flat = lambda a: a.reshape(nkv * hd, wb)

    scores, values = [], []
    for b in range(bt):
        kt, kpos = kt_ref[b], knt_ref[b]
        vt, vpos = vt_ref[b], vnt_ref[b]
        nk_ref[b] = jnp.where(keep, pltpu.roll(kt, kept, 2), kpos)
        nv_ref[b] = jnp.where(keep, pltpu.roll(vt, kept, 2), vpos)
        kcat = jnp.concatenate([flat(kt), flat(kpos)], axis=1).astype(MXU_DTYPE)
        values.append(jnp.concatenate([flat(vt), flat(vpos)], axis=1).astype(MXU_DTYPE))
        scores.append(_mm(q_ref[b], kcat))
    s = jnp.where(mask, jnp.concatenate(scores, axis=0), NEG)
    p = _sink_softmax(s, sink_col).astype(MXU_DTYPE)
    for b in range(bt):
        o = _mm_nt(p[b * rows:(b + 1) * rows], values[b])
        o_ref[b] = jnp.where(second_group, pltpu.roll(o, hd, 1), o).astype(o_ref.dtype)


def _attention_sample(qexp, kt_all, vt_all, knt, vnt, sinks, layer, n_new):
    batch, rows, width = qexp.shape
    _, _, nkv, hd, wb = kt_all.shape
    bt = min(8, batch)
    blk3 = pl.BlockSpec((bt, rows, width), lambda i: (i, 0, 0))
    blk4 = pl.BlockSpec((bt, nkv, hd, wb), lambda i: (i, 0, 0, 0))
    cache = pl.BlockSpec((None, bt, nkv, hd, wb), lambda i: (layer, i, 0, 0, 0))
    return pl.pallas_call(
        functools.partial(_attn_sample_kernel, layer=layer, n_new=n_new),
        grid=(batch // bt,),
        in_specs=[pl.BlockSpec(memory_space=pltpu.SMEM), blk3, cache, cache, blk4, blk4],
        out_specs=[blk3, blk4, blk4],
        out_shape=[jax.ShapeDtypeStruct((batch, rows, width), MXU_DTYPE),
                   jax.ShapeDtypeStruct((batch, nkv, hd, wb), F32),
                   jax.ShapeDtypeStruct((batch, nkv, hd, wb), F32)],
        compiler_params=_params("parallel"),
        name="attention_sample",
    )(sinks, qexp, kt_all, vt_all, knt, vnt)


def _s5_discretise_kernel(lr_ref, li_ref, ldt_ref, bre_ref, bim_ref, are_ref, aim_ref, bbre_ref, bbim_ref):
    lr = lr_ref[...]
    li = li_ref[...]
    dt = jnp.exp(ldt_ref[...])
    mag = jnp.exp(lr * dt)
    ab_re = mag * jnp.cos(li * dt)
    ab_im = mag * jnp.sin(li * dt)
    den = lr * lr + li * li
    nr = ab_re - 1.0
    coef_re = (nr * lr + ab_im * li) / den
    coef_im = (ab_im * lr - nr * li) / den
    are_ref[...] = ab_re
    aim_ref[...] = ab_im
    br = bre_ref[...]
    bi = bim_ref[...]
    bbre_ref[...] = coef_re * br - coef_im * bi
    bbim_ref[...] = coef_re * bi + coef_im * br


def _s5_discretise(lam_re, lam_im, log_dt, b_re, b_im):
    depth = lam_re.shape[0]
    flat = lambda a: a.reshape(depth, 1, SSM_FLAT)
    ldt = jnp.broadcast_to(log_dt[:, :, None], lam_re.shape)
    bt = lambda a: jnp.transpose(a, (0, 3, 1, 2)).reshape(depth, SSM_GROUP, SSM_FLAT)
    vec = pl.BlockSpec((None, 1, SSM_FLAT), lambda l: (l, 0, 0))
    mat = pl.BlockSpec((None, SSM_GROUP, SSM_FLAT), lambda l: (l, 0, 0))
    return pl.pallas_call(
        _s5_discretise_kernel,
        grid=(depth,),
        in_specs=[vec, vec, vec, mat, mat],
        out_specs=[vec, vec, mat, mat],
        out_shape=[jax.ShapeDtypeStruct((depth, 1, SSM_FLAT), F32)] * 2
        + [jax.ShapeDtypeStruct((depth, SSM_GROUP, SSM_FLAT), F32)] * 2,
        compiler_params=_params("parallel"),
        name="s5_discretise",
    )(flat(lam_re), flat(lam_im), flat(ldt), bt(b_re), bt(b_im))


def _group_block_diag_in(bb):
    depth = bb.shape[0]
    eye = jnp.eye(N_SSM_GROUPS, dtype=bb.dtype)
    b4 = jnp.transpose(bb.reshape(depth, SSM_GROUP, N_SSM_GROUPS, SSM_STATE), (0, 2, 1, 3))
    full = eye[None, :, None, :, None] * b4[:, :, :, None, :]
    return full.reshape(depth, SSM_WIDTH, SSM_FLAT)


def _group_block_diag_out(c):
    depth = c.shape[0]
    eye = jnp.eye(N_SSM_GROUPS, dtype=c.dtype)
    full = eye[None, :, None, :, None] * jnp.transpose(c, (0, 1, 3, 2))[:, :, :, None, :]
    return full.reshape(depth, SSM_FLAT, SSM_WIDTH)


def _s5_kernel(u_ref, hre0_ref, him0_ref, are_ref, aim_ref, bre_ref, bim_ref, *rest, nseq, emit):
    if emit:
        (cre_ref, cim_ref, d_ref, wglu_ref, bglu_ref,
         o_ref, hre_out, him_out, hre_s, him_s, bure_s, buim_s, hsre_s, hsim_s) = rest
    else:
        hre_out, him_out, hre_s, him_s, bure_s, buim_s = rest
    steps = u_ref.shape[0] // nseq

    @pl.when(pl.program_id(0) == 0)
    def _():
        hre_s[...] = hre0_ref[...]
        him_s[...] = him0_ref[...]

    u = u_ref[...]
    ub = u.astype(MXU_DTYPE)
    bure_s[...] = _mm(ub, bre_ref[...])
    buim_s[...] = _mm(ub, bim_ref[...])
    ar = jnp.broadcast_to(are_ref[...], (nseq, SSM_FLAT))
    ai = jnp.broadcast_to(aim_ref[...], (nseq, SSM_FLAT))

    def step(t, carry):
        hr, hi = carry
        rows = pl.ds(pl.multiple_of(t * nseq, nseq), nseq)
        nhr = ar * hr - ai * hi + bure_s[rows, :]
        nhi = ar * hi + ai * hr + buim_s[rows, :]
        if emit:
            hsre_s[rows, :] = nhr
            hsim_s[rows, :] = nhi
        return nhr, nhi

    hr, hi = lax.fori_loop(0, steps, step, (hre_s[...], him_s[...]), unroll=min(steps, 8))
    hre_s[...] = hr
    him_s[...] = hi
    hre_out[...] = hr
    him_out[...] = hi
    if emit:
        y = _mm(hsre_s[...].astype(MXU_DTYPE), cre_ref[...]) - _mm(hsim_s[...].astype(MXU_DTYPE), cim_ref[...])
        z = jax.nn.gelu(y + d_ref[...] * u)
        gate = _mm(z.astype(MXU_DTYPE), wglu_ref[...]) + bglu_ref[...]
        o_ref[...] = (z * _sigmoid(gate)).astype(o_ref.dtype)


def _s5_scan(u_tm, hre0, him0, p, layer, emit, nseq, steps_per_tile):
    rows = u_tm.shape[0]
    tr = steps_per_tile * nseq
    fixed = lambda i: (0, 0)
    state = pl.BlockSpec((nseq, SSM_FLAT), fixed)
    in_specs = [pl.BlockSpec((tr, SSM_WIDTH), lambda i: (i, 0)), state, state,
                _layer_block((1, SSM_FLAT), layer), _layer_block((1, SSM_FLAT), layer),
                _layer_block((SSM_WIDTH, SSM_FLAT), layer), _layer_block((SSM_WIDTH, SSM_FLAT), layer)]
    args = [u_tm, hre0, him0, p['a_re'], p['a_im'], p['bre_blk'], p['bim_blk']]
    out_specs = [state, state]
    out_shape = [jax.ShapeDtypeStruct((nseq, SSM_FLAT), F32)] * 2
    scratch = [pltpu.VMEM((nseq, SSM_FLAT), F32)] * 2 + [pltpu.VMEM((tr, SSM_FLAT), F32)] * 2
    if emit:
        in_specs += [_layer_block((SSM_FLAT, SSM_WIDTH), layer), _layer_block((SSM_FLAT, SSM_WIDTH), layer),
                     _layer_block((1, SSM_WIDTH), layer), _layer_block((SSM_WIDTH, SSM_WIDTH), layer),
                     _layer_block((1, SSM_WIDTH), layer)]
        args += [p['cre_blk'], p['cim_blk'], p['ssm_d'], p['ssm_w_glu'], p['ssm_b_glu']]
        out_specs = [pl.BlockSpec((tr, SSM_WIDTH), lambda i: (i, 0))] + out_specs
        out_shape = [jax.ShapeDtypeStruct((rows, SSM_WIDTH), MXU_DTYPE)] + out_shape
        scratch += [pltpu.VMEM((tr, SSM_FLAT), F32)] * 2
    return pl.pallas_call(
        functools.partial(_s5_kernel, nseq=nseq, emit=emit),
        grid=(rows // tr,),
        in_specs=in_specs,
        out_specs=out_specs,
        out_shape=out_shape,
        scratch_shapes=scratch,
        compiler_params=_params("arbitrary"),
        name="s5_scan" if emit else "s5_segment_states",
    )(*args)


def _layernorm_silu(y, g, b):
    yc = y - jnp.mean(y, axis=-1, keepdims=True)
    var = jnp.mean(yc * yc, axis=-1, keepdims=True)
    return _silu(yc * lax.rsqrt(var + EPS) * g + b)


def _conv_prompt_kernel(prev_ref, cur_ref, w_ref, b_ref, g_ref, beta_ref, o_ref, buf, shifted,
                        *, tiles_per_batch, sub):
    tc = cur_ref.shape[0]
    first = (pl.program_id(0) % tiles_per_batch) == 0
    buf[0:CONV_HALO, :] = jnp.where(first, 0.0, prev_ref[...])
    buf[CONV_HALO:CONV_HALO + tc, :] = cur_ref[...]
    base = CONV_HALO - (CONV_K - 1)
    span = shifted.shape[1]
    for s in range(1, SUBLANES):
        shifted[s - 1] = buf[s:s + span, :]
    for r in range(tc // sub):
        acc = jnp.zeros((sub, CONV_WIDTH), F32)
        for k in range(CONV_K):
            off = base + k
            j, s = off // SUBLANES, off % SUBLANES
            start = r * sub + j * SUBLANES
            tap = buf[start:start + sub, :] if s == 0 else shifted[s - 1, start:start + sub, :]
            acc = acc + w_ref[k:k + 1, :] * tap
        y = _layernorm_silu(acc + b_ref[...], g_ref[...], beta_ref[...])
        o_ref[r * sub:(r + 1) * sub, :] = y.astype(o_ref.dtype)


def _conv_prompt(vc, p, layer, seq, tc):
    t = vc.shape[0]
    tiles_per_batch = seq // tc
    halo_per_tile = tc // CONV_HALO
    vec = _layer_block((1, CONV_WIDTH), layer)
    span = tc + CONV_HALO - SUBLANES
    return pl.pallas_call(
        functools.partial(_conv_prompt_kernel, tiles_per_batch=tiles_per_batch, sub=64),
        grid=(t // tc,),
        in_specs=[
            pl.BlockSpec((CONV_HALO, CONV_WIDTH), lambda i: (jnp.maximum(i * halo_per_tile - 1, 0), 0)),
            pl.BlockSpec((tc, CONV_WIDTH), lambda i: (i, 0)),
            _layer_block((CONV_K, CONV_WIDTH), layer), vec, vec, vec,
        ],
        out_specs=pl.BlockSpec((tc, CONV_WIDTH), lambda i: (i, 0)),
        out_shape=jax.ShapeDtypeStruct((t, CONV_WIDTH), MXU_DTYPE),
        scratch_shapes=[pltpu.VMEM((CONV_HALO + tc, CONV_WIDTH), F32),
                        pltpu.VMEM((SUBLANES - 1, span, CONV_WIDTH), F32)],
        compiler_params=_params("parallel"),
        name="conv_prompt",
    )(vc, vc, p['conv_w'], p['conv_b'], p['conv_ln_g'], p['conv_ln_b'])


def _conv_sample_kernel(state_ref, v_ref, w_ref, b_ref, g_ref, beta_ref, o_ref, ns_ref):
    ns = state_ref.shape[0]
    n_new = v_ref.shape[0]
    row = lambda j: state_ref[j] if j < ns else v_ref[j - ns]
    for t in range(n_new):
        acc = jnp.zeros(o_ref.shape[1:], F32)
        for k in range(CONV_K):
            acc = acc + w_ref[k:k + 1, :] * row(t + k)
        o_ref[t] = _layernorm_silu(acc + b_ref[...], g_ref[...], beta_ref[...]).astype(o_ref.dtype)
    for j in range(ns):
        ns_ref[j] = row(j + n_new)


def _conv_sample(state_all, v, p, layer):
    _, ns, batch, width = state_all.shape
    n_new = v.shape[0]
    bt = min(32, batch)
    vec = _layer_block((1, width), layer)
    return pl.pallas_call(
        _conv_sample_kernel,
        grid=(batch // bt,),
        in_specs=[
            pl.BlockSpec((None, ns, bt, width), lambda i: (layer, 0, i, 0)),
            pl.BlockSpec((n_new, bt, width), lambda i: (0, i, 0)),
            _layer_block((CONV_K, width), layer), vec, vec, vec,
        ],
        out_specs=[pl.BlockSpec((n_new, bt, width), lambda i: (0, i, 0)),
                   pl.BlockSpec((ns, bt, width), lambda i: (0, i, 0))],
        out_shape=[jax.ShapeDtypeStruct((n_new, batch, width), MXU_DTYPE),
                   jax.ShapeDtypeStruct((ns, batch, width), F32)],
        compiler_params=_params("parallel"),
        name="conv_sample",
    )(state_all, v, p['conv_w'], p['conv_b'], p['conv_ln_g'], p['conv_ln_b'])


def _outffn_kernel(x_ref, a_ref, s_ref, c_ref, g1_ref, sh2_ref, sc2_ref, g2_ref, n2_ref,
                   wo_ref, wg_ref, wu_ref, wd_ref, fg_ref, o_ref, *, final_norm, ff_chunk):
    tm = x_ref.shape[0]
    o1 = ATTN_WIDTH
    o2 = o1 + SSM_WIDTH
    proj = (_mm(a_ref[...], wo_ref[0:o1, :]) + _mm(s_ref[...], wo_ref[o1:o2, :])
            + _mm(c_ref[...], wo_ref[o2:o2 + CONV_WIDTH, :]))
    x2 = x_ref[...] + _rows(g1_ref[...], tm) * proj
    h2 = _rms(x2, n2_ref[...]) * (1.0 + _rows(sc2_ref[...], tm)) + _rows(sh2_ref[...], tm)
    h2 = h2.astype(MXU_DTYPE)
    ffn = jnp.zeros(x2.shape, F32)
    for c in range(D_FF // ff_chunk):
        cs = slice(c * ff_chunk, (c + 1) * ff_chunk)
        act = _silu(_mm(h2, wg_ref[:, cs])) * _mm(h2, wu_ref[:, cs])
        ffn = ffn + _mm(act.astype(MXU_DTYPE), wd_ref[cs, :])
    y = x2 + _rows(g2_ref[...], tm) * ffn
    if final_norm:
        y = _rms(y, fg_ref[...])
    o_ref[...] = y


def _out_ffn(x, o_attn, o_ssm, o_conv, mod, p, layer, final_norm, tm, tiles_per_batch):
    t = x.shape[0]
    row = lambda i: (i, 0)
    resident = lambda shape: _layer_block(shape, layer, pipeline_mode=pl.Buffered(1))
    return pl.pallas_call(
        functools.partial(_outffn_kernel, final_norm=final_norm, ff_chunk=D_FF // 2),
        grid=(t // tm,),
        in_specs=[
            pl.BlockSpec((tm, D_MODEL), row),
            pl.BlockSpec((tm, ATTN_WIDTH), row),
            pl.BlockSpec((tm, SSM_WIDTH), row),
            pl.BlockSpec((tm, CONV_WIDTH), row),
            _mod_spec(mod, layer, 2, tiles_per_batch),
            _mod_spec(mod, layer, 3, tiles_per_batch),
            _mod_spec(mod, layer, 4, tiles_per_batch),
            _mod_spec(mod, layer, 5, tiles_per_batch),
            _layer_block((1, D_MODEL), layer),
            resident((D_MODEL, D_MODEL)),
            resident((D_MODEL, D_FF)),
            resident((D_MODEL, D_FF)),
            resident((D_FF, D_MODEL)),
            pl.BlockSpec((1, D_MODEL), lambda i: (0, 0)),
        ],
        out_specs=pl.BlockSpec((tm, D_MODEL), row),
        out_shape=jax.ShapeDtypeStruct((t, D_MODEL), F32),
        compiler_params=_params("parallel"),
        name="out_projection_ffn",
    )(x, o_attn, o_ssm, o_conv, mod, mod, mod, mod, p['norm2_g'],
      p['w_out'], p['w_gate'], p['w_up'], p['w_down'], p['final_g'])


def _layer_prompt(x, mod, p, layer, rope, batch, seq, final_norm):
    tm = min(512, seq)
    tiles_per_batch = seq // tm
    q, kv, u, vc = _in_projection(x, mod, p, layer, rope[0], rope[1], tm, tiles_per_batch)
    o_attn = _attention_prompt(q, kv, p['sinks'], layer, batch, seq)
    n_keep = min(WINDOW, seq)
    kv_keep = kv.reshape(batch, seq, 2 * KV_WIDTH)[:, seq - n_keep:]
    new_k = kv_keep[:, :, :KV_WIDTH].reshape(batch, n_keep, N_KV_HEADS, HEAD_DIM)
    new_v = kv_keep[:, :, KV_WIDTH:].reshape(batch, n_keep, N_KV_HEADS, HEAD_DIM)

    nseg = PROMPT_SEGMENTS
    lseg = seq // nseg
    nseq = batch * nseg
    u_tm = jnp.transpose(u.reshape(nseq, lseg, SSM_WIDTH), (1, 0, 2)).reshape(lseg * nseq, SSM_WIDTH)
    zero = jnp.zeros((nseq, SSM_FLAT), F32)
    steps = min(128, lseg)
    end_re, end_im = _s5_scan(u_tm, zero, zero, p, layer, False, nseq, steps)

    def shift(e):
        e = e.reshape(batch, nseg, SSM_FLAT)
        return jnp.concatenate([jnp.zeros_like(e[:, :1]), e[:, :1]], axis=1).reshape(nseq, SSM_FLAT)

    o_tm, h_re, h_im = _s5_scan(u_tm, shift(end_re), shift(end_im), p, layer, True, nseq, steps)
    o_ssm = jnp.transpose(o_tm.reshape(lseg, nseq, SSM_WIDTH), (1, 0, 2)).reshape(batch * seq, SSM_WIDTH)
    last = lambda h: h.reshape(batch, nseg, N_SSM_GROUPS, SSM_STATE)[:, nseg - 1]

    o_conv = _conv_prompt(vc, p, layer, seq, min(512, seq))
    new_conv = vc.reshape(batch, seq, CONV_WIDTH)[:, seq - (CONV_K - 1):]

    x = _out_ffn(x, o_attn, o_ssm, o_conv, mod, p, layer, final_norm, tm, tiles_per_batch)
    return x, new_k, new_v, last(h_re), last(h_im), new_conv


def _layer_sample(x, mod, p, layer, rope, batch, n_new, kt_all, vt_all, h0_re, h0_im, conv_all, final_norm):
    t = batch * n_new
    q, kv, u, vc = _in_projection(x, mod, p, layer, rope[0], rope[1], t, 1)

    q4 = jnp.transpose(q.reshape(n_new, batch, N_Q_HEADS, HEAD_DIM), (1, 2, 0, 3))
    q4 = q4.reshape(batch, N_KV_HEADS, Q_PER_KV * n_new, HEAD_DIM)
    zq = jnp.zeros_like(q4[:, 0])
    qexp = jnp.concatenate([jnp.concatenate([q4[:, 0], zq], axis=-1),
                            jnp.concatenate([zq, q4[:, 1]], axis=-1)], axis=1)
    kv5 = jnp.transpose(kv.reshape(n_new, batch, 2, N_KV_HEADS, HEAD_DIM), (2, 1, 3, 4, 0))
    kv5 = jnp.pad(kv5, ((0, 0),) * 4 + ((kt_all.shape[-1] - n_new, 0),))
    o, new_k, new_v = _attention_sample(qexp, kt_all, vt_all, kv5[0], kv5[1], p['sinks'], layer, n_new)
    o = o[:, :, :HEAD_DIM].reshape(batch, N_Q_HEADS, n_new, HEAD_DIM)
    o_attn = jnp.transpose(o, (2, 0, 1, 3)).reshape(t, ATTN_WIDTH)

    o_ssm, h_re, h_im = _s5_scan(u, h0_re.reshape(batch, SSM_FLAT), h0_im.reshape(batch, SSM_FLAT),
                                 p, layer, True, batch, n_new)
    st = lambda h: h.reshape(batch, N_SSM_GROUPS, SSM_STATE)

    o_conv, new_conv = _conv_sample(conv_all, vc.reshape(n_new, batch, CONV_WIDTH), p, layer)
    o_conv = o_conv.reshape(t, CONV_WIDTH)

    x = _out_ffn(x, o_attn, o_ssm, o_conv, mod, p, layer, final_norm, t, 1)
    return x, new_k, new_v, st(h_re), st(h_im), new_conv


def kernel(x_prompt, x_sample, c_prompt, c_sample, cache_k, cache_v, state_ssm_re, state_ssm_im, state_conv,
           norm1_g, norm2_g, w_mod, b_mod, w_in, attn_sinks, ssm_lam_re, ssm_lam_im, ssm_log_dt,
           ssm_b_re, ssm_b_im, ssm_c_re, ssm_c_im, ssm_d, ssm_w_glu, ssm_b_glu,
           conv_w, conv_b, conv_ln_g, conv_ln_b, w_out, w_gate, w_up, w_down, final_norm_g):
    bp, seq, d = x_prompt.shape
    bs, n_new, _ = x_sample.shape
    depth = w_in.shape[0]
    assert PROMPT_SEGMENTS == 2 and seq % (PROMPT_SEGMENTS * SUBLANES) == 0

    c_all = jnp.concatenate([c_prompt, c_sample], axis=0)
    pad_rows = -c_all.shape[0] % SUBLANES
    mods = _modulation(jnp.pad(c_all, ((0, pad_rows), (0, 0))), w_mod, b_mod)
    mod_p = mods[:, :bp].reshape(depth, bp, 1, N_MOD * d)
    mod_s = mods[:, bp:bp + bs]

    a_re, a_im, bb_re, bb_im = _s5_discretise(ssm_lam_re, ssm_lam_im, ssm_log_dt, ssm_b_re, ssm_b_im)
    rope_p = _rope_tables(jnp.arange(seq))
    rope_s = _rope_tables(jnp.repeat(PAST_LEN + jnp.arange(n_new), bs))

    cast = lambda a: a.astype(MXU_DTYPE)
    vec = lambda a: a.reshape(depth, 1, a.shape[-1])
    p = {
        'norm1_g': vec(norm1_g), 'norm2_g': vec(norm2_g), 'w_in': cast(w_in), 'sinks': attn_sinks,
        'a_re': a_re, 'a_im': a_im,
        'bre_blk': cast(_group_block_diag_in(bb_re)), 'bim_blk': cast(_group_block_diag_in(bb_im)),
        'cre_blk': cast(_group_block_diag_out(ssm_c_re)), 'cim_blk': cast(_group_block_diag_out(ssm_c_im)),
        'ssm_d': vec(ssm_d), 'ssm_w_glu': cast(ssm_w_glu), 'ssm_b_glu': vec(ssm_b_glu),
        'conv_w': conv_w, 'conv_b': vec(conv_b), 'conv_ln_g': vec(conv_ln_g), 'conv_ln_b': vec(conv_ln_b),
        'w_out': cast(w_out), 'w_gate': cast(w_gate), 'w_up': cast(w_up), 'w_down': cast(w_down),
        'final_g': final_norm_g.reshape(1, d),
    }
    conv_all = jnp.transpose(state_conv, (0, 2, 1, 3))
    kt_all = jnp.transpose(cache_k, (0, 1, 3, 4, 2))
    vt_all = jnp.transpose(cache_v, (0, 1, 3, 4, 2))

    xp = x_prompt.reshape(bp * seq, d)
    xs = jnp.transpose(x_sample, (1, 0, 2)).reshape(n_new * bs, d)
    outs_p, outs_s = [], []
    for l in range(depth):
        final = l == depth - 1
        xp, *op = _layer_prompt(xp, mod_p, p, l, rope_p, bp, seq, final)
        xs, *os_ = _layer_sample(xs, mod_s, p, l, rope_s, bs, n_new, kt_all, vt_all,
                                 state_ssm_re[l], state_ssm_im[l], conv_all, final)
        outs_p.append(op)
        outs_s.append(os_)
    stack = lambda outs, i: jnp.stack([o[i] for o in outs])
    y_sample = jnp.transpose(xs.reshape(n_new, bs, d), (1, 0, 2))
    new_k_s = jnp.transpose(stack(outs_s, 0), (0, 1, 4, 2, 3))
    new_v_s = jnp.transpose(stack(outs_s, 1), (0, 1, 4, 2, 3))
    new_conv_s = jnp.transpose(stack(outs_s, 4), (0, 2, 1, 3))
    return (xp.reshape(bp, seq, d), y_sample, *[stack(outs_p, i) for i in range(5)],
            new_k_s, new_v_s, stack(outs_s, 2), stack(outs_s, 3), new_conv_s)
```

```python
import functools
import math
from typing import NamedTuple

import jax
import jax.numpy as jnp
from jax import lax
from jax.experimental import pallas as pl
from jax.experimental.pallas import tpu as pltpu

F32 = jnp.float32
MXU_DTYPE = jnp.bfloat16

V7X_VMEM_BYTES = 64 * 1024 * 1024
VMEM_LIMIT_BYTES = V7X_VMEM_BYTES - 8 * 1024 * 1024
LANES = 128
SUBLANES = 8
V7X_MXU_DIM = 256

D_MODEL = 1024
HEAD_DIM = 64
N_Q_HEADS = 8
N_KV_HEADS = 2
Q_PER_KV = N_Q_HEADS // N_KV_HEADS
ATTN_WIDTH = N_Q_HEADS * HEAD_DIM
KV_WIDTH = N_KV_HEADS * HEAD_DIM
WINDOW = 128
ROPE_THETA = 10000.0
ATTN_SCALE = 1.0 / math.sqrt(HEAD_DIM)
SSM_WIDTH = 256
SSM_GROUP = 16
N_SSM_GROUPS = 16
SSM_STATE = 64
SSM_FLAT = N_SSM_GROUPS * SSM_STATE
CONV_WIDTH = 256
CONV_K = 31
CONV_HALO = 32
IN_WIDTH = ATTN_WIDTH + 2 * KV_WIDTH + SSM_WIDTH + 2 * CONV_WIDTH
D_FF = 2816
FF_CHUNK = 6 * V7X_MXU_DIM
EPS = 1e-6
NEG = -1e30
N_MOD = 6
PROMPT_SEGMENTS = 2
PAST_LEN = 8192


def _params(*semantics):
    return pltpu.CompilerParams(dimension_semantics=semantics, vmem_limit_bytes=VMEM_LIMIT_BYTES)


def _layer_block(shape, layer, **kw):
    zeros = (0,) * len(shape)
    return pl.BlockSpec((None,) + tuple(shape), lambda *_: (layer,) + zeros, **kw)


def _sigmoid(x):
    return 1.0 / (1.0 + jnp.exp(-x))


def _silu(x):
    return x * _sigmoid(x)


def _rms(x, g):
    return x * lax.rsqrt(jnp.mean(x * x, axis=-1, keepdims=True) + EPS) * g


def _mm(a, b):
    return jnp.dot(a, b, preferred_element_type=F32)


def _mm_nt(a, b):
    return lax.dot_general(a, b, (((1,), (1,)), ((), ())), preferred_element_type=F32)


def _rows(m, n):
    return m if m.shape[0] == 1 else jnp.concatenate([m] * (n // m.shape[0]), axis=0)


def _mod_kernel(c_ref, w_ref, b_ref, o_ref):
    a = _silu(c_ref[...]).astype(MXU_DTYPE)
    o_ref[...] = _mm(a, w_ref[...].astype(MXU_DTYPE)) + b_ref[...]


def _modulation(c, w_mod, b_mod):
    depth, d, n = w_mod.shape
    rows = c.shape[0]
    tn = 1536
    return pl.pallas_call(
        _mod_kernel,
        grid=(depth, n // tn),
        in_specs=[
            pl.BlockSpec((rows, d), lambda l, j: (0, 0)),
            pl.BlockSpec((None, d, tn), lambda l, j: (l, 0, j)),
            pl.BlockSpec((None, 1, tn), lambda l, j: (l, 0, j)),
        ],
        out_specs=pl.BlockSpec((None, rows, tn), lambda l, j: (l, 0, j)),
        out_shape=jax.ShapeDtypeStruct((depth, rows, n), F32),
        compiler_params=_params("parallel", "parallel"),
        name="modulation",
    )(c, w_mod, b_mod.reshape(depth, 1, n))


class _Mod(NamedTuple):
    array: jax.Array
    batch_rows: int | None


def _mod_spec(mod, layer, chunk, tiles_per_batch):
    if mod.batch_rows is None:
        return pl.BlockSpec((None, None, 1, D_MODEL), lambda i: (layer, i // tiles_per_batch, 0, chunk))
    return pl.BlockSpec((None, mod.batch_rows, D_MODEL), lambda i: (layer, 0, chunk))


def _inproj_kernel(x_ref, sh_ref, sc_ref, g_ref, w_ref, cos_ref, sin_ref,
                   q_ref, kv_ref, u_ref, vc_ref):
    tm = x_ref.shape[0]
    h = _rms(x_ref[...], g_ref[...]) * (1.0 + _rows(sc_ref[...], tm)) + _rows(sh_ref[...], tm)
    z = _mm(h.astype(MXU_DTYPE), w_ref[...])
    cos = cos_ref[...]
    sin = sin_ref[...]
    lane = lax.broadcasted_iota(jnp.int32, (tm, LANES), 1)
    first_half = (lane % HEAD_DIM) < (HEAD_DIM // 2)

    def rope(t):
        partner = jnp.where(first_half, pltpu.roll(t, LANES - HEAD_DIM // 2, 1),
                            pltpu.roll(t, HEAD_DIM // 2, 1))
        return t * cos + partner * sin

    for j in range(ATTN_WIDTH // LANES):
        sl = slice(j * LANES, (j + 1) * LANES)
        q_ref[:, sl] = (rope(z[:, sl]) * ATTN_SCALE).astype(q_ref.dtype)
    o = ATTN_WIDTH
    kv_ref[:, 0:KV_WIDTH] = rope(z[:, o:o + KV_WIDTH])
    kv_ref[:, KV_WIDTH:2 * KV_WIDTH] = z[:, o + KV_WIDTH:o + 2 * KV_WIDTH]
    o += 2 * KV_WIDTH
    u_ref[...] = z[:, o:o + SSM_WIDTH]
    o += SSM_WIDTH
    za = z[:, o:o + CONV_WIDTH]
    zg = z[:, o + CONV_WIDTH:o + 2 * CONV_WIDTH]
    vc_ref[...] = za * _sigmoid(zg)


def _in_projection(x, mod, p, layer, cos, sin, tm, tiles_per_batch):
    t = x.shape[0]
    pos_tiles = cos.shape[0] // tm
    row = lambda i: (i, 0)
    return pl.pallas_call(
        _inproj_kernel,
        grid=(t // tm,),
        in_specs=[
            pl.BlockSpec((tm, D_MODEL), row),
            _mod_spec(mod, layer, 0, tiles_per_batch),
            _mod_spec(mod, layer, 1, tiles_per_batch),
            _layer_block((1, D_MODEL), layer),
            _layer_block((D_MODEL, IN_WIDTH), layer),
            pl.BlockSpec((tm, LANES), lambda i: (i % pos_tiles, 0)),
            pl.BlockSpec((tm, LANES), lambda i: (i % pos_tiles, 0)),
        ],
        out_specs=[
            pl.BlockSpec((tm, ATTN_WIDTH), row),
            pl.BlockSpec((tm, 2 * KV_WIDTH), row),
            pl.BlockSpec((tm, SSM_WIDTH), row),
            pl.BlockSpec((tm, CONV_WIDTH), row),
        ],
        out_shape=[
            jax.ShapeDtypeStruct((t, ATTN_WIDTH), MXU_DTYPE),
            jax.ShapeDtypeStruct((t, 2 * KV_WIDTH), F32),
            jax.ShapeDtypeStruct((t, SSM_WIDTH), F32),
            jax.ShapeDtypeStruct((t, CONV_WIDTH), F32),
        ],
        compiler_params=_params("parallel"),
        name="in_projection",
    )(x, mod.array, mod.array, p['norm1_g'], p['w_in'], cos, sin)


def _rope_tables(pos):
    half = HEAD_DIM // 2
    inv_freq = ROPE_THETA ** (-jnp.arange(half, dtype=F32) / half)
    ang = pos.astype(F32)[:, None] * inv_freq[None, :]
    cos = jnp.tile(jnp.cos(ang), (1, LANES // half))
    sin = jnp.sin(ang)
    sin = jnp.tile(jnp.concatenate([-sin, sin], axis=1), (1, LANES // HEAD_DIM))
    return cos, sin


def _sink_softmax(s, sink_col):
    m = jnp.maximum(jnp.max(s, axis=-1, keepdims=True), sink_col)
    e = jnp.exp(s - m)
    return e * (1.0 / (jnp.sum(e, axis=-1, keepdims=True) + jnp.exp(sink_col - m)))


def _head_pair_select(x, pick_second):
    lane = lax.broadcasted_iota(jnp.int32, x.shape, 1)
    swapped = pltpu.roll(x, HEAD_DIM, 1)
    low = lane < HEAD_DIM
    return jnp.where(low, swapped, x) if pick_second else jnp.where(low, x, swapped)


def _attn_prompt_kernel(sink_ref, q_ref, kvp_ref, kvc_ref, o_ref, *, layer):
    n = pl.program_id(1)
    w = WINDOW
    n_blocks = q_ref.shape[0] // w
    rows = Q_PER_KV * w
    r_idx = lax.broadcasted_iota(jnp.int32, (rows, w), 0) % w
    c_idx = lax.broadcasted_iota(jnp.int32, (rows, w), 1)
    from_prev = c_idx > r_idx
    first_bias = jnp.where(n > 0, 0.0, NEG)
    row_head = lax.broadcasted_iota(jnp.int32, (rows, 1), 0) // w
    low = lax.broadcasted_iota(jnp.int32, (w, LANES), 1) < HEAD_DIM
    half_mask = [jnp.where(low, 1.0, 0.0).astype(MXU_DTYPE), jnp.where(low, 0.0, 1.0).astype(MXU_DTYPE)]
    sink_cols = []
    for kvh in range(N_KV_HEADS):
        sink_col = jnp.zeros((rows, 1), F32)
        for g in range(Q_PER_KV):
            sink_col = jnp.where(row_head == g, sink_ref[layer, kvh * Q_PER_KV + g], sink_col)
        sink_cols.append(sink_col)
    chains = [(i, kvh) for i in range(n_blocks) for kvh in range(N_KV_HEADS)]
    windows, values = [], []
    for i, kvh in chains:
        own = slice(i * w, (i + 1) * w)
        prev_ref, prev = (kvp_ref, slice(0, w)) if i == 0 else (kvc_ref, slice((i - 1) * w, i * w))
        kk = jnp.concatenate([prev_ref[prev, 0:KV_WIDTH], kvc_ref[own, 0:KV_WIDTH]], axis=0)
        vv = jnp.concatenate([prev_ref[prev, KV_WIDTH:2 * KV_WIDTH], kvc_ref[own, KV_WIDTH:2 * KV_WIDTH]], axis=0)
        k2 = _head_pair_select(kk, kvh == 1).astype(MXU_DTYPE)
        values.append(_head_pair_select(vv, kvh == 1).astype(MXU_DTYPE))
        pieces = []
        for g in range(Q_PER_KV):
            h = kvh * Q_PER_KV + g
            qcol = q_ref[own, (h // 2) * LANES:(h // 2 + 1) * LANES]
            pieces.append(qcol * half_mask[h % 2])
        s = _mm_nt(jnp.concatenate(pieces, axis=0), k2)
        s_prev = s[:, 0:w] + first_bias if i == 0 else s[:, 0:w]
        windows.append(jnp.where(from_prev, s_prev, s[:, w:2 * w]))
    probs = []
    for (i, kvh), sc in zip(chains, windows):
        p = _sink_softmax(sc, sink_cols[kvh])
        p2 = jnp.concatenate([jnp.where(from_prev, p, 0.0), jnp.where(from_prev, 0.0, p)], axis=1)
        probs.append(p2.astype(MXU_DTYPE))
    for (i, kvh), p2, v2 in zip(chains, probs, values):
        own = slice(i * w, (i + 1) * w)
        r = _mm(p2, v2)
        for j in range(Q_PER_KV // 2):
            col = kvh * (Q_PER_KV // 2) + j
            o_ref[own, col * LANES:(col + 1) * LANES] = jnp.where(
                low, r[2 * j * w:(2 * j + 1) * w], r[(2 * j + 1) * w:(2 * j + 2) * w]).astype(o_ref.dtype)


def _attention_prompt(q, kv, sinks, layer, batch, seq):
    tq = min(4 * WINDOW, seq)
    nt = seq // tq
    per = tq // WINDOW
    return pl.pallas_call(
        functools.partial(_attn_prompt_kernel, layer=layer),
        grid=(batch, nt),
        in_specs=[
            pl.BlockSpec(memory_space=pltpu.SMEM),
            pl.BlockSpec((tq, ATTN_WIDTH), lambda b, n: (b * nt + n, 0)),
            pl.BlockSpec((WINDOW, 2 * KV_WIDTH), lambda b, n: ((b * nt + n) * per - jnp.minimum(n, 1), 0)),
            pl.BlockSpec((tq, 2 * KV_WIDTH), lambda b, n: (b * nt + n, 0)),
        ],
        out_specs=pl.BlockSpec((tq, ATTN_WIDTH), lambda b, n: (b * nt + n, 0)),
        out_shape=jax.ShapeDtypeStruct((batch * seq, ATTN_WIDTH), MXU_DTYPE),
        compiler_params=_params("parallel", "parallel"),
        name="attention_prompt",
    )(sinks, q, kv, kv)


def _attn_sample_kernel(sink_ref, q_ref, kt_ref, vt_ref, knt_ref, vnt_ref, o_ref, nk_ref, nv_ref, *, layer, n_new):
    bt, rows, _ = q_ref.shape
    nkv, hd, wb = kt_ref.shape[1:]
    kept = wb - n_new
    r = lax.broadcasted_iota(jnp.int32, (bt * rows, 2 * wb), 0)
    j = lax.broadcasted_iota(jnp.int32, (bt * rows, 2 * wb), 1)
    t_idx = r % n_new
    t_new = j - wb - kept
    mask = ((j < wb) & (j > t_idx)) | ((t_new >= 0) & (t_new <= t_idx))
    row_head = (lax.broadcasted_iota(jnp.int32, (bt * rows, 1), 0) % rows) // n_new
    sink_col = jnp.zeros((bt * rows, 1), F32)
    for h in range(N_Q_HEADS):
        sink_col = jnp.where(row_head == h, sink_ref[layer, h], sink_col)
    keep = lax.broadcasted_iota(jnp.int32, (nkv, hd, wb), 2) < kept
    second_group = lax.broadcasted_iota(jnp.int32, (rows, nkv * hd), 0) >= (rows // nkv)
    flat = lambda a: a.reshape(nkv * hd, wb)

    scores, values = [], []
    for b in range(bt):
        kt, kpos = kt_ref[b], knt_ref[b]
        vt, vpos = vt_ref[b], vnt_ref[b]
        nk_ref[b] = jnp.where(keep, pltpu.roll(kt, kept, 2), kpos)
        nv_ref[b] = jnp.where(keep, pltpu.roll(vt, kept, 2), vpos)
        kcat = jnp.concatenate([flat(kt), flat(kpos)], axis=1).astype(MXU_DTYPE)
        values.append(jnp.concatenate([flat(vt), flat(vpos)], axis=1).astype(MXU_DTYPE))
        scores.append(_mm(q_ref[b], kcat))
    s = jnp.where(mask, jnp.concatenate(scores, axis=0), NEG)
    p = _sink_softmax(s, sink_col).astype(MXU_DTYPE)
    for b in range(bt):
        o = _mm_nt(p[b * rows:(b + 1) * rows], values[b])
        o_ref[b] = jnp.where(second_group, pltpu.roll(o, hd, 1), o).astype(o_ref.dtype)


def _attention_sample(qexp, kt_all, vt_all, knt, vnt, sinks, layer, n_new):
    batch, rows, width = qexp.shape
    _, _, nkv, hd, wb = kt_all.shape
    bt = min(8, batch)
    blk3 = pl.BlockSpec((bt, rows, width), lambda i: (i, 0, 0))
    blk4 = pl.BlockSpec((bt, nkv, hd, wb), lambda i: (i, 0, 0, 0))
    cache = pl.BlockSpec((None, bt, nkv, hd, wb), lambda i: (layer, i, 0, 0, 0))
    return pl.pallas_call(
        functools.partial(_attn_sample_kernel, layer=layer, n_new=n_new),
        grid=(batch // bt,),
        in_specs=[pl.BlockSpec(memory_space=pltpu.SMEM), blk3, cache, cache, blk4, blk4],
        out_specs=[blk3, blk4, blk4],
        out_shape=[jax.ShapeDtypeStruct((batch, rows, width), MXU_DTYPE),
                   jax.ShapeDtypeStruct((batch, nkv, hd, wb), F32),
                   jax.ShapeDtypeStruct((batch, nkv, hd, wb), F32)],
        compiler_params=_params("parallel"),
        name="attention_sample",
    )(sinks, qexp, kt_all, vt_all, knt, vnt)


def _s5_discretise_kernel(lr_ref, li_ref, ldt_ref, bre_ref, bim_ref, are_ref, aim_ref, bbre_ref, bbim_ref):
    lr = lr_ref[...]
    li = li_ref[...]
    dt = jnp.exp(ldt_ref[...])
    mag = jnp.exp(lr * dt)
    ab_re = mag * jnp.cos(li * dt)
    ab_im = mag * jnp.sin(li * dt)
    den = lr * lr + li * li
    nr = ab_re - 1.0
    coef_re = (nr * lr + ab_im * li) / den
    coef_im = (ab_im * lr - nr * li) / den
    are_ref[...] = ab_re
    aim_ref[...] = ab_im
    br = bre_ref[...]
    bi = bim_ref[...]
    bb_re = coef_re * br - coef_im * bi
    bb_im = coef_re * bi + coef_im * br
    lane_group = lax.broadcasted_iota(jnp.int32, bb_re.shape, 1) // SSM_STATE
    for g in range(N_SSM_GROUPS):
        rows = slice(g * SSM_GROUP, (g + 1) * SSM_GROUP)
        bbre_ref[rows, :] = jnp.where(lane_group == g, bb_re, 0.0).astype(bbre_ref.dtype)
        bbim_ref[rows, :] = jnp.where(lane_group == g, bb_im, 0.0).astype(bbim_ref.dtype)


def _s5_discretise(lam_re, lam_im, log_dt, b_re, b_im):
    depth = lam_re.shape[0]
    flat = lambda a: a.reshape(depth, 1, SSM_FLAT)
    ldt = jnp.broadcast_to(log_dt[:, :, None], lam_re.shape)
    bt = lambda a: jnp.transpose(a, (0, 3, 1, 2)).reshape(depth, SSM_GROUP, SSM_FLAT)
    vec = pl.BlockSpec((None, 1, SSM_FLAT), lambda l: (l, 0, 0))
    mat = pl.BlockSpec((None, SSM_GROUP, SSM_FLAT), lambda l: (l, 0, 0))
    blk = pl.BlockSpec((None, SSM_WIDTH, SSM_FLAT), lambda l: (l, 0, 0))
    return pl.pallas_call(
        _s5_discretise_kernel,
        grid=(depth,),
        in_specs=[vec, vec, vec, mat, mat],
        out_specs=[vec, vec, blk, blk],
        out_shape=[jax.ShapeDtypeStruct((depth, 1, SSM_FLAT), F32)] * 2
        + [jax.ShapeDtypeStruct((depth, SSM_WIDTH, SSM_FLAT), MXU_DTYPE)] * 2,
        compiler_params=_params("parallel"),
        name="s5_discretise",
    )(flat(lam_re), flat(lam_im), flat(ldt), bt(b_re), bt(b_im))


def _group_block_diag_out(c):
    depth = c.shape[0]
    eye = jnp.eye(N_SSM_GROUPS, dtype=c.dtype)
    full = eye[None, :, None, :, None] * jnp.transpose(c, (0, 1, 3, 2))[:, :, :, None, :]
    return full.reshape(depth, SSM_FLAT, SSM_WIDTH)


def _s5_kernel(u_ref, hre0_ref, him0_ref, are_ref, aim_ref, bre_ref, bim_ref, *rest, nseq, emit):
    if emit:
        (cre_ref, cim_ref, d_ref, wglu_ref, bglu_ref,
         o_ref, hre_out, him_out, hre_s, him_s, bure_s, buim_s, hsre_s, hsim_s) = rest
    else:
        hre_out, him_out, hre_s, him_s, bure_s, buim_s = rest
    steps = u_ref.shape[0] // nseq

    @pl.when(pl.program_id(0) == 0)
    def _():
        hre_s[...] = hre0_ref[...]
        him_s[...] = him0_ref[...]

    u = u_ref[...]
    ub = u.astype(MXU_DTYPE)
    bure_s[...] = _mm(ub, bre_ref[...])
    buim_s[...] = _mm(ub, bim_ref[...])
    ar = jnp.broadcast_to(are_ref[...], (nseq, SSM_FLAT))
    ai = jnp.broadcast_to(aim_ref[...], (nseq, SSM_FLAT))

    def step(t, carry):
        hr, hi = carry
        rows = pl.ds(pl.multiple_of(t * nseq, nseq), nseq)
        nhr = ar * hr - ai * hi + bure_s[rows, :]
        nhi = ar * hi + ai * hr + buim_s[rows, :]
        if emit:
            hsre_s[rows, :] = nhr
            hsim_s[rows, :] = nhi
        return nhr, nhi

    hr, hi = lax.fori_loop(0, steps, step, (hre_s[...], him_s[...]), unroll=min(steps, 8))
    hre_s[...] = hr
    him_s[...] = hi
    hre_out[...] = hr
    him_out[...] = hi
    if emit:
        y = _mm(hsre_s[...].astype(MXU_DTYPE), cre_ref[...]) - _mm(hsim_s[...].astype(MXU_DTYPE), cim_ref[...])
        z = jax.nn.gelu(y + d_ref[...] * u)
        gate = _mm(z.astype(MXU_DTYPE), wglu_ref[...]) + bglu_ref[...]
        o_ref[...] = (z * _sigmoid(gate)).astype(o_ref.dtype)


def _s5_scan(u_tm, hre0, him0, p, layer, emit, nseq, steps_per_tile):
    rows = u_tm.shape[0]
    tr = steps_per_tile * nseq
    fixed = lambda i: (0, 0)
    state = pl.BlockSpec((nseq, SSM_FLAT), fixed)
    in_specs = [pl.BlockSpec((tr, SSM_WIDTH), lambda i: (i, 0)), state, state,
                _layer_block((1, SSM_FLAT), layer), _layer_block((1, SSM_FLAT), layer),
                _layer_block((SSM_WIDTH, SSM_FLAT), layer), _layer_block((SSM_WIDTH, SSM_FLAT), layer)]
    args = [u_tm, hre0, him0, p['a_re'], p['a_im'], p['bre_blk'], p['bim_blk']]
    out_specs = [state, state]
    out_shape = [jax.ShapeDtypeStruct((nseq, SSM_FLAT), F32)] * 2
    scratch = [pltpu.VMEM((nseq, SSM_FLAT), F32)] * 2 + [pltpu.VMEM((tr, SSM_FLAT), F32)] * 2
    if emit:
        in_specs += [_layer_block((SSM_FLAT, SSM_WIDTH), layer), _layer_block((SSM_FLAT, SSM_WIDTH), layer),
                     _layer_block((1, SSM_WIDTH), layer), _layer_block((SSM_WIDTH, SSM_WIDTH), layer),
                     _layer_block((1, SSM_WIDTH), layer)]
        args += [p['cre_blk'], p['cim_blk'], p['ssm_d'], p['ssm_w_glu'], p['ssm_b_glu']]
        out_specs = [pl.BlockSpec((tr, SSM_WIDTH), lambda i: (i, 0))] + out_specs
        out_shape = [jax.ShapeDtypeStruct((rows, SSM_WIDTH), MXU_DTYPE)] + out_shape
        scratch += [pltpu.VMEM((tr, SSM_FLAT), F32)] * 2
    return pl.pallas_call(
        functools.partial(_s5_kernel, nseq=nseq, emit=emit),
        grid=(rows // tr,),
        in_specs=in_specs,
        out_specs=out_specs,
        out_shape=out_shape,
        scratch_shapes=scratch,
        compiler_params=_params("arbitrary"),
        name="s5_scan" if emit else "s5_segment_states",
    )(*args)


def _layernorm_silu(y, g, b):
    yc = y - jnp.mean(y, axis=-1, keepdims=True)
    var = jnp.mean(yc * yc, axis=-1, keepdims=True)
    return _silu(yc * lax.rsqrt(var + EPS) * g + b)


def _conv_prompt_kernel(prev_ref, cur_ref, w_ref, b_ref, g_ref, beta_ref, o_ref, buf, shifted,
                        *, tiles_per_batch, sub):
    tc = cur_ref.shape[0]
    first = (pl.program_id(0) % tiles_per_batch) == 0
    buf[0:CONV_HALO, :] = jnp.where(first, 0.0, prev_ref[...])
    buf[CONV_HALO:CONV_HALO + tc, :] = cur_ref[...]
    base = CONV_HALO - (CONV_K - 1)
    span = shifted.shape[1]
    for s in range(1, SUBLANES):
        shifted[s - 1] = buf[s:s + span, :]
    for r in range(tc // sub):
        acc = jnp.zeros((sub, CONV_WIDTH), F32)
        for k in range(CONV_K):
            off = base + k
            j, s = off // SUBLANES, off % SUBLANES
            start = r * sub + j * SUBLANES
            tap = buf[start:start + sub, :] if s == 0 else shifted[s - 1, start:start + sub, :]
            acc = acc + w_ref[k:k + 1, :] * tap
        y = _layernorm_silu(acc + b_ref[...], g_ref[...], beta_ref[...])
        o_ref[r * sub:(r + 1) * sub, :] = y.astype(o_ref.dtype)


def _conv_prompt(vc, p, layer, seq, tc):
    t = vc.shape[0]
    tiles_per_batch = seq // tc
    halo_per_tile = tc // CONV_HALO
    vec = _layer_block((1, CONV_WIDTH), layer)
    span = tc + CONV_HALO - SUBLANES
    return pl.pallas_call(
        functools.partial(_conv_prompt_kernel, tiles_per_batch=tiles_per_batch, sub=64),
        grid=(t // tc,),
        in_specs=[
            pl.BlockSpec((CONV_HALO, CONV_WIDTH), lambda i: (jnp.maximum(i * halo_per_tile - 1, 0), 0)),
            pl.BlockSpec((tc, CONV_WIDTH), lambda i: (i, 0)),
            _layer_block((CONV_K, CONV_WIDTH), layer), vec, vec, vec,
        ],
        out_specs=pl.BlockSpec((tc, CONV_WIDTH), lambda i: (i, 0)),
        out_shape=jax.ShapeDtypeStruct((t, CONV_WIDTH), MXU_DTYPE),
        scratch_shapes=[pltpu.VMEM((CONV_HALO + tc, CONV_WIDTH), F32),
                        pltpu.VMEM((SUBLANES - 1, span, CONV_WIDTH), F32)],
        compiler_params=_params("parallel"),
        name="conv_prompt",
    )(vc, vc, p['conv_w'], p['conv_b'], p['conv_ln_g'], p['conv_ln_b'])


def _conv_sample_kernel(state_ref, v_ref, w_ref, b_ref, g_ref, beta_ref, o_ref, ns_ref):
    ns = state_ref.shape[0]
    n_new = v_ref.shape[0]
    row = lambda j: state_ref[j] if j < ns else v_ref[j - ns]
    for t in range(n_new):
        acc = jnp.zeros(o_ref.shape[1:], F32)
        for k in range(CONV_K):
            acc = acc + w_ref[k:k + 1, :] * row(t + k)
        o_ref[t] = _layernorm_silu(acc + b_ref[...], g_ref[...], beta_ref[...]).astype(o_ref.dtype)
    for j in range(ns):
        ns_ref[j] = row(j + n_new)


def _conv_sample(state_all, v, p, layer):
    _, ns, batch, width = state_all.shape
    n_new = v.shape[0]
    bt = min(32, batch)
    vec = _layer_block((1, width), layer)
    return pl.pallas_call(
        _conv_sample_kernel,
        grid=(batch // bt,),
        in_specs=[
            pl.BlockSpec((None, ns, bt, width), lambda i: (layer, 0, i, 0)),
            pl.BlockSpec((n_new, bt, width), lambda i: (0, i, 0)),
            _layer_block((CONV_K, width), layer), vec, vec, vec,
        ],
        out_specs=[pl.BlockSpec((n_new, bt, width), lambda i: (0, i, 0)),
                   pl.BlockSpec((ns, bt, width), lambda i: (0, i, 0))],
        out_shape=[jax.ShapeDtypeStruct((n_new, batch, width), MXU_DTYPE),
                   jax.ShapeDtypeStruct((ns, batch, width), F32)],
        compiler_params=_params("parallel"),
        name="conv_sample",
    )(state_all, v, p['conv_w'], p['conv_b'], p['conv_ln_g'], p['conv_ln_b'])


def _outffn_kernel(x_ref, a_ref, s_ref, c_ref, g1_ref, sh2_ref, sc2_ref, g2_ref, n2_ref,
                   wo_ref, wg_ref, wu_ref, wd_ref, fg_ref, o_ref, *, final_norm, ff_chunk):
    tm = x_ref.shape[0]
    o1 = ATTN_WIDTH
    o2 = o1 + SSM_WIDTH
    proj = (_mm(a_ref[...], wo_ref[0:o1, :]) + _mm(s_ref[...], wo_ref[o1:o2, :])
            + _mm(c_ref[...], wo_ref[o2:o2 + CONV_WIDTH, :]))
    x2 = x_ref[...] + _rows(g1_ref[...], tm) * proj
    h2 = _rms(x2, n2_ref[...]) * (1.0 + _rows(sc2_ref[...], tm)) + _rows(sh2_ref[...], tm)
    h2 = h2.astype(MXU_DTYPE)
    ffn = jnp.zeros(x2.shape, F32)
    for start in range(0, D_FF, ff_chunk):
        cs = slice(start, min(start + ff_chunk, D_FF))
        act = _silu(_mm(h2, wg_ref[:, cs])) * _mm(h2, wu_ref[:, cs])
        ffn = ffn + _mm(act.astype(MXU_DTYPE), wd_ref[cs, :])
    y = x2 + _rows(g2_ref[...], tm) * ffn
    if final_norm:
        y = _rms(y, fg_ref[...])
    o_ref[...] = y


def _out_ffn(x, o_attn, o_ssm, o_conv, mod, p, layer, final_norm, tm, tiles_per_batch):
    t = x.shape[0]
    row = lambda i: (i, 0)
    resident = lambda shape: _layer_block(shape, layer, pipeline_mode=pl.Buffered(1))
    return pl.pallas_call(
        functools.partial(_outffn_kernel, final_norm=final_norm, ff_chunk=FF_CHUNK),
        grid=(t // tm,),
        in_specs=[
            pl.BlockSpec((tm, D_MODEL), row),
            pl.BlockSpec((tm, ATTN_WIDTH), row),
            pl.BlockSpec((tm, SSM_WIDTH), row),
            pl.BlockSpec((tm, CONV_WIDTH), row),
            _mod_spec(mod, layer, 2, tiles_per_batch),
            _mod_spec(mod, layer, 3, tiles_per_batch),
            _mod_spec(mod, layer, 4, tiles_per_batch),
            _mod_spec(mod, layer, 5, tiles_per_batch),
            _layer_block((1, D_MODEL), layer),
            resident((D_MODEL, D_MODEL)),
            resident((D_MODEL, D_FF)),
            resident((D_MODEL, D_FF)),
            resident((D_FF, D_MODEL)),
            pl.BlockSpec((1, D_MODEL), lambda i: (0, 0)),
        ],
        out_specs=pl.BlockSpec((tm, D_MODEL), row),
        out_shape=jax.ShapeDtypeStruct((t, D_MODEL), F32),
        compiler_params=_params("parallel"),
        name="out_projection_ffn",
    )(x, o_attn, o_ssm, o_conv, mod.array, mod.array, mod.array, mod.array, p['norm2_g'],
      p['w_out'], p['w_gate'], p['w_up'], p['w_down'], p['final_g'])


def _layer_prompt(x, mod, p, layer, rope, batch, seq, final_norm):
    tm = min(512, seq)
    tiles_per_batch = seq // tm
    q, kv, u, vc = _in_projection(x, mod, p, layer, rope[0], rope[1], tm, tiles_per_batch)
    o_attn = _attention_prompt(q, kv, p['sinks'], layer, batch, seq)
    n_keep = min(WINDOW, seq)
    kv_keep = kv.reshape(batch, seq, 2 * KV_WIDTH)[:, seq - n_keep:]
    new_k = kv_keep[:, :, :KV_WIDTH].reshape(batch, n_keep, N_KV_HEADS, HEAD_DIM)
    new_v = kv_keep[:, :, KV_WIDTH:].reshape(batch, n_keep, N_KV_HEADS, HEAD_DIM)

    nseg = PROMPT_SEGMENTS
    lseg = seq // nseg
    nseq = batch * nseg
    u_tm = jnp.transpose(u.reshape(nseq, lseg, SSM_WIDTH), (1, 0, 2)).reshape(lseg * nseq, SSM_WIDTH)
    zero = jnp.zeros((nseq, SSM_FLAT), F32)
    steps = min(128, lseg)
    end_re, end_im = _s5_scan(u_tm, zero, zero, p, layer, False, nseq, steps)

    def shift(e):
        e = e.reshape(batch, nseg, SSM_FLAT)
        return jnp.concatenate([jnp.zeros_like(e[:, :1]), e[:, :1]], axis=1).reshape(nseq, SSM_FLAT)

    o_tm, h_re, h_im = _s5_scan(u_tm, shift(end_re), shift(end_im), p, layer, True, nseq, steps)
    o_ssm = jnp.transpose(o_tm.reshape(lseg, nseq, SSM_WIDTH), (1, 0, 2)).reshape(batch * seq, SSM_WIDTH)
    last = lambda h: h.reshape(batch, nseg, N_SSM_GROUPS, SSM_STATE)[:, nseg - 1]

    o_conv = _conv_prompt(vc, p, layer, seq, min(512, seq))
    new_conv = vc.reshape(batch, seq, CONV_WIDTH)[:, seq - (CONV_K - 1):]

    x = _out_ffn(x, o_attn, o_ssm, o_conv, mod, p, layer, final_norm, tm, tiles_per_batch)
    return x, new_k, new_v, last(h_re), last(h_im), new_conv


def _layer_sample(x, mod, p, layer, rope, batch, n_new, kt_all, vt_all, h0_re, h0_im, conv_all, final_norm):
    t = batch * n_new
    q, kv, u, vc = _in_projection(x, mod, p, layer, rope[0], rope[1], t, 1)

    q4 = jnp.transpose(q.reshape(n_new, batch, N_Q_HEADS, HEAD_DIM), (1, 2, 0, 3))
    q4 = q4.reshape(batch, N_KV_HEADS, Q_PER_KV * n_new, HEAD_DIM)
    zq = jnp.zeros_like(q4[:, 0])
    qexp = jnp.concatenate([jnp.concatenate([q4[:, 0], zq], axis=-1),
                            jnp.concatenate([zq, q4[:, 1]], axis=-1)], axis=1)
    kv5 = jnp.transpose(kv.reshape(n_new, batch, 2, N_KV_HEADS, HEAD_DIM), (2, 1, 3, 4, 0))
    kv5 = jnp.pad(kv5, ((0, 0),) * 4 + ((kt_all.shape[-1] - n_new, 0),))
    o, new_k, new_v = _attention_sample(qexp, kt_all, vt_all, kv5[0], kv5[1], p['sinks'], layer, n_new)
    o = o[:, :, :HEAD_DIM].reshape(batch, N_Q_HEADS, n_new, HEAD_DIM)
    o_attn = jnp.transpose(o, (2, 0, 1, 3)).reshape(t, ATTN_WIDTH)

    o_ssm, h_re, h_im = _s5_scan(u, h0_re.reshape(batch, SSM_FLAT), h0_im.reshape(batch, SSM_FLAT),
                                 p, layer, True, batch, n_new)
    st = lambda h: h.reshape(batch, N_SSM_GROUPS, SSM_STATE)

    o_conv, new_conv = _conv_sample(conv_all, vc.reshape(n_new, batch, CONV_WIDTH), p, layer)
    o_conv = o_conv.reshape(t, CONV_WIDTH)

    x = _out_ffn(x, o_attn, o_ssm, o_conv, mod, p, layer, final_norm, t, 1)
    return x, new_k, new_v, st(h_re), st(h_im), new_conv


def kernel(x_prompt, x_sample, c_prompt, c_sample, cache_k, cache_v, state_ssm_re, state_ssm_im, state_conv,
           norm1_g, norm2_g, w_mod, b_mod, w_in, attn_sinks, ssm_lam_re, ssm_lam_im, ssm_log_dt,
           ssm_b_re, ssm_b_im, ssm_c_re, ssm_c_im, ssm_d, ssm_w_glu, ssm_b_glu,
           conv_w, conv_b, conv_ln_g, conv_ln_b, w_out, w_gate, w_up, w_down, final_norm_g):
    bp, seq, d = x_prompt.shape
    bs, n_new, _ = x_sample.shape
    depth = w_in.shape[0]
    assert PROMPT_SEGMENTS == 2 and seq % (PROMPT_SEGMENTS * SUBLANES) == 0

    c_all = jnp.concatenate([c_sample, c_prompt], axis=0)
    pad_rows = -c_all.shape[0] % SUBLANES
    mods = _modulation(jnp.pad(c_all, ((0, pad_rows), (0, 0))), w_mod, b_mod)
    mod_p = _Mod(mods[:, bs:bs + bp].reshape(depth, bp, 1, N_MOD * d), None)
    mod_s = _Mod(mods, bs)

    a_re, a_im, bre_blk, bim_blk = _s5_discretise(ssm_lam_re, ssm_lam_im, ssm_log_dt, ssm_b_re, ssm_b_im)
    rope_p = _rope_tables(jnp.arange(seq))
    rope_s = _rope_tables(jnp.repeat(PAST_LEN + jnp.arange(n_new), bs))

    cast = lambda a: a.astype(MXU_DTYPE)
    vec = lambda a: a.reshape(depth, 1, a.shape[-1])
    p = {
        'norm1_g': vec(norm1_g), 'norm2_g': vec(norm2_g), 'w_in': cast(w_in), 'sinks': attn_sinks,
        'a_re': a_re, 'a_im': a_im,
        'bre_blk': bre_blk, 'bim_blk': bim_blk,
        'cre_blk': cast(_group_block_diag_out(ssm_c_re)), 'cim_blk': cast(_group_block_diag_out(ssm_c_im)),
        'ssm_d': vec(ssm_d), 'ssm_w_glu': cast(ssm_w_glu), 'ssm_b_glu': vec(ssm_b_glu),
        'conv_w': conv_w, 'conv_b': vec(conv_b), 'conv_ln_g': vec(conv_ln_g), 'conv_ln_b': vec(conv_ln_b),
        'w_out': cast(w_out), 'w_gate': cast(w_gate), 'w_up': cast(w_up), 'w_down': cast(w_down),
        'final_g': final_norm_g.reshape(1, d),
    }
    conv_all = jnp.transpose(state_conv, (0, 2, 1, 3))
    kt_all = jnp.transpose(cache_k, (0, 1, 3, 4, 2))
    vt_all = jnp.transpose(cache_v, (0, 1, 3, 4, 2))

    xp = x_prompt.reshape(bp * seq, d)
    xs = jnp.transpose(x_sample, (1, 0, 2)).reshape(n_new * bs, d)
    outs_p, outs_s = [], []
    for l in range(depth):
        final = l == depth - 1
        xp, *op = _layer_prompt(xp, mod_p, p, l, rope_p, bp, seq, final)
        xs, *os_ = _layer_sample(xs, mod_s, p, l, rope_s, bs, n_new, kt_all, vt_all,
                                 state_ssm_re[l], state_ssm_im[l], conv_all, final)
        outs_p.append(op)
        outs_s.append(os_)
    stack = lambda outs, i: jnp.stack([o[i] for o in outs])
    y_sample = jnp.transpose(xs.reshape(n_new, bs, d), (1, 0, 2))
    new_k_s = jnp.transpose(stack(outs_s, 0), (0, 1, 4, 2, 3))
    new_v_s = jnp.transpose(stack(outs_s, 1), (0, 1, 4, 2, 3))
    new_conv_s = jnp.transpose(stack(outs_s, 4), (0, 2, 1, 3))
    return (xp.reshape(bp, seq, d), y_sample, *[stack(outs_p, i) for i in range(5)],
            new_k_s, new_v_s, stack(outs_s, 2), stack(outs_s, 3), new_conv_s)
```

```python
import functools
import math
from typing import NamedTuple

import jax
import jax.numpy as jnp
from jax import lax
from jax.experimental import pallas as pl
from jax.experimental.pallas import tpu as pltpu

F32 = jnp.float32
MXU_DTYPE = jnp.bfloat16

V7X_VMEM_BYTES = 64 * 1024 * 1024
VMEM_LIMIT_BYTES = V7X_VMEM_BYTES - 8 * 1024 * 1024
LANES = 128
SUBLANES = 8
V7X_MXU_DIM = 256

D_MODEL = 1024
HEAD_DIM = 64
N_Q_HEADS = 8
N_KV_HEADS = 2
Q_PER_KV = N_Q_HEADS // N_KV_HEADS
ATTN_WIDTH = N_Q_HEADS * HEAD_DIM
KV_WIDTH = N_KV_HEADS * HEAD_DIM
WINDOW = 128
ROPE_THETA = 10000.0
ATTN_SCALE = 1.0 / math.sqrt(HEAD_DIM)
SSM_WIDTH = 256
SSM_GROUP = 16
N_SSM_GROUPS = 16
SSM_STATE = 64
SSM_FLAT = N_SSM_GROUPS * SSM_STATE
CONV_WIDTH = 256
CONV_K = 31
CONV_HALO = 32
IN_WIDTH = ATTN_WIDTH + 2 * KV_WIDTH + SSM_WIDTH + 2 * CONV_WIDTH
D_FF = 2816
FF_CHUNK = 6 * V7X_MXU_DIM
EPS = 1e-6
NEG = -1e30
N_MOD = 6
PROMPT_SEGMENTS = 2
PAST_LEN = 8192


def _params(*semantics):
    return pltpu.CompilerParams(dimension_semantics=semantics, vmem_limit_bytes=VMEM_LIMIT_BYTES)


def _layer_block(shape, layer, **kw):
    zeros = (0,) * len(shape)
    return pl.BlockSpec((None,) + tuple(shape), lambda *_: (layer,) + zeros, **kw)


def _sigmoid(x):
    return 1.0 / (1.0 + jnp.exp(-x))


def _silu(x):
    return x * _sigmoid(x)


def _rms(x, g):
    return x * lax.rsqrt(jnp.mean(x * x, axis=-1, keepdims=True) + EPS) * g


def _mm(a, b):
    return jnp.dot(a, b, preferred_element_type=F32)


def _mm_nt(a, b):
    return lax.dot_general(a, b, (((1,), (1,)), ((), ())), preferred_element_type=F32)


def _rows(m, n):
    return m if m.shape[0] == 1 else jnp.concatenate([m] * (n // m.shape[0]), axis=0)


def _mod_kernel(c_ref, w_ref, b_ref, o_ref):
    a = _silu(c_ref[...]).astype(MXU_DTYPE)
    o_ref[...] = _mm(a, w_ref[...].astype(MXU_DTYPE)) + b_ref[...]


def _modulation(c, w_mod, b_mod):
    depth, d, n = w_mod.shape
    rows = c.shape[0]
    tn = 1536
    return pl.pallas_call(
        _mod_kernel,
        grid=(depth, n // tn),
        in_specs=[
            pl.BlockSpec((rows, d), lambda l, j: (0, 0)),
            pl.BlockSpec((None, d, tn), lambda l, j: (l, 0, j)),
            pl.BlockSpec((None, 1, tn), lambda l, j: (l, 0, j)),
        ],
        out_specs=pl.BlockSpec((None, rows, tn), lambda l, j: (l, 0, j)),
        out_shape=jax.ShapeDtypeStruct((depth, rows, n), F32),
        compiler_params=_params("parallel", "parallel"),
        name="modulation",
    )(c, w_mod, b_mod.reshape(depth, 1, n))


class _Mod(NamedTuple):
    array: jax.Array
    batch_rows: int | None


def _mod_spec(mod, layer, chunk, tiles_per_batch, tile=lambda i: i):
    if mod.batch_rows is None:
        return pl.BlockSpec((None, None, 1, D_MODEL), lambda i: (layer, tile(i) // tiles_per_batch, 0, chunk))
    return pl.BlockSpec((None, mod.batch_rows, D_MODEL), lambda i: (layer, 0, chunk))


def _inproj_kernel(x_ref, sh_ref, sc_ref, g_ref, w_ref, cos_ref, sin_ref,
                   q_ref, kv_ref, u_ref, vc_ref):
    tm = x_ref.shape[0]
    h = _rms(x_ref[...], g_ref[...]) * (1.0 + _rows(sc_ref[...], tm)) + _rows(sh_ref[...], tm)
    z = _mm(h.astype(MXU_DTYPE), w_ref[...])
    cos = cos_ref[...]
    sin = sin_ref[...]
    lane = lax.broadcasted_iota(jnp.int32, (tm, LANES), 1)
    first_half = (lane % HEAD_DIM) < (HEAD_DIM // 2)

    def rope(t):
        partner = jnp.where(first_half, pltpu.roll(t, LANES - HEAD_DIM // 2, 1),
                            pltpu.roll(t, HEAD_DIM // 2, 1))
        return t * cos + partner * sin

    for j in range(ATTN_WIDTH // LANES):
        sl = slice(j * LANES, (j + 1) * LANES)
        q_ref[:, sl] = (rope(z[:, sl]) * ATTN_SCALE).astype(q_ref.dtype)
    o = ATTN_WIDTH
    kv_ref[:, 0:KV_WIDTH] = rope(z[:, o:o + KV_WIDTH])
    kv_ref[:, KV_WIDTH:2 * KV_WIDTH] = z[:, o + KV_WIDTH:o + 2 * KV_WIDTH]
    o += 2 * KV_WIDTH
    u_ref[...] = z[:, o:o + SSM_WIDTH]
    o += SSM_WIDTH
    za = z[:, o:o + CONV_WIDTH]
    zg = z[:, o + CONV_WIDTH:o + 2 * CONV_WIDTH]
    vc_ref[...] = za * _sigmoid(zg)


def _in_projection(x, mod, p, layer, cos, sin, tm, tiles_per_batch):
    t = x.shape[0]
    pos_tiles = cos.shape[0] // tm
    row = lambda i: (i, 0)
    return pl.pallas_call(
        _inproj_kernel,
        grid=(t // tm,),
        in_specs=[
            pl.BlockSpec((tm, D_MODEL), row),
            _mod_spec(mod, layer, 0, tiles_per_batch),
            _mod_spec(mod, layer, 1, tiles_per_batch),
            _layer_block((1, D_MODEL), layer),
            _layer_block((D_MODEL, IN_WIDTH), layer),
            pl.BlockSpec((tm, LANES), lambda i: (i % pos_tiles, 0)),
            pl.BlockSpec((tm, LANES), lambda i: (i % pos_tiles, 0)),
        ],
        out_specs=[
            pl.BlockSpec((tm, ATTN_WIDTH), row),
            pl.BlockSpec((tm, 2 * KV_WIDTH), row),
            pl.BlockSpec((tm, SSM_WIDTH), row),
            pl.BlockSpec((tm, CONV_WIDTH), row),
        ],
        out_shape=[
            jax.ShapeDtypeStruct((t, ATTN_WIDTH), MXU_DTYPE),
            jax.ShapeDtypeStruct((t, 2 * KV_WIDTH), F32),
            jax.ShapeDtypeStruct((t, SSM_WIDTH), F32),
            jax.ShapeDtypeStruct((t, CONV_WIDTH), F32),
        ],
        compiler_params=_params("parallel"),
        name="in_projection",
    )(x, mod.array, mod.array, p['norm1_g'], p['w_in'], cos, sin)


def _rope_tables(pos):
    half = HEAD_DIM // 2
    inv_freq = ROPE_THETA ** (-jnp.arange(half, dtype=F32) / half)
    ang = pos.astype(F32)[:, None] * inv_freq[None, :]
    cos = jnp.tile(jnp.cos(ang), (1, LANES // half))
    sin = jnp.sin(ang)
    sin = jnp.tile(jnp.concatenate([-sin, sin], axis=1), (1, LANES // HEAD_DIM))
    return cos, sin


def _sink_softmax(s, sink_col):
    m = jnp.maximum(jnp.max(s, axis=-1, keepdims=True), sink_col)
    e = jnp.exp(s - m)
    return e * (1.0 / (jnp.sum(e, axis=-1, keepdims=True) + jnp.exp(sink_col - m)))


def _head_pair_select(x, pick_second):
    lane = lax.broadcasted_iota(jnp.int32, x.shape, 1)
    swapped = pltpu.roll(x, HEAD_DIM, 1)
    low = lane < HEAD_DIM
    return jnp.where(low, swapped, x) if pick_second else jnp.where(low, x, swapped)


def _attn_prompt_kernel(sink_ref, q_ref, kvp_ref, kvc_ref, o_ref, *, layer):
    n = pl.program_id(1)
    w = WINDOW
    n_blocks = q_ref.shape[0] // w
    rows = Q_PER_KV * w
    r_idx = lax.broadcasted_iota(jnp.int32, (rows, w), 0) % w
    c_idx = lax.broadcasted_iota(jnp.int32, (rows, w), 1)
    from_prev = c_idx > r_idx
    first_bias = jnp.where(n > 0, 0.0, NEG)
    row_head = lax.broadcasted_iota(jnp.int32, (rows, 1), 0) // w
    low = lax.broadcasted_iota(jnp.int32, (w, LANES), 1) < HEAD_DIM
    half_mask = [jnp.where(low, 1.0, 0.0).astype(MXU_DTYPE), jnp.where(low, 0.0, 1.0).astype(MXU_DTYPE)]
    sink_cols = []
    for kvh in range(N_KV_HEADS):
        sink_col = jnp.zeros((rows, 1), F32)
        for g in range(Q_PER_KV):
            sink_col = jnp.where(row_head == g, sink_ref[layer, kvh * Q_PER_KV + g], sink_col)
        sink_cols.append(sink_col)
    chains = [(i, kvh) for i in range(n_blocks) for kvh in range(N_KV_HEADS)]
    windows, values = [], []
    for i, kvh in chains:
        own = slice(i * w, (i + 1) * w)
        prev_ref, prev = (kvp_ref, slice(0, w)) if i == 0 else (kvc_ref, slice((i - 1) * w, i * w))
        kk = jnp.concatenate([prev_ref[prev, 0:KV_WIDTH], kvc_ref[own, 0:KV_WIDTH]], axis=0)
        vv = jnp.concatenate([prev_ref[prev, KV_WIDTH:2 * KV_WIDTH], kvc_ref[own, KV_WIDTH:2 * KV_WIDTH]], axis=0)
        k2 = _head_pair_select(kk, kvh == 1).astype(MXU_DTYPE)
        values.append(_head_pair_select(vv, kvh == 1).astype(MXU_DTYPE))
        pieces = []
        for g in range(Q_PER_KV):
            h = kvh * Q_PER_KV + g
            qcol = q_ref[own, (h // 2) * LANES:(h // 2 + 1) * LANES]
            pieces.append(qcol * half_mask[h % 2])
        s = _mm_nt(jnp.concatenate(pieces, axis=0), k2)
        s_prev = s[:, 0:w] + first_bias if i == 0 else s[:, 0:w]
        windows.append(jnp.where(from_prev, s_prev, s[:, w:2 * w]))
    probs = []
    for (i, kvh), sc in zip(chains, windows):
        p = _sink_softmax(sc, sink_cols[kvh])
        p2 = jnp.concatenate([jnp.where(from_prev, p, 0.0), jnp.where(from_prev, 0.0, p)], axis=1)
        probs.append(p2.astype(MXU_DTYPE))
    for (i, kvh), p2, v2 in zip(chains, probs, values):
        own = slice(i * w, (i + 1) * w)
        r = _mm(p2, v2)
        for j in range(Q_PER_KV // 2):
            col = kvh * (Q_PER_KV // 2) + j
            o_ref[own, col * LANES:(col + 1) * LANES] = jnp.where(
                low, r[2 * j * w:(2 * j + 1) * w], r[(2 * j + 1) * w:(2 * j + 2) * w]).astype(o_ref.dtype)


def _attention_prompt(q, kv, sinks, layer, batch, seq):
    tq = min(4 * WINDOW, seq)
    nt = seq // tq
    per = tq // WINDOW
    return pl.pallas_call(
        functools.partial(_attn_prompt_kernel, layer=layer),
        grid=(batch, nt),
        in_specs=[
            pl.BlockSpec(memory_space=pltpu.SMEM),
            pl.BlockSpec((tq, ATTN_WIDTH), lambda b, n: (b * nt + n, 0)),
            pl.BlockSpec((WINDOW, 2 * KV_WIDTH), lambda b, n: ((b * nt + n) * per - jnp.minimum(n, 1), 0)),
            pl.BlockSpec((tq, 2 * KV_WIDTH), lambda b, n: (b * nt + n, 0)),
        ],
        out_specs=pl.BlockSpec((tq, ATTN_WIDTH), lambda b, n: (b * nt + n, 0)),
        out_shape=jax.ShapeDtypeStruct((batch * seq, ATTN_WIDTH), MXU_DTYPE),
        compiler_params=_params("parallel", "parallel"),
        name="attention_prompt",
    )(sinks, q, kv, kv)


def _attn_sample_kernel(sink_ref, q_ref, kt_ref, vt_ref, knt_ref, vnt_ref, o_ref, nk_ref, nv_ref, *, layer, n_new):
    bt, rows, _ = q_ref.shape
    nkv, hd, wb = kt_ref.shape[1:]
    kept = wb - n_new
    r = lax.broadcasted_iota(jnp.int32, (bt * rows, 2 * wb), 0)
    j = lax.broadcasted_iota(jnp.int32, (bt * rows, 2 * wb), 1)
    t_idx = r % n_new
    t_new = j - wb - kept
    mask = ((j < wb) & (j > t_idx)) | ((t_new >= 0) & (t_new <= t_idx))
    row_head = (lax.broadcasted_iota(jnp.int32, (bt * rows, 1), 0) % rows) // n_new
    sink_col = jnp.zeros((bt * rows, 1), F32)
    for h in range(N_Q_HEADS):
        sink_col = jnp.where(row_head == h, sink_ref[layer, h], sink_col)
    keep = lax.broadcasted_iota(jnp.int32, (nkv, hd, wb), 2) < kept
    second_group = lax.broadcasted_iota(jnp.int32, (rows, nkv * hd), 0) >= (rows // nkv)
    flat = lambda a: a.reshape(nkv * hd, wb)

    scores, values = [], []
    for b in range(bt):
        kt, kpos = kt_ref[b], knt_ref[b]
        vt, vpos = vt_ref[b], vnt_ref[b]
        nk_ref[b] = jnp.where(keep, pltpu.roll(kt, kept, 2), kpos)
        nv_ref[b] = jnp.where(keep, pltpu.roll(vt, kept, 2), vpos)
        kcat = jnp.concatenate([flat(kt), flat(kpos)], axis=1).astype(MXU_DTYPE)
        values.append(jnp.concatenate([flat(vt), flat(vpos)], axis=1).astype(MXU_DTYPE))
        scores.append(_mm(q_ref[b], kcat))
    s = jnp.where(mask, jnp.concatenate(scores, axis=0), NEG)
    p = _sink_softmax(s, sink_col).astype(MXU_DTYPE)
    for b in range(bt):
        o = _mm_nt(p[b * rows:(b + 1) * rows], values[b])
        o_ref[b] = jnp.where(second_group, pltpu.roll(o, hd, 1), o).astype(o_ref.dtype)


def _attention_sample(qexp, kt_all, vt_all, knt, vnt, sinks, layer, n_new):
    batch, rows, width = qexp.shape
    _, _, nkv, hd, wb = kt_all.shape
    bt = min(8, batch)
    blk3 = pl.BlockSpec((bt, rows, width), lambda i: (i, 0, 0))
    blk4 = pl.BlockSpec((bt, nkv, hd, wb), lambda i: (i, 0, 0, 0))
    cache = pl.BlockSpec((None, bt, nkv, hd, wb), lambda i: (layer, i, 0, 0, 0))
    return pl.pallas_call(
        functools.partial(_attn_sample_kernel, layer=layer, n_new=n_new),
        grid=(batch // bt,),
        in_specs=[pl.BlockSpec(memory_space=pltpu.SMEM), blk3, cache, cache, blk4, blk4],
        out_specs=[blk3, blk4, blk4],
        out_shape=[jax.ShapeDtypeStruct((batch, rows, width), MXU_DTYPE),
                   jax.ShapeDtypeStruct((batch, nkv, hd, wb), F32),
                   jax.ShapeDtypeStruct((batch, nkv, hd, wb), F32)],
        compiler_params=_params("parallel"),
        name="attention_sample",
    )(sinks, qexp, kt_all, vt_all, knt, vnt)


def _s5_discretise_kernel(lr_ref, li_ref, ldt_ref, bre_ref, bim_ref, are_ref, aim_ref, bbre_ref, bbim_ref):
    lr = lr_ref[...]
    li = li_ref[...]
    dt = jnp.exp(ldt_ref[...])
    mag = jnp.exp(lr * dt)
    ab_re = mag * jnp.cos(li * dt)
    ab_im = mag * jnp.sin(li * dt)
    den = lr * lr + li * li
    nr = ab_re - 1.0
    coef_re = (nr * lr + ab_im * li) / den
    coef_im = (ab_im * lr - nr * li) / den
    are_ref[...] = ab_re
    aim_ref[...] = ab_im
    br = bre_ref[...]
    bi = bim_ref[...]
    bb_re = coef_re * br - coef_im * bi
    bb_im = coef_re * bi + coef_im * br
    lane_group = lax.broadcasted_iota(jnp.int32, bb_re.shape, 1) // SSM_STATE
    for g in range(N_SSM_GROUPS):
        rows = slice(g * SSM_GROUP, (g + 1) * SSM_GROUP)
        bbre_ref[rows, :] = jnp.where(lane_group == g, bb_re, 0.0).astype(bbre_ref.dtype)
        bbim_ref[rows, :] = jnp.where(lane_group == g, bb_im, 0.0).astype(bbim_ref.dtype)


def _s5_discretise(lam_re, lam_im, log_dt, b_re, b_im):
    depth = lam_re.shape[0]
    flat = lambda a: a.reshape(depth, 1, SSM_FLAT)
    ldt = jnp.broadcast_to(log_dt[:, :, None], lam_re.shape)
    bt = lambda a: jnp.transpose(a, (0, 3, 1, 2)).reshape(depth, SSM_GROUP, SSM_FLAT)
    vec = pl.BlockSpec((None, 1, SSM_FLAT), lambda l: (l, 0, 0))
    mat = pl.BlockSpec((None, SSM_GROUP, SSM_FLAT), lambda l: (l, 0, 0))
    blk = pl.BlockSpec((None, SSM_WIDTH, SSM_FLAT), lambda l: (l, 0, 0))
    return pl.pallas_call(
        _s5_discretise_kernel,
        grid=(depth,),
        in_specs=[vec, vec, vec, mat, mat],
        out_specs=[vec, vec, blk, blk],
        out_shape=[jax.ShapeDtypeStruct((depth, 1, SSM_FLAT), F32)] * 2
        + [jax.ShapeDtypeStruct((depth, SSM_WIDTH, SSM_FLAT), MXU_DTYPE)] * 2,
        compiler_params=_params("parallel"),
        name="s5_discretise",
    )(flat(lam_re), flat(lam_im), flat(ldt), bt(b_re), bt(b_im))


def _group_block_diag_out(c):
    depth = c.shape[0]
    eye = jnp.eye(N_SSM_GROUPS, dtype=c.dtype)
    full = eye[None, :, None, :, None] * jnp.transpose(c, (0, 1, 3, 2))[:, :, :, None, :]
    return full.reshape(depth, SSM_FLAT, SSM_WIDTH)


def _s5_kernel(u_ref, hre0_ref, him0_ref, are_ref, aim_ref, bre_ref, bim_ref, *rest, nseq, emit, seq_major):
    if seq_major:
        *rest, stage_s = rest
    if emit:
        (cre_ref, cim_ref, d_ref, wglu_ref, bglu_ref,
         o_ref, hre_out, him_out, hre_s, him_s, bure_s, buim_s, hsre_s, hsim_s) = rest
    else:
        hre_out, him_out, hre_s, him_s, bure_s, buim_s = rest
    steps = u_ref.shape[1] if seq_major else u_ref.shape[0] // nseq
    lane_halves = [slice(h * LANES, (h + 1) * LANES) for h in range(SSM_WIDTH // LANES)]

    @pl.when(pl.program_id(0) == 0)
    def _():
        hre_s[...] = hre0_ref[...]
        him_s[...] = him0_ref[...]

    if seq_major:
        for s in range(nseq):
            for h, cols in enumerate(lane_halves):
                stage_s[h, pl.ds(s, steps, stride=nseq), :] = u_ref[s, :, cols]
        u = jnp.concatenate([stage_s[h] for h in range(len(lane_halves))], axis=1)
    else:
        u = u_ref[...]
    ub = u.astype(MXU_DTYPE)
    bure_s[...] = _mm(ub, bre_ref[...])
    buim_s[...] = _mm(ub, bim_ref[...])
    ar = jnp.broadcast_to(are_ref[...], (nseq, SSM_FLAT))
    ai = jnp.broadcast_to(aim_ref[...], (nseq, SSM_FLAT))

    def step(t, carry):
        hr, hi = carry
        rows = pl.ds(pl.multiple_of(t * nseq, nseq), nseq)
        nhr = ar * hr - ai * hi + bure_s[rows, :]
        nhi = ar * hi + ai * hr + buim_s[rows, :]
        if emit:
            hsre_s[rows, :] = nhr
            hsim_s[rows, :] = nhi
        return nhr, nhi

    hr, hi = lax.fori_loop(0, steps, step, (hre_s[...], him_s[...]), unroll=min(steps, 8))
    hre_s[...] = hr
    him_s[...] = hi
    hre_out[...] = hr
    him_out[...] = hi
    if emit:
        y = _mm(hsre_s[...].astype(MXU_DTYPE), cre_ref[...]) - _mm(hsim_s[...].astype(MXU_DTYPE), cim_ref[...])
        z = jax.nn.gelu(y + d_ref[...] * u)
        gate = _mm(z.astype(MXU_DTYPE), wglu_ref[...]) + bglu_ref[...]
        out = z * _sigmoid(gate)
        if seq_major:
            for h, cols in enumerate(lane_halves):
                stage_s[h] = out[:, cols]
            for s in range(nseq):
                for h, cols in enumerate(lane_halves):
                    o_ref[s, :, cols] = stage_s[h, pl.ds(s, steps, stride=nseq), :].astype(o_ref.dtype)
        else:
            o_ref[...] = out.astype(o_ref.dtype)


def _s5_scan(u, hre0, him0, p, layer, emit, nseq, steps_per_tile):
    seq_major = u.ndim == 3
    rows = u.shape[0] * u.shape[1] if seq_major else u.shape[0]
    tr = steps_per_tile * nseq
    fixed = lambda i: (0, 0)
    state = pl.BlockSpec((nseq, SSM_FLAT), fixed)
    if seq_major:
        io_spec = pl.BlockSpec((nseq, steps_per_tile, SSM_WIDTH), lambda i: (0, i, 0))
    else:
        io_spec = pl.BlockSpec((tr, SSM_WIDTH), lambda i: (i, 0))
    in_specs = [io_spec, state, state,
                _layer_block((1, SSM_FLAT), layer), _layer_block((1, SSM_FLAT), layer),
                _layer_block((SSM_WIDTH, SSM_FLAT), layer), _layer_block((SSM_WIDTH, SSM_FLAT), layer)]
    args = [u, hre0, him0, p['a_re'], p['a_im'], p['bre_blk'], p['bim_blk']]
    out_specs = [state, state]
    out_shape = [jax.ShapeDtypeStruct((nseq, SSM_FLAT), F32)] * 2
    scratch = [pltpu.VMEM((nseq, SSM_FLAT), F32)] * 2 + [pltpu.VMEM((tr, SSM_FLAT), F32)] * 2
    if emit:
        in_specs += [_layer_block((SSM_FLAT, SSM_WIDTH), layer), _layer_block((SSM_FLAT, SSM_WIDTH), layer),
                     _layer_block((1, SSM_WIDTH), layer), _layer_block((SSM_WIDTH, SSM_WIDTH), layer),
                     _layer_block((1, SSM_WIDTH), layer)]
        args += [p['cre_blk'], p['cim_blk'], p['ssm_d'], p['ssm_w_glu'], p['ssm_b_glu']]
        out_specs = [io_spec] + out_specs
        out_shape = [jax.ShapeDtypeStruct(u.shape, MXU_DTYPE)] + out_shape
        scratch += [pltpu.VMEM((tr, SSM_FLAT), F32)] * 2
    if seq_major:
        scratch += [pltpu.VMEM((SSM_WIDTH // LANES, tr, LANES), F32)]
    return pl.pallas_call(
        functools.partial(_s5_kernel, nseq=nseq, emit=emit, seq_major=seq_major),
        grid=(rows // tr,),
        in_specs=in_specs,
        out_specs=out_specs,
        out_shape=out_shape,
        scratch_shapes=scratch,
        compiler_params=_params("arbitrary"),
        name="s5_scan" if emit else "s5_segment_states",
    )(*args)


def _layernorm_silu(y, g, b):
    yc = y - jnp.mean(y, axis=-1, keepdims=True)
    var = jnp.mean(yc * yc, axis=-1, keepdims=True)
    return _silu(yc * lax.rsqrt(var + EPS) * g + b)


def _conv_sample_kernel(state_ref, v_ref, w_ref, b_ref, g_ref, beta_ref, o_ref, ns_ref):
    ns = state_ref.shape[0]
    n_new = v_ref.shape[0]
    row = lambda j: state_ref[j] if j < ns else v_ref[j - ns]
    for t in range(n_new):
        acc = jnp.zeros(o_ref.shape[1:], F32)
        for k in range(CONV_K):
            acc = acc + w_ref[k:k + 1, :] * row(t + k)
        o_ref[t] = _layernorm_silu(acc + b_ref[...], g_ref[...], beta_ref[...]).astype(o_ref.dtype)
    for j in range(ns):
        ns_ref[j] = row(j + n_new)


def _conv_sample(state_all, v, p, layer):
    _, ns, batch, width = state_all.shape
    n_new = v.shape[0]
    bt = min(32, batch)
    vec = _layer_block((1, width), layer)
    return pl.pallas_call(
        _conv_sample_kernel,
        grid=(batch // bt,),
        in_specs=[
            pl.BlockSpec((None, ns, bt, width), lambda i: (layer, 0, i, 0)),
            pl.BlockSpec((n_new, bt, width), lambda i: (0, i, 0)),
            _layer_block((CONV_K, width), layer), vec, vec, vec,
        ],
        out_specs=[pl.BlockSpec((n_new, bt, width), lambda i: (0, i, 0)),
                   pl.BlockSpec((ns, bt, width), lambda i: (0, i, 0))],
        out_shape=[jax.ShapeDtypeStruct((n_new, batch, width), MXU_DTYPE),
                   jax.ShapeDtypeStruct((ns, batch, width), F32)],
        compiler_params=_params("parallel"),
        name="conv_sample",
    )(state_all, v, p['conv_w'], p['conv_b'], p['conv_ln_g'], p['conv_ln_b'])


def _outffn_stages(x_ref, a_ref, s_ref, load_conv, g1_ref, sh2_ref, sc2_ref, g2_ref, n2_ref,
                   wo_ref, wg_ref, wu_ref, wd_ref, fg_ref, o_ref, final_norm):
    tm = x_ref.shape[0]
    o1 = ATTN_WIDTH
    o2 = o1 + SSM_WIDTH
    st = {}

    def proj_attn_ssm():
        st['proj'] = _mm(a_ref[...], wo_ref[0:o1, :]) + _mm(s_ref[...], wo_ref[o1:o2, :])

    def proj_conv_residual_norm():
        proj = st.pop('proj') + _mm(load_conv(), wo_ref[o2:o2 + CONV_WIDTH, :])
        x2 = x_ref[...] + _rows(g1_ref[...], tm) * proj
        h2 = _rms(x2, n2_ref[...]) * (1.0 + _rows(sc2_ref[...], tm)) + _rows(sh2_ref[...], tm)
        st['x2'] = x2
        st['h2'] = h2.astype(MXU_DTYPE)

    def gate(cs):
        st['gate'] = _mm(st['h2'], wg_ref[:, cs])

    def up(cs):
        st['act'] = (_silu(st.pop('gate')) * _mm(st['h2'], wu_ref[:, cs])).astype(MXU_DTYPE)

    def down(cs, last):
        ffn = _mm(st.pop('act'), wd_ref[cs, :])
        st['ffn'] = ffn if 'ffn' not in st else st['ffn'] + ffn
        if last:
            y = st['x2'] + _rows(g2_ref[...], tm) * st['ffn']
            o_ref[...] = _rms(y, fg_ref[...]) if final_norm else y

    stages = [proj_attn_ssm, proj_conv_residual_norm]
    for start in range(0, D_FF, FF_CHUNK):
        cs = slice(start, min(start + FF_CHUNK, D_FF))
        stages += [functools.partial(gate, cs), functools.partial(up, cs),
                   functools.partial(down, cs, cs.stop == D_FF)]
    return stages


def _outffn_kernel(x_ref, a_ref, s_ref, c_ref, *rest, final_norm):
    for stage in _outffn_stages(x_ref, a_ref, s_ref, lambda: c_ref[...], *rest, final_norm):
        stage()


def _conv_stages(halo_ref, cur_ref, w_ref, b_ref, g_ref, beta_ref, store, buf, shifted, first, sub):
    tc = cur_ref.shape[0]
    base = CONV_HALO - (CONV_K - 1)
    span = shifted.shape[1]

    def fill():
        buf[0:CONV_HALO, :] = jnp.where(first, 0.0, halo_ref[...])
        buf[CONV_HALO:CONV_HALO + tc, :] = cur_ref[...]
        for s in range(1, SUBLANES):
            shifted[s - 1] = buf[s:s + span, :]

    def rows(r):
        acc = jnp.zeros((sub, CONV_WIDTH), F32)
        for k in range(CONV_K):
            j, s = (base + k) // SUBLANES, (base + k) % SUBLANES
            start = r * sub + j * SUBLANES
            tap = buf[start:start + sub, :] if s == 0 else shifted[s - 1, start:start + sub, :]
            acc = acc + w_ref[k:k + 1, :] * tap
        y = _layernorm_silu(acc + b_ref[...], g_ref[...], beta_ref[...])
        store(slice(r * sub, (r + 1) * sub), y)

    return [fill] + [functools.partial(rows, r) for r in range(tc // sub)]


def _outffn_conv_kernel(x_ref, a_ref, s_ref, halo_ref, vcur_ref, g1_ref, sh2_ref, sc2_ref, g2_ref, n2_ref,
                        wo_ref, wg_ref, wu_ref, wd_ref, fg_ref, cw_ref, cb_ref, cg_ref, cbeta_ref,
                        o_ref, buf, shifted, conv_out, *, final_norm, n_tiles, tiles_per_batch, sub):
    j = pl.program_id(0)
    conv_tile = jnp.minimum(j, n_tiles - 1)
    first = (conv_tile % tiles_per_batch) == 0

    def conv_into(slot):
        def store(rows, y):
            conv_out[slot, rows, :] = y.astype(conv_out.dtype)
        return _conv_stages(halo_ref, vcur_ref, cw_ref, cb_ref, cg_ref, cbeta_ref, store, buf, shifted, first, sub)

    @pl.when(j == 0)
    def _():
        for stage in conv_into(0):
            stage()

    @pl.when(j > 0)
    def _():
        slot = j % 2
        conv = conv_into(slot)
        ffn = _outffn_stages(x_ref, a_ref, s_ref, lambda: conv_out[1 - slot], g1_ref, sh2_ref, sc2_ref, g2_ref,
                             n2_ref, wo_ref, wg_ref, wu_ref, wd_ref, fg_ref, o_ref, final_norm)
        conv[0]()
        for k in range(max(len(ffn), len(conv) - 1)):
            if k + 1 < len(conv):
                conv[k + 1]()
            if k < len(ffn):
                ffn[k]()


def _out_ffn(x, o_attn, o_ssm, o_conv, mod, p, layer, final_norm, tm, tiles_per_batch):
    t = x.shape[0]
    row = lambda i: (i, 0)
    resident = lambda shape: _layer_block(shape, layer, pipeline_mode=pl.Buffered(1))
    return pl.pallas_call(
        functools.partial(_outffn_kernel, final_norm=final_norm),
        grid=(t // tm,),
        in_specs=[
            pl.BlockSpec((tm, D_MODEL), row),
            pl.BlockSpec((tm, ATTN_WIDTH), row),
            pl.BlockSpec((tm, SSM_WIDTH), row),
            pl.BlockSpec((tm, CONV_WIDTH), row),
            _mod_spec(mod, layer, 2, tiles_per_batch),
            _mod_spec(mod, layer, 3, tiles_per_batch),
            _mod_spec(mod, layer, 4, tiles_per_batch),
            _mod_spec(mod, layer, 5, tiles_per_batch),
            _layer_block((1, D_MODEL), layer),
            resident((D_MODEL, D_MODEL)),
            resident((D_MODEL, D_FF)),
            resident((D_MODEL, D_FF)),
            resident((D_FF, D_MODEL)),
            pl.BlockSpec((1, D_MODEL), lambda i: (0, 0)),
        ],
        out_specs=pl.BlockSpec((tm, D_MODEL), row),
        out_shape=jax.ShapeDtypeStruct((t, D_MODEL), F32),
        compiler_params=_params("parallel"),
        name="out_projection_ffn",
    )(x, o_attn, o_ssm, o_conv, mod.array, mod.array, mod.array, mod.array, p['norm2_g'],
      p['w_out'], p['w_gate'], p['w_up'], p['w_down'], p['final_g'])


def _out_ffn_conv(x, o_attn, o_ssm, vc, mod, p, layer, final_norm, tm, tiles_per_batch):
    t = x.shape[0]
    n_tiles = t // tm
    halo_per_tile = tm // CONV_HALO
    ffn_tile = lambda j: jnp.maximum(j - 1, 0)
    conv_tile = lambda j: jnp.minimum(j, n_tiles - 1)
    ffn_row = lambda j: (ffn_tile(j), 0)
    resident = lambda shape: _layer_block(shape, layer, pipeline_mode=pl.Buffered(1))
    cvec = _layer_block((1, CONV_WIDTH), layer)
    span = tm + CONV_HALO - SUBLANES
    return pl.pallas_call(
        functools.partial(_outffn_conv_kernel, final_norm=final_norm, n_tiles=n_tiles,
                          tiles_per_batch=tiles_per_batch, sub=tm // 8),
        grid=(n_tiles + 1,),
        in_specs=[
            pl.BlockSpec((tm, D_MODEL), ffn_row),
            pl.BlockSpec((tm, ATTN_WIDTH), ffn_row),
            pl.BlockSpec((tm, SSM_WIDTH), ffn_row),
            pl.BlockSpec((CONV_HALO, CONV_WIDTH), lambda j: (jnp.maximum(conv_tile(j) * halo_per_tile - 1, 0), 0)),
            pl.BlockSpec((tm, CONV_WIDTH), lambda j: (conv_tile(j), 0)),
            _mod_spec(mod, layer, 2, tiles_per_batch, ffn_tile),
            _mod_spec(mod, layer, 3, tiles_per_batch, ffn_tile),
            _mod_spec(mod, layer, 4, tiles_per_batch, ffn_tile),
            _mod_spec(mod, layer, 5, tiles_per_batch, ffn_tile),
            _layer_block((1, D_MODEL), layer),
            resident((D_MODEL, D_MODEL)),
            resident((D_MODEL, D_FF)),
            resident((D_MODEL, D_FF)),
            resident((D_FF, D_MODEL)),
            pl.BlockSpec((1, D_MODEL), lambda j: (0, 0)),
            _layer_block((CONV_K, CONV_WIDTH), layer), cvec, cvec, cvec,
        ],
        out_specs=pl.BlockSpec((tm, D_MODEL), ffn_row),
        out_shape=jax.ShapeDtypeStruct((t, D_MODEL), F32),
        scratch_shapes=[pltpu.VMEM((CONV_HALO + tm, CONV_WIDTH), F32),
                        pltpu.VMEM((SUBLANES - 1, span, CONV_WIDTH), F32),
                        pltpu.VMEM((2, tm, CONV_WIDTH), MXU_DTYPE)],
        compiler_params=_params("arbitrary"),
        name="conv_out_projection_ffn",
    )(x, o_attn, o_ssm, vc, vc, mod.array, mod.array, mod.array, mod.array, p['norm2_g'],
      p['w_out'], p['w_gate'], p['w_up'], p['w_down'], p['final_g'],
      p['conv_w'], p['conv_b'], p['conv_ln_g'], p['conv_ln_b'])


def _layer_prompt(x, mod, p, layer, rope, batch, seq, final_norm):
    tm = min(512, seq)
    tiles_per_batch = seq // tm
    q, kv, u, vc = _in_projection(x, mod, p, layer, rope[0], rope[1], tm, tiles_per_batch)
    o_attn = _attention_prompt(q, kv, p['sinks'], layer, batch, seq)
    n_keep = min(WINDOW, seq)
    kv_keep = kv.reshape(batch, seq, 2 * KV_WIDTH)[:, seq - n_keep:]
    new_k = kv_keep[:, :, :KV_WIDTH].reshape(batch, n_keep, N_KV_HEADS, HEAD_DIM)
    new_v = kv_keep[:, :, KV_WIDTH:].reshape(batch, n_keep, N_KV_HEADS, HEAD_DIM)

    nseg = PROMPT_SEGMENTS
    lseg = seq // nseg
    nseq = batch * nseg
    u_seq = u.reshape(nseq, lseg, SSM_WIDTH)
    zero = jnp.zeros((nseq, SSM_FLAT), F32)
    steps = min(128, lseg)
    end_re, end_im = _s5_scan(u_seq, zero, zero, p, layer, False, nseq, steps)

    def shift(e):
        e = e.reshape(batch, nseg, SSM_FLAT)
        return jnp.concatenate([jnp.zeros_like(e[:, :1]), e[:, :1]], axis=1).reshape(nseq, SSM_FLAT)

    o_seq, h_re, h_im = _s5_scan(u_seq, shift(end_re), shift(end_im), p, layer, True, nseq, steps)
    o_ssm = o_seq.reshape(batch * seq, SSM_WIDTH)
    last = lambda h: h.reshape(batch, nseg, N_SSM_GROUPS, SSM_STATE)[:, nseg - 1]

    new_conv = vc.reshape(batch, seq, CONV_WIDTH)[:, seq - (CONV_K - 1):]

    x = _out_ffn_conv(x, o_attn, o_ssm, vc, mod, p, layer, final_norm, tm, tiles_per_batch)
    return x, new_k, new_v, last(h_re), last(h_im), new_conv


def _layer_sample(x, mod, p, layer, rope, batch, n_new, kt_all, vt_all, h0_re, h0_im, conv_all, final_norm):
    t = batch * n_new
    q, kv, u, vc = _in_projection(x, mod, p, layer, rope[0], rope[1], t, 1)

    q4 = jnp.transpose(q.reshape(n_new, batch, N_Q_HEADS, HEAD_DIM), (1, 2, 0, 3))
    q4 = q4.reshape(batch, N_KV_HEADS, Q_PER_KV * n_new, HEAD_DIM)
    zq = jnp.zeros_like(q4[:, 0])
    qexp = jnp.concatenate([jnp.concatenate([q4[:, 0], zq], axis=-1),
                            jnp.concatenate([zq, q4[:, 1]], axis=-1)], axis=1)
    kv5 = jnp.transpose(kv.reshape(n_new, batch, 2, N_KV_HEADS, HEAD_DIM), (2, 1, 3, 4, 0))
    kv5 = jnp.pad(kv5, ((0, 0),) * 4 + ((kt_all.shape[-1] - n_new, 0),))
    o, new_k, new_v = _attention_sample(qexp, kt_all, vt_all, kv5[0], kv5[1], p['sinks'], layer, n_new)
    o = o[:, :, :HEAD_DIM].reshape(batch, N_Q_HEADS, n_new, HEAD_DIM)
    o_attn = jnp.transpose(o, (2, 0, 1, 3)).reshape(t, ATTN_WIDTH)

    o_ssm, h_re, h_im = _s5_scan(u, h0_re.reshape(batch, SSM_FLAT), h0_im.reshape(batch, SSM_FLAT),
                                 p, layer, True, batch, n_new)
    st = lambda h: h.reshape(batch, N_SSM_GROUPS, SSM_STATE)

    o_conv, new_conv = _conv_sample(conv_all, vc.reshape(n_new, batch, CONV_WIDTH), p, layer)
    o_conv = o_conv.reshape(t, CONV_WIDTH)

    x = _out_ffn(x, o_attn, o_ssm, o_conv, mod, p, layer, final_norm, t, 1)
    return x, new_k, new_v, st(h_re), st(h_im), new_conv


def kernel(x_prompt, x_sample, c_prompt, c_sample, cache_k, cache_v, state_ssm_re, state_ssm_im, state_conv,
           norm1_g, norm2_g, w_mod, b_mod, w_in, attn_sinks, ssm_lam_re, ssm_lam_im, ssm_log_dt,
           ssm_b_re, ssm_b_im, ssm_c_re, ssm_c_im, ssm_d, ssm_w_glu, ssm_b_glu,
           conv_w, conv_b, conv_ln_g, conv_ln_b, w_out, w_gate, w_up, w_down, final_norm_g):
    bp, seq, d = x_prompt.shape
    bs, n_new, _ = x_sample.shape
    depth = w_in.shape[0]
    assert PROMPT_SEGMENTS == 2 and seq % (PROMPT_SEGMENTS * SUBLANES) == 0

    c_all = jnp.concatenate([c_sample, c_prompt], axis=0)
    pad_rows = -c_all.shape[0] % SUBLANES
    mods = _modulation(jnp.pad(c_all, ((0, pad_rows), (0, 0))), w_mod, b_mod)
    mod_p = _Mod(mods[:, bs:bs + bp].reshape(depth, bp, 1, N_MOD * d), None)
    mod_s = _Mod(mods, bs)

    a_re, a_im, bre_blk, bim_blk = _s5_discretise(ssm_lam_re, ssm_lam_im, ssm_log_dt, ssm_b_re, ssm_b_im)
    rope_p = _rope_tables(jnp.arange(seq))
    rope_s = _rope_tables(jnp.repeat(PAST_LEN + jnp.arange(n_new), bs))

    cast = lambda a: a.astype(MXU_DTYPE)
    vec = lambda a: a.reshape(depth, 1, a.shape[-1])
    p = {
        'norm1_g': vec(norm1_g), 'norm2_g': vec(norm2_g), 'w_in': cast(w_in), 'sinks': attn_sinks,
        'a_re': a_re, 'a_im': a_im,
        'bre_blk': bre_blk, 'bim_blk': bim_blk,
        'cre_blk': cast(_group_block_diag_out(ssm_c_re)), 'cim_blk': cast(_group_block_diag_out(ssm_c_im)),
        'ssm_d': vec(ssm_d), 'ssm_w_glu': cast(ssm_w_glu), 'ssm_b_glu': vec(ssm_b_glu),
        'conv_w': conv_w, 'conv_b': vec(conv_b), 'conv_ln_g': vec(conv_ln_g), 'conv_ln_b': vec(conv_ln_b),
        'w_out': cast(w_out), 'w_gate': cast(w_gate), 'w_up': cast(w_up), 'w_down': cast(w_down),
        'final_g': final_norm_g.reshape(1, d),
    }
    conv_all = jnp.transpose(state_conv, (0, 2, 1, 3))
    kt_all = jnp.transpose(cache_k, (0, 1, 3, 4, 2))
    vt_all = jnp.transpose(cache_v, (0, 1, 3, 4, 2))

    xp = x_prompt.reshape(bp * seq, d)
    xs = jnp.transpose(x_sample, (1, 0, 2)).reshape(n_new * bs, d)
    outs_p, outs_s = [], []
    for l in range(depth):
        final = l == depth - 1
        xp, *op = _layer_prompt(xp, mod_p, p, l, rope_p, bp, seq, final)
        xs, *os_ = _layer_sample(xs, mod_s, p, l, rope_s, bs, n_new, kt_all, vt_all,
                                 state_ssm_re[l], state_ssm_im[l], conv_all, final)
        outs_p.append(op)
        outs_s.append(os_)
    stack = lambda outs, i: jnp.stack([o[i] for o in outs])
    y_sample = jnp.transpose(xs.reshape(n_new, bs, d), (1, 0, 2))
    new_k_s = jnp.transpose(stack(outs_s, 0), (0, 1, 4, 2, 3))
    new_v_s = jnp.transpose(stack(outs_s, 1), (0, 1, 4, 2, 3))
    new_conv_s = jnp.transpose(stack(outs_s, 4), (0, 2, 1, 3))
    return (xp.reshape(bp, seq, d), y_sample, *[stack(outs_p, i) for i in range(5)],
            new_k_s, new_v_s, stack(outs_s, 2), stack(outs_s, 3), new_conv_s)
```

```python
import functools
import math
from typing import NamedTuple

import jax
import jax.numpy as jnp
from jax import lax
from jax.experimental import pallas as pl
from jax.experimental.pallas import tpu as pltpu

F32 = jnp.float32
MXU_DTYPE = jnp.bfloat16

V7X_VMEM_BYTES = 64 * 1024 * 1024
VMEM_LIMIT_BYTES = V7X_VMEM_BYTES - 8 * 1024 * 1024
LANES = 128
SUBLANES = 8
V7X_MXU_DIM = 256

D_MODEL = 1024
HEAD_DIM = 64
N_Q_HEADS = 8
N_KV_HEADS = 2
Q_PER_KV = N_Q_HEADS // N_KV_HEADS
ATTN_WIDTH = N_Q_HEADS * HEAD_DIM
KV_WIDTH = N_KV_HEADS * HEAD_DIM
WINDOW = 128
ROPE_THETA = 10000.0
ATTN_SCALE = 1.0 / math.sqrt(HEAD_DIM)
SSM_WIDTH = 256
SSM_GROUP = 16
N_SSM_GROUPS = 16
SSM_STATE = 64
SSM_FLAT = N_SSM_GROUPS * SSM_STATE
CONV_WIDTH = 256
CONV_K = 31
CONV_HALO = 32
IN_WIDTH = ATTN_WIDTH + 2 * KV_WIDTH + SSM_WIDTH + 2 * CONV_WIDTH
D_FF = 2816
FF_CHUNK = 6 * V7X_MXU_DIM
EPS = 1e-6
NEG = -1e30
N_MOD = 6
PROMPT_SEGMENTS = 2
PAST_LEN = 8192


def _params(*semantics):
    return pltpu.CompilerParams(dimension_semantics=semantics, vmem_limit_bytes=VMEM_LIMIT_BYTES)


def _layer_block(shape, layer, **kw):
    zeros = (0,) * len(shape)
    return pl.BlockSpec((None,) + tuple(shape), lambda *_: (layer,) + zeros, **kw)


def _sigmoid(x):
    return 1.0 / (1.0 + jnp.exp(-x))


def _silu(x):
    return x * _sigmoid(x)


def _rms(x, g):
    return x * lax.rsqrt(jnp.mean(x * x, axis=-1, keepdims=True) + EPS) * g


def _mm(a, b):
    return jnp.dot(a, b, preferred_element_type=F32)


def _mm_nt(a, b):
    return lax.dot_general(a, b, (((1,), (1,)), ((), ())), preferred_element_type=F32)


def _rows(m, n):
    return m if m.shape[0] == 1 else jnp.concatenate([m] * (n // m.shape[0]), axis=0)


def _mod_kernel(c_ref, w_ref, b_ref, o_ref):
    a = _silu(c_ref[...]).astype(MXU_DTYPE)
    o_ref[...] = _mm(a, w_ref[...].astype(MXU_DTYPE)) + b_ref[...]


def _modulation(c, w_mod, b_mod):
    depth, d, n = w_mod.shape
    rows = c.shape[0]
    tn = 1536
    return pl.pallas_call(
        _mod_kernel,
        grid=(depth, n // tn),
        in_specs=[
            pl.BlockSpec((rows, d), lambda l, j: (0, 0)),
            pl.BlockSpec((None, d, tn), lambda l, j: (l, 0, j)),
            pl.BlockSpec((None, 1, tn), lambda l, j: (l, 0, j)),
        ],
        out_specs=pl.BlockSpec((None, rows, tn), lambda l, j: (l, 0, j)),
        out_shape=jax.ShapeDtypeStruct((depth, rows, n), F32),
        compiler_params=_params("parallel", "parallel"),
        name="modulation",
    )(c, w_mod, b_mod.reshape(depth, 1, n))


class _Mod(NamedTuple):
    array: jax.Array
    batch_rows: int | None


def _mod_spec(mod, layer, chunk, tiles_per_batch, tile=lambda i: i):
    if mod.batch_rows is None:
        return pl.BlockSpec((None, None, 1, D_MODEL), lambda i: (layer, tile(i) // tiles_per_batch, 0, chunk))
    return pl.BlockSpec((None, mod.batch_rows, D_MODEL), lambda i: (layer, 0, chunk))


def _inproj_kernel(x_ref, sh_ref, sc_ref, g_ref, w_ref, cos_ref, sin_ref,
                   q_ref, kv_ref, u_ref, vc_ref):
    tm = x_ref.shape[0]
    h = _rms(x_ref[...], g_ref[...]) * (1.0 + _rows(sc_ref[...], tm)) + _rows(sh_ref[...], tm)
    z = _mm(h.astype(MXU_DTYPE), w_ref[...])
    cos = cos_ref[...]
    sin = sin_ref[...]
    lane = lax.broadcasted_iota(jnp.int32, (tm, LANES), 1)
    first_half = (lane % HEAD_DIM) < (HEAD_DIM // 2)

    def rope(t):
        partner = jnp.where(first_half, pltpu.roll(t, LANES - HEAD_DIM // 2, 1),
                            pltpu.roll(t, HEAD_DIM // 2, 1))
        return t * cos + partner * sin

    for j in range(ATTN_WIDTH // LANES):
        sl = slice(j * LANES, (j + 1) * LANES)
        q_ref[:, sl] = (rope(z[:, sl]) * ATTN_SCALE).astype(q_ref.dtype)
    o = ATTN_WIDTH
    kv_ref[:, 0:KV_WIDTH] = rope(z[:, o:o + KV_WIDTH])
    kv_ref[:, KV_WIDTH:2 * KV_WIDTH] = z[:, o + KV_WIDTH:o + 2 * KV_WIDTH]
    o += 2 * KV_WIDTH
    u_ref[...] = z[:, o:o + SSM_WIDTH]
    o += SSM_WIDTH
    za = z[:, o:o + CONV_WIDTH]
    zg = z[:, o + CONV_WIDTH:o + 2 * CONV_WIDTH]
    vc_ref[...] = za * _sigmoid(zg)


def _in_projection(x, mod, p, layer, cos, sin, tm, tiles_per_batch):
    t = x.shape[0]
    pos_tiles = cos.shape[0] // tm
    row = lambda i: (i, 0)
    return pl.pallas_call(
        _inproj_kernel,
        grid=(t // tm,),
        in_specs=[
            pl.BlockSpec((tm, D_MODEL), row),
            _mod_spec(mod, layer, 0, tiles_per_batch),
            _mod_spec(mod, layer, 1, tiles_per_batch),
            _layer_block((1, D_MODEL), layer),
            _layer_block((D_MODEL, IN_WIDTH), layer),
            pl.BlockSpec((tm, LANES), lambda i: (i % pos_tiles, 0)),
            pl.BlockSpec((tm, LANES), lambda i: (i % pos_tiles, 0)),
        ],
        out_specs=[
            pl.BlockSpec((tm, ATTN_WIDTH), row),
            pl.BlockSpec((tm, 2 * KV_WIDTH), row),
            pl.BlockSpec((tm, SSM_WIDTH), row),
            pl.BlockSpec((tm, CONV_WIDTH), row),
        ],
        out_shape=[
            jax.ShapeDtypeStruct((t, ATTN_WIDTH), MXU_DTYPE),
            jax.ShapeDtypeStruct((t, 2 * KV_WIDTH), F32),
            jax.ShapeDtypeStruct((t, SSM_WIDTH), F32),
            jax.ShapeDtypeStruct((t, CONV_WIDTH), F32),
        ],
        compiler_params=_params("parallel"),
        name="in_projection",
    )(x, mod.array, mod.array, p['norm1_g'], p['w_in'], cos, sin)


def _rope_tables(pos):
    half = HEAD_DIM // 2
    inv_freq = ROPE_THETA ** (-jnp.arange(half, dtype=F32) / half)
    ang = pos.astype(F32)[:, None] * inv_freq[None, :]
    cos = jnp.tile(jnp.cos(ang), (1, LANES // half))
    sin = jnp.sin(ang)
    sin = jnp.tile(jnp.concatenate([-sin, sin], axis=1), (1, LANES // HEAD_DIM))
    return cos, sin


def _sink_softmax(s, sink_col):
    m = jnp.maximum(jnp.max(s, axis=-1, keepdims=True), sink_col)
    e = jnp.exp(s - m)
    return e * (1.0 / (jnp.sum(e, axis=-1, keepdims=True) + jnp.exp(sink_col - m)))


def _head_pair_select(x, pick_second):
    lane = lax.broadcasted_iota(jnp.int32, x.shape, 1)
    swapped = pltpu.roll(x, HEAD_DIM, 1)
    low = lane < HEAD_DIM
    return jnp.where(low, swapped, x) if pick_second else jnp.where(low, x, swapped)


def _attn_prompt_kernel(sink_ref, q_ref, kvp_ref, kvc_ref, o_ref, *, layer):
    n = pl.program_id(1)
    w = WINDOW
    n_blocks = q_ref.shape[0] // w
    rows = Q_PER_KV * w
    r_idx = lax.broadcasted_iota(jnp.int32, (rows, w), 0) % w
    c_idx = lax.broadcasted_iota(jnp.int32, (rows, w), 1)
    from_prev = c_idx > r_idx
    first_bias = jnp.where(n > 0, 0.0, NEG)
    row_head = lax.broadcasted_iota(jnp.int32, (rows, 1), 0) // w
    low = lax.broadcasted_iota(jnp.int32, (w, LANES), 1) < HEAD_DIM
    half_mask = [jnp.where(low, 1.0, 0.0).astype(MXU_DTYPE), jnp.where(low, 0.0, 1.0).astype(MXU_DTYPE)]
    sink_cols = []
    for kvh in range(N_KV_HEADS):
        sink_col = jnp.zeros((rows, 1), F32)
        for g in range(Q_PER_KV):
            sink_col = jnp.where(row_head == g, sink_ref[layer, kvh * Q_PER_KV + g], sink_col)
        sink_cols.append(sink_col)
    chains = [(i, kvh) for i in range(n_blocks) for kvh in range(N_KV_HEADS)]
    windows, values = [], []
    for i, kvh in chains:
        own = slice(i * w, (i + 1) * w)
        prev_ref, prev = (kvp_ref, slice(0, w)) if i == 0 else (kvc_ref, slice((i - 1) * w, i * w))
        kk = jnp.concatenate([prev_ref[prev, 0:KV_WIDTH], kvc_ref[own, 0:KV_WIDTH]], axis=0)
        vv = jnp.concatenate([prev_ref[prev, KV_WIDTH:2 * KV_WIDTH], kvc_ref[own, KV_WIDTH:2 * KV_WIDTH]], axis=0)
        k2 = _head_pair_select(kk, kvh == 1).astype(MXU_DTYPE)
        values.append(_head_pair_select(vv, kvh == 1).astype(MXU_DTYPE))
        pieces = []
        for g in range(Q_PER_KV):
            h = kvh * Q_PER_KV + g
            qcol = q_ref[own, (h // 2) * LANES:(h // 2 + 1) * LANES]
            pieces.append(qcol * half_mask[h % 2])
        s = _mm_nt(jnp.concatenate(pieces, axis=0), k2)
        s_prev = s[:, 0:w] + first_bias if i == 0 else s[:, 0:w]
        windows.append(jnp.where(from_prev, s_prev, s[:, w:2 * w]))
    probs = []
    for (i, kvh), sc in zip(chains, windows):
        p = _sink_softmax(sc, sink_cols[kvh])
        p2 = jnp.concatenate([jnp.where(from_prev, p, 0.0), jnp.where(from_prev, 0.0, p)], axis=1)
        probs.append(p2.astype(MXU_DTYPE))
    for (i, kvh), p2, v2 in zip(chains, probs, values):
        own = slice(i * w, (i + 1) * w)
        r = _mm(p2, v2)
        for j in range(Q_PER_KV // 2):
            col = kvh * (Q_PER_KV // 2) + j
            o_ref[own, col * LANES:(col + 1) * LANES] = jnp.where(
                low, r[2 * j * w:(2 * j + 1) * w], r[(2 * j + 1) * w:(2 * j + 2) * w]).astype(o_ref.dtype)


def _attention_prompt(q, kv, sinks, layer, batch, seq):
    tq = min(4 * WINDOW, seq)
    nt = seq // tq
    per = tq // WINDOW
    return pl.pallas_call(
        functools.partial(_attn_prompt_kernel, layer=layer),
        grid=(batch, nt),
        in_specs=[
            pl.BlockSpec(memory_space=pltpu.SMEM),
            pl.BlockSpec((tq, ATTN_WIDTH), lambda b, n: (b * nt + n, 0)),
            pl.BlockSpec((WINDOW, 2 * KV_WIDTH), lambda b, n: ((b * nt + n) * per - jnp.minimum(n, 1), 0)),
            pl.BlockSpec((tq, 2 * KV_WIDTH), lambda b, n: (b * nt + n, 0)),
        ],
        out_specs=pl.BlockSpec((tq, ATTN_WIDTH), lambda b, n: (b * nt + n, 0)),
        out_shape=jax.ShapeDtypeStruct((batch * seq, ATTN_WIDTH), MXU_DTYPE),
        compiler_params=_params("parallel", "parallel"),
        name="attention_prompt",
    )(sinks, q, kv, kv)


def _attn_sample_kernel(sink_ref, q_ref, kt_ref, vt_ref, kn_ref, vn_ref, o_ref, nk_ref, nv_ref, *, layer, n_new):
    bt, rows, _ = q_ref.shape
    nkv, hd, wb = kt_ref.shape[1:]
    kept = wb - n_new
    r = lax.broadcasted_iota(jnp.int32, (bt * rows, 2 * wb), 0)
    j = lax.broadcasted_iota(jnp.int32, (bt * rows, 2 * wb), 1)
    t_idx = r % n_new
    t_new = j - wb - kept
    mask = ((j < wb) & (j > t_idx)) | ((t_new >= 0) & (t_new <= t_idx))
    row_head = (lax.broadcasted_iota(jnp.int32, (bt * rows, 1), 0) % rows) // n_new
    sink_col = jnp.zeros((bt * rows, 1), F32)
    for h in range(N_Q_HEADS):
        sink_col = jnp.where(row_head == h, sink_ref[layer, h], sink_col)
    keep = lax.broadcasted_iota(jnp.int32, (nkv, hd, wb), 2) < kept
    second_group = lax.broadcasted_iota(jnp.int32, (rows, nkv * hd), 0) >= (rows // nkv)
    flat = lambda a: a.reshape(nkv * hd, wb)
    above = jnp.zeros((wb - kn_ref.shape[1], nkv * hd), F32)

    scores, values = [], []
    for b in range(bt):
        kt, vt = kt_ref[b], vt_ref[b]
        kpos = jnp.concatenate([above, kn_ref[b]], axis=0).T
        vpos = jnp.concatenate([above, vn_ref[b]], axis=0).T
        nk_ref[b] = jnp.where(keep, pltpu.roll(kt, kept, 2), kpos.reshape(nkv, hd, wb))
        nv_ref[b] = jnp.where(keep, pltpu.roll(vt, kept, 2), vpos.reshape(nkv, hd, wb))
        kcat = jnp.concatenate([flat(kt), kpos], axis=1).astype(MXU_DTYPE)
        values.append(jnp.concatenate([flat(vt), vpos], axis=1).astype(MXU_DTYPE))
        scores.append(_mm(q_ref[b], kcat))
    s = jnp.where(mask, jnp.concatenate(scores, axis=0), NEG)
    p = _sink_softmax(s, sink_col).astype(MXU_DTYPE)
    for b in range(bt):
        o = _mm_nt(p[b * rows:(b + 1) * rows], values[b])
        o_ref[b] = jnp.where(second_group, pltpu.roll(o, hd, 1), o).astype(o_ref.dtype)


def _attention_sample(qexp, kt_all, vt_all, kn, vn, sinks, layer, n_new):
    batch, rows, width = qexp.shape
    _, _, nkv, hd, wb = kt_all.shape
    bt = min(8, batch)
    blk3 = pl.BlockSpec((bt, rows, width), lambda i: (i, 0, 0))
    blk4 = pl.BlockSpec((bt, nkv, hd, wb), lambda i: (i, 0, 0, 0))
    new = pl.BlockSpec((bt,) + kn.shape[1:], lambda i: (i, 0, 0))
    cache = pl.BlockSpec((None, bt, nkv, hd, wb), lambda i: (layer, i, 0, 0, 0))
    return pl.pallas_call(
        functools.partial(_attn_sample_kernel, layer=layer, n_new=n_new),
        grid=(batch // bt,),
        in_specs=[pl.BlockSpec(memory_space=pltpu.SMEM), blk3, cache, cache, new, new],
        out_specs=[blk3, blk4, blk4],
        out_shape=[jax.ShapeDtypeStruct((batch, rows, width), MXU_DTYPE),
                   jax.ShapeDtypeStruct((batch, nkv, hd, wb), F32),
                   jax.ShapeDtypeStruct((batch, nkv, hd, wb), F32)],
        compiler_params=_params("parallel"),
        name="attention_sample",
    )(sinks, qexp, kt_all, vt_all, kn, vn)


def _s5_discretise_kernel(lr_ref, li_ref, ldt_ref, bre_ref, bim_ref, cre_ref, cim_ref,
                          are_ref, aim_ref, bbre_ref, bbim_ref, ccre_ref, ccim_ref):
    lr = lr_ref[...]
    li = li_ref[...]
    dt = jnp.exp(ldt_ref[...])
    mag = jnp.exp(lr * dt)
    ab_re = mag * jnp.cos(li * dt)
    ab_im = mag * jnp.sin(li * dt)
    den = lr * lr + li * li
    nr = ab_re - 1.0
    coef_re = (nr * lr + ab_im * li) / den
    coef_im = (ab_im * lr - nr * li) / den
    are_ref[...] = ab_re
    aim_ref[...] = ab_im
    br = bre_ref[...]
    bi = bim_ref[...]
    bb_re = coef_re * br - coef_im * bi
    bb_im = coef_re * bi + coef_im * br
    lane_group = lax.broadcasted_iota(jnp.int32, bb_re.shape, 1) // SSM_STATE
    for src, dst in ((bb_re, bbre_ref), (bb_im, bbim_ref), (cre_ref[...], ccre_ref), (cim_ref[...], ccim_ref)):
        for g in range(N_SSM_GROUPS):
            dst[g * SSM_GROUP:(g + 1) * SSM_GROUP, :] = jnp.where(lane_group == g, src, 0.0).astype(dst.dtype)


def _s5_discretise(lam_re, lam_im, log_dt, b_re, b_im, c_re, c_im):
    depth = lam_re.shape[0]
    flat = lambda a: a.reshape(depth, 1, SSM_FLAT)
    ldt = jnp.broadcast_to(log_dt[:, :, None], lam_re.shape)
    bt = lambda a: jnp.transpose(a, (0, 3, 1, 2)).reshape(depth, SSM_GROUP, SSM_FLAT)
    ct = lambda a: jnp.transpose(a, (0, 2, 1, 3)).reshape(depth, SSM_GROUP, SSM_FLAT)
    vec = pl.BlockSpec((None, 1, SSM_FLAT), lambda l: (l, 0, 0))
    mat = pl.BlockSpec((None, SSM_GROUP, SSM_FLAT), lambda l: (l, 0, 0))
    blk = pl.BlockSpec((None, SSM_WIDTH, SSM_FLAT), lambda l: (l, 0, 0))
    return pl.pallas_call(
        _s5_discretise_kernel,
        grid=(depth,),
        in_specs=[vec, vec, vec, mat, mat, mat, mat],
        out_specs=[vec, vec, blk, blk, blk, blk],
        out_shape=[jax.ShapeDtypeStruct((depth, 1, SSM_FLAT), F32)] * 2
        + [jax.ShapeDtypeStruct((depth, SSM_WIDTH, SSM_FLAT), MXU_DTYPE)] * 4,
        compiler_params=_params("parallel"),
        name="s5_discretise",
    )(flat(lam_re), flat(lam_im), flat(ldt), bt(b_re), bt(b_im), ct(c_re), ct(c_im))


def _s5_kernel(u_ref, hre0_ref, him0_ref, are_ref, aim_ref, bre_ref, bim_ref, *rest, nseq, emit, seq_major):
    if seq_major:
        *rest, stage_s = rest
    if emit:
        (cre_ref, cim_ref, d_ref, wglu_ref, bglu_ref,
         o_ref, hre_out, him_out, hre_s, him_s, bure_s, buim_s, hsre_s, hsim_s) = rest
    else:
        hre_out, him_out, hre_s, him_s, bure_s, buim_s = rest
    steps = u_ref.shape[1] if seq_major else u_ref.shape[0] // nseq
    lane_halves = [slice(h * LANES, (h + 1) * LANES) for h in range(SSM_WIDTH // LANES)]

    @pl.when(pl.program_id(0) == 0)
    def _():
        hre_s[...] = hre0_ref[...]
        him_s[...] = him0_ref[...]

    if seq_major:
        for s in range(nseq):
            for h, cols in enumerate(lane_halves):
                stage_s[h, pl.ds(s, steps, stride=nseq), :] = u_ref[s, :, cols]
        u = jnp.concatenate([stage_s[h] for h in range(len(lane_halves))], axis=1)
    else:
        u = u_ref[...]
    ub = u.astype(MXU_DTYPE)
    bure_s[...] = _mm(ub, bre_ref[...])
    buim_s[...] = _mm(ub, bim_ref[...])
    ar = jnp.broadcast_to(are_ref[...], (nseq, SSM_FLAT))
    ai = jnp.broadcast_to(aim_ref[...], (nseq, SSM_FLAT))

    def step(t, carry):
        hr, hi = carry
        rows = pl.ds(pl.multiple_of(t * nseq, nseq), nseq)
        nhr = ar * hr - ai * hi + bure_s[rows, :]
        nhi = ar * hi + ai * hr + buim_s[rows, :]
        if emit:
            hsre_s[rows, :] = nhr
            hsim_s[rows, :] = nhi
        return nhr, nhi

    hr, hi = lax.fori_loop(0, steps, step, (hre_s[...], him_s[...]), unroll=min(steps, 8))
    hre_s[...] = hr
    him_s[...] = hi
    hre_out[...] = hr
    him_out[...] = hi
    if emit:
        y = (_mm_nt(hsre_s[...].astype(MXU_DTYPE), cre_ref[...])
             - _mm_nt(hsim_s[...].astype(MXU_DTYPE), cim_ref[...]))
        z = jax.nn.gelu(y + d_ref[...] * u)
        gate = _mm(z.astype(MXU_DTYPE), wglu_ref[...]) + bglu_ref[...]
        out = z * _sigmoid(gate)
        if seq_major:
            for h, cols in enumerate(lane_halves):
                stage_s[h] = out[:, cols]
            for s in range(nseq):
                for h, cols in enumerate(lane_halves):
                    o_ref[s, :, cols] = stage_s[h, pl.ds(s, steps, stride=nseq), :].astype(o_ref.dtype)
        else:
            o_ref[...] = out.astype(o_ref.dtype)


def _s5_scan(u, hre0, him0, p, layer, emit, nseq, steps_per_tile):
    seq_major = u.ndim == 3
    rows = u.shape[0] * u.shape[1] if seq_major else u.shape[0]
    tr = steps_per_tile * nseq
    fixed = lambda i: (0, 0)
    state = pl.BlockSpec((nseq, SSM_FLAT), fixed)
    if seq_major:
        io_spec = pl.BlockSpec((nseq, steps_per_tile, SSM_WIDTH), lambda i: (0, i, 0))
    else:
        io_spec = pl.BlockSpec((tr, SSM_WIDTH), lambda i: (i, 0))
    in_specs = [io_spec, state, state,
                _layer_block((1, SSM_FLAT), layer), _layer_block((1, SSM_FLAT), layer),
                _layer_block((SSM_WIDTH, SSM_FLAT), layer), _layer_block((SSM_WIDTH, SSM_FLAT), layer)]
    args = [u, hre0, him0, p['a_re'], p['a_im'], p['bre_blk'], p['bim_blk']]
    out_specs = [state, state]
    out_shape = [jax.ShapeDtypeStruct((nseq, SSM_FLAT), F32)] * 2
    scratch = [pltpu.VMEM((nseq, SSM_FLAT), F32)] * 2 + [pltpu.VMEM((tr, SSM_FLAT), F32)] * 2
    if emit:
        in_specs += [_layer_block((SSM_WIDTH, SSM_FLAT), layer), _layer_block((SSM_WIDTH, SSM_FLAT), layer),
                     _layer_block((1, SSM_WIDTH), layer), _layer_block((SSM_WIDTH, SSM_WIDTH), layer),
                     _layer_block((1, SSM_WIDTH), layer)]
        args += [p['cre_blk'], p['cim_blk'], p['ssm_d'], p['ssm_w_glu'], p['ssm_b_glu']]
        out_specs = [io_spec] + out_specs
        out_shape = [jax.ShapeDtypeStruct(u.shape, MXU_DTYPE)] + out_shape
        scratch += [pltpu.VMEM((tr, SSM_FLAT), F32)] * 2
    if seq_major:
        scratch += [pltpu.VMEM((SSM_WIDTH // LANES, tr, LANES), F32)]
    return pl.pallas_call(
        functools.partial(_s5_kernel, nseq=nseq, emit=emit, seq_major=seq_major),
        grid=(rows // tr,),
        in_specs=in_specs,
        out_specs=out_specs,
        out_shape=out_shape,
        scratch_shapes=scratch,
        compiler_params=_params("arbitrary"),
        name="s5_scan" if emit else "s5_segment_states",
    )(*args)


def _layernorm_silu(y, g, b):
    yc = y - jnp.mean(y, axis=-1, keepdims=True)
    var = jnp.mean(yc * yc, axis=-1, keepdims=True)
    return _silu(yc * lax.rsqrt(var + EPS) * g + b)


def _conv_sample_kernel(state_ref, v_ref, w_ref, b_ref, g_ref, beta_ref, o_ref, ns_ref):
    ns = state_ref.shape[0]
    n_new = v_ref.shape[0]
    row = lambda j: state_ref[j] if j < ns else v_ref[j - ns]
    for t in range(n_new):
        acc = jnp.zeros(o_ref.shape[1:], F32)
        for k in range(CONV_K):
            acc = acc + w_ref[k:k + 1, :] * row(t + k)
        o_ref[t] = _layernorm_silu(acc + b_ref[...], g_ref[...], beta_ref[...]).astype(o_ref.dtype)
    for j in range(ns):
        ns_ref[j] = row(j + n_new)


def _conv_sample(state_all, v, p, layer):
    _, ns, batch, width = state_all.shape
    n_new = v.shape[0]
    bt = min(32, batch)
    vec = _layer_block((1, width), layer)
    return pl.pallas_call(
        _conv_sample_kernel,
        grid=(batch // bt,),
        in_specs=[
            pl.BlockSpec((None, ns, bt, width), lambda i: (layer, 0, i, 0)),
            pl.BlockSpec((n_new, bt, width), lambda i: (0, i, 0)),
            _layer_block((CONV_K, width), layer), vec, vec, vec,
        ],
        out_specs=[pl.BlockSpec((n_new, bt, width), lambda i: (0, i, 0)),
                   pl.BlockSpec((ns, bt, width), lambda i: (0, i, 0))],
        out_shape=[jax.ShapeDtypeStruct((n_new, batch, width), MXU_DTYPE),
                   jax.ShapeDtypeStruct((ns, batch, width), F32)],
        compiler_params=_params("parallel"),
        name="conv_sample",
    )(state_all, v, p['conv_w'], p['conv_b'], p['conv_ln_g'], p['conv_ln_b'])


def _outffn_stages(x_ref, a_ref, s_ref, load_conv, g1_ref, sh2_ref, sc2_ref, g2_ref, n2_ref,
                   wo_ref, wg_ref, wu_ref, wd_ref, fg_ref, o_ref, final_norm):
    tm = x_ref.shape[0]
    o1 = ATTN_WIDTH
    o2 = o1 + SSM_WIDTH
    st = {}

    def proj_attn_ssm():
        st['proj'] = _mm(a_ref[...], wo_ref[0:o1, :]) + _mm(s_ref[...], wo_ref[o1:o2, :])

    def proj_conv_residual_norm():
        proj = st.pop('proj') + _mm(load_conv(), wo_ref[o2:o2 + CONV_WIDTH, :])
        x2 = x_ref[...] + _rows(g1_ref[...], tm) * proj
        h2 = _rms(x2, n2_ref[...]) * (1.0 + _rows(sc2_ref[...], tm)) + _rows(sh2_ref[...], tm)
        st['x2'] = x2
        st['h2'] = h2.astype(MXU_DTYPE)

    def gate(cs):
        st['gate'] = _mm(st['h2'], wg_ref[:, cs])

    def up(cs):
        st['act'] = (_silu(st.pop('gate')) * _mm(st['h2'], wu_ref[:, cs])).astype(MXU_DTYPE)

    def down(cs, last):
        ffn = _mm(st.pop('act'), wd_ref[cs, :])
        st['ffn'] = ffn if 'ffn' not in st else st['ffn'] + ffn
        if last:
            y = st['x2'] + _rows(g2_ref[...], tm) * st['ffn']
            o_ref[...] = _rms(y, fg_ref[...]) if final_norm else y

    stages = [proj_attn_ssm, proj_conv_residual_norm]
    for start in range(0, D_FF, FF_CHUNK):
        cs = slice(start, min(start + FF_CHUNK, D_FF))
        stages += [functools.partial(gate, cs), functools.partial(up, cs),
                   functools.partial(down, cs, cs.stop == D_FF)]
    return stages


def _outffn_kernel(x_ref, a_ref, s_ref, c_ref, *rest, final_norm):
    for stage in _outffn_stages(x_ref, a_ref, s_ref, lambda: c_ref[...], *rest, final_norm):
        stage()


def _conv_stages(halo_ref, cur_ref, w_ref, b_ref, g_ref, beta_ref, store, buf, shifted, first, sub):
    tc = cur_ref.shape[0]
    base = CONV_HALO - (CONV_K - 1)
    span = shifted.shape[1]

    def fill():
        buf[0:CONV_HALO, :] = jnp.where(first, 0.0, halo_ref[...])
        buf[CONV_HALO:CONV_HALO + tc, :] = cur_ref[...]
        for s in range(1, SUBLANES):
            shifted[s - 1] = buf[s:s + span, :]

    def rows(r):
        acc = jnp.zeros((sub, CONV_WIDTH), F32)
        for k in range(CONV_K):
            j, s = (base + k) // SUBLANES, (base + k) % SUBLANES
            start = r * sub + j * SUBLANES
            tap = buf[start:start + sub, :] if s == 0 else shifted[s - 1, start:start + sub, :]
            acc = acc + w_ref[k:k + 1, :] * tap
        y = _layernorm_silu(acc + b_ref[...], g_ref[...], beta_ref[...])
        store(slice(r * sub, (r + 1) * sub), y)

    return [fill] + [functools.partial(rows, r) for r in range(tc // sub)]


def _outffn_conv_kernel(x_ref, a_ref, s_ref, halo_ref, vcur_ref, g1_ref, sh2_ref, sc2_ref, g2_ref, n2_ref,
                        wo_ref, wg_ref, wu_ref, wd_ref, fg_ref, cw_ref, cb_ref, cg_ref, cbeta_ref,
                        o_ref, buf, shifted, conv_out, *, final_norm, n_tiles, tiles_per_batch, sub):
    j = pl.program_id(0)
    conv_tile = jnp.minimum(j, n_tiles - 1)
    first = (conv_tile % tiles_per_batch) == 0

    def conv_into(slot):
        def store(rows, y):
            conv_out[slot, rows, :] = y.astype(conv_out.dtype)
        return _conv_stages(halo_ref, vcur_ref, cw_ref, cb_ref, cg_ref, cbeta_ref, store, buf, shifted, first, sub)

    @pl.when(j == 0)
    def _():
        for stage in conv_into(0):
            stage()

    @pl.when(j > 0)
    def _():
        slot = j % 2
        conv = conv_into(slot)
        ffn = _outffn_stages(x_ref, a_ref, s_ref, lambda: conv_out[1 - slot], g1_ref, sh2_ref, sc2_ref, g2_ref,
                             n2_ref, wo_ref, wg_ref, wu_ref, wd_ref, fg_ref, o_ref, final_norm)
        conv[0]()
        for k in range(max(len(ffn), len(conv) - 1)):
            if k + 1 < len(conv):
                conv[k + 1]()
            if k < len(ffn):
                ffn[k]()


def _out_ffn(x, o_attn, o_ssm, o_conv, mod, p, layer, final_norm, tm, tiles_per_batch):
    t = x.shape[0]
    row = lambda i: (i, 0)
    resident = lambda shape: _layer_block(shape, layer, pipeline_mode=pl.Buffered(1))
    return pl.pallas_call(
        functools.partial(_outffn_kernel, final_norm=final_norm),
        grid=(t // tm,),
        in_specs=[
            pl.BlockSpec((tm, D_MODEL), row),
            pl.BlockSpec((tm, ATTN_WIDTH), row),
            pl.BlockSpec((tm, SSM_WIDTH), row),
            pl.BlockSpec((tm, CONV_WIDTH), row),
            _mod_spec(mod, layer, 2, tiles_per_batch),
            _mod_spec(mod, layer, 3, tiles_per_batch),
            _mod_spec(mod, layer, 4, tiles_per_batch),
            _mod_spec(mod, layer, 5, tiles_per_batch),
            _layer_block((1, D_MODEL), layer),
            resident((D_MODEL, D_MODEL)),
            resident((D_MODEL, D_FF)),
            resident((D_MODEL, D_FF)),
            resident((D_FF, D_MODEL)),
            pl.BlockSpec((1, D_MODEL), lambda i: (0, 0)),
        ],
        out_specs=pl.BlockSpec((tm, D_MODEL), row),
        out_shape=jax.ShapeDtypeStruct((t, D_MODEL), F32),
        compiler_params=_params("parallel"),
        name="out_projection_ffn",
    )(x, o_attn, o_ssm, o_conv, mod.array, mod.array, mod.array, mod.array, p['norm2_g'],
      p['w_out'], p['w_gate'], p['w_up'], p['w_down'], p['final_g'])


def _out_ffn_conv(x, o_attn, o_ssm, vc, mod, p, layer, final_norm, tm, tiles_per_batch):
    t = x.shape[0]
    n_tiles = t // tm
    halo_per_tile = tm // CONV_HALO
    ffn_tile = lambda j: jnp.maximum(j - 1, 0)
    conv_tile = lambda j: jnp.minimum(j, n_tiles - 1)
    ffn_row = lambda j: (ffn_tile(j), 0)
    resident = lambda shape: _layer_block(shape, layer, pipeline_mode=pl.Buffered(1))
    cvec = _layer_block((1, CONV_WIDTH), layer)
    span = tm + CONV_HALO - SUBLANES
    return pl.pallas_call(
        functools.partial(_outffn_conv_kernel, final_norm=final_norm, n_tiles=n_tiles,
                          tiles_per_batch=tiles_per_batch, sub=tm // 8),
        grid=(n_tiles + 1,),
        in_specs=[
            pl.BlockSpec((tm, D_MODEL), ffn_row),
            pl.BlockSpec((tm, ATTN_WIDTH), ffn_row),
            pl.BlockSpec((tm, SSM_WIDTH), ffn_row),
            pl.BlockSpec((CONV_HALO, CONV_WIDTH), lambda j: (jnp.maximum(conv_tile(j) * halo_per_tile - 1, 0), 0)),
            pl.BlockSpec((tm, CONV_WIDTH), lambda j: (conv_tile(j), 0)),
            _mod_spec(mod, layer, 2, tiles_per_batch, ffn_tile),
            _mod_spec(mod, layer, 3, tiles_per_batch, ffn_tile),
            _mod_spec(mod, layer, 4, tiles_per_batch, ffn_tile),
            _mod_spec(mod, layer, 5, tiles_per_batch, ffn_tile),
            _layer_block((1, D_MODEL), layer),
            resident((D_MODEL, D_MODEL)),
            resident((D_MODEL, D_FF)),
            resident((D_MODEL, D_FF)),
            resident((D_FF, D_MODEL)),
            pl.BlockSpec((1, D_MODEL), lambda j: (0, 0)),
            _layer_block((CONV_K, CONV_WIDTH), layer), cvec, cvec, cvec,
        ],
        out_specs=pl.BlockSpec((tm, D_MODEL), ffn_row),
        out_shape=jax.ShapeDtypeStruct((t, D_MODEL), F32),
        scratch_shapes=[pltpu.VMEM((CONV_HALO + tm, CONV_WIDTH), F32),
                        pltpu.VMEM((SUBLANES - 1, span, CONV_WIDTH), F32),
                        pltpu.VMEM((2, tm, CONV_WIDTH), MXU_DTYPE)],
        compiler_params=_params("arbitrary"),
        name="conv_out_projection_ffn",
    )(x, o_attn, o_ssm, vc, vc, mod.array, mod.array, mod.array, mod.array, p['norm2_g'],
      p['w_out'], p['w_gate'], p['w_up'], p['w_down'], p['final_g'],
      p['conv_w'], p['conv_b'], p['conv_ln_g'], p['conv_ln_b'])


def _layer_prompt(x, mod, p, layer, rope, batch, seq, final_norm):
    tm = min(512, seq)
    tiles_per_batch = seq // tm
    q, kv, u, vc = _in_projection(x, mod, p, layer, rope[0], rope[1], tm, tiles_per_batch)
    o_attn = _attention_prompt(q, kv, p['sinks'], layer, batch, seq)
    n_keep = min(WINDOW, seq)
    kv_keep = kv.reshape(batch, seq, 2 * KV_WIDTH)[:, seq - n_keep:]
    new_k = kv_keep[:, :, :KV_WIDTH].reshape(batch, n_keep, N_KV_HEADS, HEAD_DIM)
    new_v = kv_keep[:, :, KV_WIDTH:].reshape(batch, n_keep, N_KV_HEADS, HEAD_DIM)

    nseg = PROMPT_SEGMENTS
    lseg = seq // nseg
    nseq = batch * nseg
    u_seq = u.reshape(nseq, lseg, SSM_WIDTH)
    zero = jnp.zeros((nseq, SSM_FLAT), F32)
    steps = min(128, lseg)
    end_re, end_im = _s5_scan(u_seq, zero, zero, p, layer, False, nseq, steps)

    def shift(e):
        e = e.reshape(batch, nseg, SSM_FLAT)
        return jnp.concatenate([jnp.zeros_like(e[:, :1]), e[:, :1]], axis=1).reshape(nseq, SSM_FLAT)

    o_seq, h_re, h_im = _s5_scan(u_seq, shift(end_re), shift(end_im), p, layer, True, nseq, steps)
    o_ssm = o_seq.reshape(batch * seq, SSM_WIDTH)
    last = lambda h: h.reshape(batch, nseg, N_SSM_GROUPS, SSM_STATE)[:, nseg - 1]

    new_conv = vc.reshape(batch, seq, CONV_WIDTH)[:, seq - (CONV_K - 1):]

    x = _out_ffn_conv(x, o_attn, o_ssm, vc, mod, p, layer, final_norm, tm, tiles_per_batch)
    return x, new_k, new_v, last(h_re), last(h_im), new_conv


def _layer_sample(x, mod, p, layer, rope, batch, n_new, kt_all, vt_all, h0_re, h0_im, conv_all, final_norm):
    t = batch * n_new
    q, kv, u, vc = _in_projection(x, mod, p, layer, rope[0], rope[1], t, 1)

    q4 = jnp.transpose(q.reshape(n_new, batch, N_Q_HEADS, HEAD_DIM), (1, 2, 0, 3))
    q4 = q4.reshape(batch, N_KV_HEADS, Q_PER_KV * n_new, HEAD_DIM)
    zq = jnp.zeros_like(q4[:, 0])
    qexp = jnp.concatenate([jnp.concatenate([q4[:, 0], zq], axis=-1),
                            jnp.concatenate([zq, q4[:, 1]], axis=-1)], axis=1)
    kv3 = jnp.transpose(kv.reshape(n_new, batch, 2 * KV_WIDTH), (1, 0, 2))
    kv3 = jnp.pad(kv3, ((0, 0), (SUBLANES - n_new, 0), (0, 0)))
    o, new_k, new_v = _attention_sample(qexp, kt_all, vt_all, kv3[:, :, :KV_WIDTH], kv3[:, :, KV_WIDTH:],
                                        p['sinks'], layer, n_new)
    o = o[:, :, :HEAD_DIM].reshape(batch, N_Q_HEADS, n_new, HEAD_DIM)
    o_attn = jnp.transpose(o, (2, 0, 1, 3)).reshape(t, ATTN_WIDTH)

    o_ssm, h_re, h_im = _s5_scan(u, h0_re.reshape(batch, SSM_FLAT), h0_im.reshape(batch, SSM_FLAT),
                                 p, layer, True, batch, n_new)
    st = lambda h: h.reshape(batch, N_SSM_GROUPS, SSM_STATE)

    o_conv, new_conv = _conv_sample(conv_all, vc.reshape(n_new, batch, CONV_WIDTH), p, layer)
    o_conv = o_conv.reshape(t, CONV_WIDTH)

    x = _out_ffn(x, o_attn, o_ssm, o_conv, mod, p, layer, final_norm, t, 1)
    return x, new_k, new_v, st(h_re), st(h_im), new_conv


def kernel(x_prompt, x_sample, c_prompt, c_sample, cache_k, cache_v, state_ssm_re, state_ssm_im, state_conv,
           norm1_g, norm2_g, w_mod, b_mod, w_in, attn_sinks, ssm_lam_re, ssm_lam_im, ssm_log_dt,
           ssm_b_re, ssm_b_im, ssm_c_re, ssm_c_im, ssm_d, ssm_w_glu, ssm_b_glu,
           conv_w, conv_b, conv_ln_g, conv_ln_b, w_out, w_gate, w_up, w_down, final_norm_g):
    bp, seq, d = x_prompt.shape
    bs, n_new, _ = x_sample.shape
    depth = w_in.shape[0]
    assert PROMPT_SEGMENTS == 2 and seq % (PROMPT_SEGMENTS * SUBLANES) == 0

    c_all = jnp.concatenate([c_sample, c_prompt], axis=0)
    pad_rows = -c_all.shape[0] % SUBLANES
    mods = _modulation(jnp.pad(c_all, ((0, pad_rows), (0, 0))), w_mod, b_mod)
    mod_p = _Mod(mods[:, bs:bs + bp].reshape(depth, bp, 1, N_MOD * d), None)
    mod_s = _Mod(mods, bs)

    a_re, a_im, bre_blk, bim_blk, cre_blk, cim_blk = _s5_discretise(
        ssm_lam_re, ssm_lam_im, ssm_log_dt, ssm_b_re, ssm_b_im, ssm_c_re, ssm_c_im)
    rope_p = _rope_tables(jnp.arange(seq))
    rope_s = _rope_tables(jnp.repeat(PAST_LEN + jnp.arange(n_new), bs))

    cast = lambda a: a.astype(MXU_DTYPE)
    vec = lambda a: a.reshape(depth, 1, a.shape[-1])
    p = {
        'norm1_g': vec(norm1_g), 'norm2_g': vec(norm2_g), 'w_in': cast(w_in), 'sinks': attn_sinks,
        'a_re': a_re, 'a_im': a_im,
        'bre_blk': bre_blk, 'bim_blk': bim_blk,
        'cre_blk': cre_blk, 'cim_blk': cim_blk,
        'ssm_d': vec(ssm_d), 'ssm_w_glu': cast(ssm_w_glu), 'ssm_b_glu': vec(ssm_b_glu),
        'conv_w': conv_w, 'conv_b': vec(conv_b), 'conv_ln_g': vec(conv_ln_g), 'conv_ln_b': vec(conv_ln_b),
        'w_out': cast(w_out), 'w_gate': cast(w_gate), 'w_up': cast(w_up), 'w_down': cast(w_down),
        'final_g': final_norm_g.reshape(1, d),
    }
    conv_all = jnp.transpose(state_conv, (0, 2, 1, 3))
    kt_all = jnp.transpose(cache_k, (0, 1, 3, 4, 2))
    vt_all = jnp.transpose(cache_v, (0, 1, 3, 4, 2))

    xp = x_prompt.reshape(bp * seq, d)
    xs = jnp.transpose(x_sample, (1, 0, 2)).reshape(n_new * bs, d)
    outs_p, outs_s = [], []
    for l in range(depth):
        final = l == depth - 1
        xp, *op = _layer_prompt(xp, mod_p, p, l, rope_p, bp, seq, final)
        xs, *os_ = _layer_sample(xs, mod_s, p, l, rope_s, bs, n_new, kt_all, vt_all,
                                 state_ssm_re[l], state_ssm_im[l], conv_all, final)
        outs_p.append(op)
        outs_s.append(os_)
    stack = lambda outs, i: jnp.stack([o[i] for o in outs])
    y_sample = jnp.transpose(xs.reshape(n_new, bs, d), (1, 0, 2))
    new_k_s = jnp.transpose(stack(outs_s, 0), (0, 1, 4, 2, 3))
    new_v_s = jnp.transpose(stack(outs_s, 1), (0, 1, 4, 2, 3))
    new_conv_s = jnp.transpose(stack(outs_s, 4), (0, 2, 1, 3))
    return (xp.reshape(bp, seq, d), y_sample, *[stack(outs_p, i) for i in range(5)],
            new_k_s, new_v_s, stack(outs_s, 2), stack(outs_s, 3), new_conv_s)
```

```python
import functools
import math
from typing import NamedTuple

import jax
import jax.numpy as jnp
from jax import lax
from jax.experimental import pallas as pl
from jax.experimental.pallas import tpu as pltpu

F32 = jnp.float32
MXU_DTYPE = jnp.bfloat16

V7X_VMEM_BYTES = 64 * 1024 * 1024
VMEM_LIMIT_BYTES = V7X_VMEM_BYTES - 8 * 1024 * 1024
LANES = 128
SUBLANES = 8
V7X_MXU_DIM = 256

D_MODEL = 1024
HEAD_DIM = 64
N_Q_HEADS = 8
N_KV_HEADS = 2
Q_PER_KV = N_Q_HEADS // N_KV_HEADS
ATTN_WIDTH = N_Q_HEADS * HEAD_DIM
KV_WIDTH = N_KV_HEADS * HEAD_DIM
WINDOW = 128
ROPE_THETA = 10000.0
ATTN_SCALE = 1.0 / math.sqrt(HEAD_DIM)
SSM_WIDTH = 256
SSM_GROUP = 16
N_SSM_GROUPS = 16
SSM_STATE = 64
SSM_FLAT = N_SSM_GROUPS * SSM_STATE
CONV_WIDTH = 256
CONV_K = 31
CONV_HALO = 32
CONV_ROWS_PER_STAGE = 64
IN_WIDTH = ATTN_WIDTH + 2 * KV_WIDTH + SSM_WIDTH + 2 * CONV_WIDTH
D_FF = 2816
FF_CHUNK = 6 * V7X_MXU_DIM
EPS = 1e-6
NEG = -1e30
N_MOD = 6
PROMPT_SEGMENTS = 2
PAST_LEN = 8192


def _params(*semantics):
    return pltpu.CompilerParams(dimension_semantics=semantics, vmem_limit_bytes=VMEM_LIMIT_BYTES)


def _layer_block(shape, layer, **kw):
    zeros = (0,) * len(shape)
    return pl.BlockSpec((None,) + tuple(shape), lambda *_: (layer,) + zeros, **kw)


def _sigmoid(x):
    return 1.0 / (1.0 + jnp.exp(-x))


def _silu(x):
    return x * _sigmoid(x)


def _rms(x, g):
    return x * lax.rsqrt(jnp.mean(x * x, axis=-1, keepdims=True) + EPS) * g


def _mm(a, b):
    return jnp.dot(a, b, preferred_element_type=F32)


def _mm_nt(a, b):
    return lax.dot_general(a, b, (((1,), (1,)), ((), ())), preferred_element_type=F32)


def _rows(m, n):
    return m if m.shape[0] == 1 else jnp.concatenate([m] * (n // m.shape[0]), axis=0)


def _mod_kernel(c_ref, w_ref, b_ref, o_ref):
    a = _silu(c_ref[...]).astype(MXU_DTYPE)
    o_ref[...] = _mm(a, w_ref[...].astype(MXU_DTYPE)) + b_ref[...]


def _modulation(c, w_mod, b_mod):
    depth, d, n = w_mod.shape
    rows = c.shape[0]
    tn = 1536
    return pl.pallas_call(
        _mod_kernel,
        grid=(depth, n // tn),
        in_specs=[
            pl.BlockSpec((rows, d), lambda l, j: (0, 0)),
            pl.BlockSpec((None, d, tn), lambda l, j: (l, 0, j)),
            pl.BlockSpec((None, 1, tn), lambda l, j: (l, 0, j)),
        ],
        out_specs=pl.BlockSpec((None, rows, tn), lambda l, j: (l, 0, j)),
        out_shape=jax.ShapeDtypeStruct((depth, rows, n), F32),
        compiler_params=_params("parallel", "parallel"),
        name="modulation",
    )(c, w_mod, b_mod.reshape(depth, 1, n))


class _Mod(NamedTuple):
    array: jax.Array
    batch_rows: int | None


def _mod_spec(mod, layer, chunk, tiles_per_batch, tile=lambda i: i):
    if mod.batch_rows is None:
        return pl.BlockSpec((None, None, 1, D_MODEL), lambda i: (layer, tile(i) // tiles_per_batch, 0, chunk))
    return pl.BlockSpec((None, mod.batch_rows, D_MODEL), lambda i: (layer, 0, chunk))


def _inproj_kernel(x_ref, sh_ref, sc_ref, g_ref, w_ref, cos_ref, sin_ref, *outs, n_new):
    tm = x_ref.shape[0]
    h = _rms(x_ref[...], g_ref[...]) * (1.0 + _rows(sc_ref[...], tm)) + _rows(sh_ref[...], tm)
    z = _mm(h.astype(MXU_DTYPE), w_ref[...])
    cos = cos_ref[...]
    sin = sin_ref[...]
    lane = lax.broadcasted_iota(jnp.int32, (tm, LANES), 1)
    first_half = (lane % HEAD_DIM) < (HEAD_DIM // 2)

    def rope(t):
        partner = jnp.where(first_half, pltpu.roll(t, LANES - HEAD_DIM // 2, 1),
                            pltpu.roll(t, HEAD_DIM // 2, 1))
        return t * cos + partner * sin

    q_cols = [rope(z[:, j * LANES:(j + 1) * LANES]) * ATTN_SCALE for j in range(ATTN_WIDTH // LANES)]
    o = ATTN_WIDTH
    k_rot = rope(z[:, o:o + KV_WIDTH])
    v_new = z[:, o + KV_WIDTH:o + 2 * KV_WIDTH]
    if n_new is None:
        q_ref, kv_ref, u_ref, vc_ref = outs
        for j, qc in enumerate(q_cols):
            q_ref[:, j * LANES:(j + 1) * LANES] = qc.astype(q_ref.dtype)
        kv_ref[:, 0:KV_WIDTH] = k_rot
        kv_ref[:, KV_WIDTH:2 * KV_WIDTH] = v_new
    else:
        q_ref, kn_ref, vn_ref, u_ref, vc_ref = outs
        batch = tm // n_new
        low = lax.broadcasted_iota(jnp.int32, (batch, LANES), 1) < HEAD_DIM
        kn_ref[...] = jnp.zeros(kn_ref.shape, F32)
        vn_ref[...] = jnp.zeros(vn_ref.shape, F32)
        for t in range(n_new):
            rows = slice(t * batch, (t + 1) * batch)
            tile_row = pl.ds(SUBLANES - n_new + t, batch, stride=SUBLANES)
            kn_ref[tile_row, :] = k_rot[rows]
            vn_ref[tile_row, :] = v_new[rows]
            for h in range(N_Q_HEADS):
                kvh = h // Q_PER_KV
                piece = q_cols[h // 2][rows]
                if h % 2 != kvh:
                    piece = pltpu.roll(piece, HEAD_DIM, 1)
                piece = jnp.where(low if kvh == 0 else ~low, piece, 0.0)
                q_ref[pl.ds(h * n_new + t, batch, stride=N_Q_HEADS * n_new), :] = piece
    o += 2 * KV_WIDTH
    u_ref[...] = z[:, o:o + SSM_WIDTH]
    o += SSM_WIDTH
    za = z[:, o:o + CONV_WIDTH]
    zg = z[:, o + CONV_WIDTH:o + 2 * CONV_WIDTH]
    vc_ref[...] = za * _sigmoid(zg)


def _in_projection(x, mod, p, layer, cos, sin, tm, tiles_per_batch, n_new=None):
    t = x.shape[0]
    pos_tiles = cos.shape[0] // tm
    row = lambda i: (i, 0)
    if n_new is None:
        attn_specs = [pl.BlockSpec((tm, ATTN_WIDTH), row), pl.BlockSpec((tm, 2 * KV_WIDTH), row)]
        attn_shapes = [jax.ShapeDtypeStruct((t, ATTN_WIDTH), MXU_DTYPE),
                       jax.ShapeDtypeStruct((t, 2 * KV_WIDTH), F32)]
    else:
        assert tm == t and KV_WIDTH == LANES
        batch = t // n_new
        shapes = [(batch * N_Q_HEADS * n_new, LANES), (batch * SUBLANES, LANES), (batch * SUBLANES, LANES)]
        attn_specs = [pl.BlockSpec(s, lambda i: (0, 0)) for s in shapes]
        attn_shapes = [jax.ShapeDtypeStruct(s, F32) for s in shapes]
    return pl.pallas_call(
        functools.partial(_inproj_kernel, n_new=n_new),
        grid=(t // tm,),
        in_specs=[
            pl.BlockSpec((tm, D_MODEL), row),
            _mod_spec(mod, layer, 0, tiles_per_batch),
            _mod_spec(mod, layer, 1, tiles_per_batch),
            _layer_block((1, D_MODEL), layer),
            _layer_block((D_MODEL, IN_WIDTH), layer),
            pl.BlockSpec((tm, LANES), lambda i: (i % pos_tiles, 0)),
            pl.BlockSpec((tm, LANES), lambda i: (i % pos_tiles, 0)),
        ],
        out_specs=attn_specs + [pl.BlockSpec((tm, SSM_WIDTH), row), pl.BlockSpec((tm, CONV_WIDTH), row)],
        out_shape=attn_shapes + [jax.ShapeDtypeStruct((t, SSM_WIDTH), F32),
                                 jax.ShapeDtypeStruct((t, CONV_WIDTH), F32)],
        compiler_params=_params("parallel"),
        name="in_projection",
    )(x, mod.array, mod.array, p['norm1_g'], p['w_in'], cos, sin)


def _rope_tables(pos):
    half = HEAD_DIM // 2
    inv_freq = ROPE_THETA ** (-jnp.arange(half, dtype=F32) / half)
    ang = pos.astype(F32)[:, None] * inv_freq[None, :]
    cos = jnp.tile(jnp.cos(ang), (1, LANES // half))
    sin = jnp.sin(ang)
    sin = jnp.tile(jnp.concatenate([-sin, sin], axis=1), (1, LANES // HEAD_DIM))
    return cos, sin


def _sink_softmax(s, sink_col):
    m = jnp.maximum(jnp.max(s, axis=-1, keepdims=True), sink_col)
    e = jnp.exp(s - m)
    return e * (1.0 / (jnp.sum(e, axis=-1, keepdims=True) + jnp.exp(sink_col - m)))


def _head_pair_select(x, pick_second):
    lane = lax.broadcasted_iota(jnp.int32, x.shape, 1)
    swapped = pltpu.roll(x, HEAD_DIM, 1)
    low = lane < HEAD_DIM
    return jnp.where(low, swapped, x) if pick_second else jnp.where(low, x, swapped)


def _attn_prompt_kernel(sink_ref, q_ref, kvp_ref, kvc_ref, o_ref, *, layer):
    n = pl.program_id(1)
    w = WINDOW
    n_blocks = q_ref.shape[0] // w
    rows = Q_PER_KV * w
    r_idx = lax.broadcasted_iota(jnp.int32, (rows, w), 0) % w
    c_idx = lax.broadcasted_iota(jnp.int32, (rows, w), 1)
    from_prev = c_idx > r_idx
    first_bias = jnp.where(n > 0, 0.0, NEG)
    row_head = lax.broadcasted_iota(jnp.int32, (rows, 1), 0) // w
    low = lax.broadcasted_iota(jnp.int32, (w, LANES), 1) < HEAD_DIM
    half_mask = [jnp.where(low, 1.0, 0.0).astype(MXU_DTYPE), jnp.where(low, 0.0, 1.0).astype(MXU_DTYPE)]
    sink_cols = []
    for kvh in range(N_KV_HEADS):
        sink_col = jnp.zeros((rows, 1), F32)
        for g in range(Q_PER_KV):
            sink_col = jnp.where(row_head == g, sink_ref[layer, kvh * Q_PER_KV + g], sink_col)
        sink_cols.append(sink_col)
    chains = [(i, kvh) for i in range(n_blocks) for kvh in range(N_KV_HEADS)]
    windows, values = [], []
    for i, kvh in chains:
        own = slice(i * w, (i + 1) * w)
        prev_ref, prev = (kvp_ref, slice(0, w)) if i == 0 else (kvc_ref, slice((i - 1) * w, i * w))
        kk = jnp.concatenate([prev_ref[prev, 0:KV_WIDTH], kvc_ref[own, 0:KV_WIDTH]], axis=0)
        vv = jnp.concatenate([prev_ref[prev, KV_WIDTH:2 * KV_WIDTH], kvc_ref[own, KV_WIDTH:2 * KV_WIDTH]], axis=0)
        k2 = _head_pair_select(kk, kvh == 1).astype(MXU_DTYPE)
        values.append(_head_pair_select(vv, kvh == 1).astype(MXU_DTYPE))
        pieces = []
        for g in range(Q_PER_KV):
            h = kvh * Q_PER_KV + g
            qcol = q_ref[own, (h // 2) * LANES:(h // 2 + 1) * LANES]
            pieces.append(qcol * half_mask[h % 2])
        s = _mm_nt(jnp.concatenate(pieces, axis=0), k2)
        s_prev = s[:, 0:w] + first_bias if i == 0 else s[:, 0:w]
        windows.append(jnp.where(from_prev, s_prev, s[:, w:2 * w]))
    probs = []
    for (i, kvh), sc in zip(chains, windows):
        p = _sink_softmax(sc, sink_cols[kvh])
        p2 = jnp.concatenate([jnp.where(from_prev, p, 0.0), jnp.where(from_prev, 0.0, p)], axis=1)
        probs.append(p2.astype(MXU_DTYPE))
    for (i, kvh), p2, v2 in zip(chains, probs, values):
        own = slice(i * w, (i + 1) * w)
        r = _mm(p2, v2)
        for j in range(Q_PER_KV // 2):
            col = kvh * (Q_PER_KV // 2) + j
            o_ref[own, col * LANES:(col + 1) * LANES] = jnp.where(
                low, r[2 * j * w:(2 * j + 1) * w], r[(2 * j + 1) * w:(2 * j + 2) * w]).astype(o_ref.dtype)


def _attention_prompt(q, kv, sinks, layer, batch, seq):
    tq = min(4 * WINDOW, seq)
    nt = seq // tq
    per = tq // WINDOW
    return pl.pallas_call(
        functools.partial(_attn_prompt_kernel, layer=layer),
        grid=(batch, nt),
        in_specs=[
            pl.BlockSpec(memory_space=pltpu.SMEM),
            pl.BlockSpec((tq, ATTN_WIDTH), lambda b, n: (b * nt + n, 0)),
            pl.BlockSpec((WINDOW, 2 * KV_WIDTH), lambda b, n: ((b * nt + n) * per - jnp.minimum(n, 1), 0)),
            pl.BlockSpec((tq, 2 * KV_WIDTH), lambda b, n: (b * nt + n, 0)),
        ],
        out_specs=pl.BlockSpec((tq, ATTN_WIDTH), lambda b, n: (b * nt + n, 0)),
        out_shape=jax.ShapeDtypeStruct((batch * seq, ATTN_WIDTH), MXU_DTYPE),
        compiler_params=_params("parallel", "parallel"),
        name="attention_prompt",
    )(sinks, q, kv, kv)


def _attn_sample_kernel(sink_ref, q_ref, kt_ref, vt_ref, kn_ref, vn_ref, o_ref, nk_ref, nv_ref, *, layer, n_new):
    bt, rows, _ = q_ref.shape
    nkv, hd, wb = kt_ref.shape[1:]
    kept = wb - n_new
    r = lax.broadcasted_iota(jnp.int32, (bt * rows, 2 * wb), 0)
    j = lax.broadcasted_iota(jnp.int32, (bt * rows, 2 * wb), 1)
    t_idx = r % n_new
    t_new = j - wb - kept
    mask = ((j < wb) & (j > t_idx)) | ((t_new >= 0) & (t_new <= t_idx))
    row_head = (lax.broadcasted_iota(jnp.int32, (bt * rows, 1), 0) % rows) // n_new
    sink_col = jnp.zeros((bt * rows, 1), F32)
    for h in range(N_Q_HEADS):
        sink_col = jnp.where(row_head == h, sink_ref[layer, h], sink_col)
    keep = lax.broadcasted_iota(jnp.int32, (nkv, hd, wb), 2) < kept
    second_group = lax.broadcasted_iota(jnp.int32, (rows, nkv * hd), 0) >= (rows // nkv)
    flat = lambda a: a.reshape(nkv * hd, wb)
    above = jnp.zeros((wb - kn_ref.shape[1], nkv * hd), F32)

    scores, values = [], []
    for b in range(bt):
        kt, vt = kt_ref[b], vt_ref[b]
        kpos = jnp.concatenate([above, kn_ref[b]], axis=0).T
        vpos = jnp.concatenate([above, vn_ref[b]], axis=0).T
        nk_ref[b] = jnp.where(keep, pltpu.roll(kt, kept, 2), kpos.reshape(nkv, hd, wb))
        nv_ref[b] = jnp.where(keep, pltpu.roll(vt, kept, 2), vpos.reshape(nkv, hd, wb))
        kcat = jnp.concatenate([flat(kt), kpos], axis=1).astype(MXU_DTYPE)
        values.append(jnp.concatenate([flat(vt), vpos], axis=1).astype(MXU_DTYPE))
        scores.append(_mm(q_ref[b].astype(MXU_DTYPE), kcat))
    s = jnp.where(mask, jnp.concatenate(scores, axis=0), NEG)
    p = _sink_softmax(s, sink_col).astype(MXU_DTYPE)
    for b in range(bt):
        o = _mm_nt(p[b * rows:(b + 1) * rows], values[b])
        o_ref[b] = jnp.where(second_group, pltpu.roll(o, hd, 1), o).astype(o_ref.dtype)


def _attention_sample(qexp, kt_all, vt_all, kn, vn, sinks, layer, n_new):
    batch, rows, width = qexp.shape
    _, _, nkv, hd, wb = kt_all.shape
    bt = min(16, batch)
    blk3 = pl.BlockSpec((bt, rows, width), lambda i: (i, 0, 0))
    blk4 = pl.BlockSpec((bt, nkv, hd, wb), lambda i: (i, 0, 0, 0))
    new = pl.BlockSpec((bt,) + kn.shape[1:], lambda i: (i, 0, 0))
    cache = pl.BlockSpec((None, bt, nkv, hd, wb), lambda i: (layer, i, 0, 0, 0))
    return pl.pallas_call(
        functools.partial(_attn_sample_kernel, layer=layer, n_new=n_new),
        grid=(batch // bt,),
        in_specs=[pl.BlockSpec(memory_space=pltpu.SMEM), blk3, cache, cache, new, new],
        out_specs=[blk3, blk4, blk4],
        out_shape=[jax.ShapeDtypeStruct((batch, rows, width), MXU_DTYPE),
                   jax.ShapeDtypeStruct((batch, nkv, hd, wb), F32),
                   jax.ShapeDtypeStruct((batch, nkv, hd, wb), F32)],
        compiler_params=_params("parallel"),
        name="attention_sample",
    )(sinks, qexp, kt_all, vt_all, kn, vn)


def _s5_discretise_kernel(lr_ref, li_ref, ldt_ref, bre_ref, bim_ref, cre_ref, cim_ref,
                          are_ref, aim_ref, bbre_ref, bbim_ref, ccre_ref, ccim_ref):
    lr = lr_ref[...]
    li = li_ref[...]
    dt = jnp.exp(ldt_ref[...])
    mag = jnp.exp(lr * dt)
    ab_re = mag * jnp.cos(li * dt)
    ab_im = mag * jnp.sin(li * dt)
    den = lr * lr + li * li
    nr = ab_re - 1.0
    coef_re = (nr * lr + ab_im * li) / den
    coef_im = (ab_im * lr - nr * li) / den
    are_ref[...] = ab_re
    aim_ref[...] = ab_im
    br = bre_ref[...]
    bi = bim_ref[...]
    bb_re = coef_re * br - coef_im * bi
    bb_im = coef_re * bi + coef_im * br
    lane_group = lax.broadcasted_iota(jnp.int32, bb_re.shape, 1) // SSM_STATE
    for src, dst in ((bb_re, bbre_ref), (bb_im, bbim_ref), (cre_ref[...], ccre_ref), (cim_ref[...], ccim_ref)):
        for g in range(N_SSM_GROUPS):
            dst[g * SSM_GROUP:(g + 1) * SSM_GROUP, :] = jnp.where(lane_group == g, src, 0.0).astype(dst.dtype)


def _s5_discretise(lam_re, lam_im, log_dt, b_re, b_im, c_re, c_im):
    depth = lam_re.shape[0]
    flat = lambda a: a.reshape(depth, 1, SSM_FLAT)
    ldt = jnp.broadcast_to(log_dt[:, :, None], lam_re.shape)
    bt = lambda a: jnp.transpose(a, (0, 3, 1, 2)).reshape(depth, SSM_GROUP, SSM_FLAT)
    ct = lambda a: jnp.transpose(a, (0, 2, 1, 3)).reshape(depth, SSM_GROUP, SSM_FLAT)
    vec = pl.BlockSpec((None, 1, SSM_FLAT), lambda l: (l, 0, 0))
    mat = pl.BlockSpec((None, SSM_GROUP, SSM_FLAT), lambda l: (l, 0, 0))
    blk = pl.BlockSpec((None, SSM_WIDTH, SSM_FLAT), lambda l: (l, 0, 0))
    return pl.pallas_call(
        _s5_discretise_kernel,
        grid=(depth,),
        in_specs=[vec, vec, vec, mat, mat, mat, mat],
        out_specs=[vec, vec, blk, blk, blk, blk],
        out_shape=[jax.ShapeDtypeStruct((depth, 1, SSM_FLAT), F32)] * 2
        + [jax.ShapeDtypeStruct((depth, SSM_WIDTH, SSM_FLAT), MXU_DTYPE)] * 4,
        compiler_params=_params("parallel"),
        name="s5_discretise",
    )(flat(lam_re), flat(lam_im), flat(ldt), bt(b_re), bt(b_im), ct(c_re), ct(c_im))


def _s5_kernel(u_ref, hre0_ref, him0_ref, are_ref, aim_ref, bre_ref, bim_ref, *rest, nseq, emit, seq_major):
    if seq_major:
        *rest, stage_s = rest
    if emit:
        (cre_ref, cim_ref, d_ref, wglu_ref, bglu_ref,
         o_ref, hre_out, him_out, hre_s, him_s, bure_s, buim_s, hsre_s, hsim_s) = rest
    else:
        hre_out, him_out, hre_s, him_s, bure_s, buim_s = rest
    steps = u_ref.shape[1] if seq_major else u_ref.shape[0] // nseq
    lane_halves = [slice(h * LANES, (h + 1) * LANES) for h in range(SSM_WIDTH // LANES)]

    @pl.when(pl.program_id(0) == 0)
    def _():
        hre_s[...] = hre0_ref[...]
        him_s[...] = him0_ref[...]

    if seq_major:
        for s in range(nseq):
            for h, cols in enumerate(lane_halves):
                stage_s[h, pl.ds(s, steps, stride=nseq), :] = u_ref[s, :, cols]
        u = jnp.concatenate([stage_s[h] for h in range(len(lane_halves))], axis=1)
    else:
        u = u_ref[...]
    ub = u.astype(MXU_DTYPE)
    bure_s[...] = _mm(ub, bre_ref[...])
    buim_s[...] = _mm(ub, bim_ref[...])
    ar = jnp.broadcast_to(are_ref[...], (nseq, SSM_FLAT))
    ai = jnp.broadcast_to(aim_ref[...], (nseq, SSM_FLAT))

    def step(t, carry):
        hr, hi = carry
        rows = pl.ds(pl.multiple_of(t * nseq, nseq), nseq)
        nhr = ar * hr - ai * hi + bure_s[rows, :]
        nhi = ar * hi + ai * hr + buim_s[rows, :]
        if emit:
            hsre_s[rows, :] = nhr
            hsim_s[rows, :] = nhi
        return nhr, nhi

    hr, hi = lax.fori_loop(0, steps, step, (hre_s[...], him_s[...]), unroll=min(steps, 8))
    hre_s[...] = hr
    him_s[...] = hi
    hre_out[...] = hr
    him_out[...] = hi
    if emit:
        y = (_mm_nt(hsre_s[...].astype(MXU_DTYPE), cre_ref[...])
             - _mm_nt(hsim_s[...].astype(MXU_DTYPE), cim_ref[...]))
        z = jax.nn.gelu(y + d_ref[...] * u)
        gate = _mm(z.astype(MXU_DTYPE), wglu_ref[...]) + bglu_ref[...]
        out = z * _sigmoid(gate)
        if seq_major:
            for h, cols in enumerate(lane_halves):
                stage_s[h] = out[:, cols]
            for s in range(nseq):
                for h, cols in enumerate(lane_halves):
                    o_ref[s, :, cols] = stage_s[h, pl.ds(s, steps, stride=nseq), :].astype(o_ref.dtype)
        else:
            o_ref[...] = out.astype(o_ref.dtype)


def _s5_scan(u, hre0, him0, p, layer, emit, nseq, steps_per_tile):
    seq_major = u.ndim == 3
    rows = u.shape[0] * u.shape[1] if seq_major else u.shape[0]
    tr = steps_per_tile * nseq
    fixed = lambda i: (0, 0)
    state = pl.BlockSpec((nseq, SSM_FLAT), fixed)
    if seq_major:
        io_spec = pl.BlockSpec((nseq, steps_per_tile, SSM_WIDTH), lambda i: (0, i, 0))
    else:
        io_spec = pl.BlockSpec((tr, SSM_WIDTH), lambda i: (i, 0))
    in_specs = [io_spec, state, state,
                _layer_block((1, SSM_FLAT), layer), _layer_block((1, SSM_FLAT), layer),
                _layer_block((SSM_WIDTH, SSM_FLAT), layer), _layer_block((SSM_WIDTH, SSM_FLAT), layer)]
    args = [u, hre0, him0, p['a_re'], p['a_im'], p['bre_blk'], p['bim_blk']]
    out_specs = [state, state]
    out_shape = [jax.ShapeDtypeStruct((nseq, SSM_FLAT), F32)] * 2
    scratch = [pltpu.VMEM((nseq, SSM_FLAT), F32)] * 2 + [pltpu.VMEM((tr, SSM_FLAT), F32)] * 2
    if emit:
        in_specs += [_layer_block((SSM_WIDTH, SSM_FLAT), layer), _layer_block((SSM_WIDTH, SSM_FLAT), layer),
                     _layer_block((1, SSM_WIDTH), layer), _layer_block((SSM_WIDTH, SSM_WIDTH), layer),
                     _layer_block((1, SSM_WIDTH), layer)]
        args += [p['cre_blk'], p['cim_blk'], p['ssm_d'], p['ssm_w_glu'], p['ssm_b_glu']]
        out_specs = [io_spec] + out_specs
        out_shape = [jax.ShapeDtypeStruct(u.shape, MXU_DTYPE)] + out_shape
        scratch += [pltpu.VMEM((tr, SSM_FLAT), F32)] * 2
    if seq_major:
        scratch += [pltpu.VMEM((SSM_WIDTH // LANES, tr, LANES), F32)]
    return pl.pallas_call(
        functools.partial(_s5_kernel, nseq=nseq, emit=emit, seq_major=seq_major),
        grid=(rows // tr,),
        in_specs=in_specs,
        out_specs=out_specs,
        out_shape=out_shape,
        scratch_shapes=scratch,
        compiler_params=_params("arbitrary"),
        name="s5_scan" if emit else "s5_segment_states",
    )(*args)


def _layernorm_silu(y, g, b):
    yc = y - jnp.mean(y, axis=-1, keepdims=True)
    var = jnp.mean(yc * yc, axis=-1, keepdims=True)
    return _silu(yc * lax.rsqrt(var + EPS) * g + b)


def _conv_sample_kernel(state_ref, v_ref, w_ref, b_ref, g_ref, beta_ref, o_ref, ns_ref):
    ns = state_ref.shape[0]
    n_new = v_ref.shape[0]
    row = lambda j: state_ref[j] if j < ns else v_ref[j - ns]
    for t in range(n_new):
        acc = jnp.zeros(o_ref.shape[1:], F32)
        for k in range(CONV_K):
            acc = acc + w_ref[k:k + 1, :] * row(t + k)
        o_ref[t] = _layernorm_silu(acc + b_ref[...], g_ref[...], beta_ref[...]).astype(o_ref.dtype)
    for j in range(ns):
        ns_ref[j] = row(j + n_new)


def _conv_sample(state_all, v, p, layer):
    _, ns, batch, width = state_all.shape
    n_new = v.shape[0]
    bt = min(32, batch)
    vec = _layer_block((1, width), layer)
    return pl.pallas_call(
        _conv_sample_kernel,
        grid=(batch // bt,),
        in_specs=[
            pl.BlockSpec((None, ns, bt, width), lambda i: (layer, 0, i, 0)),
            pl.BlockSpec((n_new, bt, width), lambda i: (0, i, 0)),
            _layer_block((CONV_K, width), layer), vec, vec, vec,
        ],
        out_specs=[pl.BlockSpec((n_new, bt, width), lambda i: (0, i, 0)),
                   pl.BlockSpec((ns, bt, width), lambda i: (0, i, 0))],
        out_shape=[jax.ShapeDtypeStruct((n_new, batch, width), MXU_DTYPE),
                   jax.ShapeDtypeStruct((ns, batch, width), F32)],
        compiler_params=_params("parallel"),
        name="conv_sample",
    )(state_all, v, p['conv_w'], p['conv_b'], p['conv_ln_g'], p['conv_ln_b'])


def _outffn_stages(x_ref, a_ref, s_ref, load_conv, g1_ref, sh2_ref, sc2_ref, g2_ref, n2_ref,
                   wo_ref, wg_ref, wu_ref, wd_ref, fg_ref, o_ref, final_norm):
    tm = x_ref.shape[0]
    o1 = ATTN_WIDTH
    o2 = o1 + SSM_WIDTH
    st = {}

    def proj_attn_ssm():
        st['proj'] = _mm(a_ref[...], wo_ref[0:o1, :]) + _mm(s_ref[...], wo_ref[o1:o2, :])

    def proj_conv_residual_norm():
        proj = st.pop('proj') + _mm(load_conv(), wo_ref[o2:o2 + CONV_WIDTH, :])
        x2 = x_ref[...] + _rows(g1_ref[...], tm) * proj
        h2 = _rms(x2, n2_ref[...]) * (1.0 + _rows(sc2_ref[...], tm)) + _rows(sh2_ref[...], tm)
        st['x2'] = x2
        st['h2'] = h2.astype(MXU_DTYPE)

    def gate(cs):
        st['gate'] = _mm(st['h2'], wg_ref[:, cs])

    def up(cs):
        st['act'] = (_silu(st.pop('gate')) * _mm(st['h2'], wu_ref[:, cs])).astype(MXU_DTYPE)

    def down(cs, last):
        ffn = _mm(st.pop('act'), wd_ref[cs, :])
        st['ffn'] = ffn if 'ffn' not in st else st['ffn'] + ffn
        if last:
            y = st['x2'] + _rows(g2_ref[...], tm) * st['ffn']
            o_ref[...] = _rms(y, fg_ref[...]) if final_norm else y

    stages = [proj_attn_ssm, proj_conv_residual_norm]
    for start in range(0, D_FF, FF_CHUNK):
        cs = slice(start, min(start + FF_CHUNK, D_FF))
        stages += [functools.partial(gate, cs), functools.partial(up, cs),
                   functools.partial(down, cs, cs.stop == D_FF)]
    return stages


def _outffn_kernel(x_ref, a_ref, s_ref, c_ref, *rest, final_norm):
    for stage in _outffn_stages(x_ref, a_ref, s_ref, lambda: c_ref[...], *rest, final_norm):
        stage()


def _conv_stages(halo_ref, cur_ref, w_ref, b_ref, g_ref, beta_ref, store, buf, shifted, first, sub):
    tc = cur_ref.shape[0]
    base = CONV_HALO - (CONV_K - 1)
    span = shifted.shape[1]

    def fill():
        buf[0:CONV_HALO, :] = jnp.where(first, 0.0, halo_ref[...])
        buf[CONV_HALO:CONV_HALO + tc, :] = cur_ref[...]
        for s in range(1, SUBLANES):
            shifted[s - 1] = buf[s:s + span, :]

    def rows(r):
        acc = jnp.zeros((sub, CONV_WIDTH), F32)
        for k in range(CONV_K):
            j, s = (base + k) // SUBLANES, (base + k) % SUBLANES
            start = r * sub + j * SUBLANES
            tap = buf[start:start + sub, :] if s == 0 else shifted[s - 1, start:start + sub, :]
            acc = acc + w_ref[k:k + 1, :] * tap
        y = _layernorm_silu(acc + b_ref[...], g_ref[...], beta_ref[...])
        store(slice(r * sub, (r + 1) * sub), y)

    return [fill] + [functools.partial(rows, r) for r in range(tc // sub)]


def _outffn_conv_kernel(x_ref, a_ref, s_ref, halo_ref, vcur_ref, g1_ref, sh2_ref, sc2_ref, g2_ref, n2_ref,
                        wo_ref, wg_ref, wu_ref, wd_ref, fg_ref, cw_ref, cb_ref, cg_ref, cbeta_ref,
                        o_ref, buf, shifted, conv_out, *, final_norm, n_tiles, tiles_per_batch, sub):
    j = pl.program_id(0)
    conv_tile = jnp.minimum(j, n_tiles - 1)
    first = (conv_tile % tiles_per_batch) == 0

    def conv_into(slot):
        def store(rows, y):
            conv_out[slot, rows, :] = y.astype(conv_out.dtype)
        return _conv_stages(halo_ref, vcur_ref, cw_ref, cb_ref, cg_ref, cbeta_ref, store, buf, shifted, first, sub)

    @pl.when(j == 0)
    def _():
        for stage in conv_into(0):
            stage()

    @pl.when(j > 0)
    def _():
        slot = j % 2
        conv = conv_into(slot)
        ffn = _outffn_stages(x_ref, a_ref, s_ref, lambda: conv_out[1 - slot], g1_ref, sh2_ref, sc2_ref, g2_ref,
                             n2_ref, wo_ref, wg_ref, wu_ref, wd_ref, fg_ref, o_ref, final_norm)
        conv[0]()
        per_ffn = -(-(len(conv) - 1) // len(ffn))
        for k, ffn_stage in enumerate(ffn):
            for conv_stage in conv[1 + k * per_ffn:1 + (k + 1) * per_ffn]:
                conv_stage()
            ffn_stage()


def _out_ffn(x, o_attn, o_ssm, o_conv, mod, p, layer, final_norm, tm, tiles_per_batch):
    t = x.shape[0]
    row = lambda i: (i, 0)
    resident = lambda shape: _layer_block(shape, layer, pipeline_mode=pl.Buffered(1))
    return pl.pallas_call(
        functools.partial(_outffn_kernel, final_norm=final_norm),
        grid=(t // tm,),
        in_specs=[
            pl.BlockSpec((tm, D_MODEL), row),
            pl.BlockSpec((tm, ATTN_WIDTH), row),
            pl.BlockSpec((tm, SSM_WIDTH), row),
            pl.BlockSpec((tm, CONV_WIDTH), row),
            _mod_spec(mod, layer, 2, tiles_per_batch),
            _mod_spec(mod, layer, 3, tiles_per_batch),
            _mod_spec(mod, layer, 4, tiles_per_batch),
            _mod_spec(mod, layer, 5, tiles_per_batch),
            _layer_block((1, D_MODEL), layer),
            resident((D_MODEL, D_MODEL)),
            resident((D_MODEL, D_FF)),
            resident((D_MODEL, D_FF)),
            resident((D_FF, D_MODEL)),
            pl.BlockSpec((1, D_MODEL), lambda i: (0, 0)),
        ],
        out_specs=pl.BlockSpec((tm, D_MODEL), row),
        out_shape=jax.ShapeDtypeStruct((t, D_MODEL), F32),
        compiler_params=_params("parallel"),
        name="out_projection_ffn",
    )(x, o_attn, o_ssm, o_conv, mod.array, mod.array, mod.array, mod.array, p['norm2_g'],
      p['w_out'], p['w_gate'], p['w_up'], p['w_down'], p['final_g'])


def _out_ffn_conv(x, o_attn, o_ssm, vc, mod, p, layer, final_norm, tm, tiles_per_batch):
    t = x.shape[0]
    n_tiles = t // tm
    halo_per_tile = tm // CONV_HALO
    ffn_tile = lambda j: jnp.maximum(j - 1, 0)
    conv_tile = lambda j: jnp.minimum(j, n_tiles - 1)
    ffn_row = lambda j: (ffn_tile(j), 0)
    resident = lambda shape: _layer_block(shape, layer, pipeline_mode=pl.Buffered(1))
    cvec = _layer_block((1, CONV_WIDTH), layer)
    span = tm + CONV_HALO - SUBLANES
    return pl.pallas_call(
        functools.partial(_outffn_conv_kernel, final_norm=final_norm, n_tiles=n_tiles,
                          tiles_per_batch=tiles_per_batch, sub=CONV_ROWS_PER_STAGE),
        grid=(n_tiles + 1,),
        in_specs=[
            pl.BlockSpec((tm, D_MODEL), ffn_row),
            pl.BlockSpec((tm, ATTN_WIDTH), ffn_row),
            pl.BlockSpec((tm, SSM_WIDTH), ffn_row),
            pl.BlockSpec((CONV_HALO, CONV_WIDTH), lambda j: (jnp.maximum(conv_tile(j) * halo_per_tile - 1, 0), 0)),
            pl.BlockSpec((tm, CONV_WIDTH), lambda j: (conv_tile(j), 0)),
            _mod_spec(mod, layer, 2, tiles_per_batch, ffn_tile),
            _mod_spec(mod, layer, 3, tiles_per_batch, ffn_tile),
            _mod_spec(mod, layer, 4, tiles_per_batch, ffn_tile),
            _mod_spec(mod, layer, 5, tiles_per_batch, ffn_tile),
            _layer_block((1, D_MODEL), layer),
            resident((D_MODEL, D_MODEL)),
            resident((D_MODEL, D_FF)),
            resident((D_MODEL, D_FF)),
            resident((D_FF, D_MODEL)),
            pl.BlockSpec((1, D_MODEL), lambda j: (0, 0)),
            _layer_block((CONV_K, CONV_WIDTH), layer), cvec, cvec, cvec,
        ],
        out_specs=pl.BlockSpec((tm, D_MODEL), ffn_row),
        out_shape=jax.ShapeDtypeStruct((t, D_MODEL), F32),
        scratch_shapes=[pltpu.VMEM((CONV_HALO + tm, CONV_WIDTH), F32),
                        pltpu.VMEM((SUBLANES - 1, span, CONV_WIDTH), F32),
                        pltpu.VMEM((2, tm, CONV_WIDTH), MXU_DTYPE)],
        compiler_params=_params("arbitrary"),
        name="conv_out_projection_ffn",
    )(x, o_attn, o_ssm, vc, vc, mod.array, mod.array, mod.array, mod.array, p['norm2_g'],
      p['w_out'], p['w_gate'], p['w_up'], p['w_down'], p['final_g'],
      p['conv_w'], p['conv_b'], p['conv_ln_g'], p['conv_ln_b'])


def _layer_prompt(x, mod, p, layer, rope, batch, seq, final_norm):
    tm = min(512, seq)
    tiles_per_batch = seq // tm
    q, kv, u, vc = _in_projection(x, mod, p, layer, rope[0], rope[1], tm, tiles_per_batch)
    o_attn = _attention_prompt(q, kv, p['sinks'], layer, batch, seq)
    n_keep = min(WINDOW, seq)
    kv_keep = kv.reshape(batch, seq, 2 * KV_WIDTH)[:, seq - n_keep:]
    new_k = kv_keep[:, :, :KV_WIDTH].reshape(batch, n_keep, N_KV_HEADS, HEAD_DIM)
    new_v = kv_keep[:, :, KV_WIDTH:].reshape(batch, n_keep, N_KV_HEADS, HEAD_DIM)

    nseg = PROMPT_SEGMENTS
    lseg = seq // nseg
    nseq = batch * nseg
    u_seq = u.reshape(nseq, lseg, SSM_WIDTH)
    zero = jnp.zeros((nseq, SSM_FLAT), F32)
    steps = min(256, lseg)
    end_re, end_im = _s5_scan(u_seq, zero, zero, p, layer, False, nseq, steps)

    def shift(e):
        e = e.reshape(batch, nseg, SSM_FLAT)
        return jnp.concatenate([jnp.zeros_like(e[:, :1]), e[:, :1]], axis=1).reshape(nseq, SSM_FLAT)

    o_seq, h_re, h_im = _s5_scan(u_seq, shift(end_re), shift(end_im), p, layer, True, nseq, steps)
    o_ssm = o_seq.reshape(batch * seq, SSM_WIDTH)
    last = lambda h: h.reshape(batch, nseg, N_SSM_GROUPS, SSM_STATE)[:, nseg - 1]

    new_conv = vc.reshape(batch, seq, CONV_WIDTH)[:, seq - (CONV_K - 1):]

    x = _out_ffn_conv(x, o_attn, o_ssm, vc, mod, p, layer, final_norm, tm, tiles_per_batch)
    return x, new_k, new_v, last(h_re), last(h_im), new_conv


def _layer_sample(x, mod, p, layer, rope, batch, n_new, kt_all, vt_all, h0_re, h0_im, conv_all, final_norm):
    t = batch * n_new
    qexp, kn, vn, u, vc = _in_projection(x, mod, p, layer, rope[0], rope[1], t, 1, n_new)
    o, new_k, new_v = _attention_sample(
        qexp.reshape(batch, N_Q_HEADS * n_new, LANES), kt_all, vt_all,
        kn.reshape(batch, SUBLANES, LANES), vn.reshape(batch, SUBLANES, LANES), p['sinks'], layer, n_new)
    o = o[:, :, :HEAD_DIM].reshape(batch, N_Q_HEADS, n_new, HEAD_DIM)
    o_attn = jnp.transpose(o, (2, 0, 1, 3)).reshape(t, ATTN_WIDTH)

    o_ssm, h_re, h_im = _s5_scan(u, h0_re.reshape(batch, SSM_FLAT), h0_im.reshape(batch, SSM_FLAT),
                                 p, layer, True, batch, n_new)
    st = lambda h: h.reshape(batch, N_SSM_GROUPS, SSM_STATE)

    o_conv, new_conv = _conv_sample(conv_all, vc.reshape(n_new, batch, CONV_WIDTH), p, layer)
    o_conv = o_conv.reshape(t, CONV_WIDTH)

    x = _out_ffn(x, o_attn, o_ssm, o_conv, mod, p, layer, final_norm, t, 1)
    return x, new_k, new_v, st(h_re), st(h_im), new_conv


def kernel(x_prompt, x_sample, c_prompt, c_sample, cache_k, cache_v, state_ssm_re, state_ssm_im, state_conv,
           norm1_g, norm2_g, w_mod, b_mod, w_in, attn_sinks, ssm_lam_re, ssm_lam_im, ssm_log_dt,
           ssm_b_re, ssm_b_im, ssm_c_re, ssm_c_im, ssm_d, ssm_w_glu, ssm_b_glu,
           conv_w, conv_b, conv_ln_g, conv_ln_b, w_out, w_gate, w_up, w_down, final_norm_g):
    bp, seq, d = x_prompt.shape
    bs, n_new, _ = x_sample.shape
    depth = w_in.shape[0]
    assert PROMPT_SEGMENTS == 2 and seq % (PROMPT_SEGMENTS * SUBLANES) == 0

    c_all = jnp.concatenate([c_sample, c_prompt], axis=0)
    pad_rows = -c_all.shape[0] % SUBLANES
    mods = _modulation(jnp.pad(c_all, ((0, pad_rows), (0, 0))), w_mod, b_mod)
    mod_p = _Mod(mods[:, bs:bs + bp].reshape(depth, bp, 1, N_MOD * d), None)
    mod_s = _Mod(mods, bs)

    a_re, a_im, bre_blk, bim_blk, cre_blk, cim_blk = _s5_discretise(
        ssm_lam_re, ssm_lam_im, ssm_log_dt, ssm_b_re, ssm_b_im, ssm_c_re, ssm_c_im)
    rope_p = _rope_tables(jnp.arange(seq))
    rope_s = _rope_tables(jnp.repeat(PAST_LEN + jnp.arange(n_new), bs))

    cast = lambda a: a.astype(MXU_DTYPE)
    vec = lambda a: a.reshape(depth, 1, a.shape[-1])
    p = {
        'norm1_g': vec(norm1_g), 'norm2_g': vec(norm2_g), 'w_in': cast(w_in), 'sinks': attn_sinks,
        'a_re': a_re, 'a_im': a_im,
        'bre_blk': bre_blk, 'bim_blk': bim_blk,
        'cre_blk': cre_blk, 'cim_blk': cim_blk,
        'ssm_d': vec(ssm_d), 'ssm_w_glu': cast(ssm_w_glu), 'ssm_b_glu': vec(ssm_b_glu),
        'conv_w': conv_w, 'conv_b': vec(conv_b), 'conv_ln_g': vec(conv_ln_g), 'conv_ln_b': vec(conv_ln_b),
        'w_out': cast(w_out), 'w_gate': cast(w_gate), 'w_up': cast(w_up), 'w_down': cast(w_down),
        'final_g': final_norm_g.reshape(1, d),
    }
    conv_all = jnp.transpose(state_conv, (0, 2, 1, 3))
    kt_all = jnp.transpose(cache_k, (0, 1, 3, 4, 2))
    vt_all = jnp.transpose(cache_v, (0, 1, 3, 4, 2))

    xp = x_prompt.reshape(bp * seq, d)
    xs = jnp.transpose(x_sample, (1, 0, 2)).reshape(n_new * bs, d)
    outs_p, outs_s = [], []
    for l in range(depth):
        final = l == depth - 1
        xp, *op = _layer_prompt(xp, mod_p, p, l, rope_p, bp, seq, final)
        xs, *os_ = _layer_sample(xs, mod_s, p, l, rope_s, bs, n_new, kt_all, vt_all,
                                 state_ssm_re[l], state_ssm_im[l], conv_all, final)
        outs_p.append(op)
        outs_s.append(os_)
    stack = lambda outs, i: jnp.stack([o[i] for o in outs])
    y_sample = jnp.transpose(xs.reshape(n_new, bs, d), (1, 0, 2))
    new_k_s = jnp.transpose(stack(outs_s, 0), (0, 1, 4, 2, 3))
    new_v_s = jnp.transpose(stack(outs_s, 1), (0, 1, 4, 2, 3))
    new_conv_s = jnp.transpose(stack(outs_s, 4), (0, 2, 1, 3))
    return (xp.reshape(bp, seq, d), y_sample, *[stack(outs_p, i) for i in range(5)],
            new_k_s, new_v_s, stack(outs_s, 2), stack(outs_s, 3), new_conv_s)
```

```python
import functools
import math
from typing import NamedTuple

import jax
import jax.numpy as jnp
from jax import lax
from jax.experimental import pallas as pl
from jax.experimental.pallas import tpu as pltpu

F32 = jnp.float32
MXU_DTYPE = jnp.bfloat16

V7X_VMEM_BYTES = 64 * 1024 * 1024
VMEM_LIMIT_BYTES = V7X_VMEM_BYTES - 8 * 1024 * 1024
LANES = 128
SUBLANES = 8
V7X_MXU_DIM = 256

D_MODEL = 1024
HEAD_DIM = 64
N_Q_HEADS = 8
N_KV_HEADS = 2
Q_PER_KV = N_Q_HEADS // N_KV_HEADS
ATTN_WIDTH = N_Q_HEADS * HEAD_DIM
KV_WIDTH = N_KV_HEADS * HEAD_DIM
WINDOW = 128
ROPE_THETA = 10000.0
ATTN_SCALE = 1.0 / math.sqrt(HEAD_DIM)
SSM_WIDTH = 256
SSM_GROUP = 16
N_SSM_GROUPS = 16
SSM_STATE = 64
SSM_FLAT = N_SSM_GROUPS * SSM_STATE
CONV_WIDTH = 256
CONV_K = 31
CONV_HALO = 32
CONV_ROWS_PER_STAGE = 64
IN_WIDTH = ATTN_WIDTH + 2 * KV_WIDTH + SSM_WIDTH + 2 * CONV_WIDTH
D_FF = 2816
FF_CHUNK = 6 * V7X_MXU_DIM
EPS = 1e-6
NEG = -1e30
N_MOD = 6
PROMPT_SEGMENTS = 2
S5_STATE_CHUNK = 512
PAST_LEN = 8192


def _params(*semantics):
    return pltpu.CompilerParams(dimension_semantics=semantics, vmem_limit_bytes=VMEM_LIMIT_BYTES)


def _layer_block(shape, layer, **kw):
    zeros = (0,) * len(shape)
    return pl.BlockSpec((None,) + tuple(shape), lambda *_: (layer,) + zeros, **kw)


def _sigmoid(x):
    return 1.0 / (1.0 + jnp.exp(-x))


def _silu(x):
    return x * _sigmoid(x)


def _rms(x, g):
    return x * lax.rsqrt(jnp.mean(x * x, axis=-1, keepdims=True) + EPS) * g


def _mm(a, b):
    return jnp.dot(a, b, preferred_element_type=F32)


def _mm_nt(a, b):
    return lax.dot_general(a, b, (((1,), (1,)), ((), ())), preferred_element_type=F32)


def _rows(m, n):
    return m if m.shape[0] == 1 else jnp.concatenate([m] * (n // m.shape[0]), axis=0)


def _mod_kernel(c_ref, w_ref, b_ref, o_ref):
    a = _silu(c_ref[...]).astype(MXU_DTYPE)
    o_ref[...] = _mm(a, w_ref[...].astype(MXU_DTYPE)) + b_ref[...]


def _modulation(c, w_mod, b_mod):
    depth, d, n = w_mod.shape
    rows = c.shape[0]
    tn = 1536
    return pl.pallas_call(
        _mod_kernel,
        grid=(depth, n // tn),
        in_specs=[
            pl.BlockSpec((rows, d), lambda l, j: (0, 0)),
            pl.BlockSpec((None, d, tn), lambda l, j: (l, 0, j)),
            pl.BlockSpec((None, 1, tn), lambda l, j: (l, 0, j)),
        ],
        out_specs=pl.BlockSpec((None, rows, tn), lambda l, j: (l, 0, j)),
        out_shape=jax.ShapeDtypeStruct((depth, rows, n), F32),
        compiler_params=_params("parallel", "parallel"),
        name="modulation",
    )(c, w_mod, b_mod.reshape(depth, 1, n))


class _Mod(NamedTuple):
    array: jax.Array
    batch_rows: int | None


def _mod_spec(mod, layer, chunk, tiles_per_batch, tile=lambda i: i):
    if mod.batch_rows is None:
        return pl.BlockSpec((None, None, 1, D_MODEL), lambda i: (layer, tile(i) // tiles_per_batch, 0, chunk))
    return pl.BlockSpec((None, mod.batch_rows, D_MODEL), lambda i: (layer, 0, chunk))


def _inproj_kernel(x_ref, sh_ref, sc_ref, g_ref, w_ref, cos_ref, sin_ref, *outs, n_new):
    tm = x_ref.shape[0]
    h = _rms(x_ref[...], g_ref[...]) * (1.0 + _rows(sc_ref[...], tm)) + _rows(sh_ref[...], tm)
    z = _mm(h.astype(MXU_DTYPE), w_ref[...])
    cos = cos_ref[...]
    sin = sin_ref[...]
    lane = lax.broadcasted_iota(jnp.int32, (tm, LANES), 1)
    first_half = (lane % HEAD_DIM) < (HEAD_DIM // 2)

    def rope(t):
        partner = jnp.where(first_half, pltpu.roll(t, LANES - HEAD_DIM // 2, 1),
                            pltpu.roll(t, HEAD_DIM // 2, 1))
        return t * cos + partner * sin

    q_cols = [rope(z[:, j * LANES:(j + 1) * LANES]) * ATTN_SCALE for j in range(ATTN_WIDTH // LANES)]
    o = ATTN_WIDTH
    k_rot = rope(z[:, o:o + KV_WIDTH])
    v_new = z[:, o + KV_WIDTH:o + 2 * KV_WIDTH]
    if n_new is None:
        q_ref, kv_ref, u_ref, vc_ref = outs
        for j, qc in enumerate(q_cols):
            q_ref[:, j * LANES:(j + 1) * LANES] = qc.astype(q_ref.dtype)
        kv_ref[:, 0:KV_WIDTH] = k_rot
        kv_ref[:, KV_WIDTH:2 * KV_WIDTH] = v_new
    else:
        q_ref, kn_ref, vn_ref, u_ref, vc_ref = outs
        batch = tm // n_new
        low = lax.broadcasted_iota(jnp.int32, (batch, LANES), 1) < HEAD_DIM
        kn_ref[...] = jnp.zeros(kn_ref.shape, F32)
        vn_ref[...] = jnp.zeros(vn_ref.shape, F32)
        for t in range(n_new):
            rows = slice(t * batch, (t + 1) * batch)
            tile_row = pl.ds(SUBLANES - n_new + t, batch, stride=SUBLANES)
            kn_ref[tile_row, :] = k_rot[rows]
            vn_ref[tile_row, :] = v_new[rows]
            for h in range(N_Q_HEADS):
                kvh = h // Q_PER_KV
                piece = q_cols[h // 2][rows]
                if h % 2 != kvh:
                    piece = pltpu.roll(piece, HEAD_DIM, 1)
                piece = jnp.where(low if kvh == 0 else ~low, piece, 0.0)
                q_ref[pl.ds(h * n_new + t, batch, stride=N_Q_HEADS * n_new), :] = piece
    o += 2 * KV_WIDTH
    u_ref[...] = z[:, o:o + SSM_WIDTH]
    o += SSM_WIDTH
    za = z[:, o:o + CONV_WIDTH]
    zg = z[:, o + CONV_WIDTH:o + 2 * CONV_WIDTH]
    vc_ref[...] = za * _sigmoid(zg)


def _in_projection(x, mod, p, layer, cos, sin, tm, tiles_per_batch, n_new=None):
    t = x.shape[0]
    pos_tiles = cos.shape[0] // tm
    row = lambda i: (i, 0)
    if n_new is None:
        attn_specs = [pl.BlockSpec((tm, ATTN_WIDTH), row), pl.BlockSpec((tm, 2 * KV_WIDTH), row)]
        attn_shapes = [jax.ShapeDtypeStruct((t, ATTN_WIDTH), MXU_DTYPE),
                       jax.ShapeDtypeStruct((t, 2 * KV_WIDTH), F32)]
    else:
        assert tm == t and KV_WIDTH == LANES
        batch = t // n_new
        shapes = [(batch * N_Q_HEADS * n_new, LANES), (batch * SUBLANES, LANES), (batch * SUBLANES, LANES)]
        attn_specs = [pl.BlockSpec(s, lambda i: (0, 0)) for s in shapes]
        attn_shapes = [jax.ShapeDtypeStruct(s, F32) for s in shapes]
    return pl.pallas_call(
        functools.partial(_inproj_kernel, n_new=n_new),
        grid=(t // tm,),
        in_specs=[
            pl.BlockSpec((tm, D_MODEL), row),
            _mod_spec(mod, layer, 0, tiles_per_batch),
            _mod_spec(mod, layer, 1, tiles_per_batch),
            _layer_block((1, D_MODEL), layer),
            _layer_block((D_MODEL, IN_WIDTH), layer),
            pl.BlockSpec((tm, LANES), lambda i: (i % pos_tiles, 0)),
            pl.BlockSpec((tm, LANES), lambda i: (i % pos_tiles, 0)),
        ],
        out_specs=attn_specs + [pl.BlockSpec((tm, SSM_WIDTH), row), pl.BlockSpec((tm, CONV_WIDTH), row)],
        out_shape=attn_shapes + [jax.ShapeDtypeStruct((t, SSM_WIDTH), F32),
                                 jax.ShapeDtypeStruct((t, CONV_WIDTH), F32)],
        compiler_params=_params("parallel"),
        name="in_projection",
    )(x, mod.array, mod.array, p['norm1_g'], p['w_in'], cos, sin)


def _rope_tables(pos):
    half = HEAD_DIM // 2
    inv_freq = ROPE_THETA ** (-jnp.arange(half, dtype=F32) / half)
    ang = pos.astype(F32)[:, None] * inv_freq[None, :]
    cos = jnp.tile(jnp.cos(ang), (1, LANES // half))
    sin = jnp.sin(ang)
    sin = jnp.tile(jnp.concatenate([-sin, sin], axis=1), (1, LANES // HEAD_DIM))
    return cos, sin


def _sink_softmax(s, sink_col):
    m = jnp.maximum(jnp.max(s, axis=-1, keepdims=True), sink_col)
    e = jnp.exp(s - m)
    return e * (1.0 / (jnp.sum(e, axis=-1, keepdims=True) + jnp.exp(sink_col - m)))


def _head_pair_select(x, pick_second):
    lane = lax.broadcasted_iota(jnp.int32, x.shape, 1)
    swapped = pltpu.roll(x, HEAD_DIM, 1)
    low = lane < HEAD_DIM
    return jnp.where(low, swapped, x) if pick_second else jnp.where(low, x, swapped)


def _attn_prompt_kernel(sink_ref, q_ref, kvp_ref, kvc_ref, o_ref, *, layer):
    n = pl.program_id(1)
    w = WINDOW
    n_blocks = q_ref.shape[0] // w
    rows = Q_PER_KV * w
    r_idx = lax.broadcasted_iota(jnp.int32, (rows, w), 0) % w
    c_idx = lax.broadcasted_iota(jnp.int32, (rows, w), 1)
    from_prev = c_idx > r_idx
    first_bias = jnp.where(n > 0, 0.0, NEG)
    row_head = lax.broadcasted_iota(jnp.int32, (rows, 1), 0) // w
    low = lax.broadcasted_iota(jnp.int32, (w, LANES), 1) < HEAD_DIM
    half_mask = [jnp.where(low, 1.0, 0.0).astype(MXU_DTYPE), jnp.where(low, 0.0, 1.0).astype(MXU_DTYPE)]
    sink_cols = []
    for kvh in range(N_KV_HEADS):
        sink_col = jnp.zeros((rows, 1), F32)
        for g in range(Q_PER_KV):
            sink_col = jnp.where(row_head == g, sink_ref[layer, kvh * Q_PER_KV + g], sink_col)
        sink_cols.append(sink_col)
    chains = [(i, kvh) for i in range(n_blocks) for kvh in range(N_KV_HEADS)]
    windows, values = [], []
    for i, kvh in chains:
        own = slice(i * w, (i + 1) * w)
        prev_ref, prev = (kvp_ref, slice(0, w)) if i == 0 else (kvc_ref, slice((i - 1) * w, i * w))
        kk = jnp.concatenate([prev_ref[prev, 0:KV_WIDTH], kvc_ref[own, 0:KV_WIDTH]], axis=0)
        vv = jnp.concatenate([prev_ref[prev, KV_WIDTH:2 * KV_WIDTH], kvc_ref[own, KV_WIDTH:2 * KV_WIDTH]], axis=0)
        k2 = _head_pair_select(kk, kvh == 1).astype(MXU_DTYPE)
        values.append(_head_pair_select(vv, kvh == 1).astype(MXU_DTYPE))
        pieces = []
        for g in range(Q_PER_KV):
            h = kvh * Q_PER_KV + g
            qcol = q_ref[own, (h // 2) * LANES:(h // 2 + 1) * LANES]
            pieces.append(qcol * half_mask[h % 2])
        s = _mm_nt(jnp.concatenate(pieces, axis=0), k2)
        s_prev = s[:, 0:w] + first_bias if i == 0 else s[:, 0:w]
        windows.append(jnp.where(from_prev, s_prev, s[:, w:2 * w]))
    probs = []
    for (i, kvh), sc in zip(chains, windows):
        p = _sink_softmax(sc, sink_cols[kvh])
        p2 = jnp.concatenate([jnp.where(from_prev, p, 0.0), jnp.where(from_prev, 0.0, p)], axis=1)
        probs.append(p2.astype(MXU_DTYPE))
    for (i, kvh), p2, v2 in zip(chains, probs, values):
        own = slice(i * w, (i + 1) * w)
        r = _mm(p2, v2)
        for j in range(Q_PER_KV // 2):
            col = kvh * (Q_PER_KV // 2) + j
            o_ref[own, col * LANES:(col + 1) * LANES] = jnp.where(
                low, r[2 * j * w:(2 * j + 1) * w], r[(2 * j + 1) * w:(2 * j + 2) * w]).astype(o_ref.dtype)


def _attention_prompt(q, kv, sinks, layer, batch, seq):
    tq = min(4 * WINDOW, seq)
    nt = seq // tq
    per = tq // WINDOW
    return pl.pallas_call(
        functools.partial(_attn_prompt_kernel, layer=layer),
        grid=(batch, nt),
        in_specs=[
            pl.BlockSpec(memory_space=pltpu.SMEM),
            pl.BlockSpec((tq, ATTN_WIDTH), lambda b, n: (b * nt + n, 0)),
            pl.BlockSpec((WINDOW, 2 * KV_WIDTH), lambda b, n: ((b * nt + n) * per - jnp.minimum(n, 1), 0)),
            pl.BlockSpec((tq, 2 * KV_WIDTH), lambda b, n: (b * nt + n, 0)),
        ],
        out_specs=pl.BlockSpec((tq, ATTN_WIDTH), lambda b, n: (b * nt + n, 0)),
        out_shape=jax.ShapeDtypeStruct((batch * seq, ATTN_WIDTH), MXU_DTYPE),
        compiler_params=_params("parallel", "parallel"),
        name="attention_prompt",
    )(sinks, q, kv, kv)


def _attn_sample_kernel(sink_ref, q_ref, kt_ref, vt_ref, kn_ref, vn_ref, o_ref, nk_ref, nv_ref, *, layer, n_new):
    bt, rows, _ = q_ref.shape
    nkv, hd, wb = kt_ref.shape[1:]
    kept = wb - n_new
    r = lax.broadcasted_iota(jnp.int32, (bt * rows, 2 * wb), 0)
    j = lax.broadcasted_iota(jnp.int32, (bt * rows, 2 * wb), 1)
    t_idx = r % n_new
    t_new = j - wb - kept
    mask = ((j < wb) & (j > t_idx)) | ((t_new >= 0) & (t_new <= t_idx))
    row_head = (lax.broadcasted_iota(jnp.int32, (bt * rows, 1), 0) % rows) // n_new
    sink_col = jnp.zeros((bt * rows, 1), F32)
    for h in range(N_Q_HEADS):
        sink_col = jnp.where(row_head == h, sink_ref[layer, h], sink_col)
    keep = lax.broadcasted_iota(jnp.int32, (nkv, hd, wb), 2) < kept
    second_group = lax.broadcasted_iota(jnp.int32, (rows, nkv * hd), 0) >= (rows // nkv)
    flat = lambda a: a.reshape(nkv * hd, wb)
    above = jnp.zeros((wb - kn_ref.shape[1], nkv * hd), F32)

    scores, values = [], []
    for b in range(bt):
        kt, vt = kt_ref[b], vt_ref[b]
        kpos = jnp.concatenate([above, kn_ref[b]], axis=0).T
        vpos = jnp.concatenate([above, vn_ref[b]], axis=0).T
        nk_ref[b] = jnp.where(keep, pltpu.roll(kt, kept, 2), kpos.reshape(nkv, hd, wb))
        nv_ref[b] = jnp.where(keep, pltpu.roll(vt, kept, 2), vpos.reshape(nkv, hd, wb))
        kcat = jnp.concatenate([flat(kt), kpos], axis=1).astype(MXU_DTYPE)
        values.append(jnp.concatenate([flat(vt), vpos], axis=1).astype(MXU_DTYPE))
        scores.append(_mm(q_ref[b].astype(MXU_DTYPE), kcat))
    s = jnp.where(mask, jnp.concatenate(scores, axis=0), NEG)
    p = _sink_softmax(s, sink_col).astype(MXU_DTYPE)
    for b in range(bt):
        o = _mm_nt(p[b * rows:(b + 1) * rows], values[b])
        o_ref[b] = jnp.where(second_group, pltpu.roll(o, hd, 1), o).astype(o_ref.dtype)


def _attention_sample(qexp, kt_all, vt_all, kn, vn, sinks, layer, n_new):
    batch, rows, width = qexp.shape
    _, _, nkv, hd, wb = kt_all.shape
    bt = min(16, batch)
    blk3 = pl.BlockSpec((bt, rows, width), lambda i: (i, 0, 0))
    blk4 = pl.BlockSpec((bt, nkv, hd, wb), lambda i: (i, 0, 0, 0))
    new = pl.BlockSpec((bt,) + kn.shape[1:], lambda i: (i, 0, 0))
    cache = pl.BlockSpec((None, bt, nkv, hd, wb), lambda i: (layer, i, 0, 0, 0))
    return pl.pallas_call(
        functools.partial(_attn_sample_kernel, layer=layer, n_new=n_new),
        grid=(batch // bt,),
        in_specs=[pl.BlockSpec(memory_space=pltpu.SMEM), blk3, cache, cache, new, new],
        out_specs=[blk3, blk4, blk4],
        out_shape=[jax.ShapeDtypeStruct((batch, rows, width), MXU_DTYPE),
                   jax.ShapeDtypeStruct((batch, nkv, hd, wb), F32),
                   jax.ShapeDtypeStruct((batch, nkv, hd, wb), F32)],
        compiler_params=_params("parallel"),
        name="attention_sample",
    )(sinks, qexp, kt_all, vt_all, kn, vn)


def _cmul(ar, ai, br, bi):
    return ar * br - ai * bi, ar * bi + ai * br


def _s5_discretise_kernel(lr_ref, li_ref, ldt_ref, bre_ref, bim_ref, cre_ref, cim_ref,
                          are_ref, aim_ref, bbre_ref, bbim_ref, ccre_ref, ccim_ref,
                          decay_re_ref, decay_im_ref, span_re_ref, span_im_ref):
    lr = lr_ref[...]
    li = li_ref[...]
    dt = jnp.exp(ldt_ref[...])
    mag = jnp.exp(lr * dt)
    ab_re = mag * jnp.cos(li * dt)
    ab_im = mag * jnp.sin(li * dt)
    den = lr * lr + li * li
    nr = ab_re - 1.0
    coef_re = (nr * lr + ab_im * li) / den
    coef_im = (ab_im * lr - nr * li) / den
    are_ref[...] = ab_re
    aim_ref[...] = ab_im
    br = bre_ref[...]
    bi = bim_ref[...]
    bb_re = coef_re * br - coef_im * bi
    bb_im = coef_re * bi + coef_im * br
    lane_group = lax.broadcasted_iota(jnp.int32, bb_re.shape, 1) // SSM_STATE
    for src, dst in ((bb_re, bbre_ref), (bb_im, bbim_ref), (cre_ref[...], ccre_ref), (cim_ref[...], ccim_ref)):
        for g in range(N_SSM_GROUPS):
            dst[g * SSM_GROUP:(g + 1) * SSM_GROUP, :] = jnp.where(lane_group == g, src, 0.0).astype(dst.dtype)
    t_chunk = decay_re_ref.shape[0]
    row = lax.broadcasted_iota(jnp.int32, (SUBLANES, SSM_FLAT), 0)
    pr, pi = jnp.ones_like(ab_re), jnp.zeros_like(ab_re)
    tile_re, tile_im = jnp.zeros((SUBLANES, SSM_FLAT), F32), jnp.zeros((SUBLANES, SSM_FLAT), F32)
    for k in range(SUBLANES):
        tile_re = jnp.where(row == SUBLANES - 1 - k, pr, tile_re)
        tile_im = jnp.where(row == SUBLANES - 1 - k, pi, tile_im)
        pr, pi = _cmul(pr, pi, ab_re, ab_im)
    decay_re_ref[t_chunk - SUBLANES:t_chunk, :] = tile_re
    decay_im_ref[t_chunk - SUBLANES:t_chunk, :] = tile_im
    n = SUBLANES
    while n < t_chunk:
        lo, hi = slice(t_chunk - 2 * n, t_chunk - n), slice(t_chunk - n, t_chunk)
        decay_re_ref[lo, :], decay_im_ref[lo, :] = _cmul(decay_re_ref[hi, :], decay_im_ref[hi, :], pr, pi)
        pr, pi = _cmul(pr, pi, pr, pi)
        n *= 2
    span_re_ref[...] = pr
    span_im_ref[...] = pi


def _s5_discretise(lam_re, lam_im, log_dt, b_re, b_im, c_re, c_im, t_chunk):
    depth = lam_re.shape[0]
    assert t_chunk % SUBLANES == 0 and (t_chunk // SUBLANES) & (t_chunk // SUBLANES - 1) == 0
    flat = lambda a: a.reshape(depth, 1, SSM_FLAT)
    ldt = jnp.broadcast_to(log_dt[:, :, None], lam_re.shape)
    bt = lambda a: jnp.transpose(a, (0, 3, 1, 2)).reshape(depth, SSM_GROUP, SSM_FLAT)
    ct = lambda a: jnp.transpose(a, (0, 2, 1, 3)).reshape(depth, SSM_GROUP, SSM_FLAT)
    vec = pl.BlockSpec((None, 1, SSM_FLAT), lambda l: (l, 0, 0))
    mat = pl.BlockSpec((None, SSM_GROUP, SSM_FLAT), lambda l: (l, 0, 0))
    blk = pl.BlockSpec((None, SSM_WIDTH, SSM_FLAT), lambda l: (l, 0, 0))
    tab = pl.BlockSpec((None, t_chunk, SSM_FLAT), lambda l: (l, 0, 0))
    return pl.pallas_call(
        _s5_discretise_kernel,
        grid=(depth,),
        in_specs=[vec, vec, vec, mat, mat, mat, mat],
        out_specs=[vec, vec, blk, blk, blk, blk, tab, tab, vec, vec],
        out_shape=[jax.ShapeDtypeStruct((depth, 1, SSM_FLAT), F32)] * 2
        + [jax.ShapeDtypeStruct((depth, SSM_WIDTH, SSM_FLAT), MXU_DTYPE)] * 4
        + [jax.ShapeDtypeStruct((depth, t_chunk, SSM_FLAT), F32)] * 2
        + [jax.ShapeDtypeStruct((depth, 1, SSM_FLAT), F32)] * 2,
        compiler_params=_params("parallel"),
        name="s5_discretise",
    )(flat(lam_re), flat(lam_im), flat(ldt), bt(b_re), bt(b_im), ct(c_re), ct(c_im))


def _s5_first_segment_kernel(u_ref, bre_ref, bim_ref, dre_ref, dim_ref, sre_ref, sim_ref, ere_ref, eim_ref,
                             hre_s, him_s):
    @pl.when(pl.program_id(1) == 0)
    def _():
        hre_s[...] = jnp.zeros(hre_s.shape, F32)
        him_s[...] = jnp.zeros(him_s.shape, F32)

    ub = u_ref[...].astype(MXU_DTYPE)
    wr, wi = _cmul(dre_ref[...], dim_ref[...], _mm(ub, bre_ref[...]), _mm(ub, bim_ref[...]))
    cr, ci = _cmul(sre_ref[...], sim_ref[...], hre_s[...], him_s[...])
    hr = cr + jnp.sum(wr, axis=0, keepdims=True)
    hi = ci + jnp.sum(wi, axis=0, keepdims=True)
    hre_s[...] = hr
    him_s[...] = hi
    ere_ref[...] = hr
    eim_ref[...] = hi


def _s5_first_segment_states(u_seq, p, layer, nseg):
    nseq, lseg, _ = u_seq.shape
    batch = nseq // nseg
    t_chunk = p['decay_re'].shape[1]
    vec = _layer_block((1, SSM_FLAT), layer)
    tab = _layer_block((t_chunk, SSM_FLAT), layer)
    bmat = _layer_block((SSM_WIDTH, SSM_FLAT), layer)
    out = pl.BlockSpec((None, 1, SSM_FLAT), lambda b, c: (b, 0, 0))
    return pl.pallas_call(
        _s5_first_segment_kernel,
        grid=(batch, lseg // t_chunk),
        in_specs=[pl.BlockSpec((None, t_chunk, SSM_WIDTH), lambda b, c: (b * nseg, c, 0)),
                  bmat, bmat, tab, tab, vec, vec],
        out_specs=[out, out],
        out_shape=[jax.ShapeDtypeStruct((batch, 1, SSM_FLAT), F32)] * 2,
        scratch_shapes=[pltpu.VMEM((1, SSM_FLAT), F32)] * 2,
        compiler_params=_params("parallel", "arbitrary"),
        name="s5_first_segment_states",
    )(u_seq, p['bre_blk'], p['bim_blk'], p['decay_re'], p['decay_im'], p['span_re'], p['span_im'])


def _s5_kernel(u_ref, hre0_ref, him0_ref, are_ref, aim_ref, bre_ref, bim_ref, cre_ref, cim_ref, d_ref,
               wglu_ref, bglu_ref, o_ref, hre_out, him_out, hre_s, him_s, bure_s, buim_s, hsre_s, hsim_s,
               *stage, nseq):
    seq_major = bool(stage)
    steps = u_ref.shape[1] if seq_major else u_ref.shape[0] // nseq
    lane_halves = [slice(h * LANES, (h + 1) * LANES) for h in range(SSM_WIDTH // LANES)]

    @pl.when(pl.program_id(0) == 0)
    def _():
        hre_s[...] = hre0_ref[...]
        him_s[...] = him0_ref[...]

    if seq_major:
        stage_s, = stage
        for s in range(nseq):
            for h, cols in enumerate(lane_halves):
                stage_s[h, pl.ds(s, steps, stride=nseq), :] = u_ref[s, :, cols]
        u = jnp.concatenate([stage_s[h] for h in range(len(lane_halves))], axis=1)
    else:
        u = u_ref[...]
    ub = u.astype(MXU_DTYPE)
    bure_s[...] = _mm(ub, bre_ref[...])
    buim_s[...] = _mm(ub, bim_ref[...])
    ar = jnp.broadcast_to(are_ref[...], (nseq, SSM_FLAT))
    ai = jnp.broadcast_to(aim_ref[...], (nseq, SSM_FLAT))

    def step(t, carry):
        hr, hi = carry
        rows = pl.ds(pl.multiple_of(t * nseq, nseq), nseq)
        nhr = ar * hr - ai * hi + bure_s[rows, :]
        nhi = ar * hi + ai * hr + buim_s[rows, :]
        hsre_s[rows, :] = nhr
        hsim_s[rows, :] = nhi
        return nhr, nhi

    hr, hi = lax.fori_loop(0, steps, step, (hre_s[...], him_s[...]), unroll=min(steps, 8))
    hre_s[...] = hr
    him_s[...] = hi
    hre_out[...] = hr
    him_out[...] = hi
    y = (_mm_nt(hsre_s[...].astype(MXU_DTYPE), cre_ref[...])
         - _mm_nt(hsim_s[...].astype(MXU_DTYPE), cim_ref[...]))
    z = jax.nn.gelu(y + d_ref[...] * u)
    gate = _mm(z.astype(MXU_DTYPE), wglu_ref[...]) + bglu_ref[...]
    out = z * _sigmoid(gate)
    if seq_major:
        for h, cols in enumerate(lane_halves):
            stage_s[h] = out[:, cols]
        for s in range(nseq):
            for h, cols in enumerate(lane_halves):
                o_ref[s, :, cols] = stage_s[h, pl.ds(s, steps, stride=nseq), :].astype(o_ref.dtype)
    else:
        o_ref[...] = out.astype(o_ref.dtype)


def _s5_scan(u, hre0, him0, p, layer, nseq, steps_per_tile):
    seq_major = u.ndim == 3
    rows = u.shape[0] * u.shape[1] if seq_major else u.shape[0]
    tr = steps_per_tile * nseq
    state = pl.BlockSpec((nseq, SSM_FLAT), lambda i: (0, 0))
    if seq_major:
        io_spec = pl.BlockSpec((nseq, steps_per_tile, SSM_WIDTH), lambda i: (0, i, 0))
    else:
        io_spec = pl.BlockSpec((tr, SSM_WIDTH), lambda i: (i, 0))
    state_vec = _layer_block((1, SSM_FLAT), layer)
    group_map = _layer_block((SSM_WIDTH, SSM_FLAT), layer)
    width_vec = _layer_block((1, SSM_WIDTH), layer)
    scratch = ([pltpu.VMEM((nseq, SSM_FLAT), F32)] * 2 + [pltpu.VMEM((tr, SSM_FLAT), F32)] * 4
               + [pltpu.VMEM((SSM_WIDTH // LANES, tr, LANES), F32)] * seq_major)
    return pl.pallas_call(
        functools.partial(_s5_kernel, nseq=nseq),
        grid=(rows // tr,),
        in_specs=[io_spec, state, state, state_vec, state_vec, group_map, group_map, group_map, group_map,
                  width_vec, _layer_block((SSM_WIDTH, SSM_WIDTH), layer), width_vec],
        out_specs=[io_spec, state, state],
        out_shape=[jax.ShapeDtypeStruct(u.shape, MXU_DTYPE)] + [jax.ShapeDtypeStruct((nseq, SSM_FLAT), F32)] * 2,
        scratch_shapes=scratch,
        compiler_params=_params("arbitrary"),
        name="s5_scan",
    )(u, hre0, him0, p['a_re'], p['a_im'], p['bre_blk'], p['bim_blk'], p['cre_blk'], p['cim_blk'],
      p['ssm_d'], p['ssm_w_glu'], p['ssm_b_glu'])


def _layernorm_silu(y, g, b):
    yc = y - jnp.mean(y, axis=-1, keepdims=True)
    var = jnp.mean(yc * yc, axis=-1, keepdims=True)
    return _silu(yc * lax.rsqrt(var + EPS) * g + b)


def _conv_sample_kernel(state_ref, v_ref, w_ref, b_ref, g_ref, beta_ref, o_ref, ns_ref):
    ns = state_ref.shape[0]
    n_new = v_ref.shape[0]
    row = lambda j: state_ref[j] if j < ns else v_ref[j - ns]
    for t in range(n_new):
        acc = jnp.zeros(o_ref.shape[1:], F32)
        for k in range(CONV_K):
            acc = acc + w_ref[k:k + 1, :] * row(t + k)
        o_ref[t] = _layernorm_silu(acc + b_ref[...], g_ref[...], beta_ref[...]).astype(o_ref.dtype)
    for j in range(ns):
        ns_ref[j] = row(j + n_new)


def _conv_sample(state_all, v, p, layer):
    _, ns, batch, width = state_all.shape
    n_new = v.shape[0]
    bt = min(32, batch)
    vec = _layer_block((1, width), layer)
    return pl.pallas_call(
        _conv_sample_kernel,
        grid=(batch // bt,),
        in_specs=[
            pl.BlockSpec((None, ns, bt, width), lambda i: (layer, 0, i, 0)),
            pl.BlockSpec((n_new, bt, width), lambda i: (0, i, 0)),
            _layer_block((CONV_K, width), layer), vec, vec, vec,
        ],
        out_specs=[pl.BlockSpec((n_new, bt, width), lambda i: (0, i, 0)),
                   pl.BlockSpec((ns, bt, width), lambda i: (0, i, 0))],
        out_shape=[jax.ShapeDtypeStruct((n_new, batch, width), MXU_DTYPE),
                   jax.ShapeDtypeStruct((ns, batch, width), F32)],
        compiler_params=_params("parallel"),
        name="conv_sample",
    )(state_all, v, p['conv_w'], p['conv_b'], p['conv_ln_g'], p['conv_ln_b'])


def _outffn_stages(x_ref, a_ref, s_ref, load_conv, g1_ref, sh2_ref, sc2_ref, g2_ref, n2_ref,
                   wo_ref, wg_ref, wu_ref, wd_ref, fg_ref, o_ref, final_norm):
    tm = x_ref.shape[0]
    o1 = ATTN_WIDTH
    o2 = o1 + SSM_WIDTH
    st = {}

    def proj_attn_ssm():
        st['proj'] = _mm(a_ref[...], wo_ref[0:o1, :]) + _mm(s_ref[...], wo_ref[o1:o2, :])

    def proj_conv_residual_norm():
        proj = st.pop('proj') + _mm(load_conv(), wo_ref[o2:o2 + CONV_WIDTH, :])
        x2 = x_ref[...] + _rows(g1_ref[...], tm) * proj
        h2 = _rms(x2, n2_ref[...]) * (1.0 + _rows(sc2_ref[...], tm)) + _rows(sh2_ref[...], tm)
        st['x2'] = x2
        st['h2'] = h2.astype(MXU_DTYPE)

    def gate(cs):
        st['gate'] = _mm(st['h2'], wg_ref[:, cs])

    def up(cs):
        st['act'] = (_silu(st.pop('gate')) * _mm(st['h2'], wu_ref[:, cs])).astype(MXU_DTYPE)

    def down(cs, last):
        ffn = _mm(st.pop('act'), wd_ref[cs, :])
        st['ffn'] = ffn if 'ffn' not in st else st['ffn'] + ffn
        if last:
            y = st['x2'] + _rows(g2_ref[...], tm) * st['ffn']
            o_ref[...] = _rms(y, fg_ref[...]) if final_norm else y

    stages = [proj_attn_ssm, proj_conv_residual_norm]
    for start in range(0, D_FF, FF_CHUNK):
        cs = slice(start, min(start + FF_CHUNK, D_FF))
        stages += [functools.partial(gate, cs), functools.partial(up, cs),
                   functools.partial(down, cs, cs.stop == D_FF)]
    return stages


def _outffn_kernel(x_ref, a_ref, s_ref, c_ref, *rest, final_norm):
    for stage in _outffn_stages(x_ref, a_ref, s_ref, lambda: c_ref[...], *rest, final_norm):
        stage()


def _conv_stages(halo_ref, cur_ref, w_ref, b_ref, g_ref, beta_ref, store, buf, shifted, first, sub):
    tc = cur_ref.shape[0]
    base = CONV_HALO - (CONV_K - 1)
    span = shifted.shape[1]

    def fill():
        buf[0:CONV_HALO, :] = jnp.where(first, 0.0, halo_ref[...])
        buf[CONV_HALO:CONV_HALO + tc, :] = cur_ref[...]
        for s in range(1, SUBLANES):
            shifted[s - 1] = buf[s:s + span, :]

    def rows(r):
        acc = jnp.zeros((sub, CONV_WIDTH), F32)
        for k in range(CONV_K):
            j, s = (base + k) // SUBLANES, (base + k) % SUBLANES
            start = r * sub + j * SUBLANES
            tap = buf[start:start + sub, :] if s == 0 else shifted[s - 1, start:start + sub, :]
            acc = acc + w_ref[k:k + 1, :] * tap
        y = _layernorm_silu(acc + b_ref[...], g_ref[...], beta_ref[...])
        store(slice(r * sub, (r + 1) * sub), y)

    return [fill] + [functools.partial(rows, r) for r in range(tc // sub)]


def _outffn_conv_kernel(x_ref, a_ref, s_ref, halo_ref, vcur_ref, g1_ref, sh2_ref, sc2_ref, g2_ref, n2_ref,
                        wo_ref, wg_ref, wu_ref, wd_ref, fg_ref, cw_ref, cb_ref, cg_ref, cbeta_ref,
                        o_ref, buf, shifted, conv_out, *, final_norm, n_tiles, tiles_per_batch, sub):
    j = pl.program_id(0)
    conv_tile = jnp.minimum(j, n_tiles - 1)
    first = (conv_tile % tiles_per_batch) == 0

    def conv_into(slot):
        def store(rows, y):
            conv_out[slot, rows, :] = y.astype(conv_out.dtype)
        return _conv_stages(halo_ref, vcur_ref, cw_ref, cb_ref, cg_ref, cbeta_ref, store, buf, shifted, first, sub)

    @pl.when(j == 0)
    def _():
        for stage in conv_into(0):
            stage()

    @pl.when(j > 0)
    def _():
        slot = j % 2
        conv = conv_into(slot)
        ffn = _outffn_stages(x_ref, a_ref, s_ref, lambda: conv_out[1 - slot], g1_ref, sh2_ref, sc2_ref, g2_ref,
                             n2_ref, wo_ref, wg_ref, wu_ref, wd_ref, fg_ref, o_ref, final_norm)
        conv[0]()
        per_ffn = -(-(len(conv) - 1) // len(ffn))
        for k, ffn_stage in enumerate(ffn):
            for conv_stage in conv[1 + k * per_ffn:1 + (k + 1) * per_ffn]:
                conv_stage()
            ffn_stage()


def _out_ffn(x, o_attn, o_ssm, o_conv, mod, p, layer, final_norm, tm, tiles_per_batch):
    t = x.shape[0]
    row = lambda i: (i, 0)
    resident = lambda shape: _layer_block(shape, layer, pipeline_mode=pl.Buffered(1))
    return pl.pallas_call(
        functools.partial(_outffn_kernel, final_norm=final_norm),
        grid=(t // tm,),
        in_specs=[
            pl.BlockSpec((tm, D_MODEL), row),
            pl.BlockSpec((tm, ATTN_WIDTH), row),
            pl.BlockSpec((tm, SSM_WIDTH), row),
            pl.BlockSpec((tm, CONV_WIDTH), row),
            _mod_spec(mod, layer, 2, tiles_per_batch),
            _mod_spec(mod, layer, 3, tiles_per_batch),
            _mod_spec(mod, layer, 4, tiles_per_batch),
            _mod_spec(mod, layer, 5, tiles_per_batch),
            _layer_block((1, D_MODEL), layer),
            resident((D_MODEL, D_MODEL)),
            resident((D_MODEL, D_FF)),
            resident((D_MODEL, D_FF)),
            resident((D_FF, D_MODEL)),
            pl.BlockSpec((1, D_MODEL), lambda i: (0, 0)),
        ],
        out_specs=pl.BlockSpec((tm, D_MODEL), row),
        out_shape=jax.ShapeDtypeStruct((t, D_MODEL), F32),
        compiler_params=_params("parallel"),
        name="out_projection_ffn",
    )(x, o_attn, o_ssm, o_conv, mod.array, mod.array, mod.array, mod.array, p['norm2_g'],
      p['w_out'], p['w_gate'], p['w_up'], p['w_down'], p['final_g'])


def _out_ffn_conv(x, o_attn, o_ssm, vc, mod, p, layer, final_norm, tm, tiles_per_batch):
    t = x.shape[0]
    n_tiles = t // tm
    halo_per_tile = tm // CONV_HALO
    ffn_tile = lambda j: jnp.maximum(j - 1, 0)
    conv_tile = lambda j: jnp.minimum(j, n_tiles - 1)
    ffn_row = lambda j: (ffn_tile(j), 0)
    resident = lambda shape: _layer_block(shape, layer, pipeline_mode=pl.Buffered(1))
    cvec = _layer_block((1, CONV_WIDTH), layer)
    span = tm + CONV_HALO - SUBLANES
    return pl.pallas_call(
        functools.partial(_outffn_conv_kernel, final_norm=final_norm, n_tiles=n_tiles,
                          tiles_per_batch=tiles_per_batch, sub=CONV_ROWS_PER_STAGE),
        grid=(n_tiles + 1,),
        in_specs=[
            pl.BlockSpec((tm, D_MODEL), ffn_row),
            pl.BlockSpec((tm, ATTN_WIDTH), ffn_row),
            pl.BlockSpec((tm, SSM_WIDTH), ffn_row),
            pl.BlockSpec((CONV_HALO, CONV_WIDTH), lambda j: (jnp.maximum(conv_tile(j) * halo_per_tile - 1, 0), 0)),
            pl.BlockSpec((tm, CONV_WIDTH), lambda j: (conv_tile(j), 0)),
            _mod_spec(mod, layer, 2, tiles_per_batch, ffn_tile),
            _mod_spec(mod, layer, 3, tiles_per_batch, ffn_tile),
            _mod_spec(mod, layer, 4, tiles_per_batch, ffn_tile),
            _mod_spec(mod, layer, 5, tiles_per_batch, ffn_tile),
            _layer_block((1, D_MODEL), layer),
            resident((D_MODEL, D_MODEL)),
            resident((D_MODEL, D_FF)),
            resident((D_MODEL, D_FF)),
            resident((D_FF, D_MODEL)),
            pl.BlockSpec((1, D_MODEL), lambda j: (0, 0)),
            _layer_block((CONV_K, CONV_WIDTH), layer), cvec, cvec, cvec,
        ],
        out_specs=pl.BlockSpec((tm, D_MODEL), ffn_row),
        out_shape=jax.ShapeDtypeStruct((t, D_MODEL), F32),
        scratch_shapes=[pltpu.VMEM((CONV_HALO + tm, CONV_WIDTH), F32),
                        pltpu.VMEM((SUBLANES - 1, span, CONV_WIDTH), F32),
                        pltpu.VMEM((2, tm, CONV_WIDTH), MXU_DTYPE)],
        compiler_params=_params("arbitrary"),
        name="conv_out_projection_ffn",
    )(x, o_attn, o_ssm, vc, vc, mod.array, mod.array, mod.array, mod.array, p['norm2_g'],
      p['w_out'], p['w_gate'], p['w_up'], p['w_down'], p['final_g'],
      p['conv_w'], p['conv_b'], p['conv_ln_g'], p['conv_ln_b'])


def _layer_prompt(x, mod, p, layer, rope, batch, seq, final_norm):
    tm = min(512, seq)
    tiles_per_batch = seq // tm
    q, kv, u, vc = _in_projection(x, mod, p, layer, rope[0], rope[1], tm, tiles_per_batch)
    o_attn = _attention_prompt(q, kv, p['sinks'], layer, batch, seq)
    n_keep = min(WINDOW, seq)
    kv_keep = kv.reshape(batch, seq, 2 * KV_WIDTH)[:, seq - n_keep:]
    new_k = kv_keep[:, :, :KV_WIDTH].reshape(batch, n_keep, N_KV_HEADS, HEAD_DIM)
    new_v = kv_keep[:, :, KV_WIDTH:].reshape(batch, n_keep, N_KV_HEADS, HEAD_DIM)

    nseg = PROMPT_SEGMENTS
    lseg = seq // nseg
    nseq = batch * nseg
    u_seq = u.reshape(nseq, lseg, SSM_WIDTH)
    steps = min(256, lseg)
    end_re, end_im = _s5_first_segment_states(u_seq, p, layer, nseg)
    start = lambda e: jnp.concatenate([jnp.zeros_like(e), e], axis=1).reshape(nseq, SSM_FLAT)
    o_seq, h_re, h_im = _s5_scan(u_seq, start(end_re), start(end_im), p, layer, nseq, steps)
    o_ssm = o_seq.reshape(batch * seq, SSM_WIDTH)
    last = lambda h: h.reshape(batch, nseg, N_SSM_GROUPS, SSM_STATE)[:, nseg - 1]

    new_conv = vc.reshape(batch, seq, CONV_WIDTH)[:, seq - (CONV_K - 1):]

    x = _out_ffn_conv(x, o_attn, o_ssm, vc, mod, p, layer, final_norm, tm, tiles_per_batch)
    return x, new_k, new_v, last(h_re), last(h_im), new_conv


def _layer_sample(x, mod, p, layer, rope, batch, n_new, kt_all, vt_all, h0_re, h0_im, conv_all, final_norm):
    t = batch * n_new
    qexp, kn, vn, u, vc = _in_projection(x, mod, p, layer, rope[0], rope[1], t, 1, n_new)
    o, new_k, new_v = _attention_sample(
        qexp.reshape(batch, N_Q_HEADS * n_new, LANES), kt_all, vt_all,
        kn.reshape(batch, SUBLANES, LANES), vn.reshape(batch, SUBLANES, LANES), p['sinks'], layer, n_new)
    o = o[:, :, :HEAD_DIM].reshape(batch, N_Q_HEADS, n_new, HEAD_DIM)
    o_attn = jnp.transpose(o, (2, 0, 1, 3)).reshape(t, ATTN_WIDTH)

    o_ssm, h_re, h_im = _s5_scan(u, h0_re.reshape(batch, SSM_FLAT), h0_im.reshape(batch, SSM_FLAT),
                                 p, layer, batch, n_new)
    st = lambda h: h.reshape(batch, N_SSM_GROUPS, SSM_STATE)

    o_conv, new_conv = _conv_sample(conv_all, vc.reshape(n_new, batch, CONV_WIDTH), p, layer)
    o_conv = o_conv.reshape(t, CONV_WIDTH)

    x = _out_ffn(x, o_attn, o_ssm, o_conv, mod, p, layer, final_norm, t, 1)
    return x, new_k, new_v, st(h_re), st(h_im), new_conv


def kernel(x_prompt, x_sample, c_prompt, c_sample, cache_k, cache_v, state_ssm_re, state_ssm_im, state_conv,
           norm1_g, norm2_g, w_mod, b_mod, w_in, attn_sinks, ssm_lam_re, ssm_lam_im, ssm_log_dt,
           ssm_b_re, ssm_b_im, ssm_c_re, ssm_c_im, ssm_d, ssm_w_glu, ssm_b_glu,
           conv_w, conv_b, conv_ln_g, conv_ln_b, w_out, w_gate, w_up, w_down, final_norm_g):
    bp, seq, d = x_prompt.shape
    bs, n_new, _ = x_sample.shape
    depth = w_in.shape[0]
    assert PROMPT_SEGMENTS == 2 and seq % (PROMPT_SEGMENTS * SUBLANES) == 0

    c_all = jnp.concatenate([c_sample, c_prompt], axis=0)
    pad_rows = -c_all.shape[0] % SUBLANES
    mods = _modulation(jnp.pad(c_all, ((0, pad_rows), (0, 0))), w_mod, b_mod)
    mod_p = _Mod(mods[:, bs:bs + bp].reshape(depth, bp, 1, N_MOD * d), None)
    mod_s = _Mod(mods, bs)

    a_re, a_im, bre_blk, bim_blk, cre_blk, cim_blk, decay_re, decay_im, span_re, span_im = _s5_discretise(
        ssm_lam_re, ssm_lam_im, ssm_log_dt, ssm_b_re, ssm_b_im, ssm_c_re, ssm_c_im,
        min(S5_STATE_CHUNK, seq // PROMPT_SEGMENTS))
    rope_p = _rope_tables(jnp.arange(seq))
    rope_s = _rope_tables(jnp.repeat(PAST_LEN + jnp.arange(n_new), bs))

    cast = lambda a: a.astype(MXU_DTYPE)
    vec = lambda a: a.reshape(depth, 1, a.shape[-1])
    p = {
        'norm1_g': vec(norm1_g), 'norm2_g': vec(norm2_g), 'w_in': cast(w_in), 'sinks': attn_sinks,
        'a_re': a_re, 'a_im': a_im,
        'bre_blk': bre_blk, 'bim_blk': bim_blk,
        'cre_blk': cre_blk, 'cim_blk': cim_blk,
        'decay_re': decay_re, 'decay_im': decay_im, 'span_re': span_re, 'span_im': span_im,
        'ssm_d': vec(ssm_d), 'ssm_w_glu': cast(ssm_w_glu), 'ssm_b_glu': vec(ssm_b_glu),
        'conv_w': conv_w, 'conv_b': vec(conv_b), 'conv_ln_g': vec(conv_ln_g), 'conv_ln_b': vec(conv_ln_b),
        'w_out': cast(w_out), 'w_gate': cast(w_gate), 'w_up': cast(w_up), 'w_down': cast(w_down),
        'final_g': final_norm_g.reshape(1, d),
    }
    conv_all = jnp.transpose(state_conv, (0, 2, 1, 3))
    kt_all = jnp.transpose(cache_k, (0, 1, 3, 4, 2))
    vt_all = jnp.transpose(cache_v, (0, 1, 3, 4, 2))

    xp = x_prompt.reshape(bp * seq, d)
    xs = jnp.transpose(x_sample, (1, 0, 2)).reshape(n_new * bs, d)
    outs_p, outs_s = [], []
    for l in range(depth):
        final = l == depth - 1
        xp, *op = _layer_prompt(xp, mod_p, p, l, rope_p, bp, seq, final)
        xs, *os_ = _layer_sample(xs, mod_s, p, l, rope_s, bs, n_new, kt_all, vt_all,
                                 state_ssm_re[l], state_ssm_im[l], conv_all, final)
        outs_p.append(op)
        outs_s.append(os_)
    stack = lambda outs, i: jnp.stack([o[i] for o in outs])
    y_sample = jnp.transpose(xs.reshape(n_new, bs, d), (1, 0, 2))
    new_k_s = jnp.transpose(stack(outs_s, 0), (0, 1, 4, 2, 3))
    new_v_s = jnp.transpose(stack(outs_s, 1), (0, 1, 4, 2, 3))
    new_conv_s = jnp.transpose(stack(outs_s, 4), (0, 2, 1, 3))
    return (xp.reshape(bp, seq, d), y_sample, *[stack(outs_p, i) for i in range(5)],
            new_k_s, new_v_s, stack(outs_s, 2), stack(outs_s, 3), new_conv_s)
```

```python
import functools
import math
from typing import NamedTuple

import jax
import jax.numpy as jnp
from jax import lax
from jax.experimental import pallas as pl
from jax.experimental.pallas import tpu as pltpu

F32 = jnp.float32
MXU_DTYPE = jnp.bfloat16

V7X_VMEM_BYTES = 64 * 1024 * 1024
VMEM_LIMIT_BYTES = V7X_VMEM_BYTES - 8 * 1024 * 1024
LANES = 128
SUBLANES = 8
V7X_MXU_DIM = 256

D_MODEL = 1024
HEAD_DIM = 64
N_Q_HEADS = 8
N_KV_HEADS = 2
Q_PER_KV = N_Q_HEADS // N_KV_HEADS
ATTN_WIDTH = N_Q_HEADS * HEAD_DIM
KV_WIDTH = N_KV_HEADS * HEAD_DIM
WINDOW = 128
ROPE_THETA = 10000.0
ATTN_SCALE = 1.0 / math.sqrt(HEAD_DIM)
SSM_WIDTH = 256
SSM_GROUP = 16
N_SSM_GROUPS = 16
SSM_STATE = 64
SSM_FLAT = N_SSM_GROUPS * SSM_STATE
CONV_WIDTH = 256
CONV_K = 31
CONV_HALO = 32
CONV_ROWS_PER_STAGE = 64
IN_WIDTH = ATTN_WIDTH + 2 * KV_WIDTH + SSM_WIDTH + 2 * CONV_WIDTH
D_FF = 2816
FF_CHUNK = 6 * V7X_MXU_DIM
EPS = 1e-6
NEG = -1e30
N_MOD = 6
PROMPT_SEGMENTS = 2
S5_STATE_CHUNK = 512
PAST_LEN = 8192


def _params(*semantics):
    return pltpu.CompilerParams(dimension_semantics=semantics, vmem_limit_bytes=VMEM_LIMIT_BYTES)


def _layer_block(shape, layer, **kw):
    zeros = (0,) * len(shape)
    return pl.BlockSpec((None,) + tuple(shape), lambda *_: (layer,) + zeros, **kw)


def _sigmoid(x):
    return 1.0 / (1.0 + jnp.exp(-x))


def _silu(x):
    return x * _sigmoid(x)


def _rms(x, g):
    return x * lax.rsqrt(jnp.mean(x * x, axis=-1, keepdims=True) + EPS) * g


def _mm(a, b):
    return jnp.dot(a, b, preferred_element_type=F32)


def _mm_nt(a, b):
    return lax.dot_general(a, b, (((1,), (1,)), ((), ())), preferred_element_type=F32)


def _rows(m, n):
    return m if m.shape[0] == 1 else jnp.concatenate([m] * (n // m.shape[0]), axis=0)


def _mod_kernel(c_ref, w_ref, b_ref, o_ref):
    a = _silu(c_ref[...]).astype(MXU_DTYPE)
    o_ref[...] = _mm(a, w_ref[...].astype(MXU_DTYPE)) + b_ref[...]


def _modulation(c, w_mod, b_mod):
    depth, d, n = w_mod.shape
    rows = c.shape[0]
    tn = 1536
    return pl.pallas_call(
        _mod_kernel,
        grid=(depth, n // tn),
        in_specs=[
            pl.BlockSpec((rows, d), lambda l, j: (0, 0)),
            pl.BlockSpec((None, d, tn), lambda l, j: (l, 0, j)),
            pl.BlockSpec((None, 1, tn), lambda l, j: (l, 0, j)),
        ],
        out_specs=pl.BlockSpec((None, rows, tn), lambda l, j: (l, 0, j)),
        out_shape=jax.ShapeDtypeStruct((depth, rows, n), F32),
        compiler_params=_params("parallel", "parallel"),
        name="modulation",
    )(c, w_mod, b_mod.reshape(depth, 1, n))


class _Mod(NamedTuple):
    array: jax.Array
    batch_rows: int | None


def _mod_spec(mod, layer, chunk, tiles_per_batch, tile=lambda i: i):
    if mod.batch_rows is None:
        return pl.BlockSpec((None, None, 1, D_MODEL), lambda i: (layer, tile(i) // tiles_per_batch, 0, chunk))
    return pl.BlockSpec((None, mod.batch_rows, D_MODEL), lambda i: (layer, 0, chunk))


def _inproj_kernel(x_ref, sh_ref, sc_ref, g_ref, w_ref, cos_ref, sin_ref, *outs, n_new):
    tm = x_ref.shape[0]
    h = _rms(x_ref[...], g_ref[...]) * (1.0 + _rows(sc_ref[...], tm)) + _rows(sh_ref[...], tm)
    z = _mm(h.astype(MXU_DTYPE), w_ref[...])
    cos = cos_ref[...]
    sin = sin_ref[...]
    lane = lax.broadcasted_iota(jnp.int32, (tm, LANES), 1)
    first_half = (lane % HEAD_DIM) < (HEAD_DIM // 2)

    def rope(t):
        partner = jnp.where(first_half, pltpu.roll(t, LANES - HEAD_DIM // 2, 1),
                            pltpu.roll(t, HEAD_DIM // 2, 1))
        return t * cos + partner * sin

    q_cols = [rope(z[:, j * LANES:(j + 1) * LANES]) * ATTN_SCALE for j in range(ATTN_WIDTH // LANES)]
    o = ATTN_WIDTH
    k_rot = rope(z[:, o:o + KV_WIDTH])
    v_new = z[:, o + KV_WIDTH:o + 2 * KV_WIDTH]
    if n_new is None:
        q_ref, kv_ref, u_ref, vc_ref = outs
        for j, qc in enumerate(q_cols):
            q_ref[:, j * LANES:(j + 1) * LANES] = qc.astype(q_ref.dtype)
        kv_ref[:, 0:KV_WIDTH] = k_rot
        kv_ref[:, KV_WIDTH:2 * KV_WIDTH] = v_new
    else:
        q_ref, kn_ref, vn_ref, u_ref, vc_ref = outs
        batch = tm // n_new
        low = lax.broadcasted_iota(jnp.int32, (batch, LANES), 1) < HEAD_DIM
        kn_ref[...] = jnp.zeros(kn_ref.shape, F32)
        vn_ref[...] = jnp.zeros(vn_ref.shape, F32)
        for t in range(n_new):
            rows = slice(t * batch, (t + 1) * batch)
            tile_row = pl.ds(SUBLANES - n_new + t, batch, stride=SUBLANES)
            kn_ref[tile_row, :] = k_rot[rows]
            vn_ref[tile_row, :] = v_new[rows]
            for h in range(N_Q_HEADS):
                kvh = h // Q_PER_KV
                piece = q_cols[h // 2][rows]
                if h % 2 != kvh:
                    piece = pltpu.roll(piece, HEAD_DIM, 1)
                piece = jnp.where(low if kvh == 0 else ~low, piece, 0.0)
                q_ref[pl.ds(h * n_new + t, batch, stride=N_Q_HEADS * n_new), :] = piece
    o += 2 * KV_WIDTH
    u_ref[...] = z[:, o:o + SSM_WIDTH]
    o += SSM_WIDTH
    za = z[:, o:o + CONV_WIDTH]
    zg = z[:, o + CONV_WIDTH:o + 2 * CONV_WIDTH]
    vc_ref[...] = za * _sigmoid(zg)


def _in_projection(x, mod, p, layer, cos, sin, tm, tiles_per_batch, n_new=None):
    t = x.shape[0]
    pos_tiles = cos.shape[0] // tm
    row = lambda i: (i, 0)
    if n_new is None:
        attn_specs = [pl.BlockSpec((tm, ATTN_WIDTH), row), pl.BlockSpec((tm, 2 * KV_WIDTH), row)]
        attn_shapes = [jax.ShapeDtypeStruct((t, ATTN_WIDTH), MXU_DTYPE),
                       jax.ShapeDtypeStruct((t, 2 * KV_WIDTH), F32)]
    else:
        assert tm == t and KV_WIDTH == LANES
        batch = t // n_new
        shapes = [(batch * N_Q_HEADS * n_new, LANES), (batch * SUBLANES, LANES), (batch * SUBLANES, LANES)]
        attn_specs = [pl.BlockSpec(s, lambda i: (0, 0)) for s in shapes]
        attn_shapes = [jax.ShapeDtypeStruct(s, F32) for s in shapes]
    return pl.pallas_call(
        functools.partial(_inproj_kernel, n_new=n_new),
        grid=(t // tm,),
        in_specs=[
            pl.BlockSpec((tm, D_MODEL), row),
            _mod_spec(mod, layer, 0, tiles_per_batch),
            _mod_spec(mod, layer, 1, tiles_per_batch),
            _layer_block((1, D_MODEL), layer),
            _layer_block((D_MODEL, IN_WIDTH), layer),
            pl.BlockSpec((tm, LANES), lambda i: (i % pos_tiles, 0)),
            pl.BlockSpec((tm, LANES), lambda i: (i % pos_tiles, 0)),
        ],
        out_specs=attn_specs + [pl.BlockSpec((tm, SSM_WIDTH), row), pl.BlockSpec((tm, CONV_WIDTH), row)],
        out_shape=attn_shapes + [jax.ShapeDtypeStruct((t, SSM_WIDTH), F32),
                                 jax.ShapeDtypeStruct((t, CONV_WIDTH), F32)],
        compiler_params=_params("parallel"),
        name="in_projection",
    )(x, mod.array, mod.array, p['norm1_g'], p['w_in'], cos, sin)


def _rope_tables(pos):
    half = HEAD_DIM // 2
    inv_freq = ROPE_THETA ** (-jnp.arange(half, dtype=F32) / half)
    ang = pos.astype(F32)[:, None] * inv_freq[None, :]
    cos = jnp.tile(jnp.cos(ang), (1, LANES // half))
    sin = jnp.sin(ang)
    sin = jnp.tile(jnp.concatenate([-sin, sin], axis=1), (1, LANES // HEAD_DIM))
    return cos, sin


def _sink_softmax(s, sink_col):
    m = jnp.maximum(jnp.max(s, axis=-1, keepdims=True), sink_col)
    e = jnp.exp(s - m)
    return e * (1.0 / (jnp.sum(e, axis=-1, keepdims=True) + jnp.exp(sink_col - m)))


def _head_pair_select(x, pick_second):
    lane = lax.broadcasted_iota(jnp.int32, x.shape, 1)
    swapped = pltpu.roll(x, HEAD_DIM, 1)
    low = lane < HEAD_DIM
    return jnp.where(low, swapped, x) if pick_second else jnp.where(low, x, swapped)


def _attn_prompt_kernel(sink_ref, q_ref, kvp_ref, kvc_ref, o_ref, *, layer):
    n = pl.program_id(1)
    w = WINDOW
    n_blocks = q_ref.shape[0] // w
    rows = Q_PER_KV * w
    r_idx = lax.broadcasted_iota(jnp.int32, (rows, w), 0) % w
    c_idx = lax.broadcasted_iota(jnp.int32, (rows, w), 1)
    from_prev = c_idx > r_idx
    first_bias = jnp.where(n > 0, 0.0, NEG)
    row_head = lax.broadcasted_iota(jnp.int32, (rows, 1), 0) // w
    low = lax.broadcasted_iota(jnp.int32, (w, LANES), 1) < HEAD_DIM
    half_mask = [jnp.where(low, 1.0, 0.0).astype(MXU_DTYPE), jnp.where(low, 0.0, 1.0).astype(MXU_DTYPE)]
    sink_cols = []
    for kvh in range(N_KV_HEADS):
        sink_col = jnp.zeros((rows, 1), F32)
        for g in range(Q_PER_KV):
            sink_col = jnp.where(row_head == g, sink_ref[layer, kvh * Q_PER_KV + g], sink_col)
        sink_cols.append(sink_col)
    chains = [(i, kvh) for i in range(n_blocks) for kvh in range(N_KV_HEADS)]
    windows, values = [], []
    for i, kvh in chains:
        own = slice(i * w, (i + 1) * w)
        prev_ref, prev = (kvp_ref, slice(0, w)) if i == 0 else (kvc_ref, slice((i - 1) * w, i * w))
        kk = jnp.concatenate([prev_ref[prev, 0:KV_WIDTH], kvc_ref[own, 0:KV_WIDTH]], axis=0)
        vv = jnp.concatenate([prev_ref[prev, KV_WIDTH:2 * KV_WIDTH], kvc_ref[own, KV_WIDTH:2 * KV_WIDTH]], axis=0)
        k2 = _head_pair_select(kk, kvh == 1).astype(MXU_DTYPE)
        values.append(_head_pair_select(vv, kvh == 1).astype(MXU_DTYPE))
        pieces = []
        for g in range(Q_PER_KV):
            h = kvh * Q_PER_KV + g
            qcol = q_ref[own, (h // 2) * LANES:(h // 2 + 1) * LANES]
            pieces.append(qcol * half_mask[h % 2])
        s = _mm_nt(jnp.concatenate(pieces, axis=0), k2)
        s_prev = s[:, 0:w] + first_bias if i == 0 else s[:, 0:w]
        windows.append(jnp.where(from_prev, s_prev, s[:, w:2 * w]))
    probs = []
    for (i, kvh), sc in zip(chains, windows):
        p = _sink_softmax(sc, sink_cols[kvh])
        p2 = jnp.concatenate([jnp.where(from_prev, p, 0.0), jnp.where(from_prev, 0.0, p)], axis=1)
        probs.append(p2.astype(MXU_DTYPE))
    for (i, kvh), p2, v2 in zip(chains, probs, values):
        own = slice(i * w, (i + 1) * w)
        r = _mm(p2, v2)
        for j in range(Q_PER_KV // 2):
            col = kvh * (Q_PER_KV // 2) + j
            o_ref[own, col * LANES:(col + 1) * LANES] = jnp.where(
                low, r[2 * j * w:(2 * j + 1) * w], r[(2 * j + 1) * w:(2 * j + 2) * w]).astype(o_ref.dtype)


def _attention_prompt(q, kv, sinks, layer, batch, seq):
    tq = min(8 * WINDOW, seq)
    nt = seq // tq
    per = tq // WINDOW
    return pl.pallas_call(
        functools.partial(_attn_prompt_kernel, layer=layer),
        grid=(batch, nt),
        in_specs=[
            pl.BlockSpec(memory_space=pltpu.SMEM),
            pl.BlockSpec((tq, ATTN_WIDTH), lambda b, n: (b * nt + n, 0)),
            pl.BlockSpec((WINDOW, 2 * KV_WIDTH), lambda b, n: ((b * nt + n) * per - jnp.minimum(n, 1), 0)),
            pl.BlockSpec((tq, 2 * KV_WIDTH), lambda b, n: (b * nt + n, 0)),
        ],
        out_specs=pl.BlockSpec((tq, ATTN_WIDTH), lambda b, n: (b * nt + n, 0)),
        out_shape=jax.ShapeDtypeStruct((batch * seq, ATTN_WIDTH), MXU_DTYPE),
        compiler_params=_params("parallel", "parallel"),
        name="attention_prompt",
    )(sinks, q, kv, kv)


def _attn_sample_kernel(sink_ref, q_ref, kt_ref, vt_ref, kn_ref, vn_ref, o_ref, nk_ref, nv_ref, *, layer, n_new):
    bt, rows, _ = q_ref.shape
    nkv, hd, wb = kt_ref.shape[1:]
    kept = wb - n_new
    r = lax.broadcasted_iota(jnp.int32, (bt * rows, 2 * wb), 0)
    j = lax.broadcasted_iota(jnp.int32, (bt * rows, 2 * wb), 1)
    t_idx = r % n_new
    t_new = j - wb - kept
    mask = ((j < wb) & (j > t_idx)) | ((t_new >= 0) & (t_new <= t_idx))
    row_head = (lax.broadcasted_iota(jnp.int32, (bt * rows, 1), 0) % rows) // n_new
    sink_col = jnp.zeros((bt * rows, 1), F32)
    for h in range(N_Q_HEADS):
        sink_col = jnp.where(row_head == h, sink_ref[layer, h], sink_col)
    keep = lax.broadcasted_iota(jnp.int32, (nkv, hd, wb), 2) < kept
    second_group = lax.broadcasted_iota(jnp.int32, (rows, nkv * hd), 0) >= (rows // nkv)
    flat = lambda a: a.reshape(nkv * hd, wb)
    above = jnp.zeros((wb - kn_ref.shape[1], nkv * hd), F32)

    scores, values = [], []
    for b in range(bt):
        kt, vt = kt_ref[b], vt_ref[b]
        kpos = jnp.concatenate([above, kn_ref[b]], axis=0).T
        vpos = jnp.concatenate([above, vn_ref[b]], axis=0).T
        nk_ref[b] = jnp.where(keep, pltpu.roll(kt, kept, 2), kpos.reshape(nkv, hd, wb))
        nv_ref[b] = jnp.where(keep, pltpu.roll(vt, kept, 2), vpos.reshape(nkv, hd, wb))
        kcat = jnp.concatenate([flat(kt), kpos], axis=1).astype(MXU_DTYPE)
        values.append(jnp.concatenate([flat(vt), vpos], axis=1).astype(MXU_DTYPE))
        scores.append(_mm(q_ref[b].astype(MXU_DTYPE), kcat))
    s = jnp.where(mask, jnp.concatenate(scores, axis=0), NEG)
    p = _sink_softmax(s, sink_col).astype(MXU_DTYPE)
    for b in range(bt):
        o = _mm_nt(p[b * rows:(b + 1) * rows], values[b])
        o_ref[b] = jnp.where(second_group, pltpu.roll(o, hd, 1), o).astype(o_ref.dtype)


def _attention_sample(qexp, kt_all, vt_all, kn, vn, sinks, layer, n_new):
    batch, rows, width = qexp.shape
    _, _, nkv, hd, wb = kt_all.shape
    bt = min(16, batch)
    blk3 = pl.BlockSpec((bt, rows, width), lambda i: (i, 0, 0))
    blk4 = pl.BlockSpec((bt, nkv, hd, wb), lambda i: (i, 0, 0, 0))
    new = pl.BlockSpec((bt,) + kn.shape[1:], lambda i: (i, 0, 0))
    cache = pl.BlockSpec((None, bt, nkv, hd, wb), lambda i: (layer, i, 0, 0, 0))
    return pl.pallas_call(
        functools.partial(_attn_sample_kernel, layer=layer, n_new=n_new),
        grid=(batch // bt,),
        in_specs=[pl.BlockSpec(memory_space=pltpu.SMEM), blk3, cache, cache, new, new],
        out_specs=[blk3, blk4, blk4],
        out_shape=[jax.ShapeDtypeStruct((batch, rows, width), MXU_DTYPE),
                   jax.ShapeDtypeStruct((batch, nkv, hd, wb), F32),
                   jax.ShapeDtypeStruct((batch, nkv, hd, wb), F32)],
        compiler_params=_params("parallel"),
        name="attention_sample",
    )(sinks, qexp, kt_all, vt_all, kn, vn)


def _cmul(ar, ai, br, bi):
    return ar * br - ai * bi, ar * bi + ai * br


def _s5_discretise_kernel(lr_ref, li_ref, ldt_ref, bre_ref, bim_ref, cre_ref, cim_ref,
                          are_ref, aim_ref, bbre_ref, bbim_ref, ccre_ref, ccim_ref,
                          decay_re_ref, decay_im_ref, span_re_ref, span_im_ref):
    lr = lr_ref[...]
    li = li_ref[...]
    dt = jnp.exp(ldt_ref[...])
    mag = jnp.exp(lr * dt)
    ab_re = mag * jnp.cos(li * dt)
    ab_im = mag * jnp.sin(li * dt)
    den = lr * lr + li * li
    nr = ab_re - 1.0
    coef_re = (nr * lr + ab_im * li) / den
    coef_im = (ab_im * lr - nr * li) / den
    are_ref[...] = ab_re
    aim_ref[...] = ab_im
    br = bre_ref[...]
    bi = bim_ref[...]
    bb_re = coef_re * br - coef_im * bi
    bb_im = coef_re * bi + coef_im * br
    lane_group = lax.broadcasted_iota(jnp.int32, bb_re.shape, 1) // SSM_STATE
    for src, dst in ((bb_re, bbre_ref), (bb_im, bbim_ref), (cre_ref[...], ccre_ref), (cim_ref[...], ccim_ref)):
        for g in range(N_SSM_GROUPS):
            dst[g * SSM_GROUP:(g + 1) * SSM_GROUP, :] = jnp.where(lane_group == g, src, 0.0).astype(dst.dtype)
    t_chunk = decay_re_ref.shape[0]
    row = lax.broadcasted_iota(jnp.int32, (SUBLANES, SSM_FLAT), 0)
    pr, pi = jnp.ones_like(ab_re), jnp.zeros_like(ab_re)
    tile_re, tile_im = jnp.zeros((SUBLANES, SSM_FLAT), F32), jnp.zeros((SUBLANES, SSM_FLAT), F32)
    for k in range(SUBLANES):
        tile_re = jnp.where(row == SUBLANES - 1 - k, pr, tile_re)
        tile_im = jnp.where(row == SUBLANES - 1 - k, pi, tile_im)
        pr, pi = _cmul(pr, pi, ab_re, ab_im)
    decay_re_ref[t_chunk - SUBLANES:t_chunk, :] = tile_re
    decay_im_ref[t_chunk - SUBLANES:t_chunk, :] = tile_im
    n = SUBLANES
    while n < t_chunk:
        lo, hi = slice(t_chunk - 2 * n, t_chunk - n), slice(t_chunk - n, t_chunk)
        decay_re_ref[lo, :], decay_im_ref[lo, :] = _cmul(decay_re_ref[hi, :], decay_im_ref[hi, :], pr, pi)
        pr, pi = _cmul(pr, pi, pr, pi)
        n *= 2
    span_re_ref[...] = pr
    span_im_ref[...] = pi


def _s5_discretise(lam_re, lam_im, log_dt, b_re, b_im, c_re, c_im, t_chunk):
    depth = lam_re.shape[0]
    assert t_chunk % SUBLANES == 0 and (t_chunk // SUBLANES) & (t_chunk // SUBLANES - 1) == 0
    flat = lambda a: a.reshape(depth, 1, SSM_FLAT)
    ldt = jnp.broadcast_to(log_dt[:, :, None], lam_re.shape)
    bt = lambda a: jnp.transpose(a, (0, 3, 1, 2)).reshape(depth, SSM_GROUP, SSM_FLAT)
    ct = lambda a: jnp.transpose(a, (0, 2, 1, 3)).reshape(depth, SSM_GROUP, SSM_FLAT)
    vec = pl.BlockSpec((None, 1, SSM_FLAT), lambda l: (l, 0, 0))
    mat = pl.BlockSpec((None, SSM_GROUP, SSM_FLAT), lambda l: (l, 0, 0))
    blk = pl.BlockSpec((None, SSM_WIDTH, SSM_FLAT), lambda l: (l, 0, 0))
    tab = pl.BlockSpec((None, t_chunk, SSM_FLAT), lambda l: (l, 0, 0))
    return pl.pallas_call(
        _s5_discretise_kernel,
        grid=(depth,),
        in_specs=[vec, vec, vec, mat, mat, mat, mat],
        out_specs=[vec, vec, blk, blk, blk, blk, tab, tab, vec, vec],
        out_shape=[jax.ShapeDtypeStruct((depth, 1, SSM_FLAT), F32)] * 2
        + [jax.ShapeDtypeStruct((depth, SSM_WIDTH, SSM_FLAT), MXU_DTYPE)] * 4
        + [jax.ShapeDtypeStruct((depth, t_chunk, SSM_FLAT), F32)] * 2
        + [jax.ShapeDtypeStruct((depth, 1, SSM_FLAT), F32)] * 2,
        compiler_params=_params("parallel"),
        name="s5_discretise",
    )(flat(lam_re), flat(lam_im), flat(ldt), bt(b_re), bt(b_im), ct(c_re), ct(c_im))


def _s5_first_segment_kernel(u_ref, bre_ref, bim_ref, dre_ref, dim_ref, sre_ref, sim_ref, ere_ref, eim_ref,
                             hre_s, him_s):
    @pl.when(pl.program_id(1) == 0)
    def _():
        hre_s[...] = jnp.zeros(hre_s.shape, F32)
        him_s[...] = jnp.zeros(him_s.shape, F32)

    ub = u_ref[...].astype(MXU_DTYPE)
    wr, wi = _cmul(dre_ref[...], dim_ref[...], _mm(ub, bre_ref[...]), _mm(ub, bim_ref[...]))
    cr, ci = _cmul(sre_ref[...], sim_ref[...], hre_s[...], him_s[...])
    hr = cr + jnp.sum(wr, axis=0, keepdims=True)
    hi = ci + jnp.sum(wi, axis=0, keepdims=True)
    hre_s[...] = hr
    him_s[...] = hi
    ere_ref[...] = hr
    eim_ref[...] = hi


def _s5_first_segment_states(u_seq, p, layer, nseg):
    nseq, lseg, _ = u_seq.shape
    batch = nseq // nseg
    t_chunk = p['decay_re'].shape[1]
    vec = _layer_block((1, SSM_FLAT), layer)
    tab = _layer_block((t_chunk, SSM_FLAT), layer)
    bmat = _layer_block((SSM_WIDTH, SSM_FLAT), layer)
    out = pl.BlockSpec((None, 1, SSM_FLAT), lambda b, c: (b, 0, 0))
    return pl.pallas_call(
        _s5_first_segment_kernel,
        grid=(batch, lseg // t_chunk),
        in_specs=[pl.BlockSpec((None, t_chunk, SSM_WIDTH), lambda b, c: (b * nseg, c, 0)),
                  bmat, bmat, tab, tab, vec, vec],
        out_specs=[out, out],
        out_shape=[jax.ShapeDtypeStruct((batch, 1, SSM_FLAT), F32)] * 2,
        scratch_shapes=[pltpu.VMEM((1, SSM_FLAT), F32)] * 2,
        compiler_params=_params("parallel", "arbitrary"),
        name="s5_first_segment_states",
    )(u_seq, p['bre_blk'], p['bim_blk'], p['decay_re'], p['decay_im'], p['span_re'], p['span_im'])


def _s5_kernel(u_ref, hre0_ref, him0_ref, are_ref, aim_ref, bre_ref, bim_ref, cre_ref, cim_ref, d_ref,
               wglu_ref, bglu_ref, o_ref, hre_out, him_out, hre_s, him_s, bure_s, buim_s, hsre_s, hsim_s,
               *stage, nseq):
    seq_major = bool(stage)
    steps = u_ref.shape[1] if seq_major else u_ref.shape[0] // nseq
    lane_halves = [slice(h * LANES, (h + 1) * LANES) for h in range(SSM_WIDTH // LANES)]

    @pl.when(pl.program_id(0) == 0)
    def _():
        hre_s[...] = hre0_ref[...]
        him_s[...] = him0_ref[...]

    if seq_major:
        stage_s, = stage
        for s in range(nseq):
            for h, cols in enumerate(lane_halves):
                stage_s[h, pl.ds(s, steps, stride=nseq), :] = u_ref[s, :, cols]
        u = jnp.concatenate([stage_s[h] for h in range(len(lane_halves))], axis=1)
    else:
        u = u_ref[...]
    ub = u.astype(MXU_DTYPE)
    bure_s[...] = _mm(ub, bre_ref[...])
    buim_s[...] = _mm(ub, bim_ref[...])
    ar = jnp.broadcast_to(are_ref[...], (nseq, SSM_FLAT))
    ai = jnp.broadcast_to(aim_ref[...], (nseq, SSM_FLAT))

    def step(t, carry):
        hr, hi = carry
        rows = pl.ds(pl.multiple_of(t * nseq, nseq), nseq)
        nhr = ar * hr - ai * hi + bure_s[rows, :]
        nhi = ar * hi + ai * hr + buim_s[rows, :]
        hsre_s[rows, :] = nhr
        hsim_s[rows, :] = nhi
        return nhr, nhi

    hr, hi = lax.fori_loop(0, steps, step, (hre_s[...], him_s[...]), unroll=min(steps, 8))
    hre_s[...] = hr
    him_s[...] = hi
    hre_out[...] = hr
    him_out[...] = hi
    y = (_mm_nt(hsre_s[...].astype(MXU_DTYPE), cre_ref[...])
         - _mm_nt(hsim_s[...].astype(MXU_DTYPE), cim_ref[...]))
    z = jax.nn.gelu(y + d_ref[...] * u)
    gate = _mm(z.astype(MXU_DTYPE), wglu_ref[...]) + bglu_ref[...]
    out = z * _sigmoid(gate)
    if seq_major:
        for h, cols in enumerate(lane_halves):
            stage_s[h] = out[:, cols]
        for s in range(nseq):
            for h, cols in enumerate(lane_halves):
                o_ref[s, :, cols] = stage_s[h, pl.ds(s, steps, stride=nseq), :].astype(o_ref.dtype)
    else:
        o_ref[...] = out.astype(o_ref.dtype)


def _s5_scan(u, hre0, him0, p, layer, nseq, steps_per_tile):
    seq_major = u.ndim == 3
    rows = u.shape[0] * u.shape[1] if seq_major else u.shape[0]
    tr = steps_per_tile * nseq
    state = pl.BlockSpec((nseq, SSM_FLAT), lambda i: (0, 0))
    if seq_major:
        io_spec = pl.BlockSpec((nseq, steps_per_tile, SSM_WIDTH), lambda i: (0, i, 0))
    else:
        io_spec = pl.BlockSpec((tr, SSM_WIDTH), lambda i: (i, 0))
    state_vec = _layer_block((1, SSM_FLAT), layer)
    group_map = _layer_block((SSM_WIDTH, SSM_FLAT), layer)
    width_vec = _layer_block((1, SSM_WIDTH), layer)
    scratch = ([pltpu.VMEM((nseq, SSM_FLAT), F32)] * 2 + [pltpu.VMEM((tr, SSM_FLAT), F32)] * 4
               + [pltpu.VMEM((SSM_WIDTH // LANES, tr, LANES), F32)] * seq_major)
    return pl.pallas_call(
        functools.partial(_s5_kernel, nseq=nseq),
        grid=(rows // tr,),
        in_specs=[io_spec, state, state, state_vec, state_vec, group_map, group_map, group_map, group_map,
                  width_vec, _layer_block((SSM_WIDTH, SSM_WIDTH), layer), width_vec],
        out_specs=[io_spec, state, state],
        out_shape=[jax.ShapeDtypeStruct(u.shape, MXU_DTYPE)] + [jax.ShapeDtypeStruct((nseq, SSM_FLAT), F32)] * 2,
        scratch_shapes=scratch,
        compiler_params=_params("arbitrary"),
        name="s5_scan",
    )(u, hre0, him0, p['a_re'], p['a_im'], p['bre_blk'], p['bim_blk'], p['cre_blk'], p['cim_blk'],
      p['ssm_d'], p['ssm_w_glu'], p['ssm_b_glu'])


def _layernorm_silu(y, g, b):
    yc = y - jnp.mean(y, axis=-1, keepdims=True)
    var = jnp.mean(yc * yc, axis=-1, keepdims=True)
    return _silu(yc * lax.rsqrt(var + EPS) * g + b)


def _conv_sample_kernel(state_ref, v_ref, w_ref, b_ref, g_ref, beta_ref, o_ref, ns_ref):
    ns = state_ref.shape[0]
    n_new = v_ref.shape[0]
    row = lambda j: state_ref[j] if j < ns else v_ref[j - ns]
    for t in range(n_new):
        acc = jnp.zeros(o_ref.shape[1:], F32)
        for k in range(CONV_K):
            acc = acc + w_ref[k:k + 1, :] * row(t + k)
        o_ref[t] = _layernorm_silu(acc + b_ref[...], g_ref[...], beta_ref[...]).astype(o_ref.dtype)
    for j in range(ns):
        ns_ref[j] = row(j + n_new)


def _conv_sample(state_all, v, p, layer):
    _, ns, batch, width = state_all.shape
    n_new = v.shape[0]
    bt = min(32, batch)
    vec = _layer_block((1, width), layer)
    return pl.pallas_call(
        _conv_sample_kernel,
        grid=(batch // bt,),
        in_specs=[
            pl.BlockSpec((None, ns, bt, width), lambda i: (layer, 0, i, 0)),
            pl.BlockSpec((n_new, bt, width), lambda i: (0, i, 0)),
            _layer_block((CONV_K, width), layer), vec, vec, vec,
        ],
        out_specs=[pl.BlockSpec((n_new, bt, width), lambda i: (0, i, 0)),
                   pl.BlockSpec((ns, bt, width), lambda i: (0, i, 0))],
        out_shape=[jax.ShapeDtypeStruct((n_new, batch, width), MXU_DTYPE),
                   jax.ShapeDtypeStruct((ns, batch, width), F32)],
        compiler_params=_params("parallel"),
        name="conv_sample",
    )(state_all, v, p['conv_w'], p['conv_b'], p['conv_ln_g'], p['conv_ln_b'])


def _outffn_stages(x_ref, a_ref, s_ref, load_conv, g1_ref, sh2_ref, sc2_ref, g2_ref, n2_ref,
                   wo_ref, wg_ref, wu_ref, wd_ref, fg_ref, o_ref, final_norm):
    tm = x_ref.shape[0]
    o1 = ATTN_WIDTH
    o2 = o1 + SSM_WIDTH
    st = {}

    def proj_attn_ssm():
        st['proj'] = _mm(a_ref[...], wo_ref[0:o1, :]) + _mm(s_ref[...], wo_ref[o1:o2, :])

    def proj_conv_residual_norm():
        proj = st.pop('proj') + _mm(load_conv(), wo_ref[o2:o2 + CONV_WIDTH, :])
        x2 = x_ref[...] + _rows(g1_ref[...], tm) * proj
        h2 = _rms(x2, n2_ref[...]) * (1.0 + _rows(sc2_ref[...], tm)) + _rows(sh2_ref[...], tm)
        st['x2'] = x2
        st['h2'] = h2.astype(MXU_DTYPE)

    def gate(cs):
        st['gate'] = _mm(st['h2'], wg_ref[:, cs])

    def up(cs):
        st['act'] = (_silu(st.pop('gate')) * _mm(st['h2'], wu_ref[:, cs])).astype(MXU_DTYPE)

    def down(cs, last):
        ffn = _mm(st.pop('act'), wd_ref[cs, :])
        st['ffn'] = ffn if 'ffn' not in st else st['ffn'] + ffn
        if last:
            y = st['x2'] + _rows(g2_ref[...], tm) * st['ffn']
            o_ref[...] = _rms(y, fg_ref[...]) if final_norm else y

    stages = [proj_attn_ssm, proj_conv_residual_norm]
    for start in range(0, D_FF, FF_CHUNK):
        cs = slice(start, min(start + FF_CHUNK, D_FF))
        stages += [functools.partial(gate, cs), functools.partial(up, cs),
                   functools.partial(down, cs, cs.stop == D_FF)]
    return stages


def _outffn_kernel(x_ref, a_ref, s_ref, c_ref, *rest, final_norm):
    for stage in _outffn_stages(x_ref, a_ref, s_ref, lambda: c_ref[...], *rest, final_norm):
        stage()


def _conv_stages(halo_ref, cur_ref, w_ref, b_ref, g_ref, beta_ref, store, buf, shifted, first, sub):
    tc = cur_ref.shape[0]
    base = CONV_HALO - (CONV_K - 1)
    span = shifted.shape[1]

    def fill():
        buf[0:CONV_HALO, :] = jnp.where(first, 0.0, halo_ref[...])
        buf[CONV_HALO:CONV_HALO + tc, :] = cur_ref[...]
        for s in range(1, SUBLANES):
            shifted[s - 1] = buf[s:s + span, :]

    def rows(r):
        acc = jnp.zeros((sub // SUBLANES, SUBLANES, CONV_WIDTH), F32)
        for k in range(CONV_K):
            j, s = (base + k) // SUBLANES, (base + k) % SUBLANES
            start = r * sub + j * SUBLANES
            tap = buf[start:start + sub, :] if s == 0 else shifted[s - 1, start:start + sub, :]
            acc = acc + w_ref[k][None] * tap.reshape(acc.shape)
        y = _layernorm_silu(acc.reshape(sub, CONV_WIDTH) + b_ref[...], g_ref[...], beta_ref[...])
        store(slice(r * sub, (r + 1) * sub), y)

    return [fill] + [functools.partial(rows, r) for r in range(tc // sub)]


def _outffn_conv_kernel(x_ref, a_ref, s_ref, halo_ref, vcur_ref, g1_ref, sh2_ref, sc2_ref, g2_ref, n2_ref,
                        wo_ref, wg_ref, wu_ref, wd_ref, fg_ref, cw_ref, cb_ref, cg_ref, cbeta_ref,
                        o_ref, buf, shifted, conv_out, *, final_norm, n_tiles, tiles_per_batch, sub):
    j = pl.program_id(0)
    conv_tile = jnp.minimum(j, n_tiles - 1)
    first = (conv_tile % tiles_per_batch) == 0

    def conv_into(slot):
        def store(rows, y):
            conv_out[slot, rows, :] = y.astype(conv_out.dtype)
        return _conv_stages(halo_ref, vcur_ref, cw_ref, cb_ref, cg_ref, cbeta_ref, store, buf, shifted, first, sub)

    @pl.when(j == 0)
    def _():
        for stage in conv_into(0):
            stage()

    @pl.when(j > 0)
    def _():
        slot = j % 2
        conv = conv_into(slot)
        ffn = _outffn_stages(x_ref, a_ref, s_ref, lambda: conv_out[1 - slot], g1_ref, sh2_ref, sc2_ref, g2_ref,
                             n2_ref, wo_ref, wg_ref, wu_ref, wd_ref, fg_ref, o_ref, final_norm)
        conv[0]()
        per_ffn = -(-(len(conv) - 1) // len(ffn))
        for k, ffn_stage in enumerate(ffn):
            for conv_stage in conv[1 + k * per_ffn:1 + (k + 1) * per_ffn]:
                conv_stage()
            ffn_stage()


def _out_ffn(x, o_attn, o_ssm, o_conv, mod, p, layer, final_norm, tm, tiles_per_batch):
    t = x.shape[0]
    row = lambda i: (i, 0)
    resident = lambda shape: _layer_block(shape, layer, pipeline_mode=pl.Buffered(1))
    return pl.pallas_call(
        functools.partial(_outffn_kernel, final_norm=final_norm),
        grid=(t // tm,),
        in_specs=[
            pl.BlockSpec((tm, D_MODEL), row),
            pl.BlockSpec((tm, ATTN_WIDTH), row),
            pl.BlockSpec((tm, SSM_WIDTH), row),
            pl.BlockSpec((tm, CONV_WIDTH), row),
            _mod_spec(mod, layer, 2, tiles_per_batch),
            _mod_spec(mod, layer, 3, tiles_per_batch),
            _mod_spec(mod, layer, 4, tiles_per_batch),
            _mod_spec(mod, layer, 5, tiles_per_batch),
            _layer_block((1, D_MODEL), layer),
            resident((D_MODEL, D_MODEL)),
            resident((D_MODEL, D_FF)),
            resident((D_MODEL, D_FF)),
            resident((D_FF, D_MODEL)),
            pl.BlockSpec((1, D_MODEL), lambda i: (0, 0)),
        ],
        out_specs=pl.BlockSpec((tm, D_MODEL), row),
        out_shape=jax.ShapeDtypeStruct((t, D_MODEL), F32),
        compiler_params=_params("parallel"),
        name="out_projection_ffn",
    )(x, o_attn, o_ssm, o_conv, mod.array, mod.array, mod.array, mod.array, p['norm2_g'],
      p['w_out'], p['w_gate'], p['w_up'], p['w_down'], p['final_g'])


def _out_ffn_conv(x, o_attn, o_ssm, vc, mod, p, layer, final_norm, tm, tiles_per_batch):
    t = x.shape[0]
    n_tiles = t // tm
    halo_per_tile = tm // CONV_HALO
    ffn_tile = lambda j: jnp.maximum(j - 1, 0)
    conv_tile = lambda j: jnp.minimum(j, n_tiles - 1)
    ffn_row = lambda j: (ffn_tile(j), 0)
    resident = lambda shape: _layer_block(shape, layer, pipeline_mode=pl.Buffered(1))
    cvec = _layer_block((1, CONV_WIDTH), layer)
    span = tm + CONV_HALO - SUBLANES
    return pl.pallas_call(
        functools.partial(_outffn_conv_kernel, final_norm=final_norm, n_tiles=n_tiles,
                          tiles_per_batch=tiles_per_batch, sub=CONV_ROWS_PER_STAGE),
        grid=(n_tiles + 1,),
        in_specs=[
            pl.BlockSpec((tm, D_MODEL), ffn_row),
            pl.BlockSpec((tm, ATTN_WIDTH), ffn_row),
            pl.BlockSpec((tm, SSM_WIDTH), ffn_row),
            pl.BlockSpec((CONV_HALO, CONV_WIDTH), lambda j: (jnp.maximum(conv_tile(j) * halo_per_tile - 1, 0), 0)),
            pl.BlockSpec((tm, CONV_WIDTH), lambda j: (conv_tile(j), 0)),
            _mod_spec(mod, layer, 2, tiles_per_batch, ffn_tile),
            _mod_spec(mod, layer, 3, tiles_per_batch, ffn_tile),
            _mod_spec(mod, layer, 4, tiles_per_batch, ffn_tile),
            _mod_spec(mod, layer, 5, tiles_per_batch, ffn_tile),
            _layer_block((1, D_MODEL), layer),
            resident((D_MODEL, D_MODEL)),
            resident((D_MODEL, D_FF)),
            resident((D_MODEL, D_FF)),
            resident((D_FF, D_MODEL)),
            pl.BlockSpec((1, D_MODEL), lambda j: (0, 0)),
            _layer_block((CONV_K, SUBLANES, CONV_WIDTH), layer), cvec, cvec, cvec,
        ],
        out_specs=pl.BlockSpec((tm, D_MODEL), ffn_row),
        out_shape=jax.ShapeDtypeStruct((t, D_MODEL), F32),
        scratch_shapes=[pltpu.VMEM((CONV_HALO + tm, CONV_WIDTH), F32),
                        pltpu.VMEM((SUBLANES - 1, span, CONV_WIDTH), F32),
                        pltpu.VMEM((2, tm, CONV_WIDTH), MXU_DTYPE)],
        compiler_params=_params("arbitrary"),
        name="conv_out_projection_ffn",
    )(x, o_attn, o_ssm, vc, vc, mod.array, mod.array, mod.array, mod.array, p['norm2_g'],
      p['w_out'], p['w_gate'], p['w_up'], p['w_down'], p['final_g'],
      p['conv_w_tiles'], p['conv_b'], p['conv_ln_g'], p['conv_ln_b'])


def _layer_prompt(x, mod, p, layer, rope, batch, seq, final_norm):
    tm = min(512, seq)
    tiles_per_batch = seq // tm
    tp = min(1024, seq)
    q, kv, u, vc = _in_projection(x, mod, p, layer, rope[0], rope[1], tp, seq // tp)
    o_attn = _attention_prompt(q, kv, p['sinks'], layer, batch, seq)
    n_keep = min(WINDOW, seq)
    kv_keep = kv.reshape(batch, seq, 2 * KV_WIDTH)[:, seq - n_keep:]
    new_k = kv_keep[:, :, :KV_WIDTH].reshape(batch, n_keep, N_KV_HEADS, HEAD_DIM)
    new_v = kv_keep[:, :, KV_WIDTH:].reshape(batch, n_keep, N_KV_HEADS, HEAD_DIM)

    nseg = PROMPT_SEGMENTS
    lseg = seq // nseg
    nseq = batch * nseg
    u_seq = u.reshape(nseq, lseg, SSM_WIDTH)
    steps = min(256, lseg)
    end_re, end_im = _s5_first_segment_states(u_seq, p, layer, nseg)
    start = lambda e: jnp.concatenate([jnp.zeros_like(e), e], axis=1).reshape(nseq, SSM_FLAT)
    o_seq, h_re, h_im = _s5_scan(u_seq, start(end_re), start(end_im), p, layer, nseq, steps)
    o_ssm = o_seq.reshape(batch * seq, SSM_WIDTH)
    last = lambda h: h.reshape(batch, nseg, N_SSM_GROUPS, SSM_STATE)[:, nseg - 1]

    new_conv = vc.reshape(batch, seq, CONV_WIDTH)[:, seq - (CONV_K - 1):]

    x = _out_ffn_conv(x, o_attn, o_ssm, vc, mod, p, layer, final_norm, tm, tiles_per_batch)
    return x, new_k, new_v, last(h_re), last(h_im), new_conv


def _layer_sample(x, mod, p, layer, rope, batch, n_new, kt_all, vt_all, h0_re, h0_im, conv_all, final_norm):
    t = batch * n_new
    qexp, kn, vn, u, vc = _in_projection(x, mod, p, layer, rope[0], rope[1], t, 1, n_new)
    o, new_k, new_v = _attention_sample(
        qexp.reshape(batch, N_Q_HEADS * n_new, LANES), kt_all, vt_all,
        kn.reshape(batch, SUBLANES, LANES), vn.reshape(batch, SUBLANES, LANES), p['sinks'], layer, n_new)
    o = o[:, :, :HEAD_DIM].reshape(batch, N_Q_HEADS, n_new, HEAD_DIM)
    o_attn = jnp.transpose(o, (2, 0, 1, 3)).reshape(t, ATTN_WIDTH)

    o_ssm, h_re, h_im = _s5_scan(u, h0_re.reshape(batch, SSM_FLAT), h0_im.reshape(batch, SSM_FLAT),
                                 p, layer, batch, n_new)
    st = lambda h: h.reshape(batch, N_SSM_GROUPS, SSM_STATE)

    o_conv, new_conv = _conv_sample(conv_all, vc.reshape(n_new, batch, CONV_WIDTH), p, layer)
    o_conv = o_conv.reshape(t, CONV_WIDTH)

    x = _out_ffn(x, o_attn, o_ssm, o_conv, mod, p, layer, final_norm, t, 1)
    return x, new_k, new_v, st(h_re), st(h_im), new_conv


def kernel(x_prompt, x_sample, c_prompt, c_sample, cache_k, cache_v, state_ssm_re, state_ssm_im, state_conv,
           norm1_g, norm2_g, w_mod, b_mod, w_in, attn_sinks, ssm_lam_re, ssm_lam_im, ssm_log_dt,
           ssm_b_re, ssm_b_im, ssm_c_re, ssm_c_im, ssm_d, ssm_w_glu, ssm_b_glu,
           conv_w, conv_b, conv_ln_g, conv_ln_b, w_out, w_gate, w_up, w_down, final_norm_g):
    bp, seq, d = x_prompt.shape
    bs, n_new, _ = x_sample.shape
    depth = w_in.shape[0]
    assert PROMPT_SEGMENTS == 2 and seq % (PROMPT_SEGMENTS * SUBLANES) == 0

    c_all = jnp.concatenate([c_sample, c_prompt], axis=0)
    pad_rows = -c_all.shape[0] % SUBLANES
    mods = _modulation(jnp.pad(c_all, ((0, pad_rows), (0, 0))), w_mod, b_mod)
    mod_p = _Mod(mods[:, bs:bs + bp].reshape(depth, bp, 1, N_MOD * d), None)
    mod_s = _Mod(mods, bs)

    a_re, a_im, bre_blk, bim_blk, cre_blk, cim_blk, decay_re, decay_im, span_re, span_im = _s5_discretise(
        ssm_lam_re, ssm_lam_im, ssm_log_dt, ssm_b_re, ssm_b_im, ssm_c_re, ssm_c_im,
        min(S5_STATE_CHUNK, seq // PROMPT_SEGMENTS))
    rope_p = _rope_tables(jnp.arange(seq))
    rope_s = _rope_tables(jnp.repeat(PAST_LEN + jnp.arange(n_new), bs))

    cast = lambda a: a.astype(MXU_DTYPE)
    vec = lambda a: a.reshape(depth, 1, a.shape[-1])
    p = {
        'norm1_g': vec(norm1_g), 'norm2_g': vec(norm2_g), 'w_in': cast(w_in), 'sinks': attn_sinks,
        'a_re': a_re, 'a_im': a_im,
        'bre_blk': bre_blk, 'bim_blk': bim_blk,
        'cre_blk': cre_blk, 'cim_blk': cim_blk,
        'decay_re': decay_re, 'decay_im': decay_im, 'span_re': span_re, 'span_im': span_im,
        'ssm_d': vec(ssm_d), 'ssm_w_glu': cast(ssm_w_glu), 'ssm_b_glu': vec(ssm_b_glu),
        'conv_w': conv_w, 'conv_w_tiles': jnp.repeat(conv_w[:, :, None, :], SUBLANES, axis=2),
        'conv_b': vec(conv_b), 'conv_ln_g': vec(conv_ln_g), 'conv_ln_b': vec(conv_ln_b),
        'w_out': cast(w_out), 'w_gate': cast(w_gate), 'w_up': cast(w_up), 'w_down': cast(w_down),
        'final_g': final_norm_g.reshape(1, d),
    }
    conv_all = jnp.transpose(state_conv, (0, 2, 1, 3))
    kt_all = jnp.transpose(cache_k, (0, 1, 3, 4, 2))
    vt_all = jnp.transpose(cache_v, (0, 1, 3, 4, 2))

    xp = x_prompt.reshape(bp * seq, d)
    xs = jnp.transpose(x_sample, (1, 0, 2)).reshape(n_new * bs, d)
    outs_p, outs_s = [], []
    for l in range(depth):
        final = l == depth - 1
        xp, *op = _layer_prompt(xp, mod_p, p, l, rope_p, bp, seq, final)
        xs, *os_ = _layer_sample(xs, mod_s, p, l, rope_s, bs, n_new, kt_all, vt_all,
                                 state_ssm_re[l], state_ssm_im[l], conv_all, final)
        outs_p.append(op)
        outs_s.append(os_)
    stack = lambda outs, i: jnp.stack([o[i] for o in outs])
    y_sample = jnp.transpose(xs.reshape(n_new, bs, d), (1, 0, 2))
    new_k_s = jnp.transpose(stack(outs_s, 0), (0, 1, 4, 2, 3))
    new_v_s = jnp.transpose(stack(outs_s, 1), (0, 1, 4, 2, 3))
    new_conv_s = jnp.transpose(stack(outs_s, 4), (0, 2, 1, 3))
    return (xp.reshape(bp, seq, d), y_sample, *[stack(outs_p, i) for i in range(5)],
            new_k_s, new_v_s, stack(outs_s, 2), stack(outs_s, 3), new_conv_s)
```

```python
import functools
import math
from typing import NamedTuple

import jax
import jax.numpy as jnp
from jax import lax
from jax.experimental import pallas as pl
from jax.experimental.pallas import tpu as pltpu

F32 = jnp.float32
MXU_DTYPE = jnp.bfloat16

V7X_VMEM_BYTES = 64 * 1024 * 1024
VMEM_LIMIT_BYTES = V7X_VMEM_BYTES - 8 * 1024 * 1024
LANES = 128
SUBLANES = 8
V7X_MXU_DIM = 256

D_MODEL = 1024
HEAD_DIM = 64
N_Q_HEADS = 8
N_KV_HEADS = 2
Q_PER_KV = N_Q_HEADS // N_KV_HEADS
ATTN_WIDTH = N_Q_HEADS * HEAD_DIM
KV_WIDTH = N_KV_HEADS * HEAD_DIM
WINDOW = 128
ROPE_THETA = 10000.0
ATTN_SCALE = 1.0 / math.sqrt(HEAD_DIM)
SSM_WIDTH = 256
SSM_GROUP = 16
N_SSM_GROUPS = 16
SSM_STATE = 64
SSM_FLAT = N_SSM_GROUPS * SSM_STATE
CONV_WIDTH = 256
CONV_K = 31
CONV_HALO = 32
CONV_ROWS_PER_STAGE = 64
IN_WIDTH = ATTN_WIDTH + 2 * KV_WIDTH + SSM_WIDTH + 2 * CONV_WIDTH
D_FF = 2816
FF_CHUNK = 6 * V7X_MXU_DIM
FF_STREAM_CHUNK = V7X_MXU_DIM
EPS = 1e-6
NEG = -1e30
N_MOD = 6
PROMPT_SEGMENTS = 2
S5_STATE_CHUNK = 512
PAST_LEN = 8192


def _params(*semantics):
    return pltpu.CompilerParams(dimension_semantics=semantics, vmem_limit_bytes=VMEM_LIMIT_BYTES)


def _layer_block(shape, layer, **kw):
    zeros = (0,) * len(shape)
    return pl.BlockSpec((None,) + tuple(shape), lambda *_: (layer,) + zeros, **kw)


def _sigmoid(x):
    return 1.0 / (1.0 + jnp.exp(-x))


def _silu(x):
    return x * _sigmoid(x)


def _rms(x, g):
    return x * lax.rsqrt(jnp.mean(x * x, axis=-1, keepdims=True) + EPS) * g


def _mm(a, b):
    return jnp.dot(a, b, preferred_element_type=F32)


def _mm_nt(a, b):
    return lax.dot_general(a, b, (((1,), (1,)), ((), ())), preferred_element_type=F32)


def _rows(m, n):
    return m if m.shape[0] == 1 else jnp.concatenate([m] * (n // m.shape[0]), axis=0)


def _mod_kernel(c_ref, w_ref, b_ref, o_ref):
    a = _silu(c_ref[...]).astype(MXU_DTYPE)
    o_ref[...] = _mm(a, w_ref[...].astype(MXU_DTYPE)) + b_ref[...]


def _modulation(c, w_mod, b_mod):
    depth, d, n = w_mod.shape
    rows = c.shape[0]
    tn = 1536
    return pl.pallas_call(
        _mod_kernel,
        grid=(depth, n // tn),
        in_specs=[
            pl.BlockSpec((rows, d), lambda l, j: (0, 0)),
            pl.BlockSpec((None, d, tn), lambda l, j: (l, 0, j)),
            pl.BlockSpec((None, 1, tn), lambda l, j: (l, 0, j)),
        ],
        out_specs=pl.BlockSpec((None, rows, tn), lambda l, j: (l, 0, j)),
        out_shape=jax.ShapeDtypeStruct((depth, rows, n), F32),
        compiler_params=_params("parallel", "parallel"),
        name="modulation",
    )(c, w_mod, b_mod.reshape(depth, 1, n))


class _Mod(NamedTuple):
    array: jax.Array
    batch_rows: int | None


def _mod_spec(mod, layer, chunk, tiles_per_batch, tile=lambda i: i):
    if mod.batch_rows is None:
        return pl.BlockSpec((None, None, 1, D_MODEL), lambda i: (layer, tile(i) // tiles_per_batch, 0, chunk))
    return pl.BlockSpec((None, mod.batch_rows, D_MODEL), lambda i: (layer, 0, chunk))


def _inproj_kernel(x_ref, sh_ref, sc_ref, g_ref, w_ref, cos_ref, sin_ref, *outs, n_new):
    tm = x_ref.shape[0]
    h = _rms(x_ref[...], g_ref[...]) * (1.0 + _rows(sc_ref[...], tm)) + _rows(sh_ref[...], tm)
    z = _mm(h.astype(MXU_DTYPE), w_ref[...])
    cos = cos_ref[...]
    sin = sin_ref[...]
    lane = lax.broadcasted_iota(jnp.int32, (tm, LANES), 1)
    first_half = (lane % HEAD_DIM) < (HEAD_DIM // 2)

    def rope(t):
        partner = jnp.where(first_half, pltpu.roll(t, LANES - HEAD_DIM // 2, 1),
                            pltpu.roll(t, HEAD_DIM // 2, 1))
        return t * cos + partner * sin

    q_cols = [rope(z[:, j * LANES:(j + 1) * LANES]) * ATTN_SCALE for j in range(ATTN_WIDTH // LANES)]
    o = ATTN_WIDTH
    k_rot = rope(z[:, o:o + KV_WIDTH])
    v_new = z[:, o + KV_WIDTH:o + 2 * KV_WIDTH]
    if n_new is None:
        q_ref, kv_ref, u_ref, vc_ref = outs
        for j, qc in enumerate(q_cols):
            q_ref[:, j * LANES:(j + 1) * LANES] = qc.astype(q_ref.dtype)
        kv_ref[:, 0:KV_WIDTH] = k_rot
        kv_ref[:, KV_WIDTH:2 * KV_WIDTH] = v_new
    else:
        q_ref, kn_ref, vn_ref, u_ref, vc_ref = outs
        batch = tm // n_new
        low = lax.broadcasted_iota(jnp.int32, (batch, LANES), 1) < HEAD_DIM
        kn_ref[...] = jnp.zeros(kn_ref.shape, F32)
        vn_ref[...] = jnp.zeros(vn_ref.shape, F32)
        for t in range(n_new):
            rows = slice(t * batch, (t + 1) * batch)
            tile_row = pl.ds(SUBLANES - n_new + t, batch, stride=SUBLANES)
            kn_ref[tile_row, :] = k_rot[rows]
            vn_ref[tile_row, :] = v_new[rows]
            for h in range(N_Q_HEADS):
                kvh = h // Q_PER_KV
                piece = q_cols[h // 2][rows]
                if h % 2 != kvh:
                    piece = pltpu.roll(piece, HEAD_DIM, 1)
                piece = jnp.where(low if kvh == 0 else ~low, piece, 0.0)
                q_ref[pl.ds(h * n_new + t, batch, stride=N_Q_HEADS * n_new), :] = piece
    o += 2 * KV_WIDTH
    u_ref[...] = z[:, o:o + SSM_WIDTH]
    o += SSM_WIDTH
    za = z[:, o:o + CONV_WIDTH]
    zg = z[:, o + CONV_WIDTH:o + 2 * CONV_WIDTH]
    vc_ref[...] = za * _sigmoid(zg)


def _in_projection(x, mod, p, layer, cos, sin, tm, tiles_per_batch, n_new=None):
    t = x.shape[0]
    pos_tiles = cos.shape[0] // tm
    row = lambda i: (i, 0)
    if n_new is None:
        attn_specs = [pl.BlockSpec((tm, ATTN_WIDTH), row), pl.BlockSpec((tm, 2 * KV_WIDTH), row)]
        attn_shapes = [jax.ShapeDtypeStruct((t, ATTN_WIDTH), MXU_DTYPE),
                       jax.ShapeDtypeStruct((t, 2 * KV_WIDTH), F32)]
    else:
        assert tm == t and KV_WIDTH == LANES
        batch = t // n_new
        shapes = [(batch * N_Q_HEADS * n_new, LANES), (batch * SUBLANES, LANES), (batch * SUBLANES, LANES)]
        attn_specs = [pl.BlockSpec(s, lambda i: (0, 0)) for s in shapes]
        attn_shapes = [jax.ShapeDtypeStruct(s, F32) for s in shapes]
    return pl.pallas_call(
        functools.partial(_inproj_kernel, n_new=n_new),
        grid=(t // tm,),
        in_specs=[
            pl.BlockSpec((tm, D_MODEL), row),
            _mod_spec(mod, layer, 0, tiles_per_batch),
            _mod_spec(mod, layer, 1, tiles_per_batch),
            _layer_block((1, D_MODEL), layer),
            _layer_block((D_MODEL, IN_WIDTH), layer),
            pl.BlockSpec((tm, LANES), lambda i: (i % pos_tiles, 0)),
            pl.BlockSpec((tm, LANES), lambda i: (i % pos_tiles, 0)),
        ],
        out_specs=attn_specs + [pl.BlockSpec((tm, SSM_WIDTH), row), pl.BlockSpec((tm, CONV_WIDTH), row)],
        out_shape=attn_shapes + [jax.ShapeDtypeStruct((t, SSM_WIDTH), F32),
                                 jax.ShapeDtypeStruct((t, CONV_WIDTH), F32)],
        compiler_params=_params("parallel"),
        name="in_projection",
    )(x, mod.array, mod.array, p['norm1_g'], p['w_in'], cos, sin)


def _rope_tables(pos):
    half = HEAD_DIM // 2
    inv_freq = ROPE_THETA ** (-jnp.arange(half, dtype=F32) / half)
    ang = pos.astype(F32)[:, None] * inv_freq[None, :]
    cos = jnp.tile(jnp.cos(ang), (1, LANES // half))
    sin = jnp.sin(ang)
    sin = jnp.tile(jnp.concatenate([-sin, sin], axis=1), (1, LANES // HEAD_DIM))
    return cos, sin


def _sink_softmax(s, sink_col):
    m = jnp.maximum(jnp.max(s, axis=-1, keepdims=True), sink_col)
    e = jnp.exp(s - m)
    return e * (1.0 / (jnp.sum(e, axis=-1, keepdims=True) + jnp.exp(sink_col - m)))


def _head_pair_select(x, pick_second):
    lane = lax.broadcasted_iota(jnp.int32, x.shape, 1)
    swapped = pltpu.roll(x, HEAD_DIM, 1)
    low = lane < HEAD_DIM
    return jnp.where(low, swapped, x) if pick_second else jnp.where(low, x, swapped)


def _attn_prompt_kernel(sink_ref, q_ref, kvp_ref, kvc_ref, o_ref, *, layer):
    n = pl.program_id(1)
    w = WINDOW
    n_blocks = q_ref.shape[0] // w
    rows = Q_PER_KV * w
    r_idx = lax.broadcasted_iota(jnp.int32, (rows, w), 0) % w
    c_idx = lax.broadcasted_iota(jnp.int32, (rows, w), 1)
    from_prev = c_idx > r_idx
    first_bias = jnp.where(n > 0, 0.0, NEG)
    row_head = lax.broadcasted_iota(jnp.int32, (rows, 1), 0) // w
    low = lax.broadcasted_iota(jnp.int32, (w, LANES), 1) < HEAD_DIM
    half_mask = [jnp.where(low, 1.0, 0.0).astype(MXU_DTYPE), jnp.where(low, 0.0, 1.0).astype(MXU_DTYPE)]
    sink_cols = []
    for kvh in range(N_KV_HEADS):
        sink_col = jnp.zeros((rows, 1), F32)
        for g in range(Q_PER_KV):
            sink_col = jnp.where(row_head == g, sink_ref[layer, kvh * Q_PER_KV + g], sink_col)
        sink_cols.append(sink_col)
    chains = [(i, kvh) for i in range(n_blocks) for kvh in range(N_KV_HEADS)]
    windows, values = [], []
    for i, kvh in chains:
        own = slice(i * w, (i + 1) * w)
        prev_ref, prev = (kvp_ref, slice(0, w)) if i == 0 else (kvc_ref, slice((i - 1) * w, i * w))
        kk = jnp.concatenate([prev_ref[prev, 0:KV_WIDTH], kvc_ref[own, 0:KV_WIDTH]], axis=0)
        vv = jnp.concatenate([prev_ref[prev, KV_WIDTH:2 * KV_WIDTH], kvc_ref[own, KV_WIDTH:2 * KV_WIDTH]], axis=0)
        k2 = _head_pair_select(kk, kvh == 1).astype(MXU_DTYPE)
        values.append(_head_pair_select(vv, kvh == 1).astype(MXU_DTYPE))
        pieces = []
        for g in range(Q_PER_KV):
            h = kvh * Q_PER_KV + g
            qcol = q_ref[own, (h // 2) * LANES:(h // 2 + 1) * LANES]
            pieces.append(qcol * half_mask[h % 2])
        s = _mm_nt(jnp.concatenate(pieces, axis=0), k2)
        s_prev = s[:, 0:w] + first_bias if i == 0 else s[:, 0:w]
        windows.append(jnp.where(from_prev, s_prev, s[:, w:2 * w]))
    probs = []
    for (i, kvh), sc in zip(chains, windows):
        p = _sink_softmax(sc, sink_cols[kvh])
        p2 = jnp.concatenate([jnp.where(from_prev, p, 0.0), jnp.where(from_prev, 0.0, p)], axis=1)
        probs.append(p2.astype(MXU_DTYPE))
    for (i, kvh), p2, v2 in zip(chains, probs, values):
        own = slice(i * w, (i + 1) * w)
        r = _mm(p2, v2)
        for j in range(Q_PER_KV // 2):
            col = kvh * (Q_PER_KV // 2) + j
            o_ref[own, col * LANES:(col + 1) * LANES] = jnp.where(
                low, r[2 * j * w:(2 * j + 1) * w], r[(2 * j + 1) * w:(2 * j + 2) * w]).astype(o_ref.dtype)


def _attention_prompt(q, kv, sinks, layer, batch, seq):
    tq = min(8 * WINDOW, seq)
    nt = seq // tq
    per = tq // WINDOW
    return pl.pallas_call(
        functools.partial(_attn_prompt_kernel, layer=layer),
        grid=(batch, nt),
        in_specs=[
            pl.BlockSpec(memory_space=pltpu.SMEM),
            pl.BlockSpec((tq, ATTN_WIDTH), lambda b, n: (b * nt + n, 0)),
            pl.BlockSpec((WINDOW, 2 * KV_WIDTH), lambda b, n: ((b * nt + n) * per - jnp.minimum(n, 1), 0)),
            pl.BlockSpec((tq, 2 * KV_WIDTH), lambda b, n: (b * nt + n, 0)),
        ],
        out_specs=pl.BlockSpec((tq, ATTN_WIDTH), lambda b, n: (b * nt + n, 0)),
        out_shape=jax.ShapeDtypeStruct((batch * seq, ATTN_WIDTH), MXU_DTYPE),
        compiler_params=_params("parallel", "parallel"),
        name="attention_prompt",
    )(sinks, q, kv, kv)


def _attn_sample_kernel(sink_ref, q_ref, kt_ref, vt_ref, kn_ref, vn_ref, o_ref, nk_ref, nv_ref, *, layer, n_new):
    bt, rows, _ = q_ref.shape
    nkv, hd, wb = kt_ref.shape[1:]
    kept = wb - n_new
    r = lax.broadcasted_iota(jnp.int32, (bt * rows, 2 * wb), 0)
    j = lax.broadcasted_iota(jnp.int32, (bt * rows, 2 * wb), 1)
    t_idx = r % n_new
    t_new = j - wb - kept
    mask = ((j < wb) & (j > t_idx)) | ((t_new >= 0) & (t_new <= t_idx))
    row_head = (lax.broadcasted_iota(jnp.int32, (bt * rows, 1), 0) % rows) // n_new
    sink_col = jnp.zeros((bt * rows, 1), F32)
    for h in range(N_Q_HEADS):
        sink_col = jnp.where(row_head == h, sink_ref[layer, h], sink_col)
    keep = lax.broadcasted_iota(jnp.int32, (nkv, hd, wb), 2) < kept
    second_group = lax.broadcasted_iota(jnp.int32, (rows, nkv * hd), 0) >= (rows // nkv)
    flat = lambda a: a.reshape(nkv * hd, wb)
    above = jnp.zeros((wb - kn_ref.shape[1], nkv * hd), F32)

    scores, values = [], []
    for b in range(bt):
        kt, vt = kt_ref[b], vt_ref[b]
        kpos = jnp.concatenate([above, kn_ref[b]], axis=0).T
        vpos = jnp.concatenate([above, vn_ref[b]], axis=0).T
        nk_ref[b] = jnp.where(keep, pltpu.roll(kt, kept, 2), kpos.reshape(nkv, hd, wb))
        nv_ref[b] = jnp.where(keep, pltpu.roll(vt, kept, 2), vpos.reshape(nkv, hd, wb))
        kcat = jnp.concatenate([flat(kt), kpos], axis=1).astype(MXU_DTYPE)
        values.append(jnp.concatenate([flat(vt), vpos], axis=1).astype(MXU_DTYPE))
        scores.append(_mm(q_ref[b].astype(MXU_DTYPE), kcat))
    s = jnp.where(mask, jnp.concatenate(scores, axis=0), NEG)
    p = _sink_softmax(s, sink_col).astype(MXU_DTYPE)
    for b in range(bt):
        o = _mm_nt(p[b * rows:(b + 1) * rows], values[b])
        o_ref[b] = jnp.where(second_group, pltpu.roll(o, hd, 1), o).astype(o_ref.dtype)


def _attention_sample(qexp, kt_all, vt_all, kn, vn, sinks, layer, n_new):
    batch, rows, width = qexp.shape
    _, _, nkv, hd, wb = kt_all.shape
    bt = min(16, batch)
    blk3 = pl.BlockSpec((bt, rows, width), lambda i: (i, 0, 0))
    blk4 = pl.BlockSpec((bt, nkv, hd, wb), lambda i: (i, 0, 0, 0))
    new = pl.BlockSpec((bt,) + kn.shape[1:], lambda i: (i, 0, 0))
    cache = pl.BlockSpec((None, bt, nkv, hd, wb), lambda i: (layer, i, 0, 0, 0))
    return pl.pallas_call(
        functools.partial(_attn_sample_kernel, layer=layer, n_new=n_new),
        grid=(batch // bt,),
        in_specs=[pl.BlockSpec(memory_space=pltpu.SMEM), blk3, cache, cache, new, new],
        out_specs=[blk3, blk4, blk4],
        out_shape=[jax.ShapeDtypeStruct((batch, rows, width), MXU_DTYPE),
                   jax.ShapeDtypeStruct((batch, nkv, hd, wb), F32),
                   jax.ShapeDtypeStruct((batch, nkv, hd, wb), F32)],
        compiler_params=_params("parallel"),
        name="attention_sample",
    )(sinks, qexp, kt_all, vt_all, kn, vn)


def _cmul(ar, ai, br, bi):
    return ar * br - ai * bi, ar * bi + ai * br


def _s5_discretise_kernel(lr_ref, li_ref, ldt_ref, bre_ref, bim_ref, cre_ref, cim_ref,
                          are_ref, aim_ref, bbre_ref, bbim_ref, ccre_ref, ccim_ref,
                          decay_re_ref, decay_im_ref, span_re_ref, span_im_ref):
    lr = lr_ref[...]
    li = li_ref[...]
    dt = jnp.exp(ldt_ref[...])
    mag = jnp.exp(lr * dt)
    ab_re = mag * jnp.cos(li * dt)
    ab_im = mag * jnp.sin(li * dt)
    den = lr * lr + li * li
    nr = ab_re - 1.0
    coef_re = (nr * lr + ab_im * li) / den
    coef_im = (ab_im * lr - nr * li) / den
    are_ref[...] = ab_re
    aim_ref[...] = ab_im
    br = bre_ref[...]
    bi = bim_ref[...]
    bb_re = coef_re * br - coef_im * bi
    bb_im = coef_re * bi + coef_im * br
    lane_group = lax.broadcasted_iota(jnp.int32, bb_re.shape, 1) // SSM_STATE
    for src, dst in ((bb_re, bbre_ref), (bb_im, bbim_ref), (cre_ref[...], ccre_ref), (cim_ref[...], ccim_ref)):
        for g in range(N_SSM_GROUPS):
            dst[g * SSM_GROUP:(g + 1) * SSM_GROUP, :] = jnp.where(lane_group == g, src, 0.0).astype(dst.dtype)
    t_chunk = decay_re_ref.shape[0]
    row = lax.broadcasted_iota(jnp.int32, (SUBLANES, SSM_FLAT), 0)
    pr, pi = jnp.ones_like(ab_re), jnp.zeros_like(ab_re)
    tile_re, tile_im = jnp.zeros((SUBLANES, SSM_FLAT), F32), jnp.zeros((SUBLANES, SSM_FLAT), F32)
    for k in range(SUBLANES):
        tile_re = jnp.where(row == SUBLANES - 1 - k, pr, tile_re)
        tile_im = jnp.where(row == SUBLANES - 1 - k, pi, tile_im)
        pr, pi = _cmul(pr, pi, ab_re, ab_im)
    decay_re_ref[t_chunk - SUBLANES:t_chunk, :] = tile_re
    decay_im_ref[t_chunk - SUBLANES:t_chunk, :] = tile_im
    n = SUBLANES
    while n < t_chunk:
        lo, hi = slice(t_chunk - 2 * n, t_chunk - n), slice(t_chunk - n, t_chunk)
        decay_re_ref[lo, :], decay_im_ref[lo, :] = _cmul(decay_re_ref[hi, :], decay_im_ref[hi, :], pr, pi)
        pr, pi = _cmul(pr, pi, pr, pi)
        n *= 2
    span_re_ref[...] = pr
    span_im_ref[...] = pi


def _s5_discretise(lam_re, lam_im, log_dt, b_re, b_im, c_re, c_im, t_chunk):
    depth = lam_re.shape[0]
    assert t_chunk % SUBLANES == 0 and (t_chunk // SUBLANES) & (t_chunk // SUBLANES - 1) == 0
    flat = lambda a: a.reshape(depth, 1, SSM_FLAT)
    ldt = jnp.broadcast_to(log_dt[:, :, None], lam_re.shape)
    bt = lambda a: jnp.transpose(a, (0, 3, 1, 2)).reshape(depth, SSM_GROUP, SSM_FLAT)
    ct = lambda a: jnp.transpose(a, (0, 2, 1, 3)).reshape(depth, SSM_GROUP, SSM_FLAT)
    vec = pl.BlockSpec((None, 1, SSM_FLAT), lambda l: (l, 0, 0))
    mat = pl.BlockSpec((None, SSM_GROUP, SSM_FLAT), lambda l: (l, 0, 0))
    blk = pl.BlockSpec((None, SSM_WIDTH, SSM_FLAT), lambda l: (l, 0, 0))
    tab = pl.BlockSpec((None, t_chunk, SSM_FLAT), lambda l: (l, 0, 0))
    return pl.pallas_call(
        _s5_discretise_kernel,
        grid=(depth,),
        in_specs=[vec, vec, vec, mat, mat, mat, mat],
        out_specs=[vec, vec, blk, blk, blk, blk, tab, tab, vec, vec],
        out_shape=[jax.ShapeDtypeStruct((depth, 1, SSM_FLAT), F32)] * 2
        + [jax.ShapeDtypeStruct((depth, SSM_WIDTH, SSM_FLAT), MXU_DTYPE)] * 4
        + [jax.ShapeDtypeStruct((depth, t_chunk, SSM_FLAT), F32)] * 2
        + [jax.ShapeDtypeStruct((depth, 1, SSM_FLAT), F32)] * 2,
        compiler_params=_params("parallel"),
        name="s5_discretise",
    )(flat(lam_re), flat(lam_im), flat(ldt), bt(b_re), bt(b_im), ct(c_re), ct(c_im))


def _s5_first_segment_kernel(u_ref, bre_ref, bim_ref, dre_ref, dim_ref, sre_ref, sim_ref, ere_ref, eim_ref,
                             hre_s, him_s):
    @pl.when(pl.program_id(1) == 0)
    def _():
        hre_s[...] = jnp.zeros(hre_s.shape, F32)
        him_s[...] = jnp.zeros(him_s.shape, F32)

    ub = u_ref[...].astype(MXU_DTYPE)
    wr, wi = _cmul(dre_ref[...], dim_ref[...], _mm(ub, bre_ref[...]), _mm(ub, bim_ref[...]))
    cr, ci = _cmul(sre_ref[...], sim_ref[...], hre_s[...], him_s[...])
    hr = cr + jnp.sum(wr, axis=0, keepdims=True)
    hi = ci + jnp.sum(wi, axis=0, keepdims=True)
    hre_s[...] = hr
    him_s[...] = hi
    ere_ref[...] = hr
    eim_ref[...] = hi


def _s5_first_segment_states(u_seq, p, layer, nseg):
    nseq, lseg, _ = u_seq.shape
    batch = nseq // nseg
    t_chunk = p['decay_re'].shape[1]
    vec = _layer_block((1, SSM_FLAT), layer)
    tab = _layer_block((t_chunk, SSM_FLAT), layer)
    bmat = _layer_block((SSM_WIDTH, SSM_FLAT), layer)
    out = pl.BlockSpec((None, 1, SSM_FLAT), lambda b, c: (b, 0, 0))
    return pl.pallas_call(
        _s5_first_segment_kernel,
        grid=(batch, lseg // t_chunk),
        in_specs=[pl.BlockSpec((None, t_chunk, SSM_WIDTH), lambda b, c: (b * nseg, c, 0)),
                  bmat, bmat, tab, tab, vec, vec],
        out_specs=[out, out],
        out_shape=[jax.ShapeDtypeStruct((batch, 1, SSM_FLAT), F32)] * 2,
        scratch_shapes=[pltpu.VMEM((1, SSM_FLAT), F32)] * 2,
        compiler_params=_params("parallel", "arbitrary"),
        name="s5_first_segment_states",
    )(u_seq, p['bre_blk'], p['bim_blk'], p['decay_re'], p['decay_im'], p['span_re'], p['span_im'])


def _s5_kernel(u_ref, hre0_ref, him0_ref, are_ref, aim_ref, bre_ref, bim_ref, cre_ref, cim_ref, d_ref,
               wglu_ref, bglu_ref, o_ref, hre_out, him_out, hre_s, him_s, bure_s, buim_s, hsre_s, hsim_s,
               *stage, nseq):
    seq_major = bool(stage)
    steps = u_ref.shape[1] if seq_major else u_ref.shape[0] // nseq
    lane_halves = [slice(h * LANES, (h + 1) * LANES) for h in range(SSM_WIDTH // LANES)]

    @pl.when(pl.program_id(0) == 0)
    def _():
        hre_s[...] = hre0_ref[...]
        him_s[...] = him0_ref[...]

    if seq_major:
        stage_s, = stage
        for s in range(nseq):
            for h, cols in enumerate(lane_halves):
                stage_s[h, pl.ds(s, steps, stride=nseq), :] = u_ref[s, :, cols]
        u = jnp.concatenate([stage_s[h] for h in range(len(lane_halves))], axis=1)
    else:
        u = u_ref[...]
    ub = u.astype(MXU_DTYPE)
    bure_s[...] = _mm(ub, bre_ref[...])
    buim_s[...] = _mm(ub, bim_ref[...])
    ar = jnp.broadcast_to(are_ref[...], (nseq, SSM_FLAT))
    ai = jnp.broadcast_to(aim_ref[...], (nseq, SSM_FLAT))

    def step(t, carry):
        hr, hi = carry
        rows = pl.ds(pl.multiple_of(t * nseq, nseq), nseq)
        nhr = ar * hr - ai * hi + bure_s[rows, :]
        nhi = ar * hi + ai * hr + buim_s[rows, :]
        hsre_s[rows, :] = nhr
        hsim_s[rows, :] = nhi
        return nhr, nhi

    hr, hi = lax.fori_loop(0, steps, step, (hre_s[...], him_s[...]), unroll=min(steps, 8))
    hre_s[...] = hr
    him_s[...] = hi
    hre_out[...] = hr
    him_out[...] = hi
    y = (_mm_nt(hsre_s[...].astype(MXU_DTYPE), cre_ref[...])
         - _mm_nt(hsim_s[...].astype(MXU_DTYPE), cim_ref[...]))
    z = jax.nn.gelu(y + d_ref[...] * u)
    gate = _mm(z.astype(MXU_DTYPE), wglu_ref[...]) + bglu_ref[...]
    out = z * _sigmoid(gate)
    if seq_major:
        for h, cols in enumerate(lane_halves):
            stage_s[h] = out[:, cols]
        for s in range(nseq):
            for h, cols in enumerate(lane_halves):
                o_ref[s, :, cols] = stage_s[h, pl.ds(s, steps, stride=nseq), :].astype(o_ref.dtype)
    else:
        o_ref[...] = out.astype(o_ref.dtype)


def _s5_scan(u, hre0, him0, p, layer, nseq, steps_per_tile):
    seq_major = u.ndim == 3
    rows = u.shape[0] * u.shape[1] if seq_major else u.shape[0]
    tr = steps_per_tile * nseq
    state = pl.BlockSpec((nseq, SSM_FLAT), lambda i: (0, 0))
    if seq_major:
        io_spec = pl.BlockSpec((nseq, steps_per_tile, SSM_WIDTH), lambda i: (0, i, 0))
    else:
        io_spec = pl.BlockSpec((tr, SSM_WIDTH), lambda i: (i, 0))
    state_vec = _layer_block((1, SSM_FLAT), layer)
    group_map = _layer_block((SSM_WIDTH, SSM_FLAT), layer)
    width_vec = _layer_block((1, SSM_WIDTH), layer)
    scratch = ([pltpu.VMEM((nseq, SSM_FLAT), F32)] * 2 + [pltpu.VMEM((tr, SSM_FLAT), F32)] * 4
               + [pltpu.VMEM((SSM_WIDTH // LANES, tr, LANES), F32)] * seq_major)
    return pl.pallas_call(
        functools.partial(_s5_kernel, nseq=nseq),
        grid=(rows // tr,),
        in_specs=[io_spec, state, state, state_vec, state_vec, group_map, group_map, group_map, group_map,
                  width_vec, _layer_block((SSM_WIDTH, SSM_WIDTH), layer), width_vec],
        out_specs=[io_spec, state, state],
        out_shape=[jax.ShapeDtypeStruct(u.shape, MXU_DTYPE)] + [jax.ShapeDtypeStruct((nseq, SSM_FLAT), F32)] * 2,
        scratch_shapes=scratch,
        compiler_params=_params("arbitrary"),
        name="s5_scan",
    )(u, hre0, him0, p['a_re'], p['a_im'], p['bre_blk'], p['bim_blk'], p['cre_blk'], p['cim_blk'],
      p['ssm_d'], p['ssm_w_glu'], p['ssm_b_glu'])


def _layernorm_silu(y, g, b):
    yc = y - jnp.mean(y, axis=-1, keepdims=True)
    var = jnp.mean(yc * yc, axis=-1, keepdims=True)
    return _silu(yc * lax.rsqrt(var + EPS) * g + b)


def _conv_sample_kernel(state_ref, v_ref, w_ref, b_ref, g_ref, beta_ref, o_ref, ns_ref):
    ns = state_ref.shape[0]
    n_new = v_ref.shape[0]
    row = lambda j: state_ref[j] if j < ns else v_ref[j - ns]
    for t in range(n_new):
        acc = jnp.zeros(o_ref.shape[1:], F32)
        for k in range(CONV_K):
            acc = acc + w_ref[k:k + 1, :] * row(t + k)
        o_ref[t] = _layernorm_silu(acc + b_ref[...], g_ref[...], beta_ref[...]).astype(o_ref.dtype)
    for j in range(ns):
        ns_ref[j] = row(j + n_new)


def _conv_sample(state_all, v, p, layer):
    _, ns, batch, width = state_all.shape
    n_new = v.shape[0]
    bt = min(32, batch)
    vec = _layer_block((1, width), layer)
    return pl.pallas_call(
        _conv_sample_kernel,
        grid=(batch // bt,),
        in_specs=[
            pl.BlockSpec((None, ns, bt, width), lambda i: (layer, 0, i, 0)),
            pl.BlockSpec((n_new, bt, width), lambda i: (0, i, 0)),
            _layer_block((CONV_K, width), layer), vec, vec, vec,
        ],
        out_specs=[pl.BlockSpec((n_new, bt, width), lambda i: (0, i, 0)),
                   pl.BlockSpec((ns, bt, width), lambda i: (0, i, 0))],
        out_shape=[jax.ShapeDtypeStruct((n_new, batch, width), MXU_DTYPE),
                   jax.ShapeDtypeStruct((ns, batch, width), F32)],
        compiler_params=_params("parallel"),
        name="conv_sample",
    )(state_all, v, p['conv_w'], p['conv_b'], p['conv_ln_g'], p['conv_ln_b'])


def _outffn_stages(x_ref, a_ref, s_ref, load_conv, g1_ref, sh2_ref, sc2_ref, g2_ref, n2_ref,
                   wo_ref, wg_ref, wu_ref, wd_ref, fg_ref, o_ref, final_norm):
    tm = x_ref.shape[0]
    o1 = ATTN_WIDTH
    o2 = o1 + SSM_WIDTH
    st = {}

    def proj_attn_ssm():
        st['proj'] = _mm(a_ref[...], wo_ref[0:o1, :]) + _mm(s_ref[...], wo_ref[o1:o2, :])

    def proj_conv_residual_norm():
        proj = st.pop('proj') + _mm(load_conv(), wo_ref[o2:o2 + CONV_WIDTH, :])
        x2 = x_ref[...] + _rows(g1_ref[...], tm) * proj
        h2 = _rms(x2, n2_ref[...]) * (1.0 + _rows(sc2_ref[...], tm)) + _rows(sh2_ref[...], tm)
        st['x2'] = x2
        st['h2'] = h2.astype(MXU_DTYPE)

    def gate(cs):
        st['gate'] = _mm(st['h2'], wg_ref[:, cs])

    def up(cs):
        st['act'] = (_silu(st.pop('gate')) * _mm(st['h2'], wu_ref[:, cs])).astype(MXU_DTYPE)

    def down(cs, last):
        ffn = _mm(st.pop('act'), wd_ref[cs, :])
        st['ffn'] = ffn if 'ffn' not in st else st['ffn'] + ffn
        if last:
            y = st['x2'] + _rows(g2_ref[...], tm) * st['ffn']
            o_ref[...] = _rms(y, fg_ref[...]) if final_norm else y

    stages = [proj_attn_ssm, proj_conv_residual_norm]
    for start in range(0, D_FF, FF_CHUNK):
        cs = slice(start, min(start + FF_CHUNK, D_FF))
        stages += [functools.partial(gate, cs), functools.partial(up, cs),
                   functools.partial(down, cs, cs.stop == D_FF)]
    return stages


def _outffn_stream_kernel(x_ref, a_ref, s_ref, c_ref, g1_ref, sh2_ref, sc2_ref, g2_ref, n2_ref,
                          wo_ref, wg_ref, wu_ref, wd_ref, fg_ref,
                          o_ref, wo_out, wg_out, wu_out, wd_out, x2_s, h2_s, acc_s, *, final_norm):
    j = pl.program_id(0)
    tm = x_ref.shape[0]

    @pl.when(j == 0)
    def _():
        o1 = ATTN_WIDTH
        o2 = o1 + SSM_WIDTH
        wo = wo_ref[...].astype(MXU_DTYPE)
        wo_out[...] = wo
        proj = _mm(a_ref[...], wo[0:o1]) + _mm(s_ref[...], wo[o1:o2]) + _mm(c_ref[...], wo[o2:o2 + CONV_WIDTH])
        x2 = x_ref[...] + _rows(g1_ref[...], tm) * proj
        h2 = _rms(x2, n2_ref[...]) * (1.0 + _rows(sc2_ref[...], tm)) + _rows(sh2_ref[...], tm)
        x2_s[...] = x2
        h2_s[...] = h2.astype(MXU_DTYPE)
        acc_s[...] = jnp.zeros(acc_s.shape, F32)

    wg = wg_ref[...].astype(MXU_DTYPE)
    wu = wu_ref[...].astype(MXU_DTYPE)
    wd = wd_ref[...].astype(MXU_DTYPE)
    wg_out[...] = wg
    wu_out[...] = wu
    wd_out[...] = wd
    h2 = h2_s[...]
    act = (_silu(_mm(h2, wg)) * _mm(h2, wu)).astype(MXU_DTYPE)
    acc_s[...] += _mm(act, wd)

    @pl.when(j == pl.num_programs(0) - 1)
    def _():
        y = x2_s[...] + _rows(g2_ref[...], tm) * acc_s[...]
        o_ref[...] = _rms(y, fg_ref[...]) if final_norm else y


def _conv_stages(halo_ref, cur_ref, w_ref, b_ref, g_ref, beta_ref, store, buf, shifted, first, sub):
    tc = cur_ref.shape[0]
    base = CONV_HALO - (CONV_K - 1)
    span = shifted.shape[1]

    def fill():
        buf[0:CONV_HALO, :] = jnp.where(first, 0.0, halo_ref[...])
        buf[CONV_HALO:CONV_HALO + tc, :] = cur_ref[...]
        for s in range(1, SUBLANES):
            shifted[s - 1] = buf[s:s + span, :]

    def rows(r):
        acc = jnp.zeros((sub // SUBLANES, SUBLANES, CONV_WIDTH), F32)
        for k in range(CONV_K):
            j, s = (base + k) // SUBLANES, (base + k) % SUBLANES
            start = r * sub + j * SUBLANES
            tap = buf[start:start + sub, :] if s == 0 else shifted[s - 1, start:start + sub, :]
            acc = acc + w_ref[k][None] * tap.reshape(acc.shape)
        y = _layernorm_silu(acc.reshape(sub, CONV_WIDTH) + b_ref[...], g_ref[...], beta_ref[...])
        store(slice(r * sub, (r + 1) * sub), y)

    return [fill] + [functools.partial(rows, r) for r in range(tc // sub)]


def _outffn_conv_kernel(x_ref, a_ref, s_ref, halo_ref, vcur_ref, g1_ref, sh2_ref, sc2_ref, g2_ref, n2_ref,
                        wo_ref, wg_ref, wu_ref, wd_ref, fg_ref, cw_ref, cb_ref, cg_ref, cbeta_ref,
                        o_ref, buf, shifted, conv_out, *, final_norm, n_tiles, tiles_per_batch, sub):
    j = pl.program_id(0)
    conv_tile = jnp.minimum(j, n_tiles - 1)
    first = (conv_tile % tiles_per_batch) == 0

    def conv_into(slot):
        def store(rows, y):
            conv_out[slot, rows, :] = y.astype(conv_out.dtype)
        return _conv_stages(halo_ref, vcur_ref, cw_ref, cb_ref, cg_ref, cbeta_ref, store, buf, shifted, first, sub)

    @pl.when(j == 0)
    def _():
        for stage in conv_into(0):
            stage()

    @pl.when(j > 0)
    def _():
        slot = j % 2
        conv = conv_into(slot)
        ffn = _outffn_stages(x_ref, a_ref, s_ref, lambda: conv_out[1 - slot], g1_ref, sh2_ref, sc2_ref, g2_ref,
                             n2_ref, wo_ref, wg_ref, wu_ref, wd_ref, fg_ref, o_ref, final_norm)
        conv[0]()
        per_ffn = -(-(len(conv) - 1) // len(ffn))
        for k, ffn_stage in enumerate(ffn):
            for conv_stage in conv[1 + k * per_ffn:1 + (k + 1) * per_ffn]:
                conv_stage()
            ffn_stage()


def _out_ffn_sample(x, o_attn, o_ssm, o_conv, mod, p, layer, final_norm):
    t = x.shape[0]
    fc = FF_STREAM_CHUNK
    whole = lambda *shape: pl.BlockSpec(shape, lambda j: (0,) * len(shape))
    gate_up = pl.BlockSpec((None, D_MODEL, fc), lambda j: (layer, 0, j))
    down = pl.BlockSpec((None, fc, D_MODEL), lambda j: (layer, j, 0))
    return pl.pallas_call(
        functools.partial(_outffn_stream_kernel, final_norm=final_norm),
        grid=(D_FF // fc,),
        in_specs=[
            whole(t, D_MODEL), whole(t, ATTN_WIDTH), whole(t, SSM_WIDTH), whole(t, CONV_WIDTH),
            _mod_spec(mod, layer, 2, 1), _mod_spec(mod, layer, 3, 1),
            _mod_spec(mod, layer, 4, 1), _mod_spec(mod, layer, 5, 1),
            _layer_block((1, D_MODEL), layer),
            _layer_block((D_MODEL, D_MODEL), layer), gate_up, gate_up, down,
            whole(1, D_MODEL),
        ],
        out_specs=[whole(t, D_MODEL), whole(D_MODEL, D_MODEL),
                   pl.BlockSpec((D_MODEL, fc), lambda j: (0, j)), pl.BlockSpec((D_MODEL, fc), lambda j: (0, j)),
                   pl.BlockSpec((fc, D_MODEL), lambda j: (j, 0))],
        out_shape=[jax.ShapeDtypeStruct((t, D_MODEL), F32), jax.ShapeDtypeStruct((D_MODEL, D_MODEL), MXU_DTYPE),
                   jax.ShapeDtypeStruct((D_MODEL, D_FF), MXU_DTYPE), jax.ShapeDtypeStruct((D_MODEL, D_FF), MXU_DTYPE),
                   jax.ShapeDtypeStruct((D_FF, D_MODEL), MXU_DTYPE)],
        scratch_shapes=[pltpu.VMEM((t, D_MODEL), F32), pltpu.VMEM((t, D_MODEL), MXU_DTYPE),
                        pltpu.VMEM((t, D_MODEL), F32)],
        compiler_params=_params("arbitrary"),
        name="out_projection_ffn_stream",
    )(x, o_attn, o_ssm, o_conv, mod.array, mod.array, mod.array, mod.array, p['norm2_g'],
      p['w_out'], p['w_gate'], p['w_up'], p['w_down'], p['final_g'])


def _out_ffn_conv(x, o_attn, o_ssm, vc, mod, p, weights, layer, final_norm, tm, tiles_per_batch):
    t = x.shape[0]
    n_tiles = t // tm
    halo_per_tile = tm // CONV_HALO
    ffn_tile = lambda j: jnp.maximum(j - 1, 0)
    conv_tile = lambda j: jnp.minimum(j, n_tiles - 1)
    ffn_row = lambda j: (ffn_tile(j), 0)
    resident = lambda shape: pl.BlockSpec(shape, lambda j: (0, 0), pipeline_mode=pl.Buffered(1))
    cvec = _layer_block((1, CONV_WIDTH), layer)
    span = tm + CONV_HALO - SUBLANES
    return pl.pallas_call(
        functools.partial(_outffn_conv_kernel, final_norm=final_norm, n_tiles=n_tiles,
                          tiles_per_batch=tiles_per_batch, sub=CONV_ROWS_PER_STAGE),
        grid=(n_tiles + 1,),
        in_specs=[
            pl.BlockSpec((tm, D_MODEL), ffn_row),
            pl.BlockSpec((tm, ATTN_WIDTH), ffn_row),
            pl.BlockSpec((tm, SSM_WIDTH), ffn_row),
            pl.BlockSpec((CONV_HALO, CONV_WIDTH), lambda j: (jnp.maximum(conv_tile(j) * halo_per_tile - 1, 0), 0)),
            pl.BlockSpec((tm, CONV_WIDTH), lambda j: (conv_tile(j), 0)),
            _mod_spec(mod, layer, 2, tiles_per_batch, ffn_tile),
            _mod_spec(mod, layer, 3, tiles_per_batch, ffn_tile),
            _mod_spec(mod, layer, 4, tiles_per_batch, ffn_tile),
            _mod_spec(mod, layer, 5, tiles_per_batch, ffn_tile),
            _layer_block((1, D_MODEL), layer),
            resident((D_MODEL, D_MODEL)),
            resident((D_MODEL, D_FF)),
            resident((D_MODEL, D_FF)),
            resident((D_FF, D_MODEL)),
            pl.BlockSpec((1, D_MODEL), lambda j: (0, 0)),
            _layer_block((CONV_K, SUBLANES, CONV_WIDTH), layer), cvec, cvec, cvec,
        ],
        out_specs=pl.BlockSpec((tm, D_MODEL), ffn_row),
        out_shape=jax.ShapeDtypeStruct((t, D_MODEL), F32),
        scratch_shapes=[pltpu.VMEM((CONV_HALO + tm, CONV_WIDTH), F32),
                        pltpu.VMEM((SUBLANES - 1, span, CONV_WIDTH), F32),
                        pltpu.VMEM((2, tm, CONV_WIDTH), MXU_DTYPE)],
        compiler_params=_params("arbitrary"),
        name="conv_out_projection_ffn",
    )(x, o_attn, o_ssm, vc, vc, mod.array, mod.array, mod.array, mod.array, p['norm2_g'],
      *weights, p['final_g'], p['conv_w_tiles'], p['conv_b'], p['conv_ln_g'], p['conv_ln_b'])


def _layer_prompt(x, mod, p, ffn_weights, layer, rope, batch, seq, final_norm):
    tm = min(512, seq)
    tiles_per_batch = seq // tm
    tp = min(1024, seq)
    q, kv, u, vc = _in_projection(x, mod, p, layer, rope[0], rope[1], tp, seq // tp)
    o_attn = _attention_prompt(q, kv, p['sinks'], layer, batch, seq)
    n_keep = min(WINDOW, seq)
    kv_keep = kv.reshape(batch, seq, 2 * KV_WIDTH)[:, seq - n_keep:]
    new_k = kv_keep[:, :, :KV_WIDTH].reshape(batch, n_keep, N_KV_HEADS, HEAD_DIM)
    new_v = kv_keep[:, :, KV_WIDTH:].reshape(batch, n_keep, N_KV_HEADS, HEAD_DIM)

    nseg = PROMPT_SEGMENTS
    lseg = seq // nseg
    nseq = batch * nseg
    u_seq = u.reshape(nseq, lseg, SSM_WIDTH)
    steps = min(256, lseg)
    end_re, end_im = _s5_first_segment_states(u_seq, p, layer, nseg)
    start = lambda e: jnp.concatenate([jnp.zeros_like(e), e], axis=1).reshape(nseq, SSM_FLAT)
    o_seq, h_re, h_im = _s5_scan(u_seq, start(end_re), start(end_im), p, layer, nseq, steps)
    o_ssm = o_seq.reshape(batch * seq, SSM_WIDTH)
    last = lambda h: h.reshape(batch, nseg, N_SSM_GROUPS, SSM_STATE)[:, nseg - 1]

    new_conv = vc.reshape(batch, seq, CONV_WIDTH)[:, seq - (CONV_K - 1):]

    x = _out_ffn_conv(x, o_attn, o_ssm, vc, mod, p, ffn_weights, layer, final_norm, tm, tiles_per_batch)
    return x, new_k, new_v, last(h_re), last(h_im), new_conv


def _layer_sample(x, mod, p, layer, rope, batch, n_new, kt_all, vt_all, h0_re, h0_im, conv_all, final_norm):
    t = batch * n_new
    qexp, kn, vn, u, vc = _in_projection(x, mod, p, layer, rope[0], rope[1], t, 1, n_new)
    o, new_k, new_v = _attention_sample(
        qexp.reshape(batch, N_Q_HEADS * n_new, LANES), kt_all, vt_all,
        kn.reshape(batch, SUBLANES, LANES), vn.reshape(batch, SUBLANES, LANES), p['sinks'], layer, n_new)
    o = o[:, :, :HEAD_DIM].reshape(batch, N_Q_HEADS, n_new, HEAD_DIM)
    o_attn = jnp.transpose(o, (2, 0, 1, 3)).reshape(t, ATTN_WIDTH)

    o_ssm, h_re, h_im = _s5_scan(u, h0_re.reshape(batch, SSM_FLAT), h0_im.reshape(batch, SSM_FLAT),
                                 p, layer, batch, n_new)
    st = lambda h: h.reshape(batch, N_SSM_GROUPS, SSM_STATE)

    o_conv, new_conv = _conv_sample(conv_all, vc.reshape(n_new, batch, CONV_WIDTH), p, layer)
    o_conv = o_conv.reshape(t, CONV_WIDTH)

    x, *ffn_weights = _out_ffn_sample(x, o_attn, o_ssm, o_conv, mod, p, layer, final_norm)
    return x, ffn_weights, new_k, new_v, st(h_re), st(h_im), new_conv


def kernel(x_prompt, x_sample, c_prompt, c_sample, cache_k, cache_v, state_ssm_re, state_ssm_im, state_conv,
           norm1_g, norm2_g, w_mod, b_mod, w_in, attn_sinks, ssm_lam_re, ssm_lam_im, ssm_log_dt,
           ssm_b_re, ssm_b_im, ssm_c_re, ssm_c_im, ssm_d, ssm_w_glu, ssm_b_glu,
           conv_w, conv_b, conv_ln_g, conv_ln_b, w_out, w_gate, w_up, w_down, final_norm_g):
    bp, seq, d = x_prompt.shape
    bs, n_new, _ = x_sample.shape
    depth = w_in.shape[0]
    assert PROMPT_SEGMENTS == 2 and seq % (PROMPT_SEGMENTS * SUBLANES) == 0

    c_all = jnp.concatenate([c_sample, c_prompt], axis=0)
    pad_rows = -c_all.shape[0] % SUBLANES
    mods = _modulation(jnp.pad(c_all, ((0, pad_rows), (0, 0))), w_mod, b_mod)
    mod_p = _Mod(mods[:, bs:bs + bp].reshape(depth, bp, 1, N_MOD * d), None)
    mod_s = _Mod(mods, bs)

    a_re, a_im, bre_blk, bim_blk, cre_blk, cim_blk, decay_re, decay_im, span_re, span_im = _s5_discretise(
        ssm_lam_re, ssm_lam_im, ssm_log_dt, ssm_b_re, ssm_b_im, ssm_c_re, ssm_c_im,
        min(S5_STATE_CHUNK, seq // PROMPT_SEGMENTS))
    rope_p = _rope_tables(jnp.arange(seq))
    rope_s = _rope_tables(jnp.repeat(PAST_LEN + jnp.arange(n_new), bs))

    cast = lambda a: a.astype(MXU_DTYPE)
    vec = lambda a: a.reshape(depth, 1, a.shape[-1])
    p = {
        'norm1_g': vec(norm1_g), 'norm2_g': vec(norm2_g), 'w_in': cast(w_in), 'sinks': attn_sinks,
        'a_re': a_re, 'a_im': a_im,
        'bre_blk': bre_blk, 'bim_blk': bim_blk,
        'cre_blk': cre_blk, 'cim_blk': cim_blk,
        'decay_re': decay_re, 'decay_im': decay_im, 'span_re': span_re, 'span_im': span_im,
        'ssm_d': vec(ssm_d), 'ssm_w_glu': cast(ssm_w_glu), 'ssm_b_glu': vec(ssm_b_glu),
        'conv_w': conv_w, 'conv_w_tiles': jnp.repeat(conv_w[:, :, None, :], SUBLANES, axis=2),
        'conv_b': vec(conv_b), 'conv_ln_g': vec(conv_ln_g), 'conv_ln_b': vec(conv_ln_b),
        'w_out': w_out, 'w_gate': w_gate, 'w_up': w_up, 'w_down': w_down,
        'final_g': final_norm_g.reshape(1, d),
    }
    conv_all = jnp.transpose(state_conv, (0, 2, 1, 3))
    kt_all = jnp.transpose(cache_k, (0, 1, 3, 4, 2))
    vt_all = jnp.transpose(cache_v, (0, 1, 3, 4, 2))

    xp = x_prompt.reshape(bp * seq, d)
    xs = jnp.transpose(x_sample, (1, 0, 2)).reshape(n_new * bs, d)
    outs_p, outs_s = [], []
    for l in range(depth):
        final = l == depth - 1
        xs, ffn_weights, *os_ = _layer_sample(xs, mod_s, p, l, rope_s, bs, n_new, kt_all, vt_all,
                                              state_ssm_re[l], state_ssm_im[l], conv_all, final)
        xp, *op = _layer_prompt(xp, mod_p, p, ffn_weights, l, rope_p, bp, seq, final)
        outs_p.append(op)
        outs_s.append(os_)
    stack = lambda outs, i: jnp.stack([o[i] for o in outs])
    y_sample = jnp.transpose(xs.reshape(n_new, bs, d), (1, 0, 2))
    new_k_s = jnp.transpose(stack(outs_s, 0), (0, 1, 4, 2, 3))
    new_v_s = jnp.transpose(stack(outs_s, 1), (0, 1, 4, 2, 3))
    new_conv_s = jnp.transpose(stack(outs_s, 4), (0, 2, 1, 3))
    return (xp.reshape(bp, seq, d), y_sample, *[stack(outs_p, i) for i in range(5)],
            new_k_s, new_v_s, stack(outs_s, 2), stack(outs_s, 3), new_conv_s)
```

```python
import functools
import math
from typing import NamedTuple

import jax
import jax.numpy as jnp
from jax import lax
from jax.experimental import pallas as pl
from jax.experimental.pallas import tpu as pltpu

F32 = jnp.float32
MXU_DTYPE = jnp.bfloat16

V7X_VMEM_BYTES = 64 * 1024 * 1024
VMEM_LIMIT_BYTES = V7X_VMEM_BYTES - 8 * 1024 * 1024
LANES = 128
SUBLANES = 8
V7X_MXU_DIM = 256

D_MODEL = 1024
HEAD_DIM = 64
N_Q_HEADS = 8
N_KV_HEADS = 2
Q_PER_KV = N_Q_HEADS // N_KV_HEADS
ATTN_WIDTH = N_Q_HEADS * HEAD_DIM
KV_WIDTH = N_KV_HEADS * HEAD_DIM
WINDOW = 128
ROPE_THETA = 10000.0
ATTN_SCALE = 1.0 / math.sqrt(HEAD_DIM)
SSM_WIDTH = 256
SSM_GROUP = 16
N_SSM_GROUPS = 16
SSM_STATE = 64
SSM_FLAT = N_SSM_GROUPS * SSM_STATE
CONV_WIDTH = 256
CONV_K = 31
CONV_HALO = 32
CONV_ROWS_PER_STAGE = 128
IN_WIDTH = ATTN_WIDTH + 2 * KV_WIDTH + SSM_WIDTH + 2 * CONV_WIDTH
D_FF = 2816
FF_CHUNK = 6 * V7X_MXU_DIM
FF_STREAM_CHUNK = V7X_MXU_DIM
EPS = 1e-6
NEG = -1e30
N_MOD = 6
PROMPT_SEGMENTS = 2
S5_STATE_CHUNK = 1024
PAST_LEN = 8192


def _params(*semantics):
    return pltpu.CompilerParams(dimension_semantics=semantics, vmem_limit_bytes=VMEM_LIMIT_BYTES)


def _layer_block(shape, layer, **kw):
    zeros = (0,) * len(shape)
    return pl.BlockSpec((None,) + tuple(shape), lambda *_: (layer,) + zeros, **kw)


def _sigmoid(x):
    return 1.0 / (1.0 + jnp.exp(-x))


def _silu(x):
    return x * _sigmoid(x)


def _rms(x, g):
    return x * lax.rsqrt(jnp.mean(x * x, axis=-1, keepdims=True) + EPS) * g


def _mm(a, b):
    return jnp.dot(a, b, preferred_element_type=F32)


def _mm_nt(a, b):
    return lax.dot_general(a, b, (((1,), (1,)), ((), ())), preferred_element_type=F32)


def _rows(m, n):
    return m if m.shape[0] == 1 else jnp.concatenate([m] * (n // m.shape[0]), axis=0)


def _mod_kernel(c_ref, w_ref, b_ref, o_ref):
    a = _silu(c_ref[...]).astype(MXU_DTYPE)
    o_ref[...] = _mm(a, w_ref[...].astype(MXU_DTYPE)) + b_ref[...]


def _modulation(c, w_mod, b_mod):
    depth, d, n = w_mod.shape
    rows = c.shape[0]
    tn = 1536
    return pl.pallas_call(
        _mod_kernel,
        grid=(depth, n // tn),
        in_specs=[
            pl.BlockSpec((rows, d), lambda l, j: (0, 0)),
            pl.BlockSpec((None, d, tn), lambda l, j: (l, 0, j)),
            pl.BlockSpec((None, 1, tn), lambda l, j: (l, 0, j)),
        ],
        out_specs=pl.BlockSpec((None, rows, tn), lambda l, j: (l, 0, j)),
        out_shape=jax.ShapeDtypeStruct((depth, rows, n), F32),
        compiler_params=_params("parallel", "parallel"),
        name="modulation",
    )(c, w_mod, b_mod.reshape(depth, 1, n))


class _Mod(NamedTuple):
    array: jax.Array
    batch_rows: int | None


def _mod_spec(mod, layer, chunk, tiles_per_batch, tile=lambda i: i):
    if mod.batch_rows is None:
        return pl.BlockSpec((None, None, 1, D_MODEL), lambda i: (layer, tile(i) // tiles_per_batch, 0, chunk))
    return pl.BlockSpec((None, mod.batch_rows, D_MODEL), lambda i: (layer, 0, chunk))


def _inproj_kernel(x_ref, sh_ref, sc_ref, g_ref, w_ref, cos_ref, sin_ref, *outs, n_new):
    tm = x_ref.shape[0]
    h = _rms(x_ref[...], g_ref[...]) * (1.0 + _rows(sc_ref[...], tm)) + _rows(sh_ref[...], tm)
    w = w_ref[...].astype(MXU_DTYPE)
    z = _mm(h.astype(MXU_DTYPE), w)
    cos = cos_ref[...]
    sin = sin_ref[...]
    lane = lax.broadcasted_iota(jnp.int32, (tm, LANES), 1)
    first_half = (lane % HEAD_DIM) < (HEAD_DIM // 2)

    def rope(t):
        partner = jnp.where(first_half, pltpu.roll(t, LANES - HEAD_DIM // 2, 1),
                            pltpu.roll(t, HEAD_DIM // 2, 1))
        return t * cos + partner * sin

    q_cols = [rope(z[:, j * LANES:(j + 1) * LANES]) * ATTN_SCALE for j in range(ATTN_WIDTH // LANES)]
    o = ATTN_WIDTH
    k_rot = rope(z[:, o:o + KV_WIDTH])
    v_new = z[:, o + KV_WIDTH:o + 2 * KV_WIDTH]
    if n_new is None:
        q_ref, kv_ref, u_ref, vc_ref = outs
        for j, qc in enumerate(q_cols):
            q_ref[:, j * LANES:(j + 1) * LANES] = qc.astype(q_ref.dtype)
        kv_ref[:, 0:KV_WIDTH] = k_rot
        kv_ref[:, KV_WIDTH:2 * KV_WIDTH] = v_new
    else:
        q_ref, kn_ref, vn_ref, u_ref, vc_ref, w_out = outs
        w_out[...] = w
        batch = tm // n_new
        low = lax.broadcasted_iota(jnp.int32, (batch, LANES), 1) < HEAD_DIM
        kn_ref[...] = jnp.zeros(kn_ref.shape, F32)
        vn_ref[...] = jnp.zeros(vn_ref.shape, F32)
        for t in range(n_new):
            rows = slice(t * batch, (t + 1) * batch)
            tile_row = pl.ds(SUBLANES - n_new + t, batch, stride=SUBLANES)
            kn_ref[tile_row, :] = k_rot[rows]
            vn_ref[tile_row, :] = v_new[rows]
            for h in range(N_Q_HEADS):
                kvh = h // Q_PER_KV
                piece = q_cols[h // 2][rows]
                if h % 2 != kvh:
                    piece = pltpu.roll(piece, HEAD_DIM, 1)
                piece = jnp.where(low if kvh == 0 else ~low, piece, 0.0)
                q_ref[pl.ds(h * n_new + t, batch, stride=N_Q_HEADS * n_new), :] = piece
    o += 2 * KV_WIDTH
    u_ref[...] = z[:, o:o + SSM_WIDTH]
    o += SSM_WIDTH
    za = z[:, o:o + CONV_WIDTH]
    zg = z[:, o + CONV_WIDTH:o + 2 * CONV_WIDTH]
    vc_ref[...] = za * _sigmoid(zg)


def _in_projection(x, mod, p, w_in, layer, cos, sin, tm, tiles_per_batch, n_new=None):
    t = x.shape[0]
    pos_tiles = cos.shape[0] // tm
    row = lambda i: (i, 0)
    whole = lambda *shape: pl.BlockSpec(shape, lambda i: (0,) * len(shape))
    if n_new is None:
        w_spec = whole(D_MODEL, IN_WIDTH)
        attn_specs = [pl.BlockSpec((tm, ATTN_WIDTH), row), pl.BlockSpec((tm, 2 * KV_WIDTH), row)]
        attn_shapes = [jax.ShapeDtypeStruct((t, ATTN_WIDTH), MXU_DTYPE),
                       jax.ShapeDtypeStruct((t, 2 * KV_WIDTH), F32)]
        w_out_spec, w_out_shape = [], []
    else:
        assert tm == t and KV_WIDTH == LANES
        w_spec = _layer_block((D_MODEL, IN_WIDTH), layer)
        batch = t // n_new
        shapes = [(batch * N_Q_HEADS * n_new, LANES), (batch * SUBLANES, LANES), (batch * SUBLANES, LANES)]
        attn_specs = [whole(*s) for s in shapes]
        attn_shapes = [jax.ShapeDtypeStruct(s, F32) for s in shapes]
        w_out_spec = [whole(D_MODEL, IN_WIDTH)]
        w_out_shape = [jax.ShapeDtypeStruct((D_MODEL, IN_WIDTH), MXU_DTYPE)]
    return pl.pallas_call(
        functools.partial(_inproj_kernel, n_new=n_new),
        grid=(t // tm,),
        in_specs=[
            pl.BlockSpec((tm, D_MODEL), row),
            _mod_spec(mod, layer, 0, tiles_per_batch),
            _mod_spec(mod, layer, 1, tiles_per_batch),
            _layer_block((1, D_MODEL), layer),
            w_spec,
            pl.BlockSpec((tm, LANES), lambda i: (i % pos_tiles, 0)),
            pl.BlockSpec((tm, LANES), lambda i: (i % pos_tiles, 0)),
        ],
        out_specs=(attn_specs + [pl.BlockSpec((tm, SSM_WIDTH), row), pl.BlockSpec((tm, CONV_WIDTH), row)]
                   + w_out_spec),
        out_shape=(attn_shapes + [jax.ShapeDtypeStruct((t, SSM_WIDTH), F32),
                                  jax.ShapeDtypeStruct((t, CONV_WIDTH), F32)] + w_out_shape),
        compiler_params=_params("parallel"),
        name="in_projection",
    )(x, mod.array, mod.array, p['norm1_g'], w_in, cos, sin)


def _rope_tables(pos):
    half = HEAD_DIM // 2
    inv_freq = ROPE_THETA ** (-jnp.arange(half, dtype=F32) / half)
    ang = pos.astype(F32)[:, None] * inv_freq[None, :]
    cos = jnp.tile(jnp.cos(ang), (1, LANES // half))
    sin = jnp.sin(ang)
    sin = jnp.tile(jnp.concatenate([-sin, sin], axis=1), (1, LANES // HEAD_DIM))
    return cos, sin


def _sink_softmax(s, sink_col):
    m = jnp.maximum(jnp.max(s, axis=-1, keepdims=True), sink_col)
    e = jnp.exp(s - m)
    return e * (1.0 / (jnp.sum(e, axis=-1, keepdims=True) + jnp.exp(sink_col - m)))


def _head_pair_select(x, pick_second):
    lane = lax.broadcasted_iota(jnp.int32, x.shape, 1)
    swapped = pltpu.roll(x, HEAD_DIM, 1)
    low = lane < HEAD_DIM
    return jnp.where(low, swapped, x) if pick_second else jnp.where(low, x, swapped)


def _attn_prompt_kernel(sink_ref, q_ref, kvp_ref, kvc_ref, o_ref, *, layer):
    n = pl.program_id(1)
    w = WINDOW
    n_blocks = q_ref.shape[0] // w
    rows = Q_PER_KV * w
    r_idx = lax.broadcasted_iota(jnp.int32, (rows, w), 0) % w
    c_idx = lax.broadcasted_iota(jnp.int32, (rows, w), 1)
    from_prev = c_idx > r_idx
    first_bias = jnp.where(n > 0, 0.0, NEG)
    row_head = lax.broadcasted_iota(jnp.int32, (rows, 1), 0) // w
    low = lax.broadcasted_iota(jnp.int32, (w, LANES), 1) < HEAD_DIM
    half_mask = [jnp.where(low, 1.0, 0.0).astype(MXU_DTYPE), jnp.where(low, 0.0, 1.0).astype(MXU_DTYPE)]
    sink_cols = []
    for kvh in range(N_KV_HEADS):
        sink_col = jnp.zeros((rows, 1), F32)
        for g in range(Q_PER_KV):
            sink_col = jnp.where(row_head == g, sink_ref[layer, kvh * Q_PER_KV + g], sink_col)
        sink_cols.append(sink_col)
    chains = [(i, kvh) for i in range(n_blocks) for kvh in range(N_KV_HEADS)]
    windows, values = [], []
    for i, kvh in chains:
        own = slice(i * w, (i + 1) * w)
        prev_ref, prev = (kvp_ref, slice(0, w)) if i == 0 else (kvc_ref, slice((i - 1) * w, i * w))
        kk = jnp.concatenate([prev_ref[prev, 0:KV_WIDTH], kvc_ref[own, 0:KV_WIDTH]], axis=0)
        vv = jnp.concatenate([prev_ref[prev, KV_WIDTH:2 * KV_WIDTH], kvc_ref[own, KV_WIDTH:2 * KV_WIDTH]], axis=0)
        k2 = _head_pair_select(kk, kvh == 1).astype(MXU_DTYPE)
        values.append(_head_pair_select(vv, kvh == 1).astype(MXU_DTYPE))
        pieces = []
        for g in range(Q_PER_KV):
            h = kvh * Q_PER_KV + g
            qcol = q_ref[own, (h // 2) * LANES:(h // 2 + 1) * LANES]
            pieces.append(qcol * half_mask[h % 2])
        s = _mm_nt(jnp.concatenate(pieces, axis=0), k2)
        s_prev = s[:, 0:w] + first_bias if i == 0 else s[:, 0:w]
        windows.append(jnp.where(from_prev, s_prev, s[:, w:2 * w]))
    probs = []
    for (i, kvh), sc in zip(chains, windows):
        p = _sink_softmax(sc, sink_cols[kvh])
        p2 = jnp.concatenate([jnp.where(from_prev, p, 0.0), jnp.where(from_prev, 0.0, p)], axis=1)
        probs.append(p2.astype(MXU_DTYPE))
    for (i, kvh), p2, v2 in zip(chains, probs, values):
        own = slice(i * w, (i + 1) * w)
        r = _mm(p2, v2)
        for j in range(Q_PER_KV // 2):
            col = kvh * (Q_PER_KV // 2) + j
            o_ref[own, col * LANES:(col + 1) * LANES] = jnp.where(
                low, r[2 * j * w:(2 * j + 1) * w], r[(2 * j + 1) * w:(2 * j + 2) * w]).astype(o_ref.dtype)


def _attention_prompt(q, kv, sinks, layer, batch, seq):
    tq = min(8 * WINDOW, seq)
    nt = seq // tq
    per = tq // WINDOW
    return pl.pallas_call(
        functools.partial(_attn_prompt_kernel, layer=layer),
        grid=(batch, nt),
        in_specs=[
            pl.BlockSpec(memory_space=pltpu.SMEM),
            pl.BlockSpec((tq, ATTN_WIDTH), lambda b, n: (b * nt + n, 0)),
            pl.BlockSpec((WINDOW, 2 * KV_WIDTH), lambda b, n: ((b * nt + n) * per - jnp.minimum(n, 1), 0)),
            pl.BlockSpec((tq, 2 * KV_WIDTH), lambda b, n: (b * nt + n, 0)),
        ],
        out_specs=pl.BlockSpec((tq, ATTN_WIDTH), lambda b, n: (b * nt + n, 0)),
        out_shape=jax.ShapeDtypeStruct((batch * seq, ATTN_WIDTH), MXU_DTYPE),
        compiler_params=_params("parallel", "parallel"),
        name="attention_prompt",
    )(sinks, q, kv, kv)


def _attn_sample_kernel(sink_ref, q_ref, kt_ref, vt_ref, kn_ref, vn_ref, o_ref, nk_ref, nv_ref, *, layer, n_new):
    bt, rows, _ = q_ref.shape
    nkv, hd, wb = kt_ref.shape[1:]
    kept = wb - n_new
    r = lax.broadcasted_iota(jnp.int32, (bt * rows, 2 * wb), 0)
    j = lax.broadcasted_iota(jnp.int32, (bt * rows, 2 * wb), 1)
    t_idx = r % n_new
    t_new = j - wb - kept
    mask = ((j < wb) & (j > t_idx)) | ((t_new >= 0) & (t_new <= t_idx))
    row_head = (lax.broadcasted_iota(jnp.int32, (bt * rows, 1), 0) % rows) // n_new
    sink_col = jnp.zeros((bt * rows, 1), F32)
    for h in range(N_Q_HEADS):
        sink_col = jnp.where(row_head == h, sink_ref[layer, h], sink_col)
    keep = lax.broadcasted_iota(jnp.int32, (nkv, hd, wb), 2) < kept
    second_group = lax.broadcasted_iota(jnp.int32, (rows, nkv * hd), 0) >= (rows // nkv)
    flat = lambda a: a.reshape(nkv * hd, wb)
    above = jnp.zeros((wb - kn_ref.shape[1], nkv * hd), F32)

    scores, values = [], []
    for b in range(bt):
        kt, vt = kt_ref[b], vt_ref[b]
        kpos = jnp.concatenate([above, kn_ref[b]], axis=0).T
        vpos = jnp.concatenate([above, vn_ref[b]], axis=0).T
        nk_ref[b] = jnp.where(keep, pltpu.roll(kt, kept, 2), kpos.reshape(nkv, hd, wb))
        nv_ref[b] = jnp.where(keep, pltpu.roll(vt, kept, 2), vpos.reshape(nkv, hd, wb))
        kcat = jnp.concatenate([flat(kt), kpos], axis=1).astype(MXU_DTYPE)
        values.append(jnp.concatenate([flat(vt), vpos], axis=1).astype(MXU_DTYPE))
        scores.append(_mm(q_ref[b].astype(MXU_DTYPE), kcat))
    s = jnp.where(mask, jnp.concatenate(scores, axis=0), NEG)
    p = _sink_softmax(s, sink_col).astype(MXU_DTYPE)
    for b in range(bt):
        o = _mm_nt(p[b * rows:(b + 1) * rows], values[b])
        o_ref[b] = jnp.where(second_group, pltpu.roll(o, hd, 1), o).astype(o_ref.dtype)


def _attention_sample(qexp, kt_all, vt_all, kn, vn, sinks, layer, n_new):
    batch, rows, width = qexp.shape
    _, _, nkv, hd, wb = kt_all.shape
    bt = min(32, batch)
    blk3 = pl.BlockSpec((bt, rows, width), lambda i: (i, 0, 0))
    blk4 = pl.BlockSpec((bt, nkv, hd, wb), lambda i: (i, 0, 0, 0))
    new = pl.BlockSpec((bt,) + kn.shape[1:], lambda i: (i, 0, 0))
    cache = pl.BlockSpec((None, bt, nkv, hd, wb), lambda i: (layer, i, 0, 0, 0))
    return pl.pallas_call(
        functools.partial(_attn_sample_kernel, layer=layer, n_new=n_new),
        grid=(batch // bt,),
        in_specs=[pl.BlockSpec(memory_space=pltpu.SMEM), blk3, cache, cache, new, new],
        out_specs=[blk3, blk4, blk4],
        out_shape=[jax.ShapeDtypeStruct((batch, rows, width), MXU_DTYPE),
                   jax.ShapeDtypeStruct((batch, nkv, hd, wb), F32),
                   jax.ShapeDtypeStruct((batch, nkv, hd, wb), F32)],
        compiler_params=_params("parallel"),
        name="attention_sample",
    )(sinks, qexp, kt_all, vt_all, kn, vn)


def _cmul(ar, ai, br, bi):
    return ar * br - ai * bi, ar * bi + ai * br


def _s5_discretise_kernel(lr_ref, li_ref, ldt_ref, bre_ref, bim_ref, cre_ref, cim_ref,
                          are_ref, aim_ref, bbre_ref, bbim_ref, ccre_ref, ccim_ref,
                          decay_re_ref, decay_im_ref, span_re_ref, span_im_ref):
    lr = lr_ref[...]
    li = li_ref[...]
    dt = jnp.exp(ldt_ref[...])
    mag = jnp.exp(lr * dt)
    ab_re = mag * jnp.cos(li * dt)
    ab_im = mag * jnp.sin(li * dt)
    den = lr * lr + li * li
    nr = ab_re - 1.0
    coef_re = (nr * lr + ab_im * li) / den
    coef_im = (ab_im * lr - nr * li) / den
    are_ref[...] = ab_re
    aim_ref[...] = ab_im
    br = bre_ref[...]
    bi = bim_ref[...]
    bb_re = coef_re * br - coef_im * bi
    bb_im = coef_re * bi + coef_im * br
    lane_group = lax.broadcasted_iota(jnp.int32, bb_re.shape, 1) // SSM_STATE
    for src, dst in ((bb_re, bbre_ref), (bb_im, bbim_ref), (cre_ref[...], ccre_ref), (cim_ref[...], ccim_ref)):
        for g in range(N_SSM_GROUPS):
            dst[g * SSM_GROUP:(g + 1) * SSM_GROUP, :] = jnp.where(lane_group == g, src, 0.0).astype(dst.dtype)
    t_chunk = decay_re_ref.shape[0]
    row = lax.broadcasted_iota(jnp.int32, (SUBLANES, SSM_FLAT), 0)
    pr, pi = jnp.ones_like(ab_re), jnp.zeros_like(ab_re)
    tile_re, tile_im = jnp.zeros((SUBLANES, SSM_FLAT), F32), jnp.zeros((SUBLANES, SSM_FLAT), F32)
    for k in range(SUBLANES):
        tile_re = jnp.where(row == SUBLANES - 1 - k, pr, tile_re)
        tile_im = jnp.where(row == SUBLANES - 1 - k, pi, tile_im)
        pr, pi = _cmul(pr, pi, ab_re, ab_im)
    decay_re_ref[t_chunk - SUBLANES:t_chunk, :] = tile_re
    decay_im_ref[t_chunk - SUBLANES:t_chunk, :] = tile_im
    n = SUBLANES
    while n < t_chunk:
        lo, hi = slice(t_chunk - 2 * n, t_chunk - n), slice(t_chunk - n, t_chunk)
        decay_re_ref[lo, :], decay_im_ref[lo, :] = _cmul(decay_re_ref[hi, :], decay_im_ref[hi, :], pr, pi)
        pr, pi = _cmul(pr, pi, pr, pi)
        n *= 2
    span_re_ref[...] = pr
    span_im_ref[...] = pi


def _s5_discretise(lam_re, lam_im, log_dt, b_re, b_im, c_re, c_im, t_chunk):
    depth = lam_re.shape[0]
    assert t_chunk % SUBLANES == 0 and (t_chunk // SUBLANES) & (t_chunk // SUBLANES - 1) == 0
    flat = lambda a: a.reshape(depth, 1, SSM_FLAT)
    ldt = jnp.broadcast_to(log_dt[:, :, None], lam_re.shape)
    bt = lambda a: jnp.transpose(a, (0, 3, 1, 2)).reshape(depth, SSM_GROUP, SSM_FLAT)
    ct = lambda a: jnp.transpose(a, (0, 2, 1, 3)).reshape(depth, SSM_GROUP, SSM_FLAT)
    vec = pl.BlockSpec((None, 1, SSM_FLAT), lambda l: (l, 0, 0))
    mat = pl.BlockSpec((None, SSM_GROUP, SSM_FLAT), lambda l: (l, 0, 0))
    blk = pl.BlockSpec((None, SSM_WIDTH, SSM_FLAT), lambda l: (l, 0, 0))
    tab = pl.BlockSpec((None, t_chunk, SSM_FLAT), lambda l: (l, 0, 0))
    return pl.pallas_call(
        _s5_discretise_kernel,
        grid=(depth,),
        in_specs=[vec, vec, vec, mat, mat, mat, mat],
        out_specs=[vec, vec, blk, blk, blk, blk, tab, tab, vec, vec],
        out_shape=[jax.ShapeDtypeStruct((depth, 1, SSM_FLAT), F32)] * 2
        + [jax.ShapeDtypeStruct((depth, SSM_WIDTH, SSM_FLAT), MXU_DTYPE)] * 4
        + [jax.ShapeDtypeStruct((depth, t_chunk, SSM_FLAT), F32)] * 2
        + [jax.ShapeDtypeStruct((depth, 1, SSM_FLAT), F32)] * 2,
        compiler_params=_params("parallel"),
        name="s5_discretise",
    )(flat(lam_re), flat(lam_im), flat(ldt), bt(b_re), bt(b_im), ct(c_re), ct(c_im))


def _s5_first_segment_kernel(u_ref, bre_ref, bim_ref, dre_ref, dim_ref, sre_ref, sim_ref, ere_ref, eim_ref,
                             hre_s, him_s):
    @pl.when(pl.program_id(1) == 0)
    def _():
        hre_s[...] = jnp.zeros(hre_s.shape, F32)
        him_s[...] = jnp.zeros(him_s.shape, F32)

    ub = u_ref[...].astype(MXU_DTYPE)
    wr, wi = _cmul(dre_ref[...], dim_ref[...], _mm(ub, bre_ref[...]), _mm(ub, bim_ref[...]))
    cr, ci = _cmul(sre_ref[...], sim_ref[...], hre_s[...], him_s[...])
    hr = cr + jnp.sum(wr, axis=0, keepdims=True)
    hi = ci + jnp.sum(wi, axis=0, keepdims=True)
    hre_s[...] = hr
    him_s[...] = hi
    ere_ref[...] = hr
    eim_ref[...] = hi


def _s5_first_segment_states(u_seq, p, layer, nseg):
    nseq, lseg, _ = u_seq.shape
    batch = nseq // nseg
    t_chunk = p['decay_re'].shape[1]
    vec = _layer_block((1, SSM_FLAT), layer)
    tab = _layer_block((t_chunk, SSM_FLAT), layer)
    bmat = _layer_block((SSM_WIDTH, SSM_FLAT), layer)
    out = pl.BlockSpec((None, 1, SSM_FLAT), lambda b, c: (b, 0, 0))
    return pl.pallas_call(
        _s5_first_segment_kernel,
        grid=(batch, lseg // t_chunk),
        in_specs=[pl.BlockSpec((None, t_chunk, SSM_WIDTH), lambda b, c: (b * nseg, c, 0)),
                  bmat, bmat, tab, tab, vec, vec],
        out_specs=[out, out],
        out_shape=[jax.ShapeDtypeStruct((batch, 1, SSM_FLAT), F32)] * 2,
        scratch_shapes=[pltpu.VMEM((1, SSM_FLAT), F32)] * 2,
        compiler_params=_params("parallel", "arbitrary"),
        name="s5_first_segment_states",
    )(u_seq, p['bre_blk'], p['bim_blk'], p['decay_re'], p['decay_im'], p['span_re'], p['span_im'])


def _s5_kernel(u_ref, hre0_ref, him0_ref, are_ref, aim_ref, bre_ref, bim_ref, cre_ref, cim_ref, d_ref,
               wglu_ref, bglu_ref, o_ref, hre_out, him_out, hre_s, him_s, bure_s, buim_s, hsre_s, hsim_s,
               *stage, nseq):
    seq_major = bool(stage)
    steps = u_ref.shape[1] if seq_major else u_ref.shape[0] // nseq
    lane_halves = [slice(h * LANES, (h + 1) * LANES) for h in range(SSM_WIDTH // LANES)]

    @pl.when(pl.program_id(0) == 0)
    def _():
        hre_s[...] = hre0_ref[...] if hre0_ref.shape == hre_s.shape else hre0_ref[...].T
        him_s[...] = him0_ref[...] if him0_ref.shape == him_s.shape else him0_ref[...].T

    if seq_major:
        stage_s, = stage
        for s in range(nseq):
            for h, cols in enumerate(lane_halves):
                stage_s[h, pl.ds(s, steps, stride=nseq), :] = u_ref[s, :, cols]
        u = jnp.concatenate([stage_s[h] for h in range(len(lane_halves))], axis=1)
    else:
        u = u_ref[...]
    ub = u.astype(MXU_DTYPE)
    bure_s[...] = _mm(ub, bre_ref[...])
    buim_s[...] = _mm(ub, bim_ref[...])
    ar = jnp.broadcast_to(are_ref[...], (nseq, SSM_FLAT))
    ai = jnp.broadcast_to(aim_ref[...], (nseq, SSM_FLAT))

    def step(t, carry):
        hr, hi = carry
        rows = pl.ds(pl.multiple_of(t * nseq, nseq), nseq)
        nhr = ar * hr - ai * hi + bure_s[rows, :]
        nhi = ar * hi + ai * hr + buim_s[rows, :]
        hsre_s[rows, :] = nhr
        hsim_s[rows, :] = nhi
        return nhr, nhi

    hr, hi = lax.fori_loop(0, steps, step, (hre_s[...], him_s[...]), unroll=min(steps, 8))
    hre_s[...] = hr
    him_s[...] = hi
    hre_out[...] = hr if hre_out.shape == hr.shape else hr.T
    him_out[...] = hi if him_out.shape == hi.shape else hi.T
    y = (_mm_nt(hsre_s[...].astype(MXU_DTYPE), cre_ref[...])
         - _mm_nt(hsim_s[...].astype(MXU_DTYPE), cim_ref[...]))
    z = jax.nn.gelu(y + d_ref[...] * u)
    gate = _mm(z.astype(MXU_DTYPE), wglu_ref[...]) + bglu_ref[...]
    out = z * _sigmoid(gate)
    if seq_major:
        for h, cols in enumerate(lane_halves):
            stage_s[h] = out[:, cols]
        for s in range(nseq):
            for h, cols in enumerate(lane_halves):
                o_ref[s, :, cols] = stage_s[h, pl.ds(s, steps, stride=nseq), :].astype(o_ref.dtype)
    else:
        o_ref[...] = out.astype(o_ref.dtype)


def _s5_scan(u, hre0, him0, p, layer, nseq, steps_per_tile):
    seq_major = u.ndim == 3
    rows = u.shape[0] * u.shape[1] if seq_major else u.shape[0]
    tr = steps_per_tile * nseq
    if hre0.ndim == 3:
        state_in = _layer_block((SSM_FLAT, nseq), layer)
        state = pl.BlockSpec((SSM_FLAT, nseq), lambda i: (0, 0))
    else:
        state_in = state = pl.BlockSpec((nseq, SSM_FLAT), lambda i: (0, 0))
    if seq_major:
        io_spec = pl.BlockSpec((nseq, steps_per_tile, SSM_WIDTH), lambda i: (0, i, 0))
    else:
        io_spec = pl.BlockSpec((tr, SSM_WIDTH), lambda i: (i, 0))
    state_vec = _layer_block((1, SSM_FLAT), layer)
    group_map = _layer_block((SSM_WIDTH, SSM_FLAT), layer)
    width_vec = _layer_block((1, SSM_WIDTH), layer)
    scratch = ([pltpu.VMEM((nseq, SSM_FLAT), F32)] * 2 + [pltpu.VMEM((tr, SSM_FLAT), F32)] * 4
               + [pltpu.VMEM((SSM_WIDTH // LANES, tr, LANES), F32)] * seq_major)
    return pl.pallas_call(
        functools.partial(_s5_kernel, nseq=nseq),
        grid=(rows // tr,),
        in_specs=[io_spec, state_in, state_in, state_vec, state_vec, group_map, group_map, group_map, group_map,
                  width_vec, _layer_block((SSM_WIDTH, SSM_WIDTH), layer), width_vec],
        out_specs=[io_spec, state, state],
        out_shape=[jax.ShapeDtypeStruct(u.shape, MXU_DTYPE)] + [jax.ShapeDtypeStruct(state.block_shape, F32)] * 2,
        scratch_shapes=scratch,
        compiler_params=_params("arbitrary"),
        name="s5_scan",
    )(u, hre0, him0, p['a_re'], p['a_im'], p['bre_blk'], p['bim_blk'], p['cre_blk'], p['cim_blk'],
      p['ssm_d'], p['ssm_w_glu'], p['ssm_b_glu'])


def _layernorm_silu(y, g, b):
    yc = y - jnp.mean(y, axis=-1, keepdims=True)
    var = jnp.mean(yc * yc, axis=-1, keepdims=True)
    return _silu(yc * lax.rsqrt(var + EPS) * g + b)


def _conv_sample_kernel(state_ref, v_ref, w_ref, b_ref, g_ref, beta_ref, o_ref, ns_ref):
    ns = state_ref.shape[0]
    n_new = v_ref.shape[0]
    row = lambda j: state_ref[j] if j < ns else v_ref[j - ns]
    for t in range(n_new):
        acc = jnp.zeros(o_ref.shape[1:], F32)
        for k in range(CONV_K):
            acc = acc + w_ref[k:k + 1, :] * row(t + k)
        o_ref[t] = _layernorm_silu(acc + b_ref[...], g_ref[...], beta_ref[...]).astype(o_ref.dtype)
    for j in range(ns):
        ns_ref[j] = row(j + n_new)


def _conv_sample(state_all, v, p, layer):
    _, ns, batch, width = state_all.shape
    n_new = v.shape[0]
    bt = min(32, batch)
    vec = _layer_block((1, width), layer)
    return pl.pallas_call(
        _conv_sample_kernel,
        grid=(batch // bt,),
        in_specs=[
            pl.BlockSpec((None, ns, bt, width), lambda i: (layer, 0, i, 0)),
            pl.BlockSpec((n_new, bt, width), lambda i: (0, i, 0)),
            _layer_block((CONV_K, width), layer), vec, vec, vec,
        ],
        out_specs=[pl.BlockSpec((n_new, bt, width), lambda i: (0, i, 0)),
                   pl.BlockSpec((ns, bt, width), lambda i: (0, i, 0))],
        out_shape=[jax.ShapeDtypeStruct((n_new, batch, width), MXU_DTYPE),
                   jax.ShapeDtypeStruct((ns, batch, width), F32)],
        compiler_params=_params("parallel"),
        name="conv_sample",
    )(state_all, v, p['conv_w'], p['conv_b'], p['conv_ln_g'], p['conv_ln_b'])


def _outffn_stages(x_ref, a_ref, s_ref, load_conv, g1_ref, sh2_ref, sc2_ref, g2_ref, n2_ref,
                   wo_ref, wg_ref, wu_ref, wd_ref, fg_ref, o_ref, final_norm):
    tm = x_ref.shape[0]
    o1 = ATTN_WIDTH
    o2 = o1 + SSM_WIDTH
    st = {}

    def proj_attn_ssm():
        st['proj'] = _mm(a_ref[...], wo_ref[0:o1, :]) + _mm(s_ref[...], wo_ref[o1:o2, :])

    def proj_conv_residual_norm():
        proj = st.pop('proj') + _mm(load_conv(), wo_ref[o2:o2 + CONV_WIDTH, :])
        x2 = x_ref[...] + _rows(g1_ref[...], tm) * proj
        h2 = _rms(x2, n2_ref[...]) * (1.0 + _rows(sc2_ref[...], tm)) + _rows(sh2_ref[...], tm)
        st['x2'] = x2
        st['h2'] = h2.astype(MXU_DTYPE)

    def gate(cs):
        st['gate'] = _mm(st['h2'], wg_ref[:, cs])

    def up(cs):
        st['act'] = (_silu(st.pop('gate')) * _mm(st['h2'], wu_ref[:, cs])).astype(MXU_DTYPE)

    def down(cs, last):
        ffn = _mm(st.pop('act'), wd_ref[cs, :])
        st['ffn'] = ffn if 'ffn' not in st else st['ffn'] + ffn
        if last:
            y = st['x2'] + _rows(g2_ref[...], tm) * st['ffn']
            o_ref[...] = _rms(y, fg_ref[...]) if final_norm else y

    stages = [proj_attn_ssm, proj_conv_residual_norm]
    for start in range(0, D_FF, FF_CHUNK):
        cs = slice(start, min(start + FF_CHUNK, D_FF))
        stages += [functools.partial(gate, cs), functools.partial(up, cs),
                   functools.partial(down, cs, cs.stop == D_FF)]
    return stages


def _outffn_stream_kernel(x_ref, a_ref, s_ref, c_ref, g1_ref, sh2_ref, sc2_ref, g2_ref, n2_ref,
                          wo_ref, wg_ref, wu_ref, wd_ref, fg_ref,
                          o_ref, wo_out, wg_out, wu_out, wd_out, x2_s, h2_s, acc_s, *, final_norm):
    j = pl.program_id(0)
    tm = x_ref.shape[0]

    @pl.when(j == 0)
    def _():
        o1 = ATTN_WIDTH
        o2 = o1 + SSM_WIDTH
        wo = wo_ref[...].astype(MXU_DTYPE)
        wo_out[...] = wo
        proj = _mm(a_ref[...], wo[0:o1]) + _mm(s_ref[...], wo[o1:o2]) + _mm(c_ref[...], wo[o2:o2 + CONV_WIDTH])
        x2 = x_ref[...] + _rows(g1_ref[...], tm) * proj
        h2 = _rms(x2, n2_ref[...]) * (1.0 + _rows(sc2_ref[...], tm)) + _rows(sh2_ref[...], tm)
        x2_s[...] = x2
        h2_s[...] = h2.astype(MXU_DTYPE)
        acc_s[...] = jnp.zeros(acc_s.shape, F32)

    wg = wg_ref[...].astype(MXU_DTYPE)
    wu = wu_ref[...].astype(MXU_DTYPE)
    wd = wd_ref[...].astype(MXU_DTYPE)
    wg_out[...] = wg
    wu_out[...] = wu
    wd_out[...] = wd
    h2 = h2_s[...]
    act = (_silu(_mm(h2, wg)) * _mm(h2, wu)).astype(MXU_DTYPE)
    acc_s[...] += _mm(act, wd)

    @pl.when(j == pl.num_programs(0) - 1)
    def _():
        y = x2_s[...] + _rows(g2_ref[...], tm) * acc_s[...]
        o_ref[...] = _rms(y, fg_ref[...]) if final_norm else y


def _conv_stages(halo_ref, cur_ref, w_ref, b_ref, g_ref, beta_ref, store, buf, shifted, first, sub):
    tc = cur_ref.shape[0]
    base = CONV_HALO - (CONV_K - 1)
    span = shifted.shape[1]

    def fill():
        buf[0:CONV_HALO, :] = jnp.where(first, 0.0, halo_ref[...])
        buf[CONV_HALO:CONV_HALO + tc, :] = cur_ref[...]
        for s in range(1, SUBLANES):
            shifted[s - 1] = buf[s:s + span, :]

    def rows(r):
        acc = jnp.zeros((sub // SUBLANES, SUBLANES, CONV_WIDTH), F32)
        for k in range(CONV_K):
            j, s = (base + k) // SUBLANES, (base + k) % SUBLANES
            start = r * sub + j * SUBLANES
            tap = buf[start:start + sub, :] if s == 0 else shifted[s - 1, start:start + sub, :]
            acc = acc + w_ref[k][None] * tap.reshape(acc.shape)
        y = _layernorm_silu(acc.reshape(sub, CONV_WIDTH) + b_ref[...], g_ref[...], beta_ref[...])
        store(slice(r * sub, (r + 1) * sub), y)

    return [fill] + [functools.partial(rows, r) for r in range(tc // sub)]


def _outffn_conv_kernel(x_ref, a_ref, s_ref, halo_ref, vcur_ref, g1_ref, sh2_ref, sc2_ref, g2_ref, n2_ref,
                        wo_ref, wg_ref, wu_ref, wd_ref, fg_ref, cw_ref, cb_ref, cg_ref, cbeta_ref,
                        o_ref, buf, shifted, conv_out, *, final_norm, n_tiles, tiles_per_batch, sub):
    j = pl.program_id(0)
    conv_tile = jnp.minimum(j, n_tiles - 1)
    first = (conv_tile % tiles_per_batch) == 0

    def conv_into(slot):
        def store(rows, y):
            conv_out[slot, rows, :] = y.astype(conv_out.dtype)
        return _conv_stages(halo_ref, vcur_ref, cw_ref, cb_ref, cg_ref, cbeta_ref, store, buf, shifted, first, sub)

    @pl.when(j == 0)
    def _():
        for stage in conv_into(0):
            stage()

    @pl.when(j > 0)
    def _():
        slot = j % 2
        conv = conv_into(slot)
        ffn = _outffn_stages(x_ref, a_ref, s_ref, lambda: conv_out[1 - slot], g1_ref, sh2_ref, sc2_ref, g2_ref,
                             n2_ref, wo_ref, wg_ref, wu_ref, wd_ref, fg_ref, o_ref, final_norm)
        conv[0]()
        per_ffn = -(-(len(conv) - 1) // len(ffn))
        for k, ffn_stage in enumerate(ffn):
            for conv_stage in conv[1 + k * per_ffn:1 + (k + 1) * per_ffn]:
                conv_stage()
            ffn_stage()


def _out_ffn_sample(x, o_attn, o_ssm, o_conv, mod, p, layer, final_norm):
    t = x.shape[0]
    fc = FF_STREAM_CHUNK
    whole = lambda *shape: pl.BlockSpec(shape, lambda j: (0,) * len(shape))
    gate_up = pl.BlockSpec((None, D_MODEL, fc), lambda j: (layer, 0, j))
    down = pl.BlockSpec((None, fc, D_MODEL), lambda j: (layer, j, 0))
    return pl.pallas_call(
        functools.partial(_outffn_stream_kernel, final_norm=final_norm),
        grid=(D_FF // fc,),
        in_specs=[
            whole(t, D_MODEL), whole(t, ATTN_WIDTH), whole(t, SSM_WIDTH), whole(t, CONV_WIDTH),
            _mod_spec(mod, layer, 2, 1), _mod_spec(mod, layer, 3, 1),
            _mod_spec(mod, layer, 4, 1), _mod_spec(mod, layer, 5, 1),
            _layer_block((1, D_MODEL), layer),
            _layer_block((D_MODEL, D_MODEL), layer), gate_up, gate_up, down,
            whole(1, D_MODEL),
        ],
        out_specs=[whole(t, D_MODEL), whole(D_MODEL, D_MODEL),
                   pl.BlockSpec((D_MODEL, fc), lambda j: (0, j)), pl.BlockSpec((D_MODEL, fc), lambda j: (0, j)),
                   pl.BlockSpec((fc, D_MODEL), lambda j: (j, 0))],
        out_shape=[jax.ShapeDtypeStruct((t, D_MODEL), F32), jax.ShapeDtypeStruct((D_MODEL, D_MODEL), MXU_DTYPE),
                   jax.ShapeDtypeStruct((D_MODEL, D_FF), MXU_DTYPE), jax.ShapeDtypeStruct((D_MODEL, D_FF), MXU_DTYPE),
                   jax.ShapeDtypeStruct((D_FF, D_MODEL), MXU_DTYPE)],
        scratch_shapes=[pltpu.VMEM((t, D_MODEL), F32), pltpu.VMEM((t, D_MODEL), MXU_DTYPE),
                        pltpu.VMEM((t, D_MODEL), F32)],
        compiler_params=_params("arbitrary"),
        name="out_projection_ffn_stream",
    )(x, o_attn, o_ssm, o_conv, mod.array, mod.array, mod.array, mod.array, p['norm2_g'],
      p['w_out'], p['w_gate'], p['w_up'], p['w_down'], p['final_g'])


def _out_ffn_conv(x, o_attn, o_ssm, vc, mod, p, weights, layer, final_norm, tm, tiles_per_batch):
    t = x.shape[0]
    n_tiles = t // tm
    halo_per_tile = tm // CONV_HALO
    ffn_tile = lambda j: jnp.maximum(j - 1, 0)
    conv_tile = lambda j: jnp.minimum(j, n_tiles - 1)
    ffn_row = lambda j: (ffn_tile(j), 0)
    resident = lambda shape: pl.BlockSpec(shape, lambda j: (0, 0), pipeline_mode=pl.Buffered(1))
    cvec = _layer_block((1, CONV_WIDTH), layer)
    span = tm + CONV_HALO - SUBLANES
    return pl.pallas_call(
        functools.partial(_outffn_conv_kernel, final_norm=final_norm, n_tiles=n_tiles,
                          tiles_per_batch=tiles_per_batch, sub=CONV_ROWS_PER_STAGE),
        grid=(n_tiles + 1,),
        in_specs=[
            pl.BlockSpec((tm, D_MODEL), ffn_row),
            pl.BlockSpec((tm, ATTN_WIDTH), ffn_row),
            pl.BlockSpec((tm, SSM_WIDTH), ffn_row),
            pl.BlockSpec((CONV_HALO, CONV_WIDTH), lambda j: (jnp.maximum(conv_tile(j) * halo_per_tile - 1, 0), 0)),
            pl.BlockSpec((tm, CONV_WIDTH), lambda j: (conv_tile(j), 0)),
            _mod_spec(mod, layer, 2, tiles_per_batch, ffn_tile),
            _mod_spec(mod, layer, 3, tiles_per_batch, ffn_tile),
            _mod_spec(mod, layer, 4, tiles_per_batch, ffn_tile),
            _mod_spec(mod, layer, 5, tiles_per_batch, ffn_tile),
            _layer_block((1, D_MODEL), layer),
            resident((D_MODEL, D_MODEL)),
            resident((D_MODEL, D_FF)),
            resident((D_MODEL, D_FF)),
            resident((D_FF, D_MODEL)),
            pl.BlockSpec((1, D_MODEL), lambda j: (0, 0)),
            _layer_block((CONV_K, SUBLANES, CONV_WIDTH), layer), cvec, cvec, cvec,
        ],
        out_specs=pl.BlockSpec((tm, D_MODEL), ffn_row),
        out_shape=jax.ShapeDtypeStruct((t, D_MODEL), F32),
        scratch_shapes=[pltpu.VMEM((CONV_HALO + tm, CONV_WIDTH), F32),
                        pltpu.VMEM((SUBLANES - 1, span, CONV_WIDTH), F32),
                        pltpu.VMEM((2, tm, CONV_WIDTH), MXU_DTYPE)],
        compiler_params=_params("arbitrary"),
        name="conv_out_projection_ffn",
    )(x, o_attn, o_ssm, vc, vc, mod.array, mod.array, mod.array, mod.array, p['norm2_g'],
      *weights, p['final_g'], p['conv_w_tiles'], p['conv_b'], p['conv_ln_g'], p['conv_ln_b'])


def _layer_prompt(x, mod, p, weights, layer, rope, batch, seq, final_norm):
    tm = min(512, seq)
    tiles_per_batch = seq // tm
    tp = min(1024, seq)
    w_in, *ffn_weights = weights
    q, kv, u, vc = _in_projection(x, mod, p, w_in, layer, rope[0], rope[1], tp, seq // tp)
    o_attn = _attention_prompt(q, kv, p['sinks'], layer, batch, seq)
    n_keep = min(WINDOW, seq)
    kv_keep = kv.reshape(batch, seq, 2 * KV_WIDTH)[:, seq - n_keep:]
    new_k = kv_keep[:, :, :KV_WIDTH].reshape(batch, n_keep, N_KV_HEADS, HEAD_DIM)
    new_v = kv_keep[:, :, KV_WIDTH:].reshape(batch, n_keep, N_KV_HEADS, HEAD_DIM)

    nseg = PROMPT_SEGMENTS
    lseg = seq // nseg
    nseq = batch * nseg
    u_seq = u.reshape(nseq, lseg, SSM_WIDTH)
    steps = min(256, lseg)
    end_re, end_im = _s5_first_segment_states(u_seq, p, layer, nseg)
    start = lambda e: jnp.concatenate([jnp.zeros_like(e), e], axis=1).reshape(nseq, SSM_FLAT)
    o_seq, h_re, h_im = _s5_scan(u_seq, start(end_re), start(end_im), p, layer, nseq, steps)
    o_ssm = o_seq.reshape(batch * seq, SSM_WIDTH)
    last = lambda h: h.reshape(batch, nseg, N_SSM_GROUPS, SSM_STATE)[:, nseg - 1]

    new_conv = vc.reshape(batch, seq, CONV_WIDTH)[:, seq - (CONV_K - 1):]

    x = _out_ffn_conv(x, o_attn, o_ssm, vc, mod, p, ffn_weights, layer, final_norm, tm, tiles_per_batch)
    return x, new_k, new_v, last(h_re), last(h_im), new_conv


def _layer_sample(x, mod, p, layer, rope, batch, n_new, kt_all, vt_all, h0_re, h0_im, conv_all, final_norm):
    t = batch * n_new
    qexp, kn, vn, u, vc, w_in = _in_projection(x, mod, p, p['w_in'], layer, rope[0], rope[1], t, 1, n_new)
    o, new_k, new_v = _attention_sample(
        qexp.reshape(batch, N_Q_HEADS * n_new, LANES), kt_all, vt_all,
        kn.reshape(batch, SUBLANES, LANES), vn.reshape(batch, SUBLANES, LANES), p['sinks'], layer, n_new)
    o = o[:, :, :HEAD_DIM].reshape(batch, N_Q_HEADS, n_new, HEAD_DIM)
    o_attn = jnp.transpose(o, (2, 0, 1, 3)).reshape(t, ATTN_WIDTH)

    o_ssm, h_re, h_im = _s5_scan(u, h0_re, h0_im, p, layer, batch, n_new)
    st = lambda h: h

    o_conv, new_conv = _conv_sample(conv_all, vc.reshape(n_new, batch, CONV_WIDTH), p, layer)
    o_conv = o_conv.reshape(t, CONV_WIDTH)

    x, *ffn_weights = _out_ffn_sample(x, o_attn, o_ssm, o_conv, mod, p, layer, final_norm)
    return x, [w_in] + ffn_weights, new_k, new_v, st(h_re), st(h_im), new_conv


def kernel(x_prompt, x_sample, c_prompt, c_sample, cache_k, cache_v, state_ssm_re, state_ssm_im, state_conv,
           norm1_g, norm2_g, w_mod, b_mod, w_in, attn_sinks, ssm_lam_re, ssm_lam_im, ssm_log_dt,
           ssm_b_re, ssm_b_im, ssm_c_re, ssm_c_im, ssm_d, ssm_w_glu, ssm_b_glu,
           conv_w, conv_b, conv_ln_g, conv_ln_b, w_out, w_gate, w_up, w_down, final_norm_g):
    bp, seq, d = x_prompt.shape
    bs, n_new, _ = x_sample.shape
    depth = w_in.shape[0]
    assert PROMPT_SEGMENTS == 2 and seq % (PROMPT_SEGMENTS * SUBLANES) == 0

    c_all = jnp.concatenate([c_sample, c_prompt], axis=0)
    pad_rows = -c_all.shape[0] % SUBLANES
    mods = _modulation(jnp.pad(c_all, ((0, pad_rows), (0, 0))), w_mod, b_mod)
    mod_p = _Mod(mods[:, bs:bs + bp].reshape(depth, bp, 1, N_MOD * d), None)
    mod_s = _Mod(mods, bs)

    a_re, a_im, bre_blk, bim_blk, cre_blk, cim_blk, decay_re, decay_im, span_re, span_im = _s5_discretise(
        ssm_lam_re, ssm_lam_im, ssm_log_dt, ssm_b_re, ssm_b_im, ssm_c_re, ssm_c_im,
        min(S5_STATE_CHUNK, seq // PROMPT_SEGMENTS))
    rope_p = _rope_tables(jnp.arange(seq))
    rope_s = _rope_tables(jnp.repeat(PAST_LEN + jnp.arange(n_new), bs))

    cast = lambda a: a.astype(MXU_DTYPE)
    vec = lambda a: a.reshape(depth, 1, a.shape[-1])
    p = {
        'norm1_g': vec(norm1_g), 'norm2_g': vec(norm2_g), 'w_in': w_in, 'sinks': attn_sinks,
        'a_re': a_re, 'a_im': a_im,
        'bre_blk': bre_blk, 'bim_blk': bim_blk,
        'cre_blk': cre_blk, 'cim_blk': cim_blk,
        'decay_re': decay_re, 'decay_im': decay_im, 'span_re': span_re, 'span_im': span_im,
        'ssm_d': vec(ssm_d), 'ssm_w_glu': cast(ssm_w_glu), 'ssm_b_glu': vec(ssm_b_glu),
        'conv_w': conv_w, 'conv_w_tiles': jnp.repeat(conv_w[:, :, None, :], SUBLANES, axis=2),
        'conv_b': vec(conv_b), 'conv_ln_g': vec(conv_ln_g), 'conv_ln_b': vec(conv_ln_b),
        'w_out': w_out, 'w_gate': w_gate, 'w_up': w_up, 'w_down': w_down,
        'final_g': final_norm_g.reshape(1, d),
    }
    conv_all = jnp.transpose(state_conv, (0, 2, 1, 3))
    kt_all = jnp.transpose(cache_k, (0, 1, 3, 4, 2))
    vt_all = jnp.transpose(cache_v, (0, 1, 3, 4, 2))
    state_major = lambda a: jnp.transpose(a, (0, 2, 3, 1)).reshape(depth, SSM_FLAT, bs)
    batch_major = lambda a: jnp.transpose(a.reshape(depth, N_SSM_GROUPS, SSM_STATE, bs), (0, 3, 1, 2))
    ssm_re_all, ssm_im_all = state_major(state_ssm_re), state_major(state_ssm_im)

    xp = x_prompt.reshape(bp * seq, d)
    xs = jnp.transpose(x_sample, (1, 0, 2)).reshape(n_new * bs, d)
    outs_p, outs_s = [], []
    for l in range(depth):
        final = l == depth - 1
        xs, weights, *os_ = _layer_sample(xs, mod_s, p, l, rope_s, bs, n_new, kt_all, vt_all,
                                          ssm_re_all, ssm_im_all, conv_all, final)
        xp, *op = _layer_prompt(xp, mod_p, p, weights, l, rope_p, bp, seq, final)
        outs_p.append(op)
        outs_s.append(os_)
    stack = lambda outs, i: jnp.stack([o[i] for o in outs])
    y_sample = jnp.transpose(xs.reshape(n_new, bs, d), (1, 0, 2))
    new_k_s = jnp.transpose(stack(outs_s, 0), (0, 1, 4, 2, 3))
    new_v_s = jnp.transpose(stack(outs_s, 1), (0, 1, 4, 2, 3))
    new_conv_s = jnp.transpose(stack(outs_s, 4), (0, 2, 1, 3))
    return (xp.reshape(bp, seq, d), y_sample, *[stack(outs_p, i) for i in range(5)],
            new_k_s, new_v_s, batch_major(stack(outs_s, 2)), batch_major(stack(outs_s, 3)), new_conv_s)
```

```python
import functools
import math
from typing import NamedTuple

import jax
import jax.numpy as jnp
from jax import lax
from jax.experimental import pallas as pl
from jax.experimental.pallas import tpu as pltpu

F32 = jnp.float32
MXU_DTYPE = jnp.bfloat16

V7X_VMEM_BYTES = 64 * 1024 * 1024
VMEM_LIMIT_BYTES = V7X_VMEM_BYTES - 8 * 1024 * 1024
LANES = 128
SUBLANES = 8
V7X_MXU_DIM = 256

D_MODEL = 1024
HEAD_DIM = 64
N_Q_HEADS = 8
N_KV_HEADS = 2
Q_PER_KV = N_Q_HEADS // N_KV_HEADS
ATTN_WIDTH = N_Q_HEADS * HEAD_DIM
KV_WIDTH = N_KV_HEADS * HEAD_DIM
WINDOW = 128
ROPE_THETA = 10000.0
ATTN_SCALE = 1.0 / math.sqrt(HEAD_DIM)
SSM_WIDTH = 256
SSM_GROUP = 16
N_SSM_GROUPS = 16
SSM_STATE = 64
SSM_FLAT = N_SSM_GROUPS * SSM_STATE
CONV_WIDTH = 256
CONV_K = 31
CONV_HALO = 32
CONV_ROWS_PER_STAGE = 64
IN_WIDTH = ATTN_WIDTH + 2 * KV_WIDTH + SSM_WIDTH + 2 * CONV_WIDTH
D_FF = 2816
FF_CHUNK = 6 * V7X_MXU_DIM
FF_STREAM_CHUNK = V7X_MXU_DIM
EPS = 1e-6
NEG = -1e30
N_MOD = 6
PROMPT_SEGMENTS = 2
S5_STATE_CHUNK = 1024
PAST_LEN = 8192


def _params(*semantics):
    return pltpu.CompilerParams(dimension_semantics=semantics, vmem_limit_bytes=VMEM_LIMIT_BYTES)


def _layer_block(shape, layer, **kw):
    zeros = (0,) * len(shape)
    return pl.BlockSpec((None,) + tuple(shape), lambda *_: (layer,) + zeros, **kw)


def _sigmoid(x):
    return 1.0 / (1.0 + jnp.exp(-x))


def _silu(x):
    return x * _sigmoid(x)


def _rms(x, g):
    return x * lax.rsqrt(jnp.mean(x * x, axis=-1, keepdims=True) + EPS) * g


def _mm(a, b):
    return jnp.dot(a, b, preferred_element_type=F32)


def _mm_nt(a, b):
    return lax.dot_general(a, b, (((1,), (1,)), ((), ())), preferred_element_type=F32)


def _rows(m, n):
    return m if m.shape[0] == 1 else jnp.concatenate([m] * (n // m.shape[0]), axis=0)


def _mod_kernel(c_ref, w_ref, b_ref, o_ref):
    a = _silu(c_ref[...]).astype(MXU_DTYPE)
    o_ref[...] = _mm(a, w_ref[...].astype(MXU_DTYPE)) + b_ref[...]


def _modulation(c, w_mod, b_mod):
    depth, d, n = w_mod.shape
    rows = c.shape[0]
    tn = 3072
    return pl.pallas_call(
        _mod_kernel,
        grid=(depth, n // tn),
        in_specs=[
            pl.BlockSpec((rows, d), lambda l, j: (0, 0)),
            pl.BlockSpec((None, d, tn), lambda l, j: (l, 0, j)),
            pl.BlockSpec((None, 1, tn), lambda l, j: (l, 0, j)),
        ],
        out_specs=pl.BlockSpec((None, rows, tn), lambda l, j: (l, 0, j)),
        out_shape=jax.ShapeDtypeStruct((depth, rows, n), F32),
        compiler_params=_params("parallel", "parallel"),
        name="modulation",
    )(c, w_mod, b_mod.reshape(depth, 1, n))


class _Mod(NamedTuple):
    array: jax.Array
    batch_rows: int | None


def _mod_spec(mod, layer, chunk, tiles_per_batch, tile=lambda i: i):
    if mod.batch_rows is None:
        return pl.BlockSpec((None, None, 1, D_MODEL), lambda i: (layer, tile(i) // tiles_per_batch, 0, chunk))
    return pl.BlockSpec((None, mod.batch_rows, D_MODEL), lambda i: (layer, 0, chunk))


def _inproj_kernel(x_ref, sh_ref, sc_ref, g_ref, w_ref, cos_ref, sin_ref, *outs, n_new):
    tm = x_ref.shape[0]
    h = _rms(x_ref[...], g_ref[...]) * (1.0 + _rows(sc_ref[...], tm)) + _rows(sh_ref[...], tm)
    w = w_ref[...].astype(MXU_DTYPE)
    z = _mm(h.astype(MXU_DTYPE), w)
    cos = cos_ref[...]
    sin = sin_ref[...]
    lane = lax.broadcasted_iota(jnp.int32, (tm, LANES), 1)
    first_half = (lane % HEAD_DIM) < (HEAD_DIM // 2)

    def rope(t):
        partner = jnp.where(first_half, pltpu.roll(t, LANES - HEAD_DIM // 2, 1),
                            pltpu.roll(t, HEAD_DIM // 2, 1))
        return t * cos + partner * sin

    q_cols = [rope(z[:, j * LANES:(j + 1) * LANES]) * ATTN_SCALE for j in range(ATTN_WIDTH // LANES)]
    o = ATTN_WIDTH
    k_rot = rope(z[:, o:o + KV_WIDTH])
    v_new = z[:, o + KV_WIDTH:o + 2 * KV_WIDTH]
    if n_new is None:
        q_ref, kv_ref, u_ref, vc_ref = outs
        for j, qc in enumerate(q_cols):
            q_ref[:, j * LANES:(j + 1) * LANES] = qc.astype(q_ref.dtype)
        kv_ref[:, 0:KV_WIDTH] = k_rot
        kv_ref[:, KV_WIDTH:2 * KV_WIDTH] = v_new
    else:
        q_ref, kn_ref, vn_ref, u_ref, vc_ref, w_out = outs
        w_out[...] = w
        batch = tm // n_new
        low = lax.broadcasted_iota(jnp.int32, (batch, LANES), 1) < HEAD_DIM
        kn_ref[...] = jnp.zeros(kn_ref.shape, F32)
        vn_ref[...] = jnp.zeros(vn_ref.shape, F32)
        for t in range(n_new):
            rows = slice(t * batch, (t + 1) * batch)
            tile_row = pl.ds(SUBLANES - n_new + t, batch, stride=SUBLANES)
            kn_ref[tile_row, :] = k_rot[rows]
            vn_ref[tile_row, :] = v_new[rows]
            for h in range(N_Q_HEADS):
                kvh = h // Q_PER_KV
                piece = q_cols[h // 2][rows]
                if h % 2 != kvh:
                    piece = pltpu.roll(piece, HEAD_DIM, 1)
                piece = jnp.where(low if kvh == 0 else ~low, piece, 0.0)
                q_ref[pl.ds(h * n_new + t, batch, stride=N_Q_HEADS * n_new), :] = piece
    o += 2 * KV_WIDTH
    u_ref[...] = z[:, o:o + SSM_WIDTH]
    o += SSM_WIDTH
    za = z[:, o:o + CONV_WIDTH]
    zg = z[:, o + CONV_WIDTH:o + 2 * CONV_WIDTH]
    vc_ref[...] = za * _sigmoid(zg)


def _in_projection(x, mod, p, w_in, layer, cos, sin, tm, tiles_per_batch, n_new=None):
    t = x.shape[0]
    pos_tiles = cos.shape[0] // tm
    row = lambda i: (i, 0)
    whole = lambda *shape: pl.BlockSpec(shape, lambda i: (0,) * len(shape))
    if n_new is None:
        w_spec = whole(D_MODEL, IN_WIDTH)
        attn_specs = [pl.BlockSpec((tm, ATTN_WIDTH), row), pl.BlockSpec((tm, 2 * KV_WIDTH), row)]
        attn_shapes = [jax.ShapeDtypeStruct((t, ATTN_WIDTH), MXU_DTYPE),
                       jax.ShapeDtypeStruct((t, 2 * KV_WIDTH), F32)]
        w_out_spec, w_out_shape = [], []
    else:
        assert tm == t and KV_WIDTH == LANES
        w_spec = _layer_block((D_MODEL, IN_WIDTH), layer)
        batch = t // n_new
        shapes = [(batch * N_Q_HEADS * n_new, LANES), (batch * SUBLANES, LANES), (batch * SUBLANES, LANES)]
        attn_specs = [whole(*s) for s in shapes]
        attn_shapes = [jax.ShapeDtypeStruct(s, F32) for s in shapes]
        w_out_spec = [whole(D_MODEL, IN_WIDTH)]
        w_out_shape = [jax.ShapeDtypeStruct((D_MODEL, IN_WIDTH), MXU_DTYPE)]
    return pl.pallas_call(
        functools.partial(_inproj_kernel, n_new=n_new),
        grid=(t // tm,),
        in_specs=[
            pl.BlockSpec((tm, D_MODEL), row),
            _mod_spec(mod, layer, 0, tiles_per_batch),
            _mod_spec(mod, layer, 1, tiles_per_batch),
            _layer_block((1, D_MODEL), layer),
            w_spec,
            pl.BlockSpec((tm, LANES), lambda i: (i % pos_tiles, 0)),
            pl.BlockSpec((tm, LANES), lambda i: (i % pos_tiles, 0)),
        ],
        out_specs=(attn_specs + [pl.BlockSpec((tm, SSM_WIDTH), row), pl.BlockSpec((tm, CONV_WIDTH), row)]
                   + w_out_spec),
        out_shape=(attn_shapes + [jax.ShapeDtypeStruct((t, SSM_WIDTH), F32),
                                  jax.ShapeDtypeStruct((t, CONV_WIDTH), F32)] + w_out_shape),
        compiler_params=_params("parallel"),
        name="in_projection",
    )(x, mod.array, mod.array, p['norm1_g'], w_in, cos, sin)


def _rope_tables(pos):
    half = HEAD_DIM // 2
    inv_freq = ROPE_THETA ** (-jnp.arange(half, dtype=F32) / half)
    ang = pos.astype(F32)[:, None] * inv_freq[None, :]
    cos = jnp.tile(jnp.cos(ang), (1, LANES // half))
    sin = jnp.sin(ang)
    sin = jnp.tile(jnp.concatenate([-sin, sin], axis=1), (1, LANES // HEAD_DIM))
    return cos, sin


def _sink_softmax(s, sink_col):
    m = jnp.maximum(jnp.max(s, axis=-1, keepdims=True), sink_col)
    e = jnp.exp(s - m)
    return e * (1.0 / (jnp.sum(e, axis=-1, keepdims=True) + jnp.exp(sink_col - m)))


def _head_pair_select(x, pick_second):
    lane = lax.broadcasted_iota(jnp.int32, x.shape, 1)
    swapped = pltpu.roll(x, HEAD_DIM, 1)
    low = lane < HEAD_DIM
    return jnp.where(low, swapped, x) if pick_second else jnp.where(low, x, swapped)


def _attn_prompt_kernel(sink_ref, q_ref, kvp_ref, kvc_ref, o_ref, *, layer):
    n = pl.program_id(1)
    w = WINDOW
    n_blocks = q_ref.shape[0] // w
    rows = Q_PER_KV * w
    r_idx = lax.broadcasted_iota(jnp.int32, (rows, w), 0) % w
    c_idx = lax.broadcasted_iota(jnp.int32, (rows, w), 1)
    from_prev = c_idx > r_idx
    first_bias = jnp.where(n > 0, 0.0, NEG)
    row_head = lax.broadcasted_iota(jnp.int32, (rows, 1), 0) // w
    low = lax.broadcasted_iota(jnp.int32, (w, LANES), 1) < HEAD_DIM
    half_mask = [jnp.where(low, 1.0, 0.0).astype(MXU_DTYPE), jnp.where(low, 0.0, 1.0).astype(MXU_DTYPE)]
    sink_cols = []
    for kvh in range(N_KV_HEADS):
        sink_col = jnp.zeros((rows, 1), F32)
        for g in range(Q_PER_KV):
            sink_col = jnp.where(row_head == g, sink_ref[layer, kvh * Q_PER_KV + g], sink_col)
        sink_cols.append(sink_col)
    chains = [(i, kvh) for i in range(n_blocks) for kvh in range(N_KV_HEADS)]
    windows, values = [], []
    for i, kvh in chains:
        own = slice(i * w, (i + 1) * w)
        prev_ref, prev = (kvp_ref, slice(0, w)) if i == 0 else (kvc_ref, slice((i - 1) * w, i * w))
        kk = jnp.concatenate([prev_ref[prev, 0:KV_WIDTH], kvc_ref[own, 0:KV_WIDTH]], axis=0)
        vv = jnp.concatenate([prev_ref[prev, KV_WIDTH:2 * KV_WIDTH], kvc_ref[own, KV_WIDTH:2 * KV_WIDTH]], axis=0)
        k2 = _head_pair_select(kk, kvh == 1).astype(MXU_DTYPE)
        values.append(_head_pair_select(vv, kvh == 1).astype(MXU_DTYPE))
        pieces = []
        for g in range(Q_PER_KV):
            h = kvh * Q_PER_KV + g
            qcol = q_ref[own, (h // 2) * LANES:(h // 2 + 1) * LANES]
            pieces.append(qcol * half_mask[h % 2])
        s = _mm_nt(jnp.concatenate(pieces, axis=0), k2)
        s_prev = s[:, 0:w] + first_bias if i == 0 else s[:, 0:w]
        windows.append(jnp.where(from_prev, s_prev, s[:, w:2 * w]))
    probs = []
    for (i, kvh), sc in zip(chains, windows):
        p = _sink_softmax(sc, sink_cols[kvh])
        p2 = jnp.concatenate([jnp.where(from_prev, p, 0.0), jnp.where(from_prev, 0.0, p)], axis=1)
        probs.append(p2.astype(MXU_DTYPE))
    for (i, kvh), p2, v2 in zip(chains, probs, values):
        own = slice(i * w, (i + 1) * w)
        r = _mm(p2, v2)
        for j in range(Q_PER_KV // 2):
            col = kvh * (Q_PER_KV // 2) + j
            o_ref[own, col * LANES:(col + 1) * LANES] = jnp.where(
                low, r[2 * j * w:(2 * j + 1) * w], r[(2 * j + 1) * w:(2 * j + 2) * w]).astype(o_ref.dtype)


def _attention_prompt(q, kv, sinks, layer, batch, seq):
    tq = min(8 * WINDOW, seq)
    nt = seq // tq
    per = tq // WINDOW
    return pl.pallas_call(
        functools.partial(_attn_prompt_kernel, layer=layer),
        grid=(batch, nt),
        in_specs=[
            pl.BlockSpec(memory_space=pltpu.SMEM),
            pl.BlockSpec((tq, ATTN_WIDTH), lambda b, n: (b * nt + n, 0)),
            pl.BlockSpec((WINDOW, 2 * KV_WIDTH), lambda b, n: ((b * nt + n) * per - jnp.minimum(n, 1), 0)),
            pl.BlockSpec((tq, 2 * KV_WIDTH), lambda b, n: (b * nt + n, 0)),
        ],
        out_specs=pl.BlockSpec((tq, ATTN_WIDTH), lambda b, n: (b * nt + n, 0)),
        out_shape=jax.ShapeDtypeStruct((batch * seq, ATTN_WIDTH), MXU_DTYPE),
        compiler_params=_params("parallel", "parallel"),
        name="attention_prompt",
    )(sinks, q, kv, kv)


def _attn_sample_kernel(sink_ref, q_ref, kt_ref, vt_ref, kn_ref, vn_ref, o_ref, nk_ref, nv_ref, *, layer, n_new):
    bt, rows, _ = q_ref.shape
    nkv, hd, wb = kt_ref.shape[1:]
    kept = wb - n_new
    r = lax.broadcasted_iota(jnp.int32, (bt * rows, 2 * wb), 0)
    j = lax.broadcasted_iota(jnp.int32, (bt * rows, 2 * wb), 1)
    t_idx = r % n_new
    t_new = j - wb - kept
    mask = ((j < wb) & (j > t_idx)) | ((t_new >= 0) & (t_new <= t_idx))
    row_head = (lax.broadcasted_iota(jnp.int32, (bt * rows, 1), 0) % rows) // n_new
    sink_col = jnp.zeros((bt * rows, 1), F32)
    for h in range(N_Q_HEADS):
        sink_col = jnp.where(row_head == h, sink_ref[layer, h], sink_col)
    keep = lax.broadcasted_iota(jnp.int32, (nkv, hd, wb), 2) < kept
    second_group = lax.broadcasted_iota(jnp.int32, (rows, nkv * hd), 0) >= (rows // nkv)
    flat = lambda a: a.reshape(nkv * hd, wb)
    above = jnp.zeros((wb - kn_ref.shape[1], nkv * hd), F32)

    scores, values = [], []
    for b in range(bt):
        kt, vt = kt_ref[b], vt_ref[b]
        kpos = jnp.concatenate([above, kn_ref[b]], axis=0).T
        vpos = jnp.concatenate([above, vn_ref[b]], axis=0).T
        nk_ref[b] = jnp.where(keep, pltpu.roll(kt, kept, 2), kpos.reshape(nkv, hd, wb))
        nv_ref[b] = jnp.where(keep, pltpu.roll(vt, kept, 2), vpos.reshape(nkv, hd, wb))
        kcat = jnp.concatenate([flat(kt), kpos], axis=1).astype(MXU_DTYPE)
        values.append(jnp.concatenate([flat(vt), vpos], axis=1).astype(MXU_DTYPE))
        scores.append(_mm(q_ref[b].astype(MXU_DTYPE), kcat))
    s = jnp.where(mask, jnp.concatenate(scores, axis=0), NEG)
    p = _sink_softmax(s, sink_col).astype(MXU_DTYPE)
    for b in range(bt):
        o = _mm_nt(p[b * rows:(b + 1) * rows], values[b])
        o_ref[b] = jnp.where(second_group, pltpu.roll(o, hd, 1), o).astype(o_ref.dtype)


def _attention_sample(qexp, kt_all, vt_all, kn, vn, sinks, layer, n_new):
    batch, rows, width = qexp.shape
    _, _, nkv, hd, wb = kt_all.shape
    bt = min(32, batch)
    blk3 = pl.BlockSpec((bt, rows, width), lambda i: (i, 0, 0))
    blk4 = pl.BlockSpec((bt, nkv, hd, wb), lambda i: (i, 0, 0, 0))
    new = pl.BlockSpec((bt,) + kn.shape[1:], lambda i: (i, 0, 0))
    cache = pl.BlockSpec((None, bt, nkv, hd, wb), lambda i: (layer, i, 0, 0, 0))
    return pl.pallas_call(
        functools.partial(_attn_sample_kernel, layer=layer, n_new=n_new),
        grid=(batch // bt,),
        in_specs=[pl.BlockSpec(memory_space=pltpu.SMEM), blk3, cache, cache, new, new],
        out_specs=[blk3, blk4, blk4],
        out_shape=[jax.ShapeDtypeStruct((batch, rows, width), MXU_DTYPE),
                   jax.ShapeDtypeStruct((batch, nkv, hd, wb), F32),
                   jax.ShapeDtypeStruct((batch, nkv, hd, wb), F32)],
        compiler_params=_params("parallel"),
        name="attention_sample",
    )(sinks, qexp, kt_all, vt_all, kn, vn)


def _cmul(ar, ai, br, bi):
    return ar * br - ai * bi, ar * bi + ai * br


def _s5_discretise_kernel(lr_ref, li_ref, ldt_ref, bre_ref, bim_ref, cre_ref, cim_ref,
                          are_ref, aim_ref, bbre_ref, bbim_ref, ccre_ref, ccim_ref,
                          decay_re_ref, decay_im_ref, span_re_ref, span_im_ref):
    lr = lr_ref[...]
    li = li_ref[...]
    dt = jnp.exp(ldt_ref[...])
    mag = jnp.exp(lr * dt)
    ab_re = mag * jnp.cos(li * dt)
    ab_im = mag * jnp.sin(li * dt)
    den = lr * lr + li * li
    nr = ab_re - 1.0
    coef_re = (nr * lr + ab_im * li) / den
    coef_im = (ab_im * lr - nr * li) / den
    are_ref[...] = ab_re
    aim_ref[...] = ab_im
    br = bre_ref[...]
    bi = bim_ref[...]
    bb_re = coef_re * br - coef_im * bi
    bb_im = coef_re * bi + coef_im * br
    lane_group = lax.broadcasted_iota(jnp.int32, bb_re.shape, 1) // SSM_STATE
    for src, dst in ((bb_re, bbre_ref), (bb_im, bbim_ref), (cre_ref[...], ccre_ref), (cim_ref[...], ccim_ref)):
        for g in range(N_SSM_GROUPS):
            dst[g * SSM_GROUP:(g + 1) * SSM_GROUP, :] = jnp.where(lane_group == g, src, 0.0).astype(dst.dtype)
    t_chunk = decay_re_ref.shape[0]
    row = lax.broadcasted_iota(jnp.int32, (SUBLANES, SSM_FLAT), 0)
    pr, pi = jnp.ones_like(ab_re), jnp.zeros_like(ab_re)
    tile_re, tile_im = jnp.zeros((SUBLANES, SSM_FLAT), F32), jnp.zeros((SUBLANES, SSM_FLAT), F32)
    for k in range(SUBLANES):
        tile_re = jnp.where(row == SUBLANES - 1 - k, pr, tile_re)
        tile_im = jnp.where(row == SUBLANES - 1 - k, pi, tile_im)
        pr, pi = _cmul(pr, pi, ab_re, ab_im)
    decay_re_ref[t_chunk - SUBLANES:t_chunk, :] = tile_re
    decay_im_ref[t_chunk - SUBLANES:t_chunk, :] = tile_im
    n = SUBLANES
    while n < t_chunk:
        lo, hi = slice(t_chunk - 2 * n, t_chunk - n), slice(t_chunk - n, t_chunk)
        decay_re_ref[lo, :], decay_im_ref[lo, :] = _cmul(decay_re_ref[hi, :], decay_im_ref[hi, :], pr, pi)
        pr, pi = _cmul(pr, pi, pr, pi)
        n *= 2
    span_re_ref[...] = pr
    span_im_ref[...] = pi


def _s5_discretise(lam_re, lam_im, log_dt, b_re, b_im, c_re, c_im, t_chunk):
    depth = lam_re.shape[0]
    assert t_chunk % SUBLANES == 0 and (t_chunk // SUBLANES) & (t_chunk // SUBLANES - 1) == 0
    flat = lambda a: a.reshape(depth, 1, SSM_FLAT)
    ldt = jnp.broadcast_to(log_dt[:, :, None], lam_re.shape)
    bt = lambda a: jnp.transpose(a, (0, 3, 1, 2)).reshape(depth, SSM_GROUP, SSM_FLAT)
    ct = lambda a: jnp.transpose(a, (0, 2, 1, 3)).reshape(depth, SSM_GROUP, SSM_FLAT)
    vec = pl.BlockSpec((None, 1, SSM_FLAT), lambda l: (l, 0, 0))
    mat = pl.BlockSpec((None, SSM_GROUP, SSM_FLAT), lambda l: (l, 0, 0))
    blk = pl.BlockSpec((None, SSM_WIDTH, SSM_FLAT), lambda l: (l, 0, 0))
    tab = pl.BlockSpec((None, t_chunk, SSM_FLAT), lambda l: (l, 0, 0))
    return pl.pallas_call(
        _s5_discretise_kernel,
        grid=(depth,),
        in_specs=[vec, vec, vec, mat, mat, mat, mat],
        out_specs=[vec, vec, blk, blk, blk, blk, tab, tab, vec, vec],
        out_shape=[jax.ShapeDtypeStruct((depth, 1, SSM_FLAT), F32)] * 2
        + [jax.ShapeDtypeStruct((depth, SSM_WIDTH, SSM_FLAT), MXU_DTYPE)] * 4
        + [jax.ShapeDtypeStruct((depth, t_chunk, SSM_FLAT), F32)] * 2
        + [jax.ShapeDtypeStruct((depth, 1, SSM_FLAT), F32)] * 2,
        compiler_params=_params("parallel"),
        name="s5_discretise",
    )(flat(lam_re), flat(lam_im), flat(ldt), bt(b_re), bt(b_im), ct(c_re), ct(c_im))


def _s5_first_segment_kernel(u_ref, bre_ref, bim_ref, dre_ref, dim_ref, sre_ref, sim_ref, ere_ref, eim_ref,
                             hre_s, him_s):
    @pl.when(pl.program_id(1) == 0)
    def _():
        hre_s[...] = jnp.zeros(hre_s.shape, F32)
        him_s[...] = jnp.zeros(him_s.shape, F32)

    ub = u_ref[...].astype(MXU_DTYPE)
    wr, wi = _cmul(dre_ref[...], dim_ref[...], _mm(ub, bre_ref[...]), _mm(ub, bim_ref[...]))
    cr, ci = _cmul(sre_ref[...], sim_ref[...], hre_s[...], him_s[...])
    hr = cr + jnp.sum(wr, axis=0, keepdims=True)
    hi = ci + jnp.sum(wi, axis=0, keepdims=True)
    hre_s[...] = hr
    him_s[...] = hi
    ere_ref[...] = hr
    eim_ref[...] = hi


def _s5_first_segment_states(u_seq, p, layer, nseg):
    nseq, lseg, _ = u_seq.shape
    batch = nseq // nseg
    t_chunk = p['decay_re'].shape[1]
    vec = _layer_block((1, SSM_FLAT), layer)
    tab = _layer_block((t_chunk, SSM_FLAT), layer)
    bmat = _layer_block((SSM_WIDTH, SSM_FLAT), layer)
    out = pl.BlockSpec((None, 1, SSM_FLAT), lambda b, c: (b, 0, 0))
    return pl.pallas_call(
        _s5_first_segment_kernel,
        grid=(batch, lseg // t_chunk),
        in_specs=[pl.BlockSpec((None, t_chunk, SSM_WIDTH), lambda b, c: (b * nseg, c, 0)),
                  bmat, bmat, tab, tab, vec, vec],
        out_specs=[out, out],
        out_shape=[jax.ShapeDtypeStruct((batch, 1, SSM_FLAT), F32)] * 2,
        scratch_shapes=[pltpu.VMEM((1, SSM_FLAT), F32)] * 2,
        compiler_params=_params("parallel", "arbitrary"),
        name="s5_first_segment_states",
    )(u_seq, p['bre_blk'], p['bim_blk'], p['decay_re'], p['decay_im'], p['span_re'], p['span_im'])


def _s5_kernel(u_ref, hre0_ref, him0_ref, are_ref, aim_ref, bre_ref, bim_ref, cre_ref, cim_ref, d_ref,
               wglu_ref, bglu_ref, o_ref, hre_out, him_out, hre_s, him_s, bure_s, buim_s, hsre_s, hsim_s,
               *stage, nseq):
    seq_major = bool(stage)
    steps = u_ref.shape[1] if seq_major else u_ref.shape[0] // nseq
    lane_halves = [slice(h * LANES, (h + 1) * LANES) for h in range(SSM_WIDTH // LANES)]

    @pl.when(pl.program_id(0) == 0)
    def _():
        hre_s[...] = hre0_ref[...] if hre0_ref.shape == hre_s.shape else hre0_ref[...].T
        him_s[...] = him0_ref[...] if him0_ref.shape == him_s.shape else him0_ref[...].T

    if seq_major:
        stage_s, = stage
        for s in range(nseq):
            for h, cols in enumerate(lane_halves):
                stage_s[h, pl.ds(s, steps, stride=nseq), :] = u_ref[s, :, cols]
        u = jnp.concatenate([stage_s[h] for h in range(len(lane_halves))], axis=1)
    else:
        u = u_ref[...]
    ub = u.astype(MXU_DTYPE)
    bure_s[...] = _mm(ub, bre_ref[...])
    buim_s[...] = _mm(ub, bim_ref[...])
    ar = jnp.broadcast_to(are_ref[...], (nseq, SSM_FLAT))
    ai = jnp.broadcast_to(aim_ref[...], (nseq, SSM_FLAT))

    def step(t, carry):
        hr, hi = carry
        rows = pl.ds(pl.multiple_of(t * nseq, nseq), nseq)
        nhr = ar * hr - ai * hi + bure_s[rows, :]
        nhi = ar * hi + ai * hr + buim_s[rows, :]
        hsre_s[rows, :] = nhr
        hsim_s[rows, :] = nhi
        return nhr, nhi

    hr, hi = lax.fori_loop(0, steps, step, (hre_s[...], him_s[...]), unroll=min(steps, 8))
    hre_s[...] = hr
    him_s[...] = hi
    hre_out[...] = hr if hre_out.shape == hr.shape else hr.T
    him_out[...] = hi if him_out.shape == hi.shape else hi.T
    y = (_mm_nt(hsre_s[...].astype(MXU_DTYPE), cre_ref[...])
         - _mm_nt(hsim_s[...].astype(MXU_DTYPE), cim_ref[...]))
    z = jax.nn.gelu(y + d_ref[...] * u)
    gate = _mm(z.astype(MXU_DTYPE), wglu_ref[...]) + bglu_ref[...]
    out = z * _sigmoid(gate)
    if seq_major:
        for h, cols in enumerate(lane_halves):
            stage_s[h] = out[:, cols]
        for s in range(nseq):
            for h, cols in enumerate(lane_halves):
                o_ref[s, :, cols] = stage_s[h, pl.ds(s, steps, stride=nseq), :].astype(o_ref.dtype)
    else:
        o_ref[...] = out.astype(o_ref.dtype)


def _s5_scan(u, hre0, him0, p, layer, nseq, steps_per_tile):
    seq_major = u.ndim == 3
    rows = u.shape[0] * u.shape[1] if seq_major else u.shape[0]
    tr = steps_per_tile * nseq
    if hre0.ndim == 3:
        state_in = _layer_block((SSM_FLAT, nseq), layer)
        state = pl.BlockSpec((SSM_FLAT, nseq), lambda i: (0, 0))
    else:
        state_in = state = pl.BlockSpec((nseq, SSM_FLAT), lambda i: (0, 0))
    if seq_major:
        io_spec = pl.BlockSpec((nseq, steps_per_tile, SSM_WIDTH), lambda i: (0, i, 0))
    else:
        io_spec = pl.BlockSpec((tr, SSM_WIDTH), lambda i: (i, 0))
    state_vec = _layer_block((1, SSM_FLAT), layer)
    group_map = _layer_block((SSM_WIDTH, SSM_FLAT), layer)
    width_vec = _layer_block((1, SSM_WIDTH), layer)
    scratch = ([pltpu.VMEM((nseq, SSM_FLAT), F32)] * 2 + [pltpu.VMEM((tr, SSM_FLAT), F32)] * 4
               + [pltpu.VMEM((SSM_WIDTH // LANES, tr, LANES), F32)] * seq_major)
    return pl.pallas_call(
        functools.partial(_s5_kernel, nseq=nseq),
        grid=(rows // tr,),
        in_specs=[io_spec, state_in, state_in, state_vec, state_vec, group_map, group_map, group_map, group_map,
                  width_vec, _layer_block((SSM_WIDTH, SSM_WIDTH), layer), width_vec],
        out_specs=[io_spec, state, state],
        out_shape=[jax.ShapeDtypeStruct(u.shape, MXU_DTYPE)] + [jax.ShapeDtypeStruct(state.block_shape, F32)] * 2,
        scratch_shapes=scratch,
        compiler_params=_params("arbitrary"),
        name="s5_scan",
    )(u, hre0, him0, p['a_re'], p['a_im'], p['bre_blk'], p['bim_blk'], p['cre_blk'], p['cim_blk'],
      p['ssm_d'], p['ssm_w_glu'], p['ssm_b_glu'])


def _layernorm_silu(y, g, b):
    yc = y - jnp.mean(y, axis=-1, keepdims=True)
    var = jnp.mean(yc * yc, axis=-1, keepdims=True)
    return _silu(yc * lax.rsqrt(var + EPS) * g + b)


def _conv_sample_kernel(state_ref, v_ref, w_ref, b_ref, g_ref, beta_ref, o_ref, ns_ref):
    ns = state_ref.shape[0]
    n_new = v_ref.shape[0]
    row = lambda j: state_ref[j] if j < ns else v_ref[j - ns]
    for t in range(n_new):
        acc = jnp.zeros(o_ref.shape[1:], F32)
        for k in range(CONV_K):
            acc = acc + w_ref[k:k + 1, :] * row(t + k)
        o_ref[t] = _layernorm_silu(acc + b_ref[...], g_ref[...], beta_ref[...]).astype(o_ref.dtype)
    for j in range(ns):
        ns_ref[j] = row(j + n_new)


def _conv_sample(state_all, v, p, layer):
    _, ns, batch, width = state_all.shape
    n_new = v.shape[0]
    bt = min(32, batch)
    vec = _layer_block((1, width), layer)
    return pl.pallas_call(
        _conv_sample_kernel,
        grid=(batch // bt,),
        in_specs=[
            pl.BlockSpec((None, ns, bt, width), lambda i: (layer, 0, i, 0)),
            pl.BlockSpec((n_new, bt, width), lambda i: (0, i, 0)),
            _layer_block((CONV_K, width), layer), vec, vec, vec,
        ],
        out_specs=[pl.BlockSpec((n_new, bt, width), lambda i: (0, i, 0)),
                   pl.BlockSpec((ns, bt, width), lambda i: (0, i, 0))],
        out_shape=[jax.ShapeDtypeStruct((n_new, batch, width), MXU_DTYPE),
                   jax.ShapeDtypeStruct((ns, batch, width), F32)],
        compiler_params=_params("parallel"),
        name="conv_sample",
    )(state_all, v, p['conv_w'], p['conv_b'], p['conv_ln_g'], p['conv_ln_b'])


def _outffn_stages(x_ref, a_ref, s_ref, load_conv, g1_ref, sh2_ref, sc2_ref, g2_ref, n2_ref,
                   wo_ref, wg_ref, wu_ref, wd_ref, fg_ref, o_ref, final_norm):
    tm = x_ref.shape[0]
    o1 = ATTN_WIDTH
    o2 = o1 + SSM_WIDTH
    st = {}

    def proj_attn_ssm():
        st['proj'] = _mm(a_ref[...], wo_ref[0:o1, :]) + _mm(s_ref[...], wo_ref[o1:o2, :])

    def proj_conv_residual_norm():
        proj = st.pop('proj') + _mm(load_conv(), wo_ref[o2:o2 + CONV_WIDTH, :])
        x2 = x_ref[...] + _rows(g1_ref[...], tm) * proj
        h2 = _rms(x2, n2_ref[...]) * (1.0 + _rows(sc2_ref[...], tm)) + _rows(sh2_ref[...], tm)
        st['x2'] = x2
        st['h2'] = h2.astype(MXU_DTYPE)

    def gate(cs):
        st['gate'] = _mm(st['h2'], wg_ref[:, cs])

    def up(cs):
        st['act'] = (_silu(st.pop('gate')) * _mm(st['h2'], wu_ref[:, cs])).astype(MXU_DTYPE)

    def down(cs, last):
        ffn = _mm(st.pop('act'), wd_ref[cs, :])
        st['ffn'] = ffn if 'ffn' not in st else st['ffn'] + ffn
        if last:
            y = st['x2'] + _rows(g2_ref[...], tm) * st['ffn']
            o_ref[...] = _rms(y, fg_ref[...]) if final_norm else y

    stages = [proj_attn_ssm, proj_conv_residual_norm]
    for start in range(0, D_FF, FF_CHUNK):
        cs = slice(start, min(start + FF_CHUNK, D_FF))
        stages += [functools.partial(gate, cs), functools.partial(up, cs),
                   functools.partial(down, cs, cs.stop == D_FF)]
    return stages


def _outffn_stream_kernel(x_ref, a_ref, s_ref, c_ref, g1_ref, sh2_ref, sc2_ref, g2_ref, n2_ref,
                          wo_ref, wg_ref, wu_ref, wd_ref, fg_ref,
                          o_ref, wo_out, wg_out, wu_out, wd_out, x2_s, h2_s, acc_s, *, final_norm):
    j = pl.program_id(0)
    tm = x_ref.shape[0]

    @pl.when(j == 0)
    def _():
        o1 = ATTN_WIDTH
        o2 = o1 + SSM_WIDTH
        wo = wo_ref[...].astype(MXU_DTYPE)
        wo_out[...] = wo
        proj = _mm(a_ref[...], wo[0:o1]) + _mm(s_ref[...], wo[o1:o2]) + _mm(c_ref[...], wo[o2:o2 + CONV_WIDTH])
        x2 = x_ref[...] + _rows(g1_ref[...], tm) * proj
        h2 = _rms(x2, n2_ref[...]) * (1.0 + _rows(sc2_ref[...], tm)) + _rows(sh2_ref[...], tm)
        x2_s[...] = x2
        h2_s[...] = h2.astype(MXU_DTYPE)
        acc_s[...] = jnp.zeros(acc_s.shape, F32)

    wg = wg_ref[...].astype(MXU_DTYPE)
    wu = wu_ref[...].astype(MXU_DTYPE)
    wd = wd_ref[...].astype(MXU_DTYPE)
    wg_out[...] = wg
    wu_out[...] = wu
    wd_out[...] = wd
    h2 = h2_s[...]
    act = (_silu(_mm(h2, wg)) * _mm(h2, wu)).astype(MXU_DTYPE)
    acc_s[...] += _mm(act, wd)

    @pl.when(j == pl.num_programs(0) - 1)
    def _():
        y = x2_s[...] + _rows(g2_ref[...], tm) * acc_s[...]
        o_ref[...] = _rms(y, fg_ref[...]) if final_norm else y


def _conv_stages(halo_ref, cur_ref, w_ref, b_ref, g_ref, beta_ref, store, buf, shifted, first, sub):
    tc = cur_ref.shape[0]
    base = CONV_HALO - (CONV_K - 1)
    span = shifted.shape[1]

    def fill():
        buf[0:CONV_HALO, :] = jnp.where(first, 0.0, halo_ref[...])
        buf[CONV_HALO:CONV_HALO + tc, :] = cur_ref[...]
        for s in range(1, SUBLANES):
            shifted[s - 1] = buf[s:s + span, :]

    def rows(r):
        acc = jnp.zeros((sub // SUBLANES, SUBLANES, CONV_WIDTH), F32)
        for k in range(CONV_K):
            j, s = (base + k) // SUBLANES, (base + k) % SUBLANES
            start = r * sub + j * SUBLANES
            tap = buf[start:start + sub, :] if s == 0 else shifted[s - 1, start:start + sub, :]
            acc = acc + w_ref[k][None] * tap.reshape(acc.shape)
        y = _layernorm_silu(acc.reshape(sub, CONV_WIDTH) + b_ref[...], g_ref[...], beta_ref[...])
        store(slice(r * sub, (r + 1) * sub), y)

    return [fill] + [functools.partial(rows, r) for r in range(tc // sub)]


def _outffn_conv_kernel(x_ref, a_ref, s_ref, halo_ref, vcur_ref, g1_ref, sh2_ref, sc2_ref, g2_ref, n2_ref,
                        wo_ref, wg_ref, wu_ref, wd_ref, fg_ref, cw_ref, cb_ref, cg_ref, cbeta_ref,
                        o_ref, buf, shifted, conv_out, *, final_norm, n_tiles, tiles_per_batch, sub):
    j = pl.program_id(0)
    conv_tile = jnp.minimum(j, n_tiles - 1)
    first = (conv_tile % tiles_per_batch) == 0

    def conv_into(slot):
        def store(rows, y):
            conv_out[slot, rows, :] = y.astype(conv_out.dtype)
        return _conv_stages(halo_ref, vcur_ref, cw_ref, cb_ref, cg_ref, cbeta_ref, store, buf, shifted, first, sub)

    @pl.when(j == 0)
    def _():
        for stage in conv_into(0):
            stage()

    @pl.when(j > 0)
    def _():
        slot = j % 2
        conv = conv_into(slot)
        ffn = _outffn_stages(x_ref, a_ref, s_ref, lambda: conv_out[1 - slot], g1_ref, sh2_ref, sc2_ref, g2_ref,
                             n2_ref, wo_ref, wg_ref, wu_ref, wd_ref, fg_ref, o_ref, final_norm)
        conv[0]()
        per_ffn = -(-(len(conv) - 1) // len(ffn))
        for k, ffn_stage in enumerate(ffn):
            for conv_stage in conv[1 + k * per_ffn:1 + (k + 1) * per_ffn]:
                conv_stage()
            ffn_stage()


def _out_ffn_sample(x, o_attn, o_ssm, o_conv, mod, p, layer, final_norm):
    t = x.shape[0]
    fc = FF_STREAM_CHUNK
    whole = lambda *shape: pl.BlockSpec(shape, lambda j: (0,) * len(shape))
    gate_up = pl.BlockSpec((None, D_MODEL, fc), lambda j: (layer, 0, j))
    down = pl.BlockSpec((None, fc, D_MODEL), lambda j: (layer, j, 0))
    return pl.pallas_call(
        functools.partial(_outffn_stream_kernel, final_norm=final_norm),
        grid=(D_FF // fc,),
        in_specs=[
            whole(t, D_MODEL), whole(t, ATTN_WIDTH), whole(t, SSM_WIDTH), whole(t, CONV_WIDTH),
            _mod_spec(mod, layer, 2, 1), _mod_spec(mod, layer, 3, 1),
            _mod_spec(mod, layer, 4, 1), _mod_spec(mod, layer, 5, 1),
            _layer_block((1, D_MODEL), layer),
            _layer_block((D_MODEL, D_MODEL), layer), gate_up, gate_up, down,
            whole(1, D_MODEL),
        ],
        out_specs=[whole(t, D_MODEL), whole(D_MODEL, D_MODEL),
                   pl.BlockSpec((D_MODEL, fc), lambda j: (0, j)), pl.BlockSpec((D_MODEL, fc), lambda j: (0, j)),
                   pl.BlockSpec((fc, D_MODEL), lambda j: (j, 0))],
        out_shape=[jax.ShapeDtypeStruct((t, D_MODEL), F32), jax.ShapeDtypeStruct((D_MODEL, D_MODEL), MXU_DTYPE),
                   jax.ShapeDtypeStruct((D_MODEL, D_FF), MXU_DTYPE), jax.ShapeDtypeStruct((D_MODEL, D_FF), MXU_DTYPE),
                   jax.ShapeDtypeStruct((D_FF, D_MODEL), MXU_DTYPE)],
        scratch_shapes=[pltpu.VMEM((t, D_MODEL), F32), pltpu.VMEM((t, D_MODEL), MXU_DTYPE),
                        pltpu.VMEM((t, D_MODEL), F32)],
        compiler_params=_params("arbitrary"),
        name="out_projection_ffn_stream",
    )(x, o_attn, o_ssm, o_conv, mod.array, mod.array, mod.array, mod.array, p['norm2_g'],
      p['w_out'], p['w_gate'], p['w_up'], p['w_down'], p['final_g'])


def _out_ffn_conv(x, o_attn, o_ssm, vc, mod, p, weights, layer, final_norm, tm, tiles_per_batch):
    t = x.shape[0]
    n_tiles = t // tm
    halo_per_tile = tm // CONV_HALO
    ffn_tile = lambda j: jnp.maximum(j - 1, 0)
    conv_tile = lambda j: jnp.minimum(j, n_tiles - 1)
    ffn_row = lambda j: (ffn_tile(j), 0)
    resident = lambda shape: pl.BlockSpec(shape, lambda j: (0, 0), pipeline_mode=pl.Buffered(1))
    cvec = _layer_block((1, CONV_WIDTH), layer)
    span = tm + CONV_HALO - SUBLANES
    return pl.pallas_call(
        functools.partial(_outffn_conv_kernel, final_norm=final_norm, n_tiles=n_tiles,
                          tiles_per_batch=tiles_per_batch, sub=CONV_ROWS_PER_STAGE),
        grid=(n_tiles + 1,),
        in_specs=[
            pl.BlockSpec((tm, D_MODEL), ffn_row),
            pl.BlockSpec((tm, ATTN_WIDTH), ffn_row),
            pl.BlockSpec((tm, SSM_WIDTH), ffn_row),
            pl.BlockSpec((CONV_HALO, CONV_WIDTH), lambda j: (jnp.maximum(conv_tile(j) * halo_per_tile - 1, 0), 0)),
            pl.BlockSpec((tm, CONV_WIDTH), lambda j: (conv_tile(j), 0)),
            _mod_spec(mod, layer, 2, tiles_per_batch, ffn_tile),
            _mod_spec(mod, layer, 3, tiles_per_batch, ffn_tile),
            _mod_spec(mod, layer, 4, tiles_per_batch, ffn_tile),
            _mod_spec(mod, layer, 5, tiles_per_batch, ffn_tile),
            _layer_block((1, D_MODEL), layer),
            resident((D_MODEL, D_MODEL)),
            resident((D_MODEL, D_FF)),
            resident((D_MODEL, D_FF)),
            resident((D_FF, D_MODEL)),
            pl.BlockSpec((1, D_MODEL), lambda j: (0, 0)),
            _layer_block((CONV_K, SUBLANES, CONV_WIDTH), layer), cvec, cvec, cvec,
        ],
        out_specs=pl.BlockSpec((tm, D_MODEL), ffn_row),
        out_shape=jax.ShapeDtypeStruct((t, D_MODEL), F32),
        scratch_shapes=[pltpu.VMEM((CONV_HALO + tm, CONV_WIDTH), F32),
                        pltpu.VMEM((SUBLANES - 1, span, CONV_WIDTH), F32),
                        pltpu.VMEM((2, tm, CONV_WIDTH), MXU_DTYPE)],
        compiler_params=_params("arbitrary"),
        name="conv_out_projection_ffn",
    )(x, o_attn, o_ssm, vc, vc, mod.array, mod.array, mod.array, mod.array, p['norm2_g'],
      *weights, p['final_g'], p['conv_w_tiles'], p['conv_b'], p['conv_ln_g'], p['conv_ln_b'])


def _layer_prompt(x, mod, p, weights, layer, rope, batch, seq, final_norm):
    tm = min(512, seq)
    tiles_per_batch = seq // tm
    tp = min(1024, seq)
    w_in, *ffn_weights = weights
    q, kv, u, vc = _in_projection(x, mod, p, w_in, layer, rope[0], rope[1], tp, seq // tp)
    o_attn = _attention_prompt(q, kv, p['sinks'], layer, batch, seq)
    n_keep = min(WINDOW, seq)
    kv_keep = kv.reshape(batch, seq, 2 * KV_WIDTH)[:, seq - n_keep:]
    new_k = kv_keep[:, :, :KV_WIDTH].reshape(batch, n_keep, N_KV_HEADS, HEAD_DIM)
    new_v = kv_keep[:, :, KV_WIDTH:].reshape(batch, n_keep, N_KV_HEADS, HEAD_DIM)

    nseg = PROMPT_SEGMENTS
    lseg = seq // nseg
    nseq = batch * nseg
    u_seq = u.reshape(nseq, lseg, SSM_WIDTH)
    steps = min(256, lseg)
    end_re, end_im = _s5_first_segment_states(u_seq, p, layer, nseg)
    start = lambda e: jnp.concatenate([jnp.zeros_like(e), e], axis=1).reshape(nseq, SSM_FLAT)
    o_seq, h_re, h_im = _s5_scan(u_seq, start(end_re), start(end_im), p, layer, nseq, steps)
    o_ssm = o_seq.reshape(batch * seq, SSM_WIDTH)
    last = lambda h: h.reshape(batch, nseg, N_SSM_GROUPS, SSM_STATE)[:, nseg - 1]

    new_conv = vc.reshape(batch, seq, CONV_WIDTH)[:, seq - (CONV_K - 1):]

    x = _out_ffn_conv(x, o_attn, o_ssm, vc, mod, p, ffn_weights, layer, final_norm, tm, tiles_per_batch)
    return x, new_k, new_v, last(h_re), last(h_im), new_conv


def _layer_sample(x, mod, p, layer, rope, batch, n_new, kt_all, vt_all, h0_re, h0_im, conv_all, final_norm):
    t = batch * n_new
    qexp, kn, vn, u, vc, w_in = _in_projection(x, mod, p, p['w_in'], layer, rope[0], rope[1], t, 1, n_new)
    o, new_k, new_v = _attention_sample(
        qexp.reshape(batch, N_Q_HEADS * n_new, LANES), kt_all, vt_all,
        kn.reshape(batch, SUBLANES, LANES), vn.reshape(batch, SUBLANES, LANES), p['sinks'], layer, n_new)
    o = o[:, :, :HEAD_DIM].reshape(batch, N_Q_HEADS, n_new, HEAD_DIM)
    o_attn = jnp.transpose(o, (2, 0, 1, 3)).reshape(t, ATTN_WIDTH)

    o_ssm, h_re, h_im = _s5_scan(u, h0_re, h0_im, p, layer, batch, n_new)
    st = lambda h: h

    o_conv, new_conv = _conv_sample(conv_all, vc.reshape(n_new, batch, CONV_WIDTH), p, layer)
    o_conv = o_conv.reshape(t, CONV_WIDTH)

    x, *ffn_weights = _out_ffn_sample(x, o_attn, o_ssm, o_conv, mod, p, layer, final_norm)
    return x, [w_in] + ffn_weights, new_k, new_v, st(h_re), st(h_im), new_conv


def kernel(x_prompt, x_sample, c_prompt, c_sample, cache_k, cache_v, state_ssm_re, state_ssm_im, state_conv,
           norm1_g, norm2_g, w_mod, b_mod, w_in, attn_sinks, ssm_lam_re, ssm_lam_im, ssm_log_dt,
           ssm_b_re, ssm_b_im, ssm_c_re, ssm_c_im, ssm_d, ssm_w_glu, ssm_b_glu,
           conv_w, conv_b, conv_ln_g, conv_ln_b, w_out, w_gate, w_up, w_down, final_norm_g):
    bp, seq, d = x_prompt.shape
    bs, n_new, _ = x_sample.shape
    depth = w_in.shape[0]
    assert PROMPT_SEGMENTS == 2 and seq % (PROMPT_SEGMENTS * SUBLANES) == 0

    c_all = jnp.concatenate([c_sample, c_prompt], axis=0)
    pad_rows = -c_all.shape[0] % SUBLANES
    mods = _modulation(jnp.pad(c_all, ((0, pad_rows), (0, 0))), w_mod, b_mod)
    mod_p = _Mod(mods[:, bs:bs + bp].reshape(depth, bp, 1, N_MOD * d), None)
    mod_s = _Mod(mods, bs)

    a_re, a_im, bre_blk, bim_blk, cre_blk, cim_blk, decay_re, decay_im, span_re, span_im = _s5_discretise(
        ssm_lam_re, ssm_lam_im, ssm_log_dt, ssm_b_re, ssm_b_im, ssm_c_re, ssm_c_im,
        min(S5_STATE_CHUNK, seq // PROMPT_SEGMENTS))
    rope_p = _rope_tables(jnp.arange(seq))
    rope_s = _rope_tables(jnp.repeat(PAST_LEN + jnp.arange(n_new), bs))

    cast = lambda a: a.astype(MXU_DTYPE)
    vec = lambda a: a.reshape(depth, 1, a.shape[-1])
    p = {
        'norm1_g': vec(norm1_g), 'norm2_g': vec(norm2_g), 'w_in': w_in, 'sinks': attn_sinks,
        'a_re': a_re, 'a_im': a_im,
        'bre_blk': bre_blk, 'bim_blk': bim_blk,
        'cre_blk': cre_blk, 'cim_blk': cim_blk,
        'decay_re': decay_re, 'decay_im': decay_im, 'span_re': span_re, 'span_im': span_im,
        'ssm_d': vec(ssm_d), 'ssm_w_glu': cast(ssm_w_glu), 'ssm_b_glu': vec(ssm_b_glu),
        'conv_w': conv_w, 'conv_w_tiles': jnp.repeat(conv_w[:, :, None, :], SUBLANES, axis=2),
        'conv_b': vec(conv_b), 'conv_ln_g': vec(conv_ln_g), 'conv_ln_b': vec(conv_ln_b),
        'w_out': w_out, 'w_gate': w_gate, 'w_up': w_up, 'w_down': w_down,
        'final_g': final_norm_g.reshape(1, d),
    }
    conv_all = jnp.transpose(state_conv, (0, 2, 1, 3))
    kt_all = jnp.transpose(cache_k, (0, 1, 3, 4, 2))
    vt_all = jnp.transpose(cache_v, (0, 1, 3, 4, 2))
    state_major = lambda a: jnp.transpose(a, (0, 2, 3, 1)).reshape(depth, SSM_FLAT, bs)
    batch_major = lambda a: jnp.transpose(a.reshape(depth, N_SSM_GROUPS, SSM_STATE, bs), (0, 3, 1, 2))
    ssm_re_all, ssm_im_all = state_major(state_ssm_re), state_major(state_ssm_im)

    xp = x_prompt.reshape(bp * seq, d)
    xs = jnp.transpose(x_sample, (1, 0, 2)).reshape(n_new * bs, d)
    outs_p, outs_s = [], []
    for l in range(depth):
        final = l == depth - 1
        xs, weights, *os_ = _layer_sample(xs, mod_s, p, l, rope_s, bs, n_new, kt_all, vt_all,
                                          ssm_re_all, ssm_im_all, conv_all, final)
        xp, *op = _layer_prompt(xp, mod_p, p, weights, l, rope_p, bp, seq, final)
        outs_p.append(op)
        outs_s.append(os_)
    stack = lambda outs, i: jnp.stack([o[i] for o in outs])
    y_sample = jnp.transpose(xs.reshape(n_new, bs, d), (1, 0, 2))
    new_k_s = jnp.transpose(stack(outs_s, 0), (0, 1, 4, 2, 3))
    new_v_s = jnp.transpose(stack(outs_s, 1), (0, 1, 4, 2, 3))
    new_conv_s = jnp.transpose(stack(outs_s, 4), (0, 2, 1, 3))
    return (xp.reshape(bp, seq, d), y_sample, *[stack(outs_p, i) for i in range(5)],
            new_k_s, new_v_s, batch_major(stack(outs_s, 2)), batch_major(stack(outs_s, 3)), new_conv_s)
```

```python
import functools
import math
from typing import NamedTuple

import jax
import jax.numpy as jnp
from jax import lax
from jax.experimental import pallas as pl
from jax.experimental.pallas import tpu as pltpu

F32 = jnp.float32
MXU_DTYPE = jnp.bfloat16

V7X_VMEM_BYTES = 64 * 1024 * 1024
VMEM_LIMIT_BYTES = V7X_VMEM_BYTES - 8 * 1024 * 1024
LANES = 128
SUBLANES = 8
V7X_MXU_DIM = 256

D_MODEL = 1024
HEAD_DIM = 64
N_Q_HEADS = 8
N_KV_HEADS = 2
Q_PER_KV = N_Q_HEADS // N_KV_HEADS
ATTN_WIDTH = N_Q_HEADS * HEAD_DIM
KV_WIDTH = N_KV_HEADS * HEAD_DIM
WINDOW = 128
ROPE_THETA = 10000.0
ATTN_SCALE = 1.0 / math.sqrt(HEAD_DIM)
SSM_WIDTH = 256
SSM_GROUP = 16
N_SSM_GROUPS = 16
SSM_STATE = 64
SSM_FLAT = N_SSM_GROUPS * SSM_STATE
CONV_WIDTH = 256
CONV_K = 31
CONV_HALO = 32
CONV_ROWS_PER_STAGE = 64
IN_WIDTH = ATTN_WIDTH + 2 * KV_WIDTH + SSM_WIDTH + 2 * CONV_WIDTH
D_FF = 2816
FF_CHUNK = 6 * V7X_MXU_DIM
FF_STREAM_CHUNK = V7X_MXU_DIM
EPS = 1e-6
NEG = -1e30
N_MOD = 6
PROMPT_SEGMENTS = 2
S5_STATE_CHUNK = 1024
PAST_LEN = 8192


def _params(*semantics):
    return pltpu.CompilerParams(dimension_semantics=semantics, vmem_limit_bytes=VMEM_LIMIT_BYTES)


def _layer_block(shape, layer, **kw):
    zeros = (0,) * len(shape)
    return pl.BlockSpec((None,) + tuple(shape), lambda *_: (layer,) + zeros, **kw)


def _sigmoid(x):
    return 1.0 / (1.0 + jnp.exp(-x))


def _silu(x):
    return x * _sigmoid(x)


def _rms(x, g):
    return x * lax.rsqrt(jnp.mean(x * x, axis=-1, keepdims=True) + EPS) * g


def _mm(a, b):
    return jnp.dot(a, b, preferred_element_type=F32)


def _mm_nt(a, b):
    return lax.dot_general(a, b, (((1,), (1,)), ((), ())), preferred_element_type=F32)


def _rows(m, n):
    return m if m.shape[0] == 1 else jnp.concatenate([m] * (n // m.shape[0]), axis=0)


def _mod_kernel(c_ref, w_ref, b_ref, o_ref):
    a = _silu(c_ref[...]).astype(MXU_DTYPE)
    o_ref[...] = _mm(a, w_ref[...].astype(MXU_DTYPE)) + b_ref[...]


def _modulation(c, w_mod, b_mod):
    depth, d, n = w_mod.shape
    rows = c.shape[0]
    tn = 3072
    return pl.pallas_call(
        _mod_kernel,
        grid=(depth, n // tn),
        in_specs=[
            pl.BlockSpec((rows, d), lambda l, j: (0, 0)),
            pl.BlockSpec((None, d, tn), lambda l, j: (l, 0, j)),
            pl.BlockSpec((None, 1, tn), lambda l, j: (l, 0, j)),
        ],
        out_specs=pl.BlockSpec((None, rows, tn), lambda l, j: (l, 0, j)),
        out_shape=jax.ShapeDtypeStruct((depth, rows, n), F32),
        compiler_params=_params("parallel", "parallel"),
        name="modulation",
    )(c, w_mod, b_mod.reshape(depth, 1, n))


class _Mod(NamedTuple):
    array: jax.Array
    batch_rows: int | None


def _mod_spec(mod, layer, chunk, tiles_per_batch, tile=lambda i: i):
    if mod.batch_rows is None:
        return pl.BlockSpec((None, None, 1, D_MODEL), lambda i: (layer, tile(i) // tiles_per_batch, 0, chunk))
    return pl.BlockSpec((None, mod.batch_rows, D_MODEL), lambda i: (layer, 0, chunk))


def _inproj_kernel(x_ref, sh_ref, sc_ref, g_ref, w_ref, cos_ref, sin_ref, *outs, n_new):
    tm = x_ref.shape[0]
    h = _rms(x_ref[...], g_ref[...]) * (1.0 + _rows(sc_ref[...], tm)) + _rows(sh_ref[...], tm)
    w = w_ref[...].astype(MXU_DTYPE)
    z = _mm(h.astype(MXU_DTYPE), w)
    cos = cos_ref[...]
    sin = sin_ref[...]
    lane = lax.broadcasted_iota(jnp.int32, (tm, LANES), 1)
    first_half = (lane % HEAD_DIM) < (HEAD_DIM // 2)

    def rope(t):
        partner = jnp.where(first_half, pltpu.roll(t, LANES - HEAD_DIM // 2, 1),
                            pltpu.roll(t, HEAD_DIM // 2, 1))
        return t * cos + partner * sin

    q_cols = [rope(z[:, j * LANES:(j + 1) * LANES]) * ATTN_SCALE for j in range(ATTN_WIDTH // LANES)]
    o = ATTN_WIDTH
    k_rot = rope(z[:, o:o + KV_WIDTH])
    v_new = z[:, o + KV_WIDTH:o + 2 * KV_WIDTH]
    if n_new is None:
        q_ref, kv_ref, u_ref, vc_ref = outs
        for j, qc in enumerate(q_cols):
            q_ref[:, j * LANES:(j + 1) * LANES] = qc.astype(q_ref.dtype)
        kv_ref[:, 0:KV_WIDTH] = k_rot
        kv_ref[:, KV_WIDTH:2 * KV_WIDTH] = v_new
    else:
        q_ref, kn_ref, vn_ref, u_ref, vc_ref, w_out = outs
        w_out[...] = w
        batch = tm // n_new
        low = lax.broadcasted_iota(jnp.int32, (batch, LANES), 1) < HEAD_DIM
        kn_ref[...] = jnp.zeros(kn_ref.shape, F32)
        vn_ref[...] = jnp.zeros(vn_ref.shape, F32)
        for t in range(n_new):
            rows = slice(t * batch, (t + 1) * batch)
            tile_row = pl.ds(SUBLANES - n_new + t, batch, stride=SUBLANES)
            kn_ref[tile_row, :] = k_rot[rows]
            vn_ref[tile_row, :] = v_new[rows]
            for h in range(N_Q_HEADS):
                kvh = h // Q_PER_KV
                piece = q_cols[h // 2][rows]
                if h % 2 != kvh:
                    piece = pltpu.roll(piece, HEAD_DIM, 1)
                piece = jnp.where(low if kvh == 0 else ~low, piece, 0.0)
                q_ref[pl.ds(h * n_new + t, batch, stride=N_Q_HEADS * n_new), :] = piece
    o += 2 * KV_WIDTH
    u_ref[...] = z[:, o:o + SSM_WIDTH]
    o += SSM_WIDTH
    za = z[:, o:o + CONV_WIDTH]
    zg = z[:, o + CONV_WIDTH:o + 2 * CONV_WIDTH]
    vc_ref[...] = za * _sigmoid(zg)


def _in_projection(x, mod, p, w_in, layer, cos, sin, tm, tiles_per_batch, n_new=None):
    t = x.shape[0]
    pos_tiles = cos.shape[0] // tm
    row = lambda i: (i, 0)
    whole = lambda *shape: pl.BlockSpec(shape, lambda i: (0,) * len(shape))
    if n_new is None:
        w_spec = whole(D_MODEL, IN_WIDTH)
        attn_specs = [pl.BlockSpec((tm, ATTN_WIDTH), row), pl.BlockSpec((tm, 2 * KV_WIDTH), row)]
        attn_shapes = [jax.ShapeDtypeStruct((t, ATTN_WIDTH), MXU_DTYPE),
                       jax.ShapeDtypeStruct((t, 2 * KV_WIDTH), F32)]
        w_out_spec, w_out_shape = [], []
    else:
        assert tm == t and KV_WIDTH == LANES
        w_spec = _layer_block((D_MODEL, IN_WIDTH), layer)
        batch = t // n_new
        shapes = [(batch * N_Q_HEADS * n_new, LANES), (batch * SUBLANES, LANES), (batch * SUBLANES, LANES)]
        attn_specs = [whole(*s) for s in shapes]
        attn_shapes = [jax.ShapeDtypeStruct(s, F32) for s in shapes]
        w_out_spec = [whole(D_MODEL, IN_WIDTH)]
        w_out_shape = [jax.ShapeDtypeStruct((D_MODEL, IN_WIDTH), MXU_DTYPE)]
    return pl.pallas_call(
        functools.partial(_inproj_kernel, n_new=n_new),
        grid=(t // tm,),
        in_specs=[
            pl.BlockSpec((tm, D_MODEL), row),
            _mod_spec(mod, layer, 0, tiles_per_batch),
            _mod_spec(mod, layer, 1, tiles_per_batch),
            _layer_block((1, D_MODEL), layer),
            w_spec,
            pl.BlockSpec((tm, LANES), lambda i: (i % pos_tiles, 0)),
            pl.BlockSpec((tm, LANES), lambda i: (i % pos_tiles, 0)),
        ],
        out_specs=(attn_specs + [pl.BlockSpec((tm, SSM_WIDTH), row), pl.BlockSpec((tm, CONV_WIDTH), row)]
                   + w_out_spec),
        out_shape=(attn_shapes + [jax.ShapeDtypeStruct((t, SSM_WIDTH), F32),
                                  jax.ShapeDtypeStruct((t, CONV_WIDTH), F32)] + w_out_shape),
        compiler_params=_params("parallel"),
        name="in_projection",
    )(x, mod.array, mod.array, p['norm1_g'], w_in, cos, sin)


def _rope_tables(pos):
    half = HEAD_DIM // 2
    inv_freq = ROPE_THETA ** (-jnp.arange(half, dtype=F32) / half)
    ang = pos.astype(F32)[:, None] * inv_freq[None, :]
    cos = jnp.tile(jnp.cos(ang), (1, LANES // half))
    sin = jnp.sin(ang)
    sin = jnp.tile(jnp.concatenate([-sin, sin], axis=1), (1, LANES // HEAD_DIM))
    return cos, sin


def _sink_softmax(s, sink_col):
    m = jnp.maximum(jnp.max(s, axis=-1, keepdims=True), sink_col)
    e = jnp.exp(s - m)
    return e * (1.0 / (jnp.sum(e, axis=-1, keepdims=True) + jnp.exp(sink_col - m)))


def _head_pair_select(x, pick_second):
    lane = lax.broadcasted_iota(jnp.int32, x.shape, 1)
    swapped = pltpu.roll(x, HEAD_DIM, 1)
    low = lane < HEAD_DIM
    return jnp.where(low, swapped, x) if pick_second else jnp.where(low, x, swapped)


def _attn_prompt_kernel(sink_ref, q_ref, kvp_ref, kvc_ref, o_ref, *, layer):
    n = pl.program_id(1)
    w = WINDOW
    n_blocks = q_ref.shape[0] // w
    rows = Q_PER_KV * w
    r_idx = lax.broadcasted_iota(jnp.int32, (rows, w), 0) % w
    c_idx = lax.broadcasted_iota(jnp.int32, (rows, w), 1)
    from_prev = c_idx > r_idx
    first_bias = jnp.where(n > 0, 0.0, NEG)
    row_head = lax.broadcasted_iota(jnp.int32, (rows, 1), 0) // w
    low = lax.broadcasted_iota(jnp.int32, (w, LANES), 1) < HEAD_DIM
    half_mask = [jnp.where(low, 1.0, 0.0).astype(MXU_DTYPE), jnp.where(low, 0.0, 1.0).astype(MXU_DTYPE)]
    sink_cols = []
    for kvh in range(N_KV_HEADS):
        sink_col = jnp.zeros((rows, 1), F32)
        for g in range(Q_PER_KV):
            sink_col = jnp.where(row_head == g, sink_ref[layer, kvh * Q_PER_KV + g], sink_col)
        sink_cols.append(sink_col)
    chains = [(i, kvh) for i in range(n_blocks) for kvh in range(N_KV_HEADS)]
    windows, values = [], []
    for i, kvh in chains:
        own = slice(i * w, (i + 1) * w)
        prev_ref, prev = (kvp_ref, slice(0, w)) if i == 0 else (kvc_ref, slice((i - 1) * w, i * w))
        kk = jnp.concatenate([prev_ref[prev, 0:KV_WIDTH], kvc_ref[own, 0:KV_WIDTH]], axis=0)
        vv = jnp.concatenate([prev_ref[prev, KV_WIDTH:2 * KV_WIDTH], kvc_ref[own, KV_WIDTH:2 * KV_WIDTH]], axis=0)
        k2 = _head_pair_select(kk, kvh == 1).astype(MXU_DTYPE)
        values.append(_head_pair_select(vv, kvh == 1).astype(MXU_DTYPE))
        pieces = []
        for g in range(Q_PER_KV):
            h = kvh * Q_PER_KV + g
            qcol = q_ref[own, (h // 2) * LANES:(h // 2 + 1) * LANES]
            pieces.append(qcol * half_mask[h % 2])
        s = _mm_nt(jnp.concatenate(pieces, axis=0), k2)
        s_prev = s[:, 0:w] + first_bias if i == 0 else s[:, 0:w]
        windows.append(jnp.where(from_prev, s_prev, s[:, w:2 * w]))
    probs = []
    for (i, kvh), sc in zip(chains, windows):
        p = _sink_softmax(sc, sink_cols[kvh])
        p2 = jnp.concatenate([jnp.where(from_prev, p, 0.0), jnp.where(from_prev, 0.0, p)], axis=1)
        probs.append(p2.astype(MXU_DTYPE))
    for (i, kvh), p2, v2 in zip(chains, probs, values):
        own = slice(i * w, (i + 1) * w)
        r = _mm(p2, v2)
        for j in range(Q_PER_KV // 2):
            col = kvh * (Q_PER_KV // 2) + j
            o_ref[own, col * LANES:(col + 1) * LANES] = jnp.where(
                low, r[2 * j * w:(2 * j + 1) * w], r[(2 * j + 1) * w:(2 * j + 2) * w]).astype(o_ref.dtype)


def _attention_prompt(q, kv, sinks, layer, batch, seq):
    tq = min(8 * WINDOW, seq)
    nt = seq // tq
    per = tq // WINDOW
    return pl.pallas_call(
        functools.partial(_attn_prompt_kernel, layer=layer),
        grid=(batch, nt),
        in_specs=[
            pl.BlockSpec(memory_space=pltpu.SMEM),
            pl.BlockSpec((tq, ATTN_WIDTH), lambda b, n: (b * nt + n, 0)),
            pl.BlockSpec((WINDOW, 2 * KV_WIDTH), lambda b, n: ((b * nt + n) * per - jnp.minimum(n, 1), 0)),
            pl.BlockSpec((tq, 2 * KV_WIDTH), lambda b, n: (b * nt + n, 0)),
        ],
        out_specs=pl.BlockSpec((tq, ATTN_WIDTH), lambda b, n: (b * nt + n, 0)),
        out_shape=jax.ShapeDtypeStruct((batch * seq, ATTN_WIDTH), MXU_DTYPE),
        compiler_params=_params("parallel", "parallel"),
        name="attention_prompt",
    )(sinks, q, kv, kv)


def _attn_sample_kernel(sink_ref, q_ref, kt_ref, vt_ref, kn_ref, vn_ref, o_ref, *, layer, n_new):
    bt, rows, _ = q_ref.shape
    nkv, hd, wb = kt_ref.shape[1:]
    kept = wb - n_new
    r = lax.broadcasted_iota(jnp.int32, (bt * rows, 2 * wb), 0)
    j = lax.broadcasted_iota(jnp.int32, (bt * rows, 2 * wb), 1)
    t_idx = r % n_new
    t_new = j - wb - kept
    mask = ((j < wb) & (j > t_idx)) | ((t_new >= 0) & (t_new <= t_idx))
    row_head = (lax.broadcasted_iota(jnp.int32, (bt * rows, 1), 0) % rows) // n_new
    sink_col = jnp.zeros((bt * rows, 1), F32)
    for h in range(N_Q_HEADS):
        sink_col = jnp.where(row_head == h, sink_ref[layer, h], sink_col)
    second_group = lax.broadcasted_iota(jnp.int32, (rows, nkv * hd), 0) >= (rows // nkv)
    flat = lambda a: a.reshape(nkv * hd, wb)
    above = jnp.zeros((wb - kn_ref.shape[1], nkv * hd), F32)

    scores, values = [], []
    for b in range(bt):
        kt, vt = kt_ref[b], vt_ref[b]
        kpos = jnp.concatenate([above, kn_ref[b]], axis=0).T
        vpos = jnp.concatenate([above, vn_ref[b]], axis=0).T
        kcat = jnp.concatenate([flat(kt), kpos], axis=1).astype(MXU_DTYPE)
        values.append(jnp.concatenate([flat(vt), vpos], axis=1).astype(MXU_DTYPE))
        scores.append(_mm(q_ref[b].astype(MXU_DTYPE), kcat))
    s = jnp.where(mask, jnp.concatenate(scores, axis=0), NEG)
    p = _sink_softmax(s, sink_col).astype(MXU_DTYPE)
    for b in range(bt):
        o = _mm_nt(p[b * rows:(b + 1) * rows], values[b])
        o_ref[b] = jnp.where(second_group, pltpu.roll(o, hd, 1), o).astype(o_ref.dtype)


def _attention_sample(qexp, kt_all, vt_all, kn, vn, sinks, layer, n_new):
    batch, rows, width = qexp.shape
    _, _, nkv, hd, wb = kt_all.shape
    bt = min(32, batch)
    blk3 = pl.BlockSpec((bt, rows, width), lambda i: (i, 0, 0))
    new = pl.BlockSpec((bt,) + kn.shape[1:], lambda i: (i, 0, 0))
    cache = pl.BlockSpec((None, bt, nkv, hd, wb), lambda i: (layer, i, 0, 0, 0))
    return pl.pallas_call(
        functools.partial(_attn_sample_kernel, layer=layer, n_new=n_new),
        grid=(batch // bt,),
        in_specs=[pl.BlockSpec(memory_space=pltpu.SMEM), blk3, cache, cache, new, new],
        out_specs=blk3,
        out_shape=jax.ShapeDtypeStruct((batch, rows, width), MXU_DTYPE),
        compiler_params=_params("parallel"),
        name="attention_sample",
    )(sinks, qexp, kt_all, vt_all, kn, vn)


def _cache_update_kernel(kt_ref, vt_ref, kn_ref, vn_ref, nk_ref, nv_ref, *, n_new):
    bt, nkv, hd, wb = kt_ref.shape
    kept = wb - n_new
    keep = lax.broadcasted_iota(jnp.int32, (nkv, hd, wb), 2) < kept
    above = jnp.zeros((wb - kn_ref.shape[1], nkv * hd), F32)
    for old_ref, new_rows_ref, out_ref in ((kt_ref, kn_ref, nk_ref), (vt_ref, vn_ref, nv_ref)):
        for b in range(bt):
            placed = jnp.concatenate([above, new_rows_ref[b]], axis=0).T.reshape(nkv, hd, wb)
            out_ref[b] = jnp.where(keep, pltpu.roll(old_ref[b], kept, 2), placed)


def _cache_update(kt_all, vt_all, kn_all, vn_all, n_new):
    depth, batch, nkv, hd, wb = kt_all.shape
    bt = min(16, batch)
    cache = pl.BlockSpec((None, bt, nkv, hd, wb), lambda l, i: (l, i, 0, 0, 0))
    new = pl.BlockSpec((None, bt) + kn_all.shape[2:], lambda l, i: (l, i, 0, 0))
    return pl.pallas_call(
        functools.partial(_cache_update_kernel, n_new=n_new),
        grid=(depth, batch // bt),
        in_specs=[cache, cache, new, new],
        out_specs=[cache, cache],
        out_shape=[jax.ShapeDtypeStruct(kt_all.shape, F32)] * 2,
        compiler_params=_params("parallel", "parallel"),
        name="cache_update",
    )(kt_all, vt_all, kn_all, vn_all)


def _cmul(ar, ai, br, bi):
    return ar * br - ai * bi, ar * bi + ai * br


def _s5_discretise_kernel(lr_ref, li_ref, ldt_ref, bre_ref, bim_ref, cre_ref, cim_ref,
                          are_ref, aim_ref, bbre_ref, bbim_ref, ccre_ref, ccim_ref,
                          decay_re_ref, decay_im_ref, span_re_ref, span_im_ref):
    lr = lr_ref[...]
    li = li_ref[...]
    dt = jnp.exp(ldt_ref[...])
    mag = jnp.exp(lr * dt)
    ab_re = mag * jnp.cos(li * dt)
    ab_im = mag * jnp.sin(li * dt)
    den = lr * lr + li * li
    nr = ab_re - 1.0
    coef_re = (nr * lr + ab_im * li) / den
    coef_im = (ab_im * lr - nr * li) / den
    are_ref[...] = ab_re
    aim_ref[...] = ab_im
    br = bre_ref[...]
    bi = bim_ref[...]
    bb_re = coef_re * br - coef_im * bi
    bb_im = coef_re * bi + coef_im * br
    lane_group = lax.broadcasted_iota(jnp.int32, bb_re.shape, 1) // SSM_STATE
    for src, dst in ((bb_re, bbre_ref), (bb_im, bbim_ref), (cre_ref[...], ccre_ref), (cim_ref[...], ccim_ref)):
        for g in range(N_SSM_GROUPS):
            dst[g * SSM_GROUP:(g + 1) * SSM_GROUP, :] = jnp.where(lane_group == g, src, 0.0).astype(dst.dtype)
    t_chunk = decay_re_ref.shape[0]
    row = lax.broadcasted_iota(jnp.int32, (SUBLANES, SSM_FLAT), 0)
    pr, pi = jnp.ones_like(ab_re), jnp.zeros_like(ab_re)
    tile_re, tile_im = jnp.zeros((SUBLANES, SSM_FLAT), F32), jnp.zeros((SUBLANES, SSM_FLAT), F32)
    for k in range(SUBLANES):
        tile_re = jnp.where(row == SUBLANES - 1 - k, pr, tile_re)
        tile_im = jnp.where(row == SUBLANES - 1 - k, pi, tile_im)
        pr, pi = _cmul(pr, pi, ab_re, ab_im)
    decay_re_ref[t_chunk - SUBLANES:t_chunk, :] = tile_re
    decay_im_ref[t_chunk - SUBLANES:t_chunk, :] = tile_im
    n = SUBLANES
    while n < t_chunk:
        lo, hi = slice(t_chunk - 2 * n, t_chunk - n), slice(t_chunk - n, t_chunk)
        decay_re_ref[lo, :], decay_im_ref[lo, :] = _cmul(decay_re_ref[hi, :], decay_im_ref[hi, :], pr, pi)
        pr, pi = _cmul(pr, pi, pr, pi)
        n *= 2
    span_re_ref[...] = pr
    span_im_ref[...] = pi


def _s5_discretise(lam_re, lam_im, log_dt, b_re, b_im, c_re, c_im, t_chunk):
    depth = lam_re.shape[0]
    assert t_chunk % SUBLANES == 0 and (t_chunk // SUBLANES) & (t_chunk // SUBLANES - 1) == 0
    flat = lambda a: a.reshape(depth, 1, SSM_FLAT)
    ldt = jnp.broadcast_to(log_dt[:, :, None], lam_re.shape)
    bt = lambda a: jnp.transpose(a, (0, 3, 1, 2)).reshape(depth, SSM_GROUP, SSM_FLAT)
    ct = lambda a: jnp.transpose(a, (0, 2, 1, 3)).reshape(depth, SSM_GROUP, SSM_FLAT)
    vec = pl.BlockSpec((None, 1, SSM_FLAT), lambda l: (l, 0, 0))
    mat = pl.BlockSpec((None, SSM_GROUP, SSM_FLAT), lambda l: (l, 0, 0))
    blk = pl.BlockSpec((None, SSM_WIDTH, SSM_FLAT), lambda l: (l, 0, 0))
    tab = pl.BlockSpec((None, t_chunk, SSM_FLAT), lambda l: (l, 0, 0))
    return pl.pallas_call(
        _s5_discretise_kernel,
        grid=(depth,),
        in_specs=[vec, vec, vec, mat, mat, mat, mat],
        out_specs=[vec, vec, blk, blk, blk, blk, tab, tab, vec, vec],
        out_shape=[jax.ShapeDtypeStruct((depth, 1, SSM_FLAT), F32)] * 2
        + [jax.ShapeDtypeStruct((depth, SSM_WIDTH, SSM_FLAT), MXU_DTYPE)] * 4
        + [jax.ShapeDtypeStruct((depth, t_chunk, SSM_FLAT), F32)] * 2
        + [jax.ShapeDtypeStruct((depth, 1, SSM_FLAT), F32)] * 2,
        compiler_params=_params("parallel"),
        name="s5_discretise",
    )(flat(lam_re), flat(lam_im), flat(ldt), bt(b_re), bt(b_im), ct(c_re), ct(c_im))


def _s5_first_segment_kernel(u_ref, bre_ref, bim_ref, dre_ref, dim_ref, sre_ref, sim_ref, ere_ref, eim_ref,
                             hre_s, him_s):
    @pl.when(pl.program_id(1) == 0)
    def _():
        hre_s[...] = jnp.zeros(hre_s.shape, F32)
        him_s[...] = jnp.zeros(him_s.shape, F32)

    ub = u_ref[...].astype(MXU_DTYPE)
    wr, wi = _cmul(dre_ref[...], dim_ref[...], _mm(ub, bre_ref[...]), _mm(ub, bim_ref[...]))
    cr, ci = _cmul(sre_ref[...], sim_ref[...], hre_s[...], him_s[...])
    hr = cr + jnp.sum(wr, axis=0, keepdims=True)
    hi = ci + jnp.sum(wi, axis=0, keepdims=True)
    hre_s[...] = hr
    him_s[...] = hi
    ere_ref[...] = hr
    eim_ref[...] = hi


def _s5_first_segment_states(u_seq, p, layer, nseg):
    nseq, lseg, _ = u_seq.shape
    batch = nseq // nseg
    t_chunk = p['decay_re'].shape[1]
    vec = _layer_block((1, SSM_FLAT), layer)
    tab = _layer_block((t_chunk, SSM_FLAT), layer)
    bmat = _layer_block((SSM_WIDTH, SSM_FLAT), layer)
    out = pl.BlockSpec((None, 1, SSM_FLAT), lambda b, c: (b, 0, 0))
    return pl.pallas_call(
        _s5_first_segment_kernel,
        grid=(batch, lseg // t_chunk),
        in_specs=[pl.BlockSpec((None, t_chunk, SSM_WIDTH), lambda b, c: (b * nseg, c, 0)),
                  bmat, bmat, tab, tab, vec, vec],
        out_specs=[out, out],
        out_shape=[jax.ShapeDtypeStruct((batch, 1, SSM_FLAT), F32)] * 2,
        scratch_shapes=[pltpu.VMEM((1, SSM_FLAT), F32)] * 2,
        compiler_params=_params("parallel", "arbitrary"),
        name="s5_first_segment_states",
    )(u_seq, p['bre_blk'], p['bim_blk'], p['decay_re'], p['decay_im'], p['span_re'], p['span_im'])


def _s5_kernel(u_ref, hre0_ref, him0_ref, are_ref, aim_ref, bre_ref, bim_ref, cre_ref, cim_ref, d_ref,
               wglu_ref, bglu_ref, o_ref, hre_out, him_out, hre_s, him_s, bure_s, buim_s, hsre_s, hsim_s,
               *stage, nseq):
    seq_major = bool(stage)
    steps = u_ref.shape[1] if seq_major else u_ref.shape[0] // nseq
    lane_halves = [slice(h * LANES, (h + 1) * LANES) for h in range(SSM_WIDTH // LANES)]

    @pl.when(pl.program_id(0) == 0)
    def _():
        hre_s[...] = hre0_ref[...] if hre0_ref.shape == hre_s.shape else hre0_ref[...].T
        him_s[...] = him0_ref[...] if him0_ref.shape == him_s.shape else him0_ref[...].T

    if seq_major:
        stage_s, = stage
        for s in range(nseq):
            for h, cols in enumerate(lane_halves):
                stage_s[h, pl.ds(s, steps, stride=nseq), :] = u_ref[s, :, cols]
        u = jnp.concatenate([stage_s[h] for h in range(len(lane_halves))], axis=1)
    else:
        u = u_ref[...]
    ub = u.astype(MXU_DTYPE)
    bure_s[...] = _mm(ub, bre_ref[...])
    buim_s[...] = _mm(ub, bim_ref[...])
    ar = jnp.broadcast_to(are_ref[...], (nseq, SSM_FLAT))
    ai = jnp.broadcast_to(aim_ref[...], (nseq, SSM_FLAT))

    def step(t, carry):
        hr, hi = carry
        rows = pl.ds(pl.multiple_of(t * nseq, nseq), nseq)
        nhr = ar * hr - ai * hi + bure_s[rows, :]
        nhi = ar * hi + ai * hr + buim_s[rows, :]
        hsre_s[rows, :] = nhr
        hsim_s[rows, :] = nhi
        return nhr, nhi

    hr, hi = lax.fori_loop(0, steps, step, (hre_s[...], him_s[...]), unroll=min(steps, 8))
    hre_s[...] = hr
    him_s[...] = hi
    hre_out[...] = hr if hre_out.shape == hr.shape else hr.T
    him_out[...] = hi if him_out.shape == hi.shape else hi.T
    y = (_mm_nt(hsre_s[...].astype(MXU_DTYPE), cre_ref[...])
         - _mm_nt(hsim_s[...].astype(MXU_DTYPE), cim_ref[...]))
    z = jax.nn.gelu(y + d_ref[...] * u)
    gate = _mm(z.astype(MXU_DTYPE), wglu_ref[...]) + bglu_ref[...]
    out = z * _sigmoid(gate)
    if seq_major:
        for h, cols in enumerate(lane_halves):
            stage_s[h] = out[:, cols]
        for s in range(nseq):
            for h, cols in enumerate(lane_halves):
                o_ref[s, :, cols] = stage_s[h, pl.ds(s, steps, stride=nseq), :].astype(o_ref.dtype)
    else:
        o_ref[...] = out.astype(o_ref.dtype)


def _s5_scan(u, hre0, him0, p, layer, nseq, steps_per_tile):
    seq_major = u.ndim == 3
    rows = u.shape[0] * u.shape[1] if seq_major else u.shape[0]
    tr = steps_per_tile * nseq
    if hre0.ndim == 3:
        state_in = _layer_block((SSM_FLAT, nseq), layer)
        state = pl.BlockSpec((SSM_FLAT, nseq), lambda i: (0, 0))
    else:
        state_in = state = pl.BlockSpec((nseq, SSM_FLAT), lambda i: (0, 0))
    if seq_major:
        io_spec = pl.BlockSpec((nseq, steps_per_tile, SSM_WIDTH), lambda i: (0, i, 0))
    else:
        io_spec = pl.BlockSpec((tr, SSM_WIDTH), lambda i: (i, 0))
    state_vec = _layer_block((1, SSM_FLAT), layer)
    group_map = _layer_block((SSM_WIDTH, SSM_FLAT), layer)
    width_vec = _layer_block((1, SSM_WIDTH), layer)
    scratch = ([pltpu.VMEM((nseq, SSM_FLAT), F32)] * 2 + [pltpu.VMEM((tr, SSM_FLAT), F32)] * 4
               + [pltpu.VMEM((SSM_WIDTH // LANES, tr, LANES), F32)] * seq_major)
    return pl.pallas_call(
        functools.partial(_s5_kernel, nseq=nseq),
        grid=(rows // tr,),
        in_specs=[io_spec, state_in, state_in, state_vec, state_vec, group_map, group_map, group_map, group_map,
                  width_vec, _layer_block((SSM_WIDTH, SSM_WIDTH), layer), width_vec],
        out_specs=[io_spec, state, state],
        out_shape=[jax.ShapeDtypeStruct(u.shape, MXU_DTYPE)] + [jax.ShapeDtypeStruct(state.block_shape, F32)] * 2,
        scratch_shapes=scratch,
        compiler_params=_params("arbitrary"),
        name="s5_scan",
    )(u, hre0, him0, p['a_re'], p['a_im'], p['bre_blk'], p['bim_blk'], p['cre_blk'], p['cim_blk'],
      p['ssm_d'], p['ssm_w_glu'], p['ssm_b_glu'])


def _layernorm_silu(y, g, b):
    yc = y - jnp.mean(y, axis=-1, keepdims=True)
    var = jnp.mean(yc * yc, axis=-1, keepdims=True)
    return _silu(yc * lax.rsqrt(var + EPS) * g + b)


def _conv_sample_kernel(state_ref, v_ref, w_ref, b_ref, g_ref, beta_ref, o_ref, ns_ref):
    ns = state_ref.shape[0]
    n_new = v_ref.shape[0]
    row = lambda j: state_ref[j] if j < ns else v_ref[j - ns]
    for t in range(n_new):
        acc = jnp.zeros(o_ref.shape[1:], F32)
        for k in range(CONV_K):
            acc = acc + w_ref[k:k + 1, :] * row(t + k)
        o_ref[t] = _layernorm_silu(acc + b_ref[...], g_ref[...], beta_ref[...]).astype(o_ref.dtype)
    for j in range(ns):
        ns_ref[j] = row(j + n_new)


def _conv_sample(state_all, v, p, layer):
    _, ns, batch, width = state_all.shape
    n_new = v.shape[0]
    bt = min(32, batch)
    vec = _layer_block((1, width), layer)
    return pl.pallas_call(
        _conv_sample_kernel,
        grid=(batch // bt,),
        in_specs=[
            pl.BlockSpec((None, ns, bt, width), lambda i: (layer, 0, i, 0)),
            pl.BlockSpec((n_new, bt, width), lambda i: (0, i, 0)),
            _layer_block((CONV_K, width), layer), vec, vec, vec,
        ],
        out_specs=[pl.BlockSpec((n_new, bt, width), lambda i: (0, i, 0)),
                   pl.BlockSpec((ns, bt, width), lambda i: (0, i, 0))],
        out_shape=[jax.ShapeDtypeStruct((n_new, batch, width), MXU_DTYPE),
                   jax.ShapeDtypeStruct((ns, batch, width), F32)],
        compiler_params=_params("parallel"),
        name="conv_sample",
    )(state_all, v, p['conv_w'], p['conv_b'], p['conv_ln_g'], p['conv_ln_b'])


def _outffn_stages(x_ref, a_ref, s_ref, load_conv, g1_ref, sh2_ref, sc2_ref, g2_ref, n2_ref,
                   wo_ref, wg_ref, wu_ref, wd_ref, fg_ref, o_ref, final_norm):
    tm = x_ref.shape[0]
    o1 = ATTN_WIDTH
    o2 = o1 + SSM_WIDTH
    st = {}

    def proj_attn_ssm():
        st['proj'] = _mm(a_ref[...], wo_ref[0:o1, :]) + _mm(s_ref[...], wo_ref[o1:o2, :])

    def proj_conv_residual_norm():
        proj = st.pop('proj') + _mm(load_conv(), wo_ref[o2:o2 + CONV_WIDTH, :])
        x2 = x_ref[...] + _rows(g1_ref[...], tm) * proj
        h2 = _rms(x2, n2_ref[...]) * (1.0 + _rows(sc2_ref[...], tm)) + _rows(sh2_ref[...], tm)
        st['x2'] = x2
        st['h2'] = h2.astype(MXU_DTYPE)

    def gate(cs):
        st['gate'] = _mm(st['h2'], wg_ref[:, cs])

    def up(cs):
        st['act'] = (_silu(st.pop('gate')) * _mm(st['h2'], wu_ref[:, cs])).astype(MXU_DTYPE)

    def down(cs, last):
        ffn = _mm(st.pop('act'), wd_ref[cs, :])
        st['ffn'] = ffn if 'ffn' not in st else st['ffn'] + ffn
        if last:
            y = st['x2'] + _rows(g2_ref[...], tm) * st['ffn']
            o_ref[...] = _rms(y, fg_ref[...]) if final_norm else y

    stages = [proj_attn_ssm, proj_conv_residual_norm]
    for start in range(0, D_FF, FF_CHUNK):
        cs = slice(start, min(start + FF_CHUNK, D_FF))
        stages += [functools.partial(gate, cs), functools.partial(up, cs),
                   functools.partial(down, cs, cs.stop == D_FF)]
    return stages


def _outffn_stream_kernel(x_ref, a_ref, s_ref, c_ref, g1_ref, sh2_ref, sc2_ref, g2_ref, n2_ref,
                          wo_ref, wg_ref, wu_ref, wd_ref, fg_ref,
                          o_ref, wo_out, wg_out, wu_out, wd_out, x2_s, h2_s, acc_s, *, final_norm):
    j = pl.program_id(0)
    tm = x_ref.shape[0]

    @pl.when(j == 0)
    def _():
        o1 = ATTN_WIDTH
        o2 = o1 + SSM_WIDTH
        wo = wo_ref[...].astype(MXU_DTYPE)
        wo_out[...] = wo
        proj = _mm(a_ref[...], wo[0:o1]) + _mm(s_ref[...], wo[o1:o2]) + _mm(c_ref[...], wo[o2:o2 + CONV_WIDTH])
        x2 = x_ref[...] + _rows(g1_ref[...], tm) * proj
        h2 = _rms(x2, n2_ref[...]) * (1.0 + _rows(sc2_ref[...], tm)) + _rows(sh2_ref[...], tm)
        x2_s[...] = x2
        h2_s[...] = h2.astype(MXU_DTYPE)
        acc_s[...] = jnp.zeros(acc_s.shape, F32)

    wg = wg_ref[...].astype(MXU_DTYPE)
    wu = wu_ref[...].astype(MXU_DTYPE)
    wd = wd_ref[...].astype(MXU_DTYPE)
    wg_out[...] = wg
    wu_out[...] = wu
    wd_out[...] = wd
    h2 = h2_s[...]
    act = (_silu(_mm(h2, wg)) * _mm(h2, wu)).astype(MXU_DTYPE)
    acc_s[...] += _mm(act, wd)

    @pl.when(j == pl.num_programs(0) - 1)
    def _():
        y = x2_s[...] + _rows(g2_ref[...], tm) * acc_s[...]
        o_ref[...] = _rms(y, fg_ref[...]) if final_norm else y


def _conv_stages(halo_ref, cur_ref, w_ref, b_ref, g_ref, beta_ref, store, buf, shifted, first, sub):
    tc = cur_ref.shape[0]
    base = CONV_HALO - (CONV_K - 1)
    span = shifted.shape[1]

    def fill():
        buf[0:CONV_HALO, :] = jnp.where(first, 0.0, halo_ref[...])
        buf[CONV_HALO:CONV_HALO + tc, :] = cur_ref[...]
        for s in range(1, SUBLANES):
            shifted[s - 1] = buf[s:s + span, :]

    def rows(r):
        acc = jnp.zeros((sub // SUBLANES, SUBLANES, CONV_WIDTH), F32)
        for k in range(CONV_K):
            j, s = (base + k) // SUBLANES, (base + k) % SUBLANES
            start = r * sub + j * SUBLANES
            tap = buf[start:start + sub, :] if s == 0 else shifted[s - 1, start:start + sub, :]
            acc = acc + w_ref[k][None] * tap.reshape(acc.shape)
        y = _layernorm_silu(acc.reshape(sub, CONV_WIDTH) + b_ref[...], g_ref[...], beta_ref[...])
        store(slice(r * sub, (r + 1) * sub), y)

    return [fill] + [functools.partial(rows, r) for r in range(tc // sub)]


def _outffn_conv_kernel(x_ref, a_ref, s_ref, halo_ref, vcur_ref, g1_ref, sh2_ref, sc2_ref, g2_ref, n2_ref,
                        wo_ref, wg_ref, wu_ref, wd_ref, fg_ref, cw_ref, cb_ref, cg_ref, cbeta_ref,
                        o_ref, buf, shifted, conv_out, *, final_norm, n_tiles, tiles_per_batch, sub):
    j = pl.program_id(0)
    conv_tile = jnp.minimum(j, n_tiles - 1)
    first = (conv_tile % tiles_per_batch) == 0

    def conv_into(slot):
        def store(rows, y):
            conv_out[slot, rows, :] = y.astype(conv_out.dtype)
        return _conv_stages(halo_ref, vcur_ref, cw_ref, cb_ref, cg_ref, cbeta_ref, store, buf, shifted, first, sub)

    @pl.when(j == 0)
    def _():
        for stage in conv_into(0):
            stage()

    @pl.when(j > 0)
    def _():
        slot = j % 2
        conv = conv_into(slot)
        ffn = _outffn_stages(x_ref, a_ref, s_ref, lambda: conv_out[1 - slot], g1_ref, sh2_ref, sc2_ref, g2_ref,
                             n2_ref, wo_ref, wg_ref, wu_ref, wd_ref, fg_ref, o_ref, final_norm)
        conv[0]()
        per_ffn = -(-(len(conv) - 1) // len(ffn))
        for k, ffn_stage in enumerate(ffn):
            for conv_stage in conv[1 + k * per_ffn:1 + (k + 1) * per_ffn]:
                conv_stage()
            ffn_stage()


def _out_ffn_sample(x, o_attn, o_ssm, o_conv, mod, p, layer, final_norm):
    t = x.shape[0]
    fc = FF_STREAM_CHUNK
    whole = lambda *shape: pl.BlockSpec(shape, lambda j: (0,) * len(shape))
    gate_up = pl.BlockSpec((None, D_MODEL, fc), lambda j: (layer, 0, j))
    down = pl.BlockSpec((None, fc, D_MODEL), lambda j: (layer, j, 0))
    return pl.pallas_call(
        functools.partial(_outffn_stream_kernel, final_norm=final_norm),
        grid=(D_FF // fc,),
        in_specs=[
            whole(t, D_MODEL), whole(t, ATTN_WIDTH), whole(t, SSM_WIDTH), whole(t, CONV_WIDTH),
            _mod_spec(mod, layer, 2, 1), _mod_spec(mod, layer, 3, 1),
            _mod_spec(mod, layer, 4, 1), _mod_spec(mod, layer, 5, 1),
            _layer_block((1, D_MODEL), layer),
            _layer_block((D_MODEL, D_MODEL), layer), gate_up, gate_up, down,
            whole(1, D_MODEL),
        ],
        out_specs=[whole(t, D_MODEL), whole(D_MODEL, D_MODEL),
                   pl.BlockSpec((D_MODEL, fc), lambda j: (0, j)), pl.BlockSpec((D_MODEL, fc), lambda j: (0, j)),
                   pl.BlockSpec((fc, D_MODEL), lambda j: (j, 0))],
        out_shape=[jax.ShapeDtypeStruct((t, D_MODEL), F32), jax.ShapeDtypeStruct((D_MODEL, D_MODEL), MXU_DTYPE),
                   jax.ShapeDtypeStruct((D_MODEL, D_FF), MXU_DTYPE), jax.ShapeDtypeStruct((D_MODEL, D_FF), MXU_DTYPE),
                   jax.ShapeDtypeStruct((D_FF, D_MODEL), MXU_DTYPE)],
        scratch_shapes=[pltpu.VMEM((t, D_MODEL), F32), pltpu.VMEM((t, D_MODEL), MXU_DTYPE),
                        pltpu.VMEM((t, D_MODEL), F32)],
        compiler_params=_params("arbitrary"),
        name="out_projection_ffn_stream",
    )(x, o_attn, o_ssm, o_conv, mod.array, mod.array, mod.array, mod.array, p['norm2_g'],
      p['w_out'], p['w_gate'], p['w_up'], p['w_down'], p['final_g'])


def _out_ffn_conv(x, o_attn, o_ssm, vc, mod, p, weights, layer, final_norm, tm, tiles_per_batch):
    t = x.shape[0]
    n_tiles = t // tm
    halo_per_tile = tm // CONV_HALO
    ffn_tile = lambda j: jnp.maximum(j - 1, 0)
    conv_tile = lambda j: jnp.minimum(j, n_tiles - 1)
    ffn_row = lambda j: (ffn_tile(j), 0)
    resident = lambda shape: pl.BlockSpec(shape, lambda j: (0, 0), pipeline_mode=pl.Buffered(1))
    cvec = _layer_block((1, CONV_WIDTH), layer)
    span = tm + CONV_HALO - SUBLANES
    return pl.pallas_call(
        functools.partial(_outffn_conv_kernel, final_norm=final_norm, n_tiles=n_tiles,
                          tiles_per_batch=tiles_per_batch, sub=CONV_ROWS_PER_STAGE),
        grid=(n_tiles + 1,),
        in_specs=[
            pl.BlockSpec((tm, D_MODEL), ffn_row),
            pl.BlockSpec((tm, ATTN_WIDTH), ffn_row),
            pl.BlockSpec((tm, SSM_WIDTH), ffn_row),
            pl.BlockSpec((CONV_HALO, CONV_WIDTH), lambda j: (jnp.maximum(conv_tile(j) * halo_per_tile - 1, 0), 0)),
            pl.BlockSpec((tm, CONV_WIDTH), lambda j: (conv_tile(j), 0)),
            _mod_spec(mod, layer, 2, tiles_per_batch, ffn_tile),
            _mod_spec(mod, layer, 3, tiles_per_batch, ffn_tile),
            _mod_spec(mod, layer, 4, tiles_per_batch, ffn_tile),
            _mod_spec(mod, layer, 5, tiles_per_batch, ffn_tile),
            _layer_block((1, D_MODEL), layer),
            resident((D_MODEL, D_MODEL)),
            resident((D_MODEL, D_FF)),
            resident((D_MODEL, D_FF)),
            resident((D_FF, D_MODEL)),
            pl.BlockSpec((1, D_MODEL), lambda j: (0, 0)),
            _layer_block((CONV_K, SUBLANES, CONV_WIDTH), layer), cvec, cvec, cvec,
        ],
        out_specs=pl.BlockSpec((tm, D_MODEL), ffn_row),
        out_shape=jax.ShapeDtypeStruct((t, D_MODEL), F32),
        scratch_shapes=[pltpu.VMEM((CONV_HALO + tm, CONV_WIDTH), F32),
                        pltpu.VMEM((SUBLANES - 1, span, CONV_WIDTH), F32),
                        pltpu.VMEM((2, tm, CONV_WIDTH), MXU_DTYPE)],
        compiler_params=_params("arbitrary"),
        name="conv_out_projection_ffn",
    )(x, o_attn, o_ssm, vc, vc, mod.array, mod.array, mod.array, mod.array, p['norm2_g'],
      *weights, p['final_g'], p['conv_w_tiles'], p['conv_b'], p['conv_ln_g'], p['conv_ln_b'])


def _layer_prompt(x, mod, p, weights, layer, rope, batch, seq, final_norm):
    tm = min(512, seq)
    tiles_per_batch = seq // tm
    tp = min(1024, seq)
    w_in, *ffn_weights = weights
    q, kv, u, vc = _in_projection(x, mod, p, w_in, layer, rope[0], rope[1], tp, seq // tp)
    o_attn = _attention_prompt(q, kv, p['sinks'], layer, batch, seq)
    n_keep = min(WINDOW, seq)
    kv_keep = kv.reshape(batch, seq, 2 * KV_WIDTH)[:, seq - n_keep:]
    new_k = kv_keep[:, :, :KV_WIDTH].reshape(batch, n_keep, N_KV_HEADS, HEAD_DIM)
    new_v = kv_keep[:, :, KV_WIDTH:].reshape(batch, n_keep, N_KV_HEADS, HEAD_DIM)

    nseg = PROMPT_SEGMENTS
    lseg = seq // nseg
    nseq = batch * nseg
    u_seq = u.reshape(nseq, lseg, SSM_WIDTH)
    steps = min(256, lseg)
    end_re, end_im = _s5_first_segment_states(u_seq, p, layer, nseg)
    start = lambda e: jnp.concatenate([jnp.zeros_like(e), e], axis=1).reshape(nseq, SSM_FLAT)
    o_seq, h_re, h_im = _s5_scan(u_seq, start(end_re), start(end_im), p, layer, nseq, steps)
    o_ssm = o_seq.reshape(batch * seq, SSM_WIDTH)
    last = lambda h: h.reshape(batch, nseg, N_SSM_GROUPS, SSM_STATE)[:, nseg - 1]

    new_conv = vc.reshape(batch, seq, CONV_WIDTH)[:, seq - (CONV_K - 1):]

    x = _out_ffn_conv(x, o_attn, o_ssm, vc, mod, p, ffn_weights, layer, final_norm, tm, tiles_per_batch)
    return x, new_k, new_v, last(h_re), last(h_im), new_conv


def _layer_sample(x, mod, p, layer, rope, batch, n_new, kt_all, vt_all, h0_re, h0_im, conv_all, final_norm):
    t = batch * n_new
    qexp, kn, vn, u, vc, w_in = _in_projection(x, mod, p, p['w_in'], layer, rope[0], rope[1], t, 1, n_new)
    new_k = kn.reshape(batch, SUBLANES, LANES)
    new_v = vn.reshape(batch, SUBLANES, LANES)
    o = _attention_sample(qexp.reshape(batch, N_Q_HEADS * n_new, LANES), kt_all, vt_all, new_k, new_v,
                          p['sinks'], layer, n_new)
    o = o[:, :, :HEAD_DIM].reshape(batch, N_Q_HEADS, n_new, HEAD_DIM)
    o_attn = jnp.transpose(o, (2, 0, 1, 3)).reshape(t, ATTN_WIDTH)

    o_ssm, h_re, h_im = _s5_scan(u, h0_re, h0_im, p, layer, batch, n_new)
    st = lambda h: h

    o_conv, new_conv = _conv_sample(conv_all, vc.reshape(n_new, batch, CONV_WIDTH), p, layer)
    o_conv = o_conv.reshape(t, CONV_WIDTH)

    x, *ffn_weights = _out_ffn_sample(x, o_attn, o_ssm, o_conv, mod, p, layer, final_norm)
    return x, [w_in] + ffn_weights, new_k, new_v, st(h_re), st(h_im), new_conv


def kernel(x_prompt, x_sample, c_prompt, c_sample, cache_k, cache_v, state_ssm_re, state_ssm_im, state_conv,
           norm1_g, norm2_g, w_mod, b_mod, w_in, attn_sinks, ssm_lam_re, ssm_lam_im, ssm_log_dt,
           ssm_b_re, ssm_b_im, ssm_c_re, ssm_c_im, ssm_d, ssm_w_glu, ssm_b_glu,
           conv_w, conv_b, conv_ln_g, conv_ln_b, w_out, w_gate, w_up, w_down, final_norm_g):
    bp, seq, d = x_prompt.shape
    bs, n_new, _ = x_sample.shape
    depth = w_in.shape[0]
    assert PROMPT_SEGMENTS == 2 and seq % (PROMPT_SEGMENTS * SUBLANES) == 0

    c_all = jnp.concatenate([c_sample, c_prompt], axis=0)
    pad_rows = -c_all.shape[0] % SUBLANES
    mods = _modulation(jnp.pad(c_all, ((0, pad_rows), (0, 0))), w_mod, b_mod)
    mod_p = _Mod(mods[:, bs:bs + bp].reshape(depth, bp, 1, N_MOD * d), None)
    mod_s = _Mod(mods, bs)

    a_re, a_im, bre_blk, bim_blk, cre_blk, cim_blk, decay_re, decay_im, span_re, span_im = _s5_discretise(
        ssm_lam_re, ssm_lam_im, ssm_log_dt, ssm_b_re, ssm_b_im, ssm_c_re, ssm_c_im,
        min(S5_STATE_CHUNK, seq // PROMPT_SEGMENTS))
    rope_p = _rope_tables(jnp.arange(seq))
    rope_s = _rope_tables(jnp.repeat(PAST_LEN + jnp.arange(n_new), bs))

    cast = lambda a: a.astype(MXU_DTYPE)
    vec = lambda a: a.reshape(depth, 1, a.shape[-1])
    p = {
        'norm1_g': vec(norm1_g), 'norm2_g': vec(norm2_g), 'w_in': w_in, 'sinks': attn_sinks,
        'a_re': a_re, 'a_im': a_im,
        'bre_blk': bre_blk, 'bim_blk': bim_blk,
        'cre_blk': cre_blk, 'cim_blk': cim_blk,
        'decay_re': decay_re, 'decay_im': decay_im, 'span_re': span_re, 'span_im': span_im,
        'ssm_d': vec(ssm_d), 'ssm_w_glu': cast(ssm_w_glu), 'ssm_b_glu': vec(ssm_b_glu),
        'conv_w': conv_w, 'conv_w_tiles': jnp.repeat(conv_w[:, :, None, :], SUBLANES, axis=2),
        'conv_b': vec(conv_b), 'conv_ln_g': vec(conv_ln_g), 'conv_ln_b': vec(conv_ln_b),
        'w_out': w_out, 'w_gate': w_gate, 'w_up': w_up, 'w_down': w_down,
        'final_g': final_norm_g.reshape(1, d),
    }
    conv_all = jnp.transpose(state_conv, (0, 2, 1, 3))
    kt_all = jnp.transpose(cache_k, (0, 1, 3, 4, 2))
    vt_all = jnp.transpose(cache_v, (0, 1, 3, 4, 2))
    state_major = lambda a: jnp.transpose(a, (0, 2, 3, 1)).reshape(depth, SSM_FLAT, bs)
    batch_major = lambda a: jnp.transpose(a.reshape(depth, N_SSM_GROUPS, SSM_STATE, bs), (0, 3, 1, 2))
    ssm_re_all, ssm_im_all = state_major(state_ssm_re), state_major(state_ssm_im)

    xp = x_prompt.reshape(bp * seq, d)
    xs = jnp.transpose(x_sample, (1, 0, 2)).reshape(n_new * bs, d)
    outs_p, outs_s = [], []
    for l in range(depth):
        final = l == depth - 1
        xs, weights, *os_ = _layer_sample(xs, mod_s, p, l, rope_s, bs, n_new, kt_all, vt_all,
                                          ssm_re_all, ssm_im_all, conv_all, final)
        xp, *op = _layer_prompt(xp, mod_p, p, weights, l, rope_p, bp, seq, final)
        outs_p.append(op)
        outs_s.append(os_)
    stack = lambda outs, i: jnp.stack([o[i] for o in outs])
    y_sample = jnp.transpose(xs.reshape(n_new, bs, d), (1, 0, 2))
    new_kt, new_vt = _cache_update(kt_all, vt_all, stack(outs_s, 0), stack(outs_s, 1), n_new)
    new_k_s = jnp.transpose(new_kt, (0, 1, 4, 2, 3))
    new_v_s = jnp.transpose(new_vt, (0, 1, 4, 2, 3))
    new_conv_s = jnp.transpose(stack(outs_s, 4), (0, 2, 1, 3))
    return (xp.reshape(bp, seq, d), y_sample, *[stack(outs_p, i) for i in range(5)],
            new_k_s, new_v_s, batch_major(stack(outs_s, 2)), batch_major(stack(outs_s, 3)), new_conv_s)
```

```python
import functools
import math
from typing import NamedTuple

import jax
import jax.numpy as jnp
from jax import lax
from jax.experimental import pallas as pl
from jax.experimental.pallas import tpu as pltpu

F32 = jnp.float32
MXU_DTYPE = jnp.bfloat16

V7X_VMEM_BYTES = 64 * 1024 * 1024
VMEM_LIMIT_BYTES = V7X_VMEM_BYTES - 8 * 1024 * 1024
LANES = 128
SUBLANES = 8
V7X_MXU_DIM = 256

D_MODEL = 1024
HEAD_DIM = 64
N_Q_HEADS = 8
N_KV_HEADS = 2
Q_PER_KV = N_Q_HEADS // N_KV_HEADS
ATTN_WIDTH = N_Q_HEADS * HEAD_DIM
KV_WIDTH = N_KV_HEADS * HEAD_DIM
WINDOW = 128
ROPE_THETA = 10000.0
ATTN_SCALE = 1.0 / math.sqrt(HEAD_DIM)
SSM_WIDTH = 256
SSM_GROUP = 16
N_SSM_GROUPS = 16
SSM_STATE = 64
SSM_FLAT = N_SSM_GROUPS * SSM_STATE
CONV_WIDTH = 256
CONV_K = 31
CONV_HALO = 32
CONV_ROWS_PER_STAGE = 64
IN_WIDTH = ATTN_WIDTH + 2 * KV_WIDTH + SSM_WIDTH + 2 * CONV_WIDTH
D_FF = 2816
FF_CHUNK = 6 * V7X_MXU_DIM
FF_STREAM_CHUNK = V7X_MXU_DIM
EPS = 1e-6
NEG = -1e30
N_MOD = 6
PROMPT_SEGMENTS = 2
S5_STATE_CHUNK = 1024
PAST_LEN = 8192


def _params(*semantics):
    return pltpu.CompilerParams(dimension_semantics=semantics, vmem_limit_bytes=VMEM_LIMIT_BYTES)


def _layer_block(shape, layer, **kw):
    zeros = (0,) * len(shape)
    return pl.BlockSpec((None,) + tuple(shape), lambda *_: (layer,) + zeros, **kw)


def _sigmoid(x):
    return 1.0 / (1.0 + jnp.exp(-x))


def _silu(x):
    return x * _sigmoid(x)


def _rms(x, g):
    return x * lax.rsqrt(jnp.mean(x * x, axis=-1, keepdims=True) + EPS) * g


def _mm(a, b):
    return jnp.dot(a, b, preferred_element_type=F32)


def _mm_nt(a, b):
    return lax.dot_general(a, b, (((1,), (1,)), ((), ())), preferred_element_type=F32)


def _rows(m, n):
    return m if m.shape[0] == 1 else jnp.concatenate([m] * (n // m.shape[0]), axis=0)


def _mod_kernel(c_ref, w_ref, b_ref, o_ref):
    a = _silu(c_ref[...]).astype(MXU_DTYPE)
    o_ref[...] = _mm(a, w_ref[...].astype(MXU_DTYPE)) + b_ref[...]


def _modulation(c, w_mod, b_mod):
    depth, d, n = w_mod.shape
    rows = c.shape[0]
    tn = 3072
    return pl.pallas_call(
        _mod_kernel,
        grid=(depth, n // tn),
        in_specs=[
            pl.BlockSpec((rows, d), lambda l, j: (0, 0)),
            pl.BlockSpec((None, d, tn), lambda l, j: (l, 0, j)),
            pl.BlockSpec((None, 1, tn), lambda l, j: (l, 0, j)),
        ],
        out_specs=pl.BlockSpec((None, rows, tn), lambda l, j: (l, 0, j)),
        out_shape=jax.ShapeDtypeStruct((depth, rows, n), F32),
        compiler_params=_params("parallel", "parallel"),
        name="modulation",
    )(c, w_mod, b_mod.reshape(depth, 1, n))


class _Mod(NamedTuple):
    array: jax.Array
    batch_rows: int | None


def _mod_spec(mod, layer, chunk, tiles_per_batch, tile=lambda i: i):
    if mod.batch_rows is None:
        return pl.BlockSpec((None, None, 1, D_MODEL), lambda i: (layer, tile(i) // tiles_per_batch, 0, chunk))
    return pl.BlockSpec((None, mod.batch_rows, D_MODEL), lambda i: (layer, 0, chunk))


def _inproj_kernel(x_ref, sh_ref, sc_ref, g_ref, w_ref, cos_ref, sin_ref, *outs, n_new):
    tm = x_ref.shape[0]
    h = _rms(x_ref[...], g_ref[...]) * (1.0 + _rows(sc_ref[...], tm)) + _rows(sh_ref[...], tm)
    w = w_ref[...].astype(MXU_DTYPE)
    z = _mm(h.astype(MXU_DTYPE), w)
    cos = cos_ref[...]
    sin = sin_ref[...]
    lane = lax.broadcasted_iota(jnp.int32, (tm, LANES), 1)
    first_half = (lane % HEAD_DIM) < (HEAD_DIM // 2)

    def rope(t):
        partner = jnp.where(first_half, pltpu.roll(t, LANES - HEAD_DIM // 2, 1),
                            pltpu.roll(t, HEAD_DIM // 2, 1))
        return t * cos + partner * sin

    q_cols = [rope(z[:, j * LANES:(j + 1) * LANES]) * ATTN_SCALE for j in range(ATTN_WIDTH // LANES)]
    o = ATTN_WIDTH
    k_rot = rope(z[:, o:o + KV_WIDTH])
    v_new = z[:, o + KV_WIDTH:o + 2 * KV_WIDTH]
    if n_new is None:
        q_ref, kv_ref, u_ref, vc_ref = outs
        for j, qc in enumerate(q_cols):
            q_ref[:, j * LANES:(j + 1) * LANES] = qc.astype(q_ref.dtype)
        kv_ref[:, 0:KV_WIDTH] = k_rot
        kv_ref[:, KV_WIDTH:2 * KV_WIDTH] = v_new
    else:
        q_ref, kn_ref, vn_ref, u_ref, vc_ref, w_out = outs
        w_out[...] = w
        batch = tm // n_new
        low = lax.broadcasted_iota(jnp.int32, (batch, LANES), 1) < HEAD_DIM
        kn_ref[...] = jnp.zeros(kn_ref.shape, F32)
        vn_ref[...] = jnp.zeros(vn_ref.shape, F32)
        for t in range(n_new):
            rows = slice(t * batch, (t + 1) * batch)
            tile_row = pl.ds(SUBLANES - n_new + t, batch, stride=SUBLANES)
            kn_ref[tile_row, :] = k_rot[rows]
            vn_ref[tile_row, :] = v_new[rows]
            for h in range(N_Q_HEADS):
                kvh = h // Q_PER_KV
                piece = q_cols[h // 2][rows]
                if h % 2 != kvh:
                    piece = pltpu.roll(piece, HEAD_DIM, 1)
                piece = jnp.where(low if kvh == 0 else ~low, piece, 0.0)
                q_ref[pl.ds(h * n_new + t, batch, stride=N_Q_HEADS * n_new), :] = piece
    o += 2 * KV_WIDTH
    u_ref[...] = z[:, o:o + SSM_WIDTH]
    o += SSM_WIDTH
    za = z[:, o:o + CONV_WIDTH]
    zg = z[:, o + CONV_WIDTH:o + 2 * CONV_WIDTH]
    vc_ref[...] = za * _sigmoid(zg)


def _in_projection(x, mod, p, w_in, layer, cos, sin, tm, tiles_per_batch, n_new=None):
    t = x.shape[0]
    pos_tiles = cos.shape[0] // tm
    row = lambda i: (i, 0)
    whole = lambda *shape: pl.BlockSpec(shape, lambda i: (0,) * len(shape))
    if n_new is None:
        w_spec = whole(D_MODEL, IN_WIDTH)
        attn_specs = [pl.BlockSpec((tm, ATTN_WIDTH), row), pl.BlockSpec((tm, 2 * KV_WIDTH), row)]
        attn_shapes = [jax.ShapeDtypeStruct((t, ATTN_WIDTH), MXU_DTYPE),
                       jax.ShapeDtypeStruct((t, 2 * KV_WIDTH), F32)]
        w_out_spec, w_out_shape = [], []
    else:
        assert tm == t and KV_WIDTH == LANES
        w_spec = _layer_block((D_MODEL, IN_WIDTH), layer)
        batch = t // n_new
        shapes = [(batch * N_Q_HEADS * n_new, LANES), (batch * SUBLANES, LANES), (batch * SUBLANES, LANES)]
        attn_specs = [whole(*s) for s in shapes]
        attn_shapes = [jax.ShapeDtypeStruct(s, F32) for s in shapes]
        w_out_spec = [whole(D_MODEL, IN_WIDTH)]
        w_out_shape = [jax.ShapeDtypeStruct((D_MODEL, IN_WIDTH), MXU_DTYPE)]
    return pl.pallas_call(
        functools.partial(_inproj_kernel, n_new=n_new),
        grid=(t // tm,),
        in_specs=[
            pl.BlockSpec((tm, D_MODEL), row),
            _mod_spec(mod, layer, 0, tiles_per_batch),
            _mod_spec(mod, layer, 1, tiles_per_batch),
            _layer_block((1, D_MODEL), layer),
            w_spec,
            pl.BlockSpec((tm, LANES), lambda i: (i % pos_tiles, 0)),
            pl.BlockSpec((tm, LANES), lambda i: (i % pos_tiles, 0)),
        ],
        out_specs=(attn_specs + [pl.BlockSpec((tm, SSM_WIDTH), row), pl.BlockSpec((tm, CONV_WIDTH), row)]
                   + w_out_spec),
        out_shape=(attn_shapes + [jax.ShapeDtypeStruct((t, SSM_WIDTH), F32),
                                  jax.ShapeDtypeStruct((t, CONV_WIDTH), F32)] + w_out_shape),
        compiler_params=_params("parallel"),
        name="in_projection",
    )(x, mod.array, mod.array, p['norm1_g'], w_in, cos, sin)


def _rope_tables(pos):
    half = HEAD_DIM // 2
    inv_freq = ROPE_THETA ** (-jnp.arange(half, dtype=F32) / half)
    ang = pos.astype(F32)[:, None] * inv_freq[None, :]
    cos = jnp.tile(jnp.cos(ang), (1, LANES // half))
    sin = jnp.sin(ang)
    sin = jnp.tile(jnp.concatenate([-sin, sin], axis=1), (1, LANES // HEAD_DIM))
    return cos, sin


def _sink_softmax(s, sink_col):
    m = jnp.maximum(jnp.max(s, axis=-1, keepdims=True), sink_col)
    e = jnp.exp(s - m)
    return e * (1.0 / (jnp.sum(e, axis=-1, keepdims=True) + jnp.exp(sink_col - m)))


def _head_pair_select(x, pick_second):
    lane = lax.broadcasted_iota(jnp.int32, x.shape, 1)
    swapped = pltpu.roll(x, HEAD_DIM, 1)
    low = lane < HEAD_DIM
    return jnp.where(low, swapped, x) if pick_second else jnp.where(low, x, swapped)


def _attn_prompt_kernel(sink_ref, q_ref, kvp_ref, kvc_ref, o_ref, *, layer):
    n = pl.program_id(1)
    w = WINDOW
    n_blocks = q_ref.shape[0] // w
    rows = Q_PER_KV * w
    r_idx = lax.broadcasted_iota(jnp.int32, (rows, w), 0) % w
    c_idx = lax.broadcasted_iota(jnp.int32, (rows, w), 1)
    from_prev = c_idx > r_idx
    first_bias = jnp.where(n > 0, 0.0, NEG)
    row_head = lax.broadcasted_iota(jnp.int32, (rows, 1), 0) // w
    low = lax.broadcasted_iota(jnp.int32, (w, LANES), 1) < HEAD_DIM
    half_mask = [jnp.where(low, 1.0, 0.0).astype(MXU_DTYPE), jnp.where(low, 0.0, 1.0).astype(MXU_DTYPE)]
    sink_cols = []
    for kvh in range(N_KV_HEADS):
        sink_col = jnp.zeros((rows, 1), F32)
        for g in range(Q_PER_KV):
            sink_col = jnp.where(row_head == g, sink_ref[layer, kvh * Q_PER_KV + g], sink_col)
        sink_cols.append(sink_col)
    chains = [(i, kvh) for i in range(n_blocks) for kvh in range(N_KV_HEADS)]
    windows, values = [], []
    for i, kvh in chains:
        own = slice(i * w, (i + 1) * w)
        prev_ref, prev = (kvp_ref, slice(0, w)) if i == 0 else (kvc_ref, slice((i - 1) * w, i * w))
        kk = jnp.concatenate([prev_ref[prev, 0:KV_WIDTH], kvc_ref[own, 0:KV_WIDTH]], axis=0)
        vv = jnp.concatenate([prev_ref[prev, KV_WIDTH:2 * KV_WIDTH], kvc_ref[own, KV_WIDTH:2 * KV_WIDTH]], axis=0)
        k2 = _head_pair_select(kk, kvh == 1).astype(MXU_DTYPE)
        values.append(_head_pair_select(vv, kvh == 1).astype(MXU_DTYPE))
        pieces = []
        for g in range(Q_PER_KV):
            h = kvh * Q_PER_KV + g
            qcol = q_ref[own, (h // 2) * LANES:(h // 2 + 1) * LANES]
            pieces.append(qcol * half_mask[h % 2])
        s = _mm_nt(jnp.concatenate(pieces, axis=0), k2)
        s_prev = s[:, 0:w] + first_bias if i == 0 else s[:, 0:w]
        windows.append(jnp.where(from_prev, s_prev, s[:, w:2 * w]))
    probs = []
    for (i, kvh), sc in zip(chains, windows):
        p = _sink_softmax(sc, sink_cols[kvh])
        p2 = jnp.concatenate([jnp.where(from_prev, p, 0.0), jnp.where(from_prev, 0.0, p)], axis=1)
        probs.append(p2.astype(MXU_DTYPE))
    for (i, kvh), p2, v2 in zip(chains, probs, values):
        own = slice(i * w, (i + 1) * w)
        r = _mm(p2, v2)
        for j in range(Q_PER_KV // 2):
            col = kvh * (Q_PER_KV // 2) + j
            o_ref[own, col * LANES:(col + 1) * LANES] = jnp.where(
                low, r[2 * j * w:(2 * j + 1) * w], r[(2 * j + 1) * w:(2 * j + 2) * w]).astype(o_ref.dtype)


def _attention_prompt(q, kv, sinks, layer, batch, seq):
    tq = min(8 * WINDOW, seq)
    nt = seq // tq
    per = tq // WINDOW
    return pl.pallas_call(
        functools.partial(_attn_prompt_kernel, layer=layer),
        grid=(batch, nt),
        in_specs=[
            pl.BlockSpec(memory_space=pltpu.SMEM),
            pl.BlockSpec((tq, ATTN_WIDTH), lambda b, n: (b * nt + n, 0)),
            pl.BlockSpec((WINDOW, 2 * KV_WIDTH), lambda b, n: ((b * nt + n) * per - jnp.minimum(n, 1), 0)),
            pl.BlockSpec((tq, 2 * KV_WIDTH), lambda b, n: (b * nt + n, 0)),
        ],
        out_specs=pl.BlockSpec((tq, ATTN_WIDTH), lambda b, n: (b * nt + n, 0)),
        out_shape=jax.ShapeDtypeStruct((batch * seq, ATTN_WIDTH), MXU_DTYPE),
        compiler_params=_params("parallel", "parallel"),
        name="attention_prompt",
    )(sinks, q, kv, kv)


def _attn_sample_kernel(sink_ref, q_ref, kt_ref, vt_ref, kn_ref, vn_ref, o_ref, *, layer, n_new):
    bt, rows, _ = q_ref.shape
    nkv, hd, wb = kt_ref.shape[1:]
    kept = wb - n_new
    r = lax.broadcasted_iota(jnp.int32, (bt * rows, 2 * wb), 0)
    j = lax.broadcasted_iota(jnp.int32, (bt * rows, 2 * wb), 1)
    t_idx = r % n_new
    t_new = j - wb - kept
    mask = ((j < wb) & (j > t_idx)) | ((t_new >= 0) & (t_new <= t_idx))
    row_head = (lax.broadcasted_iota(jnp.int32, (bt * rows, 1), 0) % rows) // n_new
    sink_col = jnp.zeros((bt * rows, 1), F32)
    for h in range(N_Q_HEADS):
        sink_col = jnp.where(row_head == h, sink_ref[layer, h], sink_col)
    second_group = lax.broadcasted_iota(jnp.int32, (rows, nkv * hd), 0) >= (rows // nkv)
    flat = lambda a: a.reshape(nkv * hd, wb)
    above = jnp.zeros((wb - kn_ref.shape[1], nkv * hd), F32)

    scores, values = [], []
    for b in range(bt):
        kt, vt = kt_ref[b], vt_ref[b]
        kpos = jnp.concatenate([above, kn_ref[b]], axis=0).T
        vpos = jnp.concatenate([above, vn_ref[b]], axis=0).T
        kcat = jnp.concatenate([flat(kt), kpos], axis=1).astype(MXU_DTYPE)
        values.append(jnp.concatenate([flat(vt), vpos], axis=1).astype(MXU_DTYPE))
        scores.append(_mm(q_ref[b].astype(MXU_DTYPE), kcat))
    s = jnp.where(mask, jnp.concatenate(scores, axis=0), NEG)
    p = _sink_softmax(s, sink_col).astype(MXU_DTYPE)
    for b in range(bt):
        o = _mm_nt(p[b * rows:(b + 1) * rows], values[b])
        o_ref[b] = jnp.where(second_group, pltpu.roll(o, hd, 1), o).astype(o_ref.dtype)


def _attention_sample(qexp, kt_all, vt_all, kn, vn, sinks, layer, n_new):
    batch, rows, width = qexp.shape
    _, _, nkv, hd, wb = kt_all.shape
    bt = min(32, batch)
    blk3 = pl.BlockSpec((bt, rows, width), lambda i: (i, 0, 0))
    new = pl.BlockSpec((bt,) + kn.shape[1:], lambda i: (i, 0, 0))
    cache = pl.BlockSpec((None, bt, nkv, hd, wb), lambda i: (layer, i, 0, 0, 0))
    return pl.pallas_call(
        functools.partial(_attn_sample_kernel, layer=layer, n_new=n_new),
        grid=(batch // bt,),
        in_specs=[pl.BlockSpec(memory_space=pltpu.SMEM), blk3, cache, cache, new, new],
        out_specs=blk3,
        out_shape=jax.ShapeDtypeStruct((batch, rows, width), MXU_DTYPE),
        compiler_params=_params("parallel"),
        name="attention_sample",
    )(sinks, qexp, kt_all, vt_all, kn, vn)


def _cache_update_kernel(kt_ref, vt_ref, kn_ref, vn_ref, cs_ref, cn_ref, nk_ref, nv_ref, ncs_ref, *, n_new):
    ns = cs_ref.shape[0]
    for j in range(ns):
        ncs_ref[j] = cs_ref[j + n_new] if j + n_new < ns else cn_ref[j + n_new - ns]
    bt, nkv, hd, wb = kt_ref.shape
    kept = wb - n_new
    keep = lax.broadcasted_iota(jnp.int32, (nkv, hd, wb), 2) < kept
    above = jnp.zeros((wb - kn_ref.shape[1], nkv * hd), F32)
    for old_ref, new_rows_ref, out_ref in ((kt_ref, kn_ref, nk_ref), (vt_ref, vn_ref, nv_ref)):
        for b in range(bt):
            placed = jnp.concatenate([above, new_rows_ref[b]], axis=0).T.reshape(nkv, hd, wb)
            out_ref[b] = jnp.where(keep, pltpu.roll(old_ref[b], kept, 2), placed)


def _cache_update(kt_all, vt_all, kn_all, vn_all, conv_all, cn_all, n_new):
    depth, batch, nkv, hd, wb = kt_all.shape
    bt = min(16, batch)
    cache = pl.BlockSpec((None, bt, nkv, hd, wb), lambda l, i: (l, i, 0, 0, 0))
    new = pl.BlockSpec((None, bt) + kn_all.shape[2:], lambda l, i: (l, i, 0, 0))
    conv = lambda a: pl.BlockSpec((None, a.shape[1], bt, a.shape[3]), lambda l, i: (l, 0, i, 0))
    return pl.pallas_call(
        functools.partial(_cache_update_kernel, n_new=n_new),
        grid=(depth, batch // bt),
        in_specs=[cache, cache, new, new, conv(conv_all), conv(cn_all)],
        out_specs=[cache, cache, conv(conv_all)],
        out_shape=[jax.ShapeDtypeStruct(kt_all.shape, F32)] * 2 + [jax.ShapeDtypeStruct(conv_all.shape, F32)],
        compiler_params=_params("parallel", "parallel"),
        name="cache_update",
    )(kt_all, vt_all, kn_all, vn_all, conv_all, cn_all)


def _cmul(ar, ai, br, bi):
    return ar * br - ai * bi, ar * bi + ai * br


def _s5_discretise_kernel(lr_ref, li_ref, ldt_ref, bre_ref, bim_ref, cre_ref, cim_ref,
                          are_ref, aim_ref, bbre_ref, bbim_ref, ccre_ref, ccim_ref,
                          decay_re_ref, decay_im_ref, span_re_ref, span_im_ref):
    lr = lr_ref[...]
    li = li_ref[...]
    dt = jnp.exp(ldt_ref[...])
    mag = jnp.exp(lr * dt)
    ab_re = mag * jnp.cos(li * dt)
    ab_im = mag * jnp.sin(li * dt)
    den = lr * lr + li * li
    nr = ab_re - 1.0
    coef_re = (nr * lr + ab_im * li) / den
    coef_im = (ab_im * lr - nr * li) / den
    are_ref[...] = ab_re
    aim_ref[...] = ab_im
    br = bre_ref[...]
    bi = bim_ref[...]
    bb_re = coef_re * br - coef_im * bi
    bb_im = coef_re * bi + coef_im * br
    lane_group = lax.broadcasted_iota(jnp.int32, bb_re.shape, 1) // SSM_STATE
    for src, dst in ((bb_re, bbre_ref), (bb_im, bbim_ref), (cre_ref[...], ccre_ref), (cim_ref[...], ccim_ref)):
        for g in range(N_SSM_GROUPS):
            dst[g * SSM_GROUP:(g + 1) * SSM_GROUP, :] = jnp.where(lane_group == g, src, 0.0).astype(dst.dtype)
    t_chunk = decay_re_ref.shape[0]
    row = lax.broadcasted_iota(jnp.int32, (SUBLANES, SSM_FLAT), 0)
    pr, pi = jnp.ones_like(ab_re), jnp.zeros_like(ab_re)
    tile_re, tile_im = jnp.zeros((SUBLANES, SSM_FLAT), F32), jnp.zeros((SUBLANES, SSM_FLAT), F32)
    for k in range(SUBLANES):
        tile_re = jnp.where(row == SUBLANES - 1 - k, pr, tile_re)
        tile_im = jnp.where(row == SUBLANES - 1 - k, pi, tile_im)
        pr, pi = _cmul(pr, pi, ab_re, ab_im)
    decay_re_ref[t_chunk - SUBLANES:t_chunk, :] = tile_re
    decay_im_ref[t_chunk - SUBLANES:t_chunk, :] = tile_im
    n = SUBLANES
    while n < t_chunk:
        lo, hi = slice(t_chunk - 2 * n, t_chunk - n), slice(t_chunk - n, t_chunk)
        decay_re_ref[lo, :], decay_im_ref[lo, :] = _cmul(decay_re_ref[hi, :], decay_im_ref[hi, :], pr, pi)
        pr, pi = _cmul(pr, pi, pr, pi)
        n *= 2
    span_re_ref[...] = pr
    span_im_ref[...] = pi


def _s5_discretise(lam_re, lam_im, log_dt, b_re, b_im, c_re, c_im, t_chunk):
    depth = lam_re.shape[0]
    assert t_chunk % SUBLANES == 0 and (t_chunk // SUBLANES) & (t_chunk // SUBLANES - 1) == 0
    flat = lambda a: a.reshape(depth, 1, SSM_FLAT)
    ldt = jnp.broadcast_to(log_dt[:, :, None], lam_re.shape)
    bt = lambda a: jnp.transpose(a, (0, 3, 1, 2)).reshape(depth, SSM_GROUP, SSM_FLAT)
    ct = lambda a: jnp.transpose(a, (0, 2, 1, 3)).reshape(depth, SSM_GROUP, SSM_FLAT)
    vec = pl.BlockSpec((None, 1, SSM_FLAT), lambda l: (l, 0, 0))
    mat = pl.BlockSpec((None, SSM_GROUP, SSM_FLAT), lambda l: (l, 0, 0))
    blk = pl.BlockSpec((None, SSM_WIDTH, SSM_FLAT), lambda l: (l, 0, 0))
    tab = pl.BlockSpec((None, t_chunk, SSM_FLAT), lambda l: (l, 0, 0))
    return pl.pallas_call(
        _s5_discretise_kernel,
        grid=(depth,),
        in_specs=[vec, vec, vec, mat, mat, mat, mat],
        out_specs=[vec, vec, blk, blk, blk, blk, tab, tab, vec, vec],
        out_shape=[jax.ShapeDtypeStruct((depth, 1, SSM_FLAT), F32)] * 2
        + [jax.ShapeDtypeStruct((depth, SSM_WIDTH, SSM_FLAT), MXU_DTYPE)] * 4
        + [jax.ShapeDtypeStruct((depth, t_chunk, SSM_FLAT), F32)] * 2
        + [jax.ShapeDtypeStruct((depth, 1, SSM_FLAT), F32)] * 2,
        compiler_params=_params("parallel"),
        name="s5_discretise",
    )(flat(lam_re), flat(lam_im), flat(ldt), bt(b_re), bt(b_im), ct(c_re), ct(c_im))


def _s5_first_segment_kernel(u_ref, bre_ref, bim_ref, dre_ref, dim_ref, sre_ref, sim_ref, ere_ref, eim_ref,
                             hre_s, him_s):
    @pl.when(pl.program_id(1) == 0)
    def _():
        hre_s[...] = jnp.zeros(hre_s.shape, F32)
        him_s[...] = jnp.zeros(him_s.shape, F32)

    ub = u_ref[...].astype(MXU_DTYPE)
    wr, wi = _cmul(dre_ref[...], dim_ref[...], _mm(ub, bre_ref[...]), _mm(ub, bim_ref[...]))
    cr, ci = _cmul(sre_ref[...], sim_ref[...], hre_s[...], him_s[...])
    hr = cr + jnp.sum(wr, axis=0, keepdims=True)
    hi = ci + jnp.sum(wi, axis=0, keepdims=True)
    hre_s[...] = hr
    him_s[...] = hi
    ere_ref[...] = hr
    eim_ref[...] = hi


def _s5_first_segment_states(u_seq, p, layer, nseg):
    nseq, lseg, _ = u_seq.shape
    batch = nseq // nseg
    t_chunk = p['decay_re'].shape[1]
    vec = _layer_block((1, SSM_FLAT), layer)
    tab = _layer_block((t_chunk, SSM_FLAT), layer)
    bmat = _layer_block((SSM_WIDTH, SSM_FLAT), layer)
    out = pl.BlockSpec((None, 1, SSM_FLAT), lambda b, c: (b, 0, 0))
    return pl.pallas_call(
        _s5_first_segment_kernel,
        grid=(batch, lseg // t_chunk),
        in_specs=[pl.BlockSpec((None, t_chunk, SSM_WIDTH), lambda b, c: (b * nseg, c, 0)),
                  bmat, bmat, tab, tab, vec, vec],
        out_specs=[out, out],
        out_shape=[jax.ShapeDtypeStruct((batch, 1, SSM_FLAT), F32)] * 2,
        scratch_shapes=[pltpu.VMEM((1, SSM_FLAT), F32)] * 2,
        compiler_params=_params("parallel", "arbitrary"),
        name="s5_first_segment_states",
    )(u_seq, p['bre_blk'], p['bim_blk'], p['decay_re'], p['decay_im'], p['span_re'], p['span_im'])


def _s5_kernel(u_ref, hre0_ref, him0_ref, are_ref, aim_ref, bre_ref, bim_ref, cre_ref, cim_ref, d_ref,
               wglu_ref, bglu_ref, o_ref, hre_out, him_out, hre_s, him_s, bure_s, buim_s, hsre_s, hsim_s,
               *stage, nseq):
    seq_major = bool(stage)
    steps = u_ref.shape[1] if seq_major else u_ref.shape[0] // nseq
    lane_halves = [slice(h * LANES, (h + 1) * LANES) for h in range(SSM_WIDTH // LANES)]

    @pl.when(pl.program_id(0) == 0)
    def _():
        hre_s[...] = hre0_ref[...] if hre0_ref.shape == hre_s.shape else hre0_ref[...].T
        him_s[...] = him0_ref[...] if him0_ref.shape == him_s.shape else him0_ref[...].T

    if seq_major:
        stage_s, = stage
        for s in range(nseq):
            for h, cols in enumerate(lane_halves):
                stage_s[h, pl.ds(s, steps, stride=nseq), :] = u_ref[s, :, cols]
        u = jnp.concatenate([stage_s[h] for h in range(len(lane_halves))], axis=1)
    else:
        u = u_ref[...]
    ub = u.astype(MXU_DTYPE)
    bure_s[...] = _mm(ub, bre_ref[...])
    buim_s[...] = _mm(ub, bim_ref[...])
    ar = jnp.broadcast_to(are_ref[...], (nseq, SSM_FLAT))
    ai = jnp.broadcast_to(aim_ref[...], (nseq, SSM_FLAT))

    def step(t, carry):
        hr, hi = carry
        rows = pl.ds(pl.multiple_of(t * nseq, nseq), nseq)
        nhr = ar * hr - ai * hi + bure_s[rows, :]
        nhi = ar * hi + ai * hr + buim_s[rows, :]
        hsre_s[rows, :] = nhr
        hsim_s[rows, :] = nhi
        return nhr, nhi

    hr, hi = lax.fori_loop(0, steps, step, (hre_s[...], him_s[...]), unroll=min(steps, 8))
    hre_s[...] = hr
    him_s[...] = hi
    hre_out[...] = hr if hre_out.shape == hr.shape else hr.T
    him_out[...] = hi if him_out.shape == hi.shape else hi.T
    y = (_mm_nt(hsre_s[...].astype(MXU_DTYPE), cre_ref[...])
         - _mm_nt(hsim_s[...].astype(MXU_DTYPE), cim_ref[...]))
    z = jax.nn.gelu(y + d_ref[...] * u)
    gate = _mm(z.astype(MXU_DTYPE), wglu_ref[...]) + bglu_ref[...]
    out = z * _sigmoid(gate)
    if seq_major:
        for h, cols in enumerate(lane_halves):
            stage_s[h] = out[:, cols]
        for s in range(nseq):
            for h, cols in enumerate(lane_halves):
                o_ref[s, :, cols] = stage_s[h, pl.ds(s, steps, stride=nseq), :].astype(o_ref.dtype)
    else:
        o_ref[...] = out.astype(o_ref.dtype)


def _s5_scan(u, hre0, him0, p, layer, nseq, steps_per_tile):
    seq_major = u.ndim == 3
    rows = u.shape[0] * u.shape[1] if seq_major else u.shape[0]
    tr = steps_per_tile * nseq
    if hre0.ndim == 3:
        state_in = _layer_block((SSM_FLAT, nseq), layer)
        state = pl.BlockSpec((SSM_FLAT, nseq), lambda i: (0, 0))
    else:
        state_in = state = pl.BlockSpec((nseq, SSM_FLAT), lambda i: (0, 0))
    if seq_major:
        io_spec = pl.BlockSpec((nseq, steps_per_tile, SSM_WIDTH), lambda i: (0, i, 0))
    else:
        io_spec = pl.BlockSpec((tr, SSM_WIDTH), lambda i: (i, 0))
    state_vec = _layer_block((1, SSM_FLAT), layer)
    group_map = _layer_block((SSM_WIDTH, SSM_FLAT), layer)
    width_vec = _layer_block((1, SSM_WIDTH), layer)
    scratch = ([pltpu.VMEM((nseq, SSM_FLAT), F32)] * 2 + [pltpu.VMEM((tr, SSM_FLAT), F32)] * 4
               + [pltpu.VMEM((SSM_WIDTH // LANES, tr, LANES), F32)] * seq_major)
    return pl.pallas_call(
        functools.partial(_s5_kernel, nseq=nseq),
        grid=(rows // tr,),
        in_specs=[io_spec, state_in, state_in, state_vec, state_vec, group_map, group_map, group_map, group_map,
                  width_vec, _layer_block((SSM_WIDTH, SSM_WIDTH), layer), width_vec],
        out_specs=[io_spec, state, state],
        out_shape=[jax.ShapeDtypeStruct(u.shape, MXU_DTYPE)] + [jax.ShapeDtypeStruct(state.block_shape, F32)] * 2,
        scratch_shapes=scratch,
        compiler_params=_params("arbitrary"),
        name="s5_scan",
    )(u, hre0, him0, p['a_re'], p['a_im'], p['bre_blk'], p['bim_blk'], p['cre_blk'], p['cim_blk'],
      p['ssm_d'], p['ssm_w_glu'], p['ssm_b_glu'])


def _layernorm_silu(y, g, b):
    yc = y - jnp.mean(y, axis=-1, keepdims=True)
    var = jnp.mean(yc * yc, axis=-1, keepdims=True)
    return _silu(yc * lax.rsqrt(var + EPS) * g + b)


def _conv_sample_kernel(state_ref, v_ref, w_ref, b_ref, g_ref, beta_ref, o_ref):
    ns = state_ref.shape[0]
    n_new = v_ref.shape[0]
    row = lambda j: state_ref[j] if j < ns else v_ref[j - ns]
    for t in range(n_new):
        acc = jnp.zeros(o_ref.shape[1:], F32)
        for k in range(CONV_K):
            acc = acc + w_ref[k:k + 1, :] * row(t + k)
        o_ref[t] = _layernorm_silu(acc + b_ref[...], g_ref[...], beta_ref[...]).astype(o_ref.dtype)


def _conv_sample(state_all, v, p, layer):
    _, ns, batch, width = state_all.shape
    n_new = v.shape[0]
    bt = min(32, batch)
    vec = _layer_block((1, width), layer)
    return pl.pallas_call(
        _conv_sample_kernel,
        grid=(batch // bt,),
        in_specs=[
            pl.BlockSpec((None, ns, bt, width), lambda i: (layer, 0, i, 0)),
            pl.BlockSpec((n_new, bt, width), lambda i: (0, i, 0)),
            _layer_block((CONV_K, width), layer), vec, vec, vec,
        ],
        out_specs=pl.BlockSpec((n_new, bt, width), lambda i: (0, i, 0)),
        out_shape=jax.ShapeDtypeStruct((n_new, batch, width), MXU_DTYPE),
        compiler_params=_params("parallel"),
        name="conv_sample",
    )(state_all, v, p['conv_w'], p['conv_b'], p['conv_ln_g'], p['conv_ln_b'])


def _outffn_stages(x_ref, a_ref, s_ref, load_conv, g1_ref, sh2_ref, sc2_ref, g2_ref, n2_ref,
                   wo_ref, wg_ref, wu_ref, wd_ref, fg_ref, o_ref, final_norm):
    tm = x_ref.shape[0]
    o1 = ATTN_WIDTH
    o2 = o1 + SSM_WIDTH
    st = {}

    def proj_attn_ssm():
        st['proj'] = _mm(a_ref[...], wo_ref[0:o1, :]) + _mm(s_ref[...], wo_ref[o1:o2, :])

    def proj_conv_residual_norm():
        proj = st.pop('proj') + _mm(load_conv(), wo_ref[o2:o2 + CONV_WIDTH, :])
        x2 = x_ref[...] + _rows(g1_ref[...], tm) * proj
        h2 = _rms(x2, n2_ref[...]) * (1.0 + _rows(sc2_ref[...], tm)) + _rows(sh2_ref[...], tm)
        st['x2'] = x2
        st['h2'] = h2.astype(MXU_DTYPE)

    def gate(cs):
        st['gate'] = _mm(st['h2'], wg_ref[:, cs])

    def up(cs):
        st['act'] = (_silu(st.pop('gate')) * _mm(st['h2'], wu_ref[:, cs])).astype(MXU_DTYPE)

    def down(cs, last):
        ffn = _mm(st.pop('act'), wd_ref[cs, :])
        st['ffn'] = ffn if 'ffn' not in st else st['ffn'] + ffn
        if last:
            y = st['x2'] + _rows(g2_ref[...], tm) * st['ffn']
            o_ref[...] = _rms(y, fg_ref[...]) if final_norm else y

    stages = [proj_attn_ssm, proj_conv_residual_norm]
    for start in range(0, D_FF, FF_CHUNK):
        cs = slice(start, min(start + FF_CHUNK, D_FF))
        stages += [functools.partial(gate, cs), functools.partial(up, cs),
                   functools.partial(down, cs, cs.stop == D_FF)]
    return stages


def _outffn_stream_kernel(x_ref, a_ref, s_ref, c_ref, g1_ref, sh2_ref, sc2_ref, g2_ref, n2_ref,
                          wo_ref, wg_ref, wu_ref, wd_ref, fg_ref,
                          o_ref, wo_out, wg_out, wu_out, wd_out, x2_s, h2_s, acc_s, *, final_norm):
    j = pl.program_id(0)
    tm = x_ref.shape[0]

    @pl.when(j == 0)
    def _():
        o1 = ATTN_WIDTH
        o2 = o1 + SSM_WIDTH
        wo = wo_ref[...].astype(MXU_DTYPE)
        wo_out[...] = wo
        proj = _mm(a_ref[...], wo[0:o1]) + _mm(s_ref[...], wo[o1:o2]) + _mm(c_ref[...], wo[o2:o2 + CONV_WIDTH])
        x2 = x_ref[...] + _rows(g1_ref[...], tm) * proj
        h2 = _rms(x2, n2_ref[...]) * (1.0 + _rows(sc2_ref[...], tm)) + _rows(sh2_ref[...], tm)
        x2_s[...] = x2
        h2_s[...] = h2.astype(MXU_DTYPE)
        acc_s[...] = jnp.zeros(acc_s.shape, F32)

    wg = wg_ref[...].astype(MXU_DTYPE)
    wu = wu_ref[...].astype(MXU_DTYPE)
    wd = wd_ref[...].astype(MXU_DTYPE)
    wg_out[...] = wg
    wu_out[...] = wu
    wd_out[...] = wd
    h2 = h2_s[...]
    act = (_silu(_mm(h2, wg)) * _mm(h2, wu)).astype(MXU_DTYPE)
    acc_s[...] += _mm(act, wd)

    @pl.when(j == pl.num_programs(0) - 1)
    def _():
        y = x2_s[...] + _rows(g2_ref[...], tm) * acc_s[...]
        o_ref[...] = _rms(y, fg_ref[...]) if final_norm else y


def _conv_stages(halo_ref, cur_ref, w_ref, b_ref, g_ref, beta_ref, store, buf, shifted, first, sub):
    tc = cur_ref.shape[0]
    base = CONV_HALO - (CONV_K - 1)
    span = shifted.shape[1]

    def fill():
        buf[0:CONV_HALO, :] = jnp.where(first, 0.0, halo_ref[...])
        buf[CONV_HALO:CONV_HALO + tc, :] = cur_ref[...]
        for s in range(1, SUBLANES):
            shifted[s - 1] = buf[s:s + span, :]

    def rows(r):
        acc = jnp.zeros((sub // SUBLANES, SUBLANES, CONV_WIDTH), F32)
        for k in range(CONV_K):
            j, s = (base + k) // SUBLANES, (base + k) % SUBLANES
            start = r * sub + j * SUBLANES
            tap = buf[start:start + sub, :] if s == 0 else shifted[s - 1, start:start + sub, :]
            acc = acc + w_ref[k][None] * tap.reshape(acc.shape)
        y = _layernorm_silu(acc.reshape(sub, CONV_WIDTH) + b_ref[...], g_ref[...], beta_ref[...])
        store(slice(r * sub, (r + 1) * sub), y)

    return [fill] + [functools.partial(rows, r) for r in range(tc // sub)]


def _outffn_conv_kernel(x_ref, a_ref, s_ref, halo_ref, vcur_ref, g1_ref, sh2_ref, sc2_ref, g2_ref, n2_ref,
                        wo_ref, wg_ref, wu_ref, wd_ref, fg_ref, cw_ref, cb_ref, cg_ref, cbeta_ref,
                        o_ref, buf, shifted, conv_out, *, final_norm, n_tiles, tiles_per_batch, sub):
    j = pl.program_id(0)
    conv_tile = jnp.minimum(j, n_tiles - 1)
    first = (conv_tile % tiles_per_batch) == 0

    def conv_into(slot):
        def store(rows, y):
            conv_out[slot, rows, :] = y.astype(conv_out.dtype)
        return _conv_stages(halo_ref, vcur_ref, cw_ref, cb_ref, cg_ref, cbeta_ref, store, buf, shifted, first, sub)

    @pl.when(j == 0)
    def _():
        for stage in conv_into(0):
            stage()

    @pl.when(j > 0)
    def _():
        slot = j % 2
        conv = conv_into(slot)
        ffn = _outffn_stages(x_ref, a_ref, s_ref, lambda: conv_out[1 - slot], g1_ref, sh2_ref, sc2_ref, g2_ref,
                             n2_ref, wo_ref, wg_ref, wu_ref, wd_ref, fg_ref, o_ref, final_norm)
        conv[0]()
        per_ffn = -(-(len(conv) - 1) // len(ffn))
        for k, ffn_stage in enumerate(ffn):
            for conv_stage in conv[1 + k * per_ffn:1 + (k + 1) * per_ffn]:
                conv_stage()
            ffn_stage()


def _out_ffn_sample(x, o_attn, o_ssm, o_conv, mod, p, layer, final_norm):
    t = x.shape[0]
    fc = FF_STREAM_CHUNK
    whole = lambda *shape: pl.BlockSpec(shape, lambda j: (0,) * len(shape))
    gate_up = pl.BlockSpec((None, D_MODEL, fc), lambda j: (layer, 0, j))
    down = pl.BlockSpec((None, fc, D_MODEL), lambda j: (layer, j, 0))
    return pl.pallas_call(
        functools.partial(_outffn_stream_kernel, final_norm=final_norm),
        grid=(D_FF // fc,),
        in_specs=[
            whole(t, D_MODEL), whole(t, ATTN_WIDTH), whole(t, SSM_WIDTH), whole(t, CONV_WIDTH),
            _mod_spec(mod, layer, 2, 1), _mod_spec(mod, layer, 3, 1),
            _mod_spec(mod, layer, 4, 1), _mod_spec(mod, layer, 5, 1),
            _layer_block((1, D_MODEL), layer),
            _layer_block((D_MODEL, D_MODEL), layer), gate_up, gate_up, down,
            whole(1, D_MODEL),
        ],
        out_specs=[whole(t, D_MODEL), whole(D_MODEL, D_MODEL),
                   pl.BlockSpec((D_MODEL, fc), lambda j: (0, j)), pl.BlockSpec((D_MODEL, fc), lambda j: (0, j)),
                   pl.BlockSpec((fc, D_MODEL), lambda j: (j, 0))],
        out_shape=[jax.ShapeDtypeStruct((t, D_MODEL), F32), jax.ShapeDtypeStruct((D_MODEL, D_MODEL), MXU_DTYPE),
                   jax.ShapeDtypeStruct((D_MODEL, D_FF), MXU_DTYPE), jax.ShapeDtypeStruct((D_MODEL, D_FF), MXU_DTYPE),
                   jax.ShapeDtypeStruct((D_FF, D_MODEL), MXU_DTYPE)],
        scratch_shapes=[pltpu.VMEM((t, D_MODEL), F32), pltpu.VMEM((t, D_MODEL), MXU_DTYPE),
                        pltpu.VMEM((t, D_MODEL), F32)],
        compiler_params=_params("arbitrary"),
        name="out_projection_ffn_stream",
    )(x, o_attn, o_ssm, o_conv, mod.array, mod.array, mod.array, mod.array, p['norm2_g'],
      p['w_out'], p['w_gate'], p['w_up'], p['w_down'], p['final_g'])


def _out_ffn_conv(x, o_attn, o_ssm, vc, mod, p, weights, layer, final_norm, tm, tiles_per_batch):
    t = x.shape[0]
    n_tiles = t // tm
    halo_per_tile = tm // CONV_HALO
    ffn_tile = lambda j: jnp.maximum(j - 1, 0)
    conv_tile = lambda j: jnp.minimum(j, n_tiles - 1)
    ffn_row = lambda j: (ffn_tile(j), 0)
    resident = lambda shape: pl.BlockSpec(shape, lambda j: (0, 0), pipeline_mode=pl.Buffered(1))
    cvec = _layer_block((1, CONV_WIDTH), layer)
    span = tm + CONV_HALO - SUBLANES
    return pl.pallas_call(
        functools.partial(_outffn_conv_kernel, final_norm=final_norm, n_tiles=n_tiles,
                          tiles_per_batch=tiles_per_batch, sub=CONV_ROWS_PER_STAGE),
        grid=(n_tiles + 1,),
        in_specs=[
            pl.BlockSpec((tm, D_MODEL), ffn_row),
            pl.BlockSpec((tm, ATTN_WIDTH), ffn_row),
            pl.BlockSpec((tm, SSM_WIDTH), ffn_row),
            pl.BlockSpec((CONV_HALO, CONV_WIDTH), lambda j: (jnp.maximum(conv_tile(j) * halo_per_tile - 1, 0), 0)),
            pl.BlockSpec((tm, CONV_WIDTH), lambda j: (conv_tile(j), 0)),
            _mod_spec(mod, layer, 2, tiles_per_batch, ffn_tile),
            _mod_spec(mod, layer, 3, tiles_per_batch, ffn_tile),
            _mod_spec(mod, layer, 4, tiles_per_batch, ffn_tile),
            _mod_spec(mod, layer, 5, tiles_per_batch, ffn_tile),
            _layer_block((1, D_MODEL), layer),
            resident((D_MODEL, D_MODEL)),
            resident((D_MODEL, D_FF)),
            resident((D_MODEL, D_FF)),
            resident((D_FF, D_MODEL)),
            pl.BlockSpec((1, D_MODEL), lambda j: (0, 0)),
            _layer_block((CONV_K, SUBLANES, CONV_WIDTH), layer), cvec, cvec, cvec,
        ],
        out_specs=pl.BlockSpec((tm, D_MODEL), ffn_row),
        out_shape=jax.ShapeDtypeStruct((t, D_MODEL), F32),
        scratch_shapes=[pltpu.VMEM((CONV_HALO + tm, CONV_WIDTH), F32),
                        pltpu.VMEM((SUBLANES - 1, span, CONV_WIDTH), F32),
                        pltpu.VMEM((2, tm, CONV_WIDTH), MXU_DTYPE)],
        compiler_params=_params("arbitrary"),
        name="conv_out_projection_ffn",
    )(x, o_attn, o_ssm, vc, vc, mod.array, mod.array, mod.array, mod.array, p['norm2_g'],
      *weights, p['final_g'], p['conv_w_tiles'], p['conv_b'], p['conv_ln_g'], p['conv_ln_b'])


def _layer_prompt(x, mod, p, weights, layer, rope, batch, seq, final_norm):
    tm = min(512, seq)
    tiles_per_batch = seq // tm
    tp = min(1024, seq)
    w_in, *ffn_weights = weights
    q, kv, u, vc = _in_projection(x, mod, p, w_in, layer, rope[0], rope[1], tp, seq // tp)
    o_attn = _attention_prompt(q, kv, p['sinks'], layer, batch, seq)
    n_keep = min(WINDOW, seq)
    kv_keep = kv.reshape(batch, seq, 2 * KV_WIDTH)[:, seq - n_keep:]
    new_k = kv_keep[:, :, :KV_WIDTH].reshape(batch, n_keep, N_KV_HEADS, HEAD_DIM)
    new_v = kv_keep[:, :, KV_WIDTH:].reshape(batch, n_keep, N_KV_HEADS, HEAD_DIM)

    nseg = PROMPT_SEGMENTS
    lseg = seq // nseg
    nseq = batch * nseg
    u_seq = u.reshape(nseq, lseg, SSM_WIDTH)
    steps = min(256, lseg)
    end_re, end_im = _s5_first_segment_states(u_seq, p, layer, nseg)
    start = lambda e: jnp.concatenate([jnp.zeros_like(e), e], axis=1).reshape(nseq, SSM_FLAT)
    o_seq, h_re, h_im = _s5_scan(u_seq, start(end_re), start(end_im), p, layer, nseq, steps)
    o_ssm = o_seq.reshape(batch * seq, SSM_WIDTH)
    last = lambda h: h.reshape(batch, nseg, N_SSM_GROUPS, SSM_STATE)[:, nseg - 1]

    new_conv = vc.reshape(batch, seq, CONV_WIDTH)[:, seq - (CONV_K - 1):]

    x = _out_ffn_conv(x, o_attn, o_ssm, vc, mod, p, ffn_weights, layer, final_norm, tm, tiles_per_batch)
    return x, new_k, new_v, last(h_re), last(h_im), new_conv


def _layer_sample(x, mod, p, layer, rope, batch, n_new, kt_all, vt_all, h0_re, h0_im, conv_all, final_norm):
    t = batch * n_new
    qexp, kn, vn, u, vc, w_in = _in_projection(x, mod, p, p['w_in'], layer, rope[0], rope[1], t, 1, n_new)
    new_k = kn.reshape(batch, SUBLANES, LANES)
    new_v = vn.reshape(batch, SUBLANES, LANES)
    o = _attention_sample(qexp.reshape(batch, N_Q_HEADS * n_new, LANES), kt_all, vt_all, new_k, new_v,
                          p['sinks'], layer, n_new)
    o = o[:, :, :HEAD_DIM].reshape(batch, N_Q_HEADS, n_new, HEAD_DIM)
    o_attn = jnp.transpose(o, (2, 0, 1, 3)).reshape(t, ATTN_WIDTH)

    o_ssm, h_re, h_im = _s5_scan(u, h0_re, h0_im, p, layer, batch, n_new)
    st = lambda h: h

    new_conv = vc.reshape(n_new, batch, CONV_WIDTH)
    o_conv = _conv_sample(conv_all, new_conv, p, layer).reshape(t, CONV_WIDTH)

    x, *ffn_weights = _out_ffn_sample(x, o_attn, o_ssm, o_conv, mod, p, layer, final_norm)
    return x, [w_in] + ffn_weights, new_k, new_v, st(h_re), st(h_im), new_conv


def kernel(x_prompt, x_sample, c_prompt, c_sample, cache_k, cache_v, state_ssm_re, state_ssm_im, state_conv,
           norm1_g, norm2_g, w_mod, b_mod, w_in, attn_sinks, ssm_lam_re, ssm_lam_im, ssm_log_dt,
           ssm_b_re, ssm_b_im, ssm_c_re, ssm_c_im, ssm_d, ssm_w_glu, ssm_b_glu,
           conv_w, conv_b, conv_ln_g, conv_ln_b, w_out, w_gate, w_up, w_down, final_norm_g):
    bp, seq, d = x_prompt.shape
    bs, n_new, _ = x_sample.shape
    depth = w_in.shape[0]
    assert PROMPT_SEGMENTS == 2 and seq % (PROMPT_SEGMENTS * SUBLANES) == 0

    c_all = jnp.concatenate([c_sample, c_prompt], axis=0)
    pad_rows = -c_all.shape[0] % SUBLANES
    mods = _modulation(jnp.pad(c_all, ((0, pad_rows), (0, 0))), w_mod, b_mod)
    mod_p = _Mod(mods[:, bs:bs + bp].reshape(depth, bp, 1, N_MOD * d), None)
    mod_s = _Mod(mods, bs)

    a_re, a_im, bre_blk, bim_blk, cre_blk, cim_blk, decay_re, decay_im, span_re, span_im = _s5_discretise(
        ssm_lam_re, ssm_lam_im, ssm_log_dt, ssm_b_re, ssm_b_im, ssm_c_re, ssm_c_im,
        min(S5_STATE_CHUNK, seq // PROMPT_SEGMENTS))
    rope_p = _rope_tables(jnp.arange(seq))
    rope_s = _rope_tables(jnp.repeat(PAST_LEN + jnp.arange(n_new), bs))

    cast = lambda a: a.astype(MXU_DTYPE)
    vec = lambda a: a.reshape(depth, 1, a.shape[-1])
    p = {
        'norm1_g': vec(norm1_g), 'norm2_g': vec(norm2_g), 'w_in': w_in, 'sinks': attn_sinks,
        'a_re': a_re, 'a_im': a_im,
        'bre_blk': bre_blk, 'bim_blk': bim_blk,
        'cre_blk': cre_blk, 'cim_blk': cim_blk,
        'decay_re': decay_re, 'decay_im': decay_im, 'span_re': span_re, 'span_im': span_im,
        'ssm_d': vec(ssm_d), 'ssm_w_glu': cast(ssm_w_glu), 'ssm_b_glu': vec(ssm_b_glu),
        'conv_w': conv_w, 'conv_w_tiles': jnp.repeat(conv_w[:, :, None, :], SUBLANES, axis=2),
        'conv_b': vec(conv_b), 'conv_ln_g': vec(conv_ln_g), 'conv_ln_b': vec(conv_ln_b),
        'w_out': w_out, 'w_gate': w_gate, 'w_up': w_up, 'w_down': w_down,
        'final_g': final_norm_g.reshape(1, d),
    }
    conv_all = jnp.transpose(state_conv, (0, 2, 1, 3))
    kt_all = jnp.transpose(cache_k, (0, 1, 3, 4, 2))
    vt_all = jnp.transpose(cache_v, (0, 1, 3, 4, 2))
    state_major = lambda a: jnp.transpose(a, (0, 2, 3, 1)).reshape(depth, SSM_FLAT, bs)
    batch_major = lambda a: jnp.transpose(a.reshape(depth, N_SSM_GROUPS, SSM_STATE, bs), (0, 3, 1, 2))
    ssm_re_all, ssm_im_all = state_major(state_ssm_re), state_major(state_ssm_im)

    xp = x_prompt.reshape(bp * seq, d)
    xs = jnp.transpose(x_sample, (1, 0, 2)).reshape(n_new * bs, d)
    outs_p, outs_s = [], []
    for l in range(depth):
        final = l == depth - 1
        xs, weights, *os_ = _layer_sample(xs, mod_s, p, l, rope_s, bs, n_new, kt_all, vt_all,
                                          ssm_re_all, ssm_im_all, conv_all, final)
        xp, *op = _layer_prompt(xp, mod_p, p, weights, l, rope_p, bp, seq, final)
        outs_p.append(op)
        outs_s.append(os_)
    stack = lambda outs, i: jnp.stack([o[i] for o in outs])
    y_sample = jnp.transpose(xs.reshape(n_new, bs, d), (1, 0, 2))
    new_kt, new_vt, new_conv_pm = _cache_update(kt_all, vt_all, stack(outs_s, 0), stack(outs_s, 1),
                                                conv_all, stack(outs_s, 4), n_new)
    new_k_s = jnp.transpose(new_kt, (0, 1, 4, 2, 3))
    new_v_s = jnp.transpose(new_vt, (0, 1, 4, 2, 3))
    new_conv_s = jnp.transpose(new_conv_pm, (0, 2, 1, 3))
    return (xp.reshape(bp, seq, d), y_sample, *[stack(outs_p, i) for i in range(5)],
            new_k_s, new_v_s, batch_major(stack(outs_s, 2)), batch_major(stack(outs_s, 3)), new_conv_s)
```
